```python
import math
import jax, jax.numpy as jnp
from jax import lax
import numpy as np

D_MODEL = 2048
BATCH = 8
SEQ = 4096
DEPTH = 4

PLE_DIM = 256
ATTN_WIDTH = D_MODEL // 2
POOL_WIDTH = D_MODEL - ATTN_WIDTH
HEAD_DIM = 64
N_HEADS = ATTN_WIDTH // HEAD_DIM
N_KV_HEADS = max(1, N_HEADS // 8)
KV_GROUP = N_HEADS // N_KV_HEADS
WINDOW = 128
BLOCK = WINDOW
POOL_WINDOWS = (2, 4, 8, 16)
N_POOL_GROUPS = len(POOL_WINDOWS)
POOL_GROUP_DIM = POOL_WIDTH // N_POOL_GROUPS
REL_BUCKETS = 32
REL_MAX_DIST = 128
LN_EPS = 1e-5
DEEPNORM_ALPHA = (2.0 * DEPTH) ** 0.25
DEEPNORM_BETA = (8.0 * DEPTH) ** -0.25
Q_COLS = N_HEADS * HEAD_DIM
KV_COLS = N_KV_HEADS * HEAD_DIM
SPLIT_SIZES = (Q_COLS, KV_COLS, KV_COLS, ATTN_WIDTH, POOL_WIDTH, POOL_WIDTH)
SPLIT_POINTS = tuple(int(c) for c in np.cumsum(SPLIT_SIZES)[:-1])
IN_COLS = int(sum(SPLIT_SIZES))

kernel_name = "hymba_swa_sink_pool_deepnorm"


def t5_causal_bucket(dist):
    max_exact = REL_BUCKETS // 2
    d = jnp.maximum(dist, 0)
    d_f = jnp.maximum(d, 1).astype(jnp.float32)
    large = max_exact + (jnp.log(d_f / max_exact) / math.log(REL_MAX_DIST / max_exact)
                         * (REL_BUCKETS - max_exact)).astype(jnp.int32)
    large = jnp.minimum(large, REL_BUCKETS - 1)
    return jnp.where(d < max_exact, d, large)


def band_geometry():
    qq = jnp.arange(BLOCK)[:, None]
    kk = jnp.arange(2 * BLOCK)[None, :]
    dist = qq + BLOCK - kk
    in_window = (dist >= 0) & (dist < WINDOW)
    return dist, in_window


def layer_norm(x, gain, bias):
    xf = x.astype(jnp.float32)
    mu = jnp.mean(xf, axis=-1, keepdims=True)
    var = jnp.mean(jnp.square(xf - mu), axis=-1, keepdims=True)
    y = (xf - mu) * lax.rsqrt(var + LN_EPS)
    return (y * gain.astype(jnp.float32) + bias.astype(jnp.float32)).astype(x.dtype)


def banded_sink_attention(q, k, v, sinks, bias_hqk, valid):
    B, S, _ = q.shape
    nblk = S // BLOCK
    qb = q.reshape(B, nblk, BLOCK, N_KV_HEADS, KV_GROUP, HEAD_DIM)
    pad = ((0, 0), (BLOCK, 0), (0, 0))
    kp = jnp.pad(k, pad).reshape(B, nblk + 1, BLOCK, N_KV_HEADS, HEAD_DIM)
    vp = jnp.pad(v, pad).reshape(B, nblk + 1, BLOCK, N_KV_HEADS, HEAD_DIM)
    kb = jnp.concatenate([kp[:, :-1], kp[:, 1:]], axis=2)
    vb = jnp.concatenate([vp[:, :-1], vp[:, 1:]], axis=2)
    scale = 1.0 / math.sqrt(HEAD_DIM)
    scores = jnp.einsum('bnqhgd,bnkhd->bnhgqk', qb, kb).astype(jnp.float32) * scale
    scores = scores + bias_hqk[None, None]
    scores = jnp.where(valid[None, :, None, None], scores, -1e30)
    s_sink = sinks.astype(jnp.float32).reshape(N_KV_HEADS, KV_GROUP)[None, None, :, :, None, None]
    m = jnp.maximum(jnp.max(scores, axis=-1, keepdims=True), s_sink)
    e = jnp.exp(scores - m)
    denom = jnp.sum(e, axis=-1, keepdims=True) + jnp.exp(s_sink - m)
    probs = (e / denom).astype(v.dtype)
    out = jnp.einsum('bnhgqk,bnkhd->bnqhgd', probs, vb)
    return out.reshape(B, S, N_HEADS * HEAD_DIM)


def multiscale_pool(u, w_pool, pool_scale):
    B, S, _ = u.shape
    ug = u.reshape(B, S, N_POOL_GROUPS, POOL_GROUP_DIM).astype(jnp.float32)
    cs = jnp.cumsum(ug, axis=1)
    t = jnp.arange(S)
    means = []
    for g, w in enumerate(POOL_WINDOWS):
        c = cs[:, :, g]
        lagged = jnp.pad(c[:, :S - w], ((0, 0), (w, 0), (0, 0)))
        count = jnp.minimum(t + 1, w).astype(jnp.float32)[None, :, None]
        means.append((c - lagged) / count)
    pooled = jnp.stack(means, axis=2)
    diff = (pooled - ug).astype(u.dtype)
    mixed = jnp.einsum('bsgc,gcd->bsgd', diff, w_pool)
    return mixed.reshape(B, S, POOL_WIDTH) * pool_scale


def hybrid_layer(x, p_i, w_in, b_in, w_out, sinks, w_pool, pool_scale, w_ple, w_gate_ple,
                 ln_gain, ln_bias, bias_hqk, valid):
    h = jnp.einsum('bsd,dc->bsc', x, w_in) + b_in
    q, k, v, g_attn, u_pool, g_pool = jnp.split(h, SPLIT_POINTS, axis=-1)
    a = banded_sink_attention(q, k, v, sinks, bias_hqk, valid) * jax.nn.silu(g_attn)
    b = multiscale_pool(u_pool, w_pool, pool_scale) * jax.nn.silu(g_pool)
    mix = jnp.einsum('bsc,cd->bsd', jnp.concatenate([a, b], axis=-1), w_out)
    ple = jax.nn.sigmoid(jnp.einsum('bsd,de->bse', x, w_gate_ple)) * jnp.einsum('bsp,pd->bsd', p_i, w_ple)
    return layer_norm(DEEPNORM_ALPHA * x + mix + ple, ln_gain, ln_bias)


def _fwd_setup_inputs(seed: int = 0) -> dict:
    key = jax.random.key(seed)
    ks = jax.random.split(key, 14)
    f32 = jnp.float32
    x = jax.random.normal(ks[0], (BATCH, SEQ, D_MODEL), f32)
    p = jax.random.normal(ks[1], (DEPTH, BATCH, SEQ, PLE_DIM), f32)
    w_in = jax.random.normal(ks[2], (DEPTH, D_MODEL, IN_COLS), f32) * D_MODEL ** -0.5
    b_in = jax.random.normal(ks[3], (DEPTH, IN_COLS), f32) * 0.02
    w_out = jax.random.normal(ks[4], (DEPTH, D_MODEL, D_MODEL), f32) * (D_MODEL ** -0.5 * DEEPNORM_BETA)
    attn_sinks = jax.random.normal(ks[5], (DEPTH, N_HEADS), f32) * 0.5
    rel_bias = jax.random.normal(ks[6], (REL_BUCKETS, N_HEADS), f32) * 0.1
    w_pool = jax.random.normal(ks[7], (DEPTH, N_POOL_GROUPS, POOL_GROUP_DIM, POOL_GROUP_DIM), f32) * POOL_GROUP_DIM ** -0.5
    pool_scale = 1.0 + 0.1 * jax.random.normal(ks[8], (DEPTH, POOL_WIDTH), f32)
    w_ple = jax.random.normal(ks[9], (DEPTH, PLE_DIM, D_MODEL), f32) * PLE_DIM ** -0.5
    w_gate_ple = jax.random.normal(ks[10], (DEPTH, D_MODEL, D_MODEL), f32) * D_MODEL ** -0.5
    ln_gain = 1.0 + 0.02 * jax.random.normal(ks[11], (DEPTH, D_MODEL), f32)
    ln_bias = 0.02 * jax.random.normal(ks[12], (DEPTH, D_MODEL), f32)
    return {"x": x, "p": p, "w_in": w_in, "b_in": b_in, "w_out": w_out,
            "attn_sinks": attn_sinks, "rel_bias": rel_bias, "w_pool": w_pool,
            "pool_scale": pool_scale, "w_ple": w_ple, "w_gate_ple": w_gate_ple,
            "ln_gain": ln_gain, "ln_bias": ln_bias}


def _fwd_reference(x, p, w_in, b_in, w_out, attn_sinks, rel_bias, w_pool, pool_scale, w_ple,
              w_gate_ple, ln_gain, ln_bias):
    S = x.shape[1]
    nblk = S // BLOCK
    dist, in_window = band_geometry()
    bias_hqk = jnp.transpose(rel_bias[t5_causal_bucket(dist)], (2, 0, 1)).astype(jnp.float32)
    bias_hqk = bias_hqk.reshape(N_KV_HEADS, KV_GROUP, BLOCK, 2 * BLOCK)
    k_pos = (jnp.arange(nblk) * BLOCK - BLOCK)[:, None, None] + jnp.arange(2 * BLOCK)[None, None, :]
    valid = in_window[None] & (k_pos >= 0)
    for i in range(DEPTH):
        x = hybrid_layer(x, p[i], w_in[i], b_in[i], w_out[i], attn_sinks[i], w_pool[i],
                         pool_scale[i], w_ple[i], w_gate_ple[i], ln_gain[i], ln_bias[i],
                         bias_hqk, valid)
    return x


import jax as _jax
import jax.numpy as _jnp

TWIN_FORMAT = 'train_step'
FWD_PARAMS = ['x', 'p', 'w_in', 'b_in', 'w_out', 'attn_sinks', 'rel_bias', 'w_pool', 'pool_scale', 'w_ple', 'w_gate_ple', 'ln_gain', 'ln_bias']
TWIN_WEIGHTS = ['w_in', 'b_in', 'w_out', 'attn_sinks', 'rel_bias', 'w_pool', 'pool_scale', 'w_ple', 'w_gate_ple', 'ln_gain', 'ln_bias']
TWIN_DIFF_INPUT = 'x'
TWIN_INPUTS = ['x', 'p', 'w_in', 'b_in', 'w_out', 'attn_sinks', 'rel_bias', 'w_pool', 'pool_scale', 'w_ple', 'w_gate_ple', 'ln_gain', 'ln_bias', 'loss_target', 'm_w_in', 'm_b_in', 'm_w_out', 'm_attn_sinks', 'm_rel_bias', 'm_w_pool', 'm_pool_scale', 'm_w_ple', 'm_w_gate_ple', 'm_ln_gain', 'm_ln_bias', 'v_w_in', 'v_b_in', 'v_w_out', 'v_attn_sinks', 'v_rel_bias', 'v_w_pool', 'v_pool_scale', 'v_w_ple', 'v_w_gate_ple', 'v_ln_gain', 'v_ln_bias']
TWIN_OUTPUTS = ['loss', 'grad_x', 'grad_w_in', 'grad_b_in', 'grad_w_out', 'grad_attn_sinks', 'grad_rel_bias', 'grad_w_pool', 'grad_pool_scale', 'grad_w_ple', 'grad_w_gate_ple', 'grad_ln_gain', 'grad_ln_bias', 'delta_w_in', 'delta_b_in', 'delta_w_out', 'delta_attn_sinks', 'delta_rel_bias', 'delta_w_pool', 'delta_pool_scale', 'delta_w_ple', 'delta_w_gate_ple', 'delta_ln_gain', 'delta_ln_bias', 'new_m_w_in', 'new_m_b_in', 'new_m_w_out', 'new_m_attn_sinks', 'new_m_rel_bias', 'new_m_w_pool', 'new_m_pool_scale', 'new_m_w_ple', 'new_m_w_gate_ple', 'new_m_ln_gain', 'new_m_ln_bias', 'new_v_w_in', 'new_v_b_in', 'new_v_w_out', 'new_v_attn_sinks', 'new_v_rel_bias', 'new_v_w_pool', 'new_v_pool_scale', 'new_v_w_ple', 'new_v_w_gate_ple', 'new_v_ln_gain', 'new_v_ln_bias']
TWIN_LEAF_KINDS = {'loss': 'loss', 'grad_x': 'grad_x', 'grad_w_in': 'grad_w', 'grad_b_in': 'grad_w', 'grad_w_out': 'grad_w', 'grad_attn_sinks': 'grad_w', 'grad_rel_bias': 'grad_w', 'grad_w_pool': 'grad_w', 'grad_pool_scale': 'grad_w', 'grad_w_ple': 'grad_w', 'grad_w_gate_ple': 'grad_w', 'grad_ln_gain': 'grad_w', 'grad_ln_bias': 'grad_w', 'delta_w_in': 'delta_w', 'delta_b_in': 'delta_w', 'delta_w_out': 'delta_w', 'delta_attn_sinks': 'delta_w', 'delta_rel_bias': 'delta_w', 'delta_w_pool': 'delta_w', 'delta_pool_scale': 'delta_w', 'delta_w_ple': 'delta_w', 'delta_w_gate_ple': 'delta_w', 'delta_ln_gain': 'delta_w', 'delta_ln_bias': 'delta_w', 'new_m_w_in': 'new_m', 'new_m_b_in': 'new_m', 'new_m_w_out': 'new_m', 'new_m_attn_sinks': 'new_m', 'new_m_rel_bias': 'new_m', 'new_m_w_pool': 'new_m', 'new_m_pool_scale': 'new_m', 'new_m_w_ple': 'new_m', 'new_m_w_gate_ple': 'new_m', 'new_m_ln_gain': 'new_m', 'new_m_ln_bias': 'new_m', 'new_v_w_in': 'new_v', 'new_v_b_in': 'new_v', 'new_v_w_out': 'new_v', 'new_v_attn_sinks': 'new_v', 'new_v_rel_bias': 'new_v', 'new_v_w_pool': 'new_v', 'new_v_pool_scale': 'new_v', 'new_v_w_ple': 'new_v', 'new_v_w_gate_ple': 'new_v', 'new_v_ln_gain': 'new_v', 'new_v_ln_bias': 'new_v'}


def _forward(args):
    return _fwd_reference(*[args[k] for k in FWD_PARAMS])


def _output_shape():
    def fwd():
        inp = _fwd_setup_inputs(0)
        return _fwd_reference(*[inp[k] for k in FWD_PARAMS])
    out = _jax.eval_shape(fwd)
    return out.shape, out.dtype

N_MICROBATCH = 1
ADAM_LR = 0.001
ADAM_B1 = 0.9
ADAM_B2 = 0.999
ADAM_EPS = 1e-08
ADAM_WD = 0.01
ADAM_STEP = 10
PER_EXAMPLE_BATCH_AXIS = {'x': 0, 'p': 1, 'loss_target': 0}
SHARED_INPUTS = []
_WEIGHT_DTYPES = {'w_in': _jnp.float32, 'b_in': _jnp.float32, 'w_out': _jnp.float32, 'attn_sinks': _jnp.float32, 'rel_bias': _jnp.float32, 'w_pool': _jnp.float32, 'pool_scale': _jnp.float32, 'w_ple': _jnp.float32, 'w_gate_ple': _jnp.float32, 'ln_gain': _jnp.float32, 'ln_bias': _jnp.float32}
MOMENT_SCALE = {'w_in': 7.619252e-03, 'b_in': 9.432457e-03, 'w_out': 1.805420e-02, 'attn_sinks': 1.837568e-03, 'rel_bias': 5.516451e-03, 'w_pool': 1.046130e-02, 'pool_scale': 1.028242e-02, 'w_ple': 2.571933e-02, 'w_gate_ple': 1.001403e-02, 'ln_gain': 8.010128e+00, 'ln_bias': 3.623596e-01}


def _to_microbatches(a, axis):
    t = _jnp.moveaxis(a, axis, 0)
    t = t.reshape((N_MICROBATCH, t.shape[0] // N_MICROBATCH) + t.shape[1:])
    return _jnp.moveaxis(t, 1, axis + 1)


def setup_inputs(seed: int = 0) -> dict:
    inp = _fwd_setup_inputs(seed)
    key = _jax.random.fold_in(_jax.random.key(seed), 7919)
    shape, _ = _output_shape()
    out = dict(inp)
    out["loss_target"] = _jax.random.normal(_jax.random.fold_in(key, 0), shape, _jnp.float32)
    for i, name in enumerate(TWIN_WEIGHTS):
        w = inp[name].astype(_jnp.float32)
        if MOMENT_SCALE is None:
            s = _jnp.sqrt(_jnp.mean(_jnp.square(w)) + 1e-30)
        else:
            s = MOMENT_SCALE[name]
        km, kv = _jax.random.split(_jax.random.fold_in(key, i + 1))
        out[name] = w
        out["m_" + name] = s * _jax.random.normal(km, w.shape, _jnp.float32)
        out["v_" + name] = (s * s) * _jax.random.uniform(kv, w.shape, _jnp.float32, 0.5, 1.5)
    if N_MICROBATCH > 1:
        for name, axis in PER_EXAMPLE_BATCH_AXIS.items():
            out[name] = _to_microbatches(out[name], axis)
    return {'x': out['x'], 'p': out['p'], 'w_in': out['w_in'], 'b_in': out['b_in'], 'w_out': out['w_out'], 'attn_sinks': out['attn_sinks'], 'rel_bias': out['rel_bias'], 'w_pool': out['w_pool'], 'pool_scale': out['pool_scale'], 'w_ple': out['w_ple'], 'w_gate_ple': out['w_gate_ple'], 'ln_gain': out['ln_gain'], 'ln_bias': out['ln_bias'], 'loss_target': out['loss_target'], 'm_w_in': out['m_w_in'], 'm_b_in': out['m_b_in'], 'm_w_out': out['m_w_out'], 'm_attn_sinks': out['m_attn_sinks'], 'm_rel_bias': out['m_rel_bias'], 'm_w_pool': out['m_w_pool'], 'm_pool_scale': out['m_pool_scale'], 'm_w_ple': out['m_w_ple'], 'm_w_gate_ple': out['m_w_gate_ple'], 'm_ln_gain': out['m_ln_gain'], 'm_ln_bias': out['m_ln_bias'], 'v_w_in': out['v_w_in'], 'v_b_in': out['v_b_in'], 'v_w_out': out['v_w_out'], 'v_attn_sinks': out['v_attn_sinks'], 'v_rel_bias': out['v_rel_bias'], 'v_w_pool': out['v_w_pool'], 'v_pool_scale': out['v_pool_scale'], 'v_w_ple': out['v_w_ple'], 'v_w_gate_ple': out['v_w_gate_ple'], 'v_ln_gain': out['v_ln_gain'], 'v_ln_bias': out['v_ln_bias']}


def _loss(weights, diff, rest, loss_target):
    with _jax.named_scope("forward"):
        args = {**rest, TWIN_DIFF_INPUT: diff, **{k: w.astype(_WEIGHT_DTYPES[k]) for k, w in weights.items()}}
        y = _forward(args)
    with _jax.named_scope("loss_head"):
        err = _jnp.square(y.astype(_jnp.float32) - loss_target)
        return 0.5 * _jnp.sum(_jnp.mean(err, axis=-1)) if err.ndim else 0.5 * err


def _adamw(w, g, m, v):
    m = ADAM_B1 * m + (1.0 - ADAM_B1) * g
    v = ADAM_B2 * v + (1.0 - ADAM_B2) * _jnp.square(g)
    m_hat = m / (1.0 - ADAM_B1 ** ADAM_STEP)
    v_hat = v / (1.0 - ADAM_B2 ** ADAM_STEP)
    delta = -ADAM_LR * (m_hat / (_jnp.sqrt(v_hat) + ADAM_EPS) + ADAM_WD * w)
    return delta, m, v


def reference(x, p, w_in, b_in, w_out, attn_sinks, rel_bias, w_pool, pool_scale, w_ple, w_gate_ple, ln_gain, ln_bias, loss_target, m_w_in, m_b_in, m_w_out, m_attn_sinks, m_rel_bias, m_w_pool, m_pool_scale, m_w_ple, m_w_gate_ple, m_ln_gain, m_ln_bias, v_w_in, v_b_in, v_w_out, v_attn_sinks, v_rel_bias, v_w_pool, v_pool_scale, v_w_ple, v_w_gate_ple, v_ln_gain, v_ln_bias):
    given = dict(x=x, p=p, w_in=w_in, b_in=b_in, w_out=w_out, attn_sinks=attn_sinks, rel_bias=rel_bias, w_pool=w_pool, pool_scale=pool_scale, w_ple=w_ple, w_gate_ple=w_gate_ple, ln_gain=ln_gain, ln_bias=ln_bias, loss_target=loss_target, m_w_in=m_w_in, m_b_in=m_b_in, m_w_out=m_w_out, m_attn_sinks=m_attn_sinks, m_rel_bias=m_rel_bias, m_w_pool=m_w_pool, m_pool_scale=m_pool_scale, m_w_ple=m_w_ple, m_w_gate_ple=m_w_gate_ple, m_ln_gain=m_ln_gain, m_ln_bias=m_ln_bias, v_w_in=v_w_in, v_b_in=v_b_in, v_w_out=v_w_out, v_attn_sinks=v_attn_sinks, v_rel_bias=v_rel_bias, v_w_pool=v_w_pool, v_pool_scale=v_pool_scale, v_w_ple=v_w_ple, v_w_gate_ple=v_w_gate_ple, v_ln_gain=v_ln_gain, v_ln_bias=v_ln_bias)
    weights = {n: given[n] for n in TWIN_WEIGHTS}
    shared = {n: given[n] for n in SHARED_INPUTS}
    per_example = {n: given[n] for n in ['x', 'p']}
    grad_fn = _jax.value_and_grad(_loss, argnums=(0, 1))

    def one_microbatch(ex, loss_target):
        ex = dict(ex)
        diff = ex.pop(TWIN_DIFF_INPUT)
        return grad_fn(weights, diff, {**shared, **ex}, loss_target)

    if N_MICROBATCH == 1:
        loss, (grad_w, grad_x) = one_microbatch(per_example, given["loss_target"])
    else:
        def body(carry, xs):
            loss_sum, grad_sum = carry
            l_k, (gw_k, gx_k) = one_microbatch(xs[0], xs[1])
            with _jax.named_scope("update"):
                return (loss_sum + l_k, _jax.tree.map(_jnp.add, grad_sum, gw_k)), gx_k

        init = (_jnp.zeros((), _jnp.float32), _jax.tree.map(_jnp.zeros_like, weights))
        (loss, grad_w), grad_x = _jax.lax.scan(body, init, (per_example, given["loss_target"]))
    with _jax.named_scope("update"):
        delta_w, new_m, new_v = {}, {}, {}
        for n in TWIN_WEIGHTS:
            delta_w[n], new_m[n], new_v[n] = _adamw(weights[n], grad_w[n], given["m_" + n], given["v_" + n])
    return (loss, grad_x, *[grad_w[n] for n in TWIN_WEIGHTS], *[delta_w[n] for n in TWIN_WEIGHTS],
            *[new_m[n] for n in TWIN_WEIGHTS], *[new_v[n] for n in TWIN_WEIGHTS])
```

```python
import functools
import math

import numpy as np
import jax
import jax.numpy as jnp
from jax import lax
from jax.experimental import pallas as pl
from jax.experimental.pallas import tpu as pltpu

F32 = jnp.float32
BF16 = jnp.bfloat16

D_MODEL = 2048
PLE_DIM = 256
ATTN_WIDTH = 1024
POOL_WIDTH = 1024
HEAD_DIM = 64
N_HEADS = 16
N_KV_HEADS = 2
KV_GROUP = 8
WINDOW = 128
BLOCK = 128
POOL_WINDOWS = (2, 4, 8, 16)
POOL_GROUP_DIM = 256
POOL_HALO = 16
REL_BUCKETS = 32
REL_MAX_DIST = 128
LN_EPS = 1e-5
KV_COLS = N_KV_HEADS * HEAD_DIM
IN_COLS = 4352
Q_OFF, KV_OFF, GA_OFF, U_OFF, GB_OFF = 0, 1024, 1280, 2304, 3328
ATTN_SCALE = 1.0 / math.sqrt(HEAD_DIM)
NEG_BIG = -1e30

ADAM_LR = 0.001
ADAM_B1 = 0.9
ADAM_B2 = 0.999
ADAM_EPS = 1e-08
ADAM_WD = 0.01
ADAM_STEP = 10

N_DEV = 8
MESH_ID = pl.DeviceIdType.MESH
VMEM_LIMIT_BYTES = 52 * 1024 * 1024
SMALL_COLS = 1024


def _params(sem=None):
    return pltpu.CompilerParams(dimension_semantics=sem, vmem_limit_bytes=VMEM_LIMIT_BYTES)


def _sigmoid(x):
    return 1.0 / (1.0 + jnp.exp(-x))


def _tile(n, pref, unit=16):
    if n <= pref:
        return n
    t = pref - pref % unit
    while n % t:
        t -= unit
    assert t > 0, (n, pref)
    return t


def _matmul(a, b, *, name, ta=False, tb=False, tm, tn, tk, out_dtype, bias=None, add=None, add_scale=1.0):
    M, K = (a.shape[1], a.shape[0]) if ta else a.shape
    N = b.shape[0] if tb else b.shape[1]
    assert (b.shape[1] if tb else b.shape[0]) == K
    tm, tn, tk = _tile(M, tm), _tile(N, tn), _tile(K, tk)
    nm, nn, nk = M // tm, N // tn, K // tk
    a_spec = pl.BlockSpec((tk, tm), lambda j, i, k: (k, i)) if ta else pl.BlockSpec((tm, tk), lambda j, i, k: (i, k))
    b_spec = pl.BlockSpec((tn, tk), lambda j, i, k: (j, k)) if tb else pl.BlockSpec((tk, tn), lambda j, i, k: (k, j))
    dims = (((0 if ta else 1,), (1 if tb else 0,)), ((), ()))
    operands, in_specs = [a, b], [a_spec, b_spec]
    if bias is not None:
        operands.append(bias)
        in_specs.append(pl.BlockSpec((1, tn), lambda j, i, k: (0, j)))
    if add is not None:
        operands.append(add)
        in_specs.append(pl.BlockSpec((tm, tn), lambda j, i, k: (i, j)))

    def body(*refs):
        a_ref, b_ref = refs[0], refs[1]
        pos = 2
        bias_ref = add_ref = None
        if bias is not None:
            bias_ref = refs[pos]
            pos += 1
        if add is not None:
            add_ref = refs[pos]
            pos += 1
        o_ref = refs[pos]
        part = lax.dot_general(a_ref[...].astype(BF16), b_ref[...].astype(BF16), dims, preferred_element_type=F32)

        def finish(acc):
            if bias_ref is not None:
                acc = acc + bias_ref[...]
            if add_ref is not None:
                acc = acc + add_scale * add_ref[...].astype(F32)
            o_ref[...] = acc.astype(out_dtype)

        if nk == 1:
            finish(part)
        else:
            acc_ref = refs[pos + 1]
            k = pl.program_id(2)

            @pl.when(k == 0)
            def _():
                acc_ref[...] = part

            @pl.when(k > 0)
            def _():
                acc_ref[...] += part

            @pl.when(k == nk - 1)
            def _():
                finish(acc_ref[...])

    return pl.pallas_call(
        body,
        name=name,
        grid=(nn, nm, nk),
        in_specs=in_specs,
        out_specs=pl.BlockSpec((tm, tn), lambda j, i, k: (i, j)),
        out_shape=jax.ShapeDtypeStruct((M, N), out_dtype),
        scratch_shapes=[pltpu.VMEM((tm, tn), F32)] if nk > 1 else [],
        compiler_params=_params(("parallel", "parallel", "arbitrary")),
    )(*operands)


def _band_constants():
    qq = np.arange(BLOCK)[:, None]
    kk = np.arange(2 * BLOCK)[None, :]
    dist = qq + BLOCK - kk
    in_window = (dist >= 0) & (dist < WINDOW)
    max_exact = REL_BUCKETS // 2
    d = np.maximum(dist, 0)
    d_f = np.maximum(d, 1).astype(np.float32)
    large = max_exact + (
        np.log(d_f / np.float32(max_exact)) / np.float32(math.log(REL_MAX_DIST / max_exact)) * np.float32(REL_BUCKETS - max_exact)
    ).astype(np.int32)
    large = np.minimum(large, REL_BUCKETS - 1)
    bucket = np.where(d < max_exact, d, large).astype(np.int32)
    bucket = np.where(in_window, bucket, 0).astype(np.int32)
    first = in_window & (kk >= BLOCK)
    masks = np.stack([np.tile(first, (KV_GROUP, 1)), np.tile(in_window, (KV_GROUP, 1))]).astype(np.float32)
    return bucket, masks, in_window.astype(np.float32)


def _bias_build(rel_bias, bucket):
    def body(rb_ref, bkt_ref, o_ref):
        h = pl.program_id(0)
        bkt = bkt_ref[...]

        def step(b, acc):
            return jnp.where(bkt == b, rb_ref[b, h], acc)

        o_ref[0] = lax.fori_loop(0, REL_BUCKETS, step, jnp.zeros((BLOCK, 2 * BLOCK), F32))

    return pl.pallas_call(
        body,
        name="bias_build",
        grid=(N_HEADS,),
        in_specs=[pl.BlockSpec(memory_space=pltpu.SMEM), pl.BlockSpec((BLOCK, 2 * BLOCK), lambda h: (0, 0))],
        out_specs=pl.BlockSpec((1, BLOCK, 2 * BLOCK), lambda h: (h, 0, 0)),
        out_shape=jax.ShapeDtypeStruct((N_HEADS, BLOCK, 2 * BLOCK), F32),
        compiler_params=_params(("arbitrary",)),
    )(rel_bias, bucket)


def _bias_bwd(dbias, bucket, window):
    def body(db_ref, bkt_ref, win_ref, o_ref):
        h = pl.program_id(0)

        @pl.when(h == 0)
        def _():
            o_ref[...] = jnp.zeros_like(o_ref)

        bkt = bkt_ref[...]
        x = jnp.where(win_ref[...] > 0.5, db_ref[0], 0.0)
        row = lax.broadcasted_iota(jnp.int32, (REL_BUCKETS, 128), 0)
        col = lax.broadcasted_iota(jnp.int32, (REL_BUCKETS, 128), 1)

        def step(b, acc):
            s = jnp.sum(jnp.where(bkt == b, x, 0.0), axis=1, keepdims=True)
            s = jnp.sum(s, axis=0, keepdims=True)
            return acc + jnp.where((row == b) & (col == h), s, 0.0)

        o_ref[...] += lax.fori_loop(0, REL_BUCKETS, step, jnp.zeros((REL_BUCKETS, 128), F32))

    return pl.pallas_call(
        body,
        name="bias_bwd",
        grid=(N_HEADS,),
        in_specs=[
            pl.BlockSpec((1, BLOCK, 2 * BLOCK), lambda h: (h, 0, 0)),
            pl.BlockSpec((BLOCK, 2 * BLOCK), lambda h: (0, 0)),
            pl.BlockSpec((BLOCK, 2 * BLOCK), lambda h: (0, 0)),
        ],
        out_specs=pl.BlockSpec((REL_BUCKETS, 128), lambda h: (0, 0)),
        out_shape=jax.ShapeDtypeStruct((REL_BUCKETS, 128), F32),
        compiler_params=_params(("arbitrary",)),
    )(dbias, bucket, window)


def _stack_heads(x, g):
    return jnp.concatenate(
        [x[:, HEAD_DIM * (KV_GROUP * g + j):HEAD_DIM * (KV_GROUP * g + j + 1)] for j in range(KV_GROUP)], axis=0)


def _unstack_heads(scr_ref, o, g):
    for j in range(KV_GROUP):
        c = HEAD_DIM * (KV_GROUP * g + j)
        scr_ref[:, c:c + HEAD_DIM] = o[BLOCK * j:BLOCK * (j + 1), :]


def _kv_pair(kvp, kvc, g):
    k2 = jnp.concatenate([kvp[:, HEAD_DIM * g:HEAD_DIM * (g + 1)], kvc[:, HEAD_DIM * g:HEAD_DIM * (g + 1)]], axis=0)
    v2 = jnp.concatenate([kvp[:, KV_COLS + HEAD_DIM * g:KV_COLS + HEAD_DIM * (g + 1)],
                          kvc[:, KV_COLS + HEAD_DIM * g:KV_COLS + HEAD_DIM * (g + 1)]], axis=0)
    return k2, v2


def _softmax_with_sink(qg, k2, bias, mask, sink):
    s = lax.dot_general(qg, k2, (((1,), (1,)), ((), ())), preferred_element_type=F32) * ATTN_SCALE + bias
    s = jnp.where(mask > 0.5, s, NEG_BIG)
    m = jnp.maximum(jnp.max(s, axis=-1, keepdims=True), sink)
    e = jnp.exp(s - m)
    e_sink = jnp.exp(sink - m)
    denom = jnp.sum(e, axis=-1, keepdims=True) + e_sink
    return e / denom, e_sink / denom


def _attn_specs(nb, order):
    return [
        pl.BlockSpec((BLOCK, ATTN_WIDTH), lambda t: (order(t), Q_OFF // ATTN_WIDTH)),
        pl.BlockSpec((BLOCK, 2 * KV_COLS), lambda t: (order(t), KV_OFF // (2 * KV_COLS))),
        pl.BlockSpec((BLOCK, 2 * KV_COLS), lambda t: (jnp.maximum(order(t) - 1, 0), KV_OFF // (2 * KV_COLS))),
    ] + [
        pl.BlockSpec((BLOCK, 256), functools.partial(lambda t, c: (order(t), GA_OFF // 256 + c), c=c)) for c in range(4)
    ] + [
        pl.BlockSpec((N_KV_HEADS, KV_GROUP * BLOCK, 2 * BLOCK), lambda t: (0, 0, 0)),
        pl.BlockSpec((None, KV_GROUP * BLOCK, 2 * BLOCK), lambda t: (jnp.minimum(order(t), 1), 0, 0)),
        pl.BlockSpec((N_KV_HEADS, KV_GROUP * BLOCK, 1), lambda t: (0, 0, 0)),
    ]


def _attn_fwd(h, bias, masks, sink_rows, name):
    S = h.shape[0]
    nb = S // BLOCK

    def body(q_ref, kvc_ref, kvp_ref, ga0, ga1, ga2, ga3, bias_ref, mask_ref, sink_ref, o_ref, attn_scr):
        qb = q_ref[...].astype(BF16)
        kvc = kvc_ref[...].astype(BF16)
        kvp = kvp_ref[...].astype(BF16)
        mask = mask_ref[...]
        for g in range(N_KV_HEADS):
            k2, v2 = _kv_pair(kvp, kvc, g)
            p, _ = _softmax_with_sink(_stack_heads(qb, g), k2, bias_ref[g], mask, sink_ref[g])
            o = jnp.dot(p.astype(BF16), v2, preferred_element_type=F32)
            _unstack_heads(attn_scr, o, g)
        for c, ga_ref in enumerate((ga0, ga1, ga2, ga3)):
            ga = ga_ref[...]
            o_ref[:, 256 * c:256 * (c + 1)] = (attn_scr[:, 256 * c:256 * (c + 1)] * (ga * _sigmoid(ga))).astype(BF16)

    return pl.pallas_call(
        body,
        name=name,
        grid=(nb,),
        in_specs=_attn_specs(nb, lambda t: t),
        out_specs=pl.BlockSpec((BLOCK, ATTN_WIDTH), lambda t: (t, 0)),
        out_shape=jax.ShapeDtypeStruct((S, ATTN_WIDTH), BF16),
        scratch_shapes=[pltpu.VMEM((BLOCK, ATTN_WIDTH), F32)],
        compiler_params=_params(("arbitrary",)),
    )(h, h, h, h, h, h, h, bias, masks, sink_rows)


def _attn_bwd(h, dab, bias, masks, sink_rows, dbias_in, name):
    S = h.shape[0]
    nb = S // BLOCK

    def order(t):
        return nb - 1 - t

    def body(q_ref, kvc_ref, kvp_ref, ga0, ga1, ga2, ga3, bias_ref, mask_ref, sink_ref, da_ref, dbin_ref,
             dq_ref, dkv_ref, dga_ref, dbias_ref, dsink_ref, attn_scr, dq_scr, dkv_scr, carry_scr):
        t = pl.program_id(0)

        @pl.when(t == 0)
        def _():
            dbias_ref[...] = dbin_ref[...]
            dsink_ref[...] = jnp.zeros_like(dsink_ref)
            carry_scr[...] = jnp.zeros_like(carry_scr)

        qb = q_ref[...].astype(BF16)
        kvc = kvc_ref[...].astype(BF16)
        kvp = kvp_ref[...].astype(BF16)
        mask = mask_ref[...]
        ga = jnp.concatenate([r[...] for r in (ga0, ga1, ga2, ga3)], axis=1)
        sg = _sigmoid(ga)
        da = da_ref[...]
        do_b = (da * (ga * sg)).astype(BF16)
        for g in range(N_KV_HEADS):
            k2, v2 = _kv_pair(kvp, kvc, g)
            qg = _stack_heads(qb, g)
            p, p_sink = _softmax_with_sink(qg, k2, bias_ref[g], mask, sink_ref[g])
            pb = p.astype(BF16)
            _unstack_heads(attn_scr, jnp.dot(pb, v2, preferred_element_type=F32), g)
            dog = _stack_heads(do_b, g)
            dp = lax.dot_general(dog, v2, (((1,), (1,)), ((), ())), preferred_element_type=F32)
            dsum = jnp.sum(p * dp, axis=-1, keepdims=True)
            ds = p * (dp - dsum)
            dbias_ref[g] += ds
            d_sink = -(p_sink * dsum)
            for j in range(KV_GROUP):
                tot = jnp.sum(d_sink[BLOCK * j:BLOCK * (j + 1), :], axis=0, keepdims=True)
                dsink_ref[g, j:j + 1, :] += jnp.broadcast_to(tot, (1, 128))
            dsb = ds.astype(BF16)
            _unstack_heads(dq_scr, jnp.dot(dsb, k2, preferred_element_type=F32) * ATTN_SCALE, g)
            dk2 = lax.dot_general(dsb, qg, (((0,), (0,)), ((), ())), preferred_element_type=F32) * ATTN_SCALE
            dv2 = lax.dot_general(pb, dog, (((0,), (0,)), ((), ())), preferred_element_type=F32)
            dkv_scr[:, HEAD_DIM * g:HEAD_DIM * (g + 1)] = dk2
            dkv_scr[:, KV_COLS + HEAD_DIM * g:KV_COLS + HEAD_DIM * (g + 1)] = dv2
        dq_ref[...] = dq_scr[...].astype(BF16)
        dga_ref[...] = (da * attn_scr[...] * (sg * (1.0 + ga * (1.0 - sg)))).astype(BF16)
        dkv_ref[...] = (dkv_scr[BLOCK:, :] + carry_scr[...]).astype(BF16)
        carry_scr[...] = dkv_scr[:BLOCK, :]

    return pl.pallas_call(
        body,
        name=name,
        grid=(nb,),
        in_specs=_attn_specs(nb, order) + [
            pl.BlockSpec((BLOCK, ATTN_WIDTH), lambda t: (order(t), 0)),
            pl.BlockSpec((N_KV_HEADS, KV_GROUP * BLOCK, 2 * BLOCK), lambda t: (0, 0, 0)),
        ],
        out_specs=[
            pl.BlockSpec((BLOCK, ATTN_WIDTH), lambda t: (order(t), 0)),
            pl.BlockSpec((BLOCK, 2 * KV_COLS), lambda t: (order(t), 0)),
            pl.BlockSpec((BLOCK, ATTN_WIDTH), lambda t: (order(t), 0)),
            pl.BlockSpec((N_KV_HEADS, KV_GROUP * BLOCK, 2 * BLOCK), lambda t: (0, 0, 0)),
            pl.BlockSpec((N_KV_HEADS, KV_GROUP, 128), lambda t: (0, 0, 0)),
        ],
        out_shape=[
            jax.ShapeDtypeStruct((S, ATTN_WIDTH), BF16),
            jax.ShapeDtypeStruct((S, 2 * KV_COLS), BF16),
            jax.ShapeDtypeStruct((S, ATTN_WIDTH), BF16),
            jax.ShapeDtypeStruct((N_KV_HEADS, KV_GROUP * BLOCK, 2 * BLOCK), F32),
            jax.ShapeDtypeStruct((N_KV_HEADS, KV_GROUP, 128), F32),
        ],
        scratch_shapes=[
            pltpu.VMEM((BLOCK, ATTN_WIDTH), F32),
            pltpu.VMEM((BLOCK, ATTN_WIDTH), F32),
            pltpu.VMEM((2 * BLOCK, 2 * KV_COLS), F32),
            pltpu.VMEM((BLOCK, 2 * KV_COLS), F32),
        ],
        compiler_params=_params(("arbitrary",)),
    )(h, h, h, h, h, h, h, bias, masks, sink_rows, dab, dbias_in)


def _window_sum(x, w, back):
    n = x.shape[0]
    s, sh = x, 1
    while sh < w:
        s = s + pltpu.roll(s, sh if back else n - sh, axis=0)
        sh *= 2
    return s


def _pool_counts(first_row, n, w):
    t = first_row + lax.broadcasted_iota(jnp.int32, (n, 1), 0)
    return jnp.minimum(t + 1, w).astype(F32)


def _pool_diff(u_ref, uh_ref, i, T, g):
    u = u_ref[...]
    halo = jnp.where(i > 0, uh_ref[...], 0.0)
    ext = jnp.concatenate([halo, u], axis=0)
    w = POOL_WINDOWS[g]
    s = _window_sum(ext, w, back=True)[POOL_HALO:, :]
    return s / _pool_counts(i * T, T, w) - u


def _pool_in_specs(T):
    hb = T // POOL_HALO
    specs = []
    for g in range(4):
        specs.append(pl.BlockSpec((T, 256), functools.partial(lambda i, g: (i, U_OFF // 256 + g), g=g)))
        specs.append(pl.BlockSpec((POOL_HALO, 256), functools.partial(
            lambda i, g: (jnp.maximum(i * hb - 1, 0), U_OFF // 256 + g), g=g)))
    return specs


def _pool_fwd(h, w_pool, pool_scale, name):
    S = h.shape[0]
    T = _tile(S, 512)

    def body(*refs):
        u_refs = refs[0:8]
        gb_refs = refs[8:12]
        wp_ref, sc_ref, o_ref = refs[12], refs[13], refs[14]
        i = pl.program_id(0)
        for g in range(4):
            diff = _pool_diff(u_refs[2 * g], u_refs[2 * g + 1], i, T, g).astype(BF16)
            mixed = jnp.dot(diff, wp_ref[g], preferred_element_type=F32) * sc_ref[:, 256 * g:256 * (g + 1)]
            gb = gb_refs[g][...]
            o_ref[:, 256 * g:256 * (g + 1)] = (mixed * (gb * _sigmoid(gb))).astype(BF16)

    in_specs = _pool_in_specs(T) + [
        pl.BlockSpec((T, 256), functools.partial(lambda i, g: (i, GB_OFF // 256 + g), g=g)) for g in range(4)
    ] + [
        pl.BlockSpec((4, 256, 256), lambda i: (0, 0, 0)),
        pl.BlockSpec((1, POOL_WIDTH), lambda i: (0, 0)),
    ]
    return pl.pallas_call(
        body,
        name=name,
        grid=(S // T,),
        in_specs=in_specs,
        out_specs=pl.BlockSpec((T, POOL_WIDTH), lambda i: (i, 0)),
        out_shape=jax.ShapeDtypeStruct((S, POOL_WIDTH), BF16),
        compiler_params=_params(("arbitrary",)),
    )(*([h] * 12), w_pool, pool_scale)


def _pool_bwd(h, dab, w_pool, pool_scale, name):
    S = h.shape[0]
    T = _tile(S, 512)
    nt = S // T
    hb = T // POOL_HALO
    E = T + POOL_HALO

    def body(*refs):
        u_refs = refs[0:8]
        gb_refs = refs[8:16]
        db_refs = refs[16:24]
        wp_ref, sc_ref = refs[24], refs[25]
        du_ref, dgb_ref, dwp_ref, dsc_ref = refs[26:30]
        i = pl.program_id(0)

        @pl.when(i == 0)
        def _():
            dwp_ref[...] = jnp.zeros_like(dwp_ref)
            dsc_ref[...] = jnp.zeros_like(dsc_ref)

        for g in range(4):
            w = POOL_WINDOWS[g]
            cols = slice(256 * g, 256 * (g + 1))
            scale = sc_ref[:, cols]
            wp = wp_ref[g]
            diff = _pool_diff(u_refs[2 * g], u_refs[2 * g + 1], i, T, g).astype(BF16)
            mixed = jnp.dot(diff, wp, preferred_element_type=F32)
            keep = i < nt - 1
            gb = jnp.concatenate([gb_refs[2 * g][...], jnp.where(keep, gb_refs[2 * g + 1][...], 0.0)], axis=0)
            db = jnp.concatenate([db_refs[2 * g][...], jnp.where(keep, db_refs[2 * g + 1][...], 0.0)], axis=0)
            sg = _sigmoid(gb)
            dms = db * (gb * sg)
            dmixed = (dms * scale).astype(BF16)
            ddiff = lax.dot_general(dmixed, wp, (((1,), (1,)), ((), ())), preferred_element_type=F32)
            r = ddiff / _pool_counts(i * T, E, w)
            du = _window_sum(r, w, back=False)[:T, :] - ddiff[:T, :]
            du_ref[:, cols] = du.astype(BF16)
            dgb_ref[:, cols] = (db[:T, :] * (mixed * scale) * (sg[:T, :] * (1.0 + gb[:T, :] * (1.0 - sg[:T, :])))).astype(BF16)
            dsc_ref[:, cols] += jnp.sum(dms[:T, :] * mixed, axis=0, keepdims=True)
            dwp_ref[g] += lax.dot_general(diff, dmixed[:T, :], (((0,), (0,)), ((), ())), preferred_element_type=F32)

    def after(i):
        return jnp.minimum((i + 1) * hb, S // POOL_HALO - 1)

    in_specs = _pool_in_specs(T)
    for off, arr_cols in ((GB_OFF // 256, None), (POOL_WIDTH // 256, None)):
        for g in range(4):
            in_specs.append(pl.BlockSpec((T, 256), functools.partial(lambda i, c: (i, c), c=off + g)))
            in_specs.append(pl.BlockSpec((POOL_HALO, 256), functools.partial(lambda i, c: (after(i), c), c=off + g)))
    in_specs += [pl.BlockSpec((4, 256, 256), lambda i: (0, 0, 0)), pl.BlockSpec((1, POOL_WIDTH), lambda i: (0, 0))]
    return pl.pallas_call(
        body,
        name=name,
        grid=(nt,),
        in_specs=in_specs,
        out_specs=[
            pl.BlockSpec((T, POOL_WIDTH), lambda i: (i, 0)),
            pl.BlockSpec((T, POOL_WIDTH), lambda i: (i, 0)),
            pl.BlockSpec((4, 256, 256), lambda i: (0, 0, 0)),
            pl.BlockSpec((1, POOL_WIDTH), lambda i: (0, 0)),
        ],
        out_shape=[
            jax.ShapeDtypeStruct((S, POOL_WIDTH), BF16),
            jax.ShapeDtypeStruct((S, POOL_WIDTH), BF16),
            jax.ShapeDtypeStruct((4, 256, 256), F32),
            jax.ShapeDtypeStruct((1, POOL_WIDTH), F32),
        ],
        compiler_params=_params(("arbitrary",)),
    )(*([h] * 16), *([dab] * 8), w_pool, pool_scale)


def _ln_fwd(x, mix, gp, pe, gain, bias, alpha, name):
    S = x.shape[0]
    T = _tile(S, 256)

    def body(x_ref, mix_ref, gp_ref, pe_ref, g_ref, b_ref, y_ref, yb_ref, xh_ref, rs_ref):
        z = alpha * x_ref[...] + mix_ref[...] + _sigmoid(gp_ref[...]) * pe_ref[...]
        mu = jnp.mean(z, axis=-1, keepdims=True)
        zc = z - mu
        var = jnp.mean(zc * zc, axis=-1, keepdims=True)
        rstd = lax.rsqrt(var + LN_EPS)
        xhat = zc * rstd
        y = xhat * g_ref[...] + b_ref[...]
        y_ref[...] = y
        yb_ref[...] = y.astype(BF16)
        xh_ref[...] = xhat
        rs_ref[...] = rstd

    row = pl.BlockSpec((T, D_MODEL), lambda i: (i, 0))
    vec = pl.BlockSpec((1, D_MODEL), lambda i: (0, 0))
    return pl.pallas_call(
        body,
        name=name,
        grid=(S // T,),
        in_specs=[row, row, row, row, vec, vec],
        out_specs=[row, row, row, pl.BlockSpec((T, 1), lambda i: (i, 0))],
        out_shape=[
            jax.ShapeDtypeStruct((S, D_MODEL), F32),
            jax.ShapeDtypeStruct((S, D_MODEL), BF16),
            jax.ShapeDtypeStruct((S, D_MODEL), F32),
            jax.ShapeDtypeStruct((S, 1), F32),
        ],
        compiler_params=_params(("parallel",)),
    )(x, mix, gp, pe, gain, bias)


def _ln_bwd(dy, xhat, rstd, gain, gp, pe, name):
    S = dy.shape[0]
    T = _tile(S, 256)

    def body(dy_ref, xh_ref, rs_ref, g_ref, gp_ref, pe_ref, dz_ref, dzb_ref, dpe_ref, dgp_ref, dgain_ref, dbias_ref):
        @pl.when(pl.program_id(0) == 0)
        def _():
            dgain_ref[...] = jnp.zeros_like(dgain_ref)
            dbias_ref[...] = jnp.zeros_like(dbias_ref)

        dy = dy_ref[...]
        xhat = xh_ref[...]
        dyg = dy * g_ref[...]
        c1 = jnp.mean(dyg, axis=-1, keepdims=True)
        c2 = jnp.mean(dyg * xhat, axis=-1, keepdims=True)
        dz = rs_ref[...] * (dyg - c1 - xhat * c2)
        dgain_ref[...] += jnp.sum(dy * xhat, axis=0, keepdims=True)
        dbias_ref[...] += jnp.sum(dy, axis=0, keepdims=True)
        sg = _sigmoid(gp_ref[...])
        dz_ref[...] = dz
        dzb_ref[...] = dz.astype(BF16)
        dpe_ref[...] = (dz * sg).astype(BF16)
        dgp_ref[...] = (dz * pe_ref[...] * (sg * (1.0 - sg))).astype(BF16)

    row = pl.BlockSpec((T, D_MODEL), lambda i: (i, 0))
    vec = pl.BlockSpec((1, D_MODEL), lambda i: (0, 0))
    return pl.pallas_call(
        body,
        name=name,
        grid=(S // T,),
        in_specs=[row, row, pl.BlockSpec((T, 1), lambda i: (i, 0)), vec, row, row],
        out_specs=[row, row, row, row, vec, vec],
        out_shape=[
            jax.ShapeDtypeStruct((S, D_MODEL), F32),
            jax.ShapeDtypeStruct((S, D_MODEL), BF16),
            jax.ShapeDtypeStruct((S, D_MODEL), BF16),
            jax.ShapeDtypeStruct((S, D_MODEL), BF16),
            jax.ShapeDtypeStruct((1, D_MODEL), F32),
            jax.ShapeDtypeStruct((1, D_MODEL), F32),
        ],
        compiler_params=_params(("arbitrary",)),
    )(dy, xhat, rstd, gain, gp, pe)


def _loss_head(y, target):
    S = y.shape[0]
    T = _tile(S, 256)

    def body(y_ref, t_ref, dy_ref, l_ref):
        @pl.when(pl.program_id(0) == 0)
        def _():
            l_ref[...] = jnp.zeros_like(l_ref)

        err = y_ref[...] - t_ref[...]
        dy_ref[...] = err * (1.0 / D_MODEL)
        per_token = jnp.mean(err * err, axis=-1, keepdims=True)
        l_ref[...] += 0.5 * jnp.sum(per_token, axis=0, keepdims=True)

    row = pl.BlockSpec((T, D_MODEL), lambda i: (i, 0))
    return pl.pallas_call(
        body,
        name="loss_head",
        grid=(S // T,),
        in_specs=[row, row],
        out_specs=[row, pl.BlockSpec((8, 128), lambda i: (0, 0))],
        out_shape=[jax.ShapeDtypeStruct((S, D_MODEL), F32), jax.ShapeDtypeStruct((8, 128), F32)],
        compiler_params=_params(("arbitrary",)),
    )(y, target)


def _col_sum(a, name):
    S, C = a.shape
    T = _tile(S, 512)

    def body(a_ref, o_ref):
        @pl.when(pl.program_id(0) == 0)
        def _():
            o_ref[...] = jnp.zeros_like(o_ref)

        o_ref[...] += jnp.sum(a_ref[...].astype(F32), axis=0, keepdims=True)

    return pl.pallas_call(
        body,
        name=name,
        grid=(S // T,),
        in_specs=[pl.BlockSpec((T, C), lambda i: (i, 0))],
        out_specs=pl.BlockSpec((1, C), lambda i: (0, 0)),
        out_shape=jax.ShapeDtypeStruct((1, C), F32),
        compiler_params=_params(("arbitrary",)),
    )(a)


def _sum_slabs(r, name):
    _, R, C = r.shape
    T = _tile(R, 256)

    def body(r_ref, o_ref):
        acc = r_ref[0].astype(F32)
        for s in range(1, N_DEV):
            acc = acc + r_ref[s].astype(F32)
        o_ref[...] = acc

    return pl.pallas_call(
        body,
        name=name,
        grid=(R // T,),
        in_specs=[pl.BlockSpec((N_DEV, T, C), lambda i: (0, i, 0))],
        out_specs=pl.BlockSpec((T, C), lambda i: (i, 0)),
        out_shape=jax.ShapeDtypeStruct((R, C), F32),
        compiler_params=_params(("parallel",)),
    )(r)


def _adamw(w, g, m, v, name):
    R, C = w.shape
    T = _tile(R, 256)

    def body(w_ref, g_ref, m_ref, v_ref, d_ref, nm_ref, nv_ref):
        g = g_ref[...]
        m = ADAM_B1 * m_ref[...] + (1.0 - ADAM_B1) * g
        v = ADAM_B2 * v_ref[...] + (1.0 - ADAM_B2) * jnp.square(g)
        m_hat = m / (1.0 - ADAM_B1 ** ADAM_STEP)
        v_hat = v / (1.0 - ADAM_B2 ** ADAM_STEP)
        d_ref[...] = -ADAM_LR * (m_hat / (jnp.sqrt(v_hat) + ADAM_EPS) + ADAM_WD * w_ref[...])
        nm_ref[...] = m
        nv_ref[...] = v

    blk = pl.BlockSpec((T, C), lambda i: (i, 0))
    shp = jax.ShapeDtypeStruct((R, C), F32)
    return pl.pallas_call(
        body,
        name=name,
        grid=(R // T,),
        in_specs=[blk] * 4,
        out_specs=[blk] * 3,
        out_shape=[shp] * 3,
        compiler_params=_params(("parallel",)),
    )(w, g, m, v)


def _mesh_pos():
    return lax.axis_index("x"), lax.axis_index("y"), lax.axis_index("c")


def _flip(pos, k):
    x, y, c = pos
    return (1 - x if k & 4 else x, 1 - y if k & 2 else y, 1 - c if k & 1 else c)


def _index(pos):
    return 4 * pos[0] + 2 * pos[1] + pos[2]


def _gather_weights(shards):
    n = len(shards)
    chips = (4, 2, 6)

    def body(*refs):
        ins, outs = refs[:n], refs[n:2 * n]
        send_sems, recv_sems, local_sems = refs[2 * n:]
        me = _mesh_pos()
        sibling = _flip(me, 1)

        def rows(a, pos):
            r = shards[a].shape[1]
            return outs[a].at[:, pl.ds(_index(pos) * r, r), :]

        def copy(a, k, block, to, src=None):
            return pltpu.make_async_remote_copy(
                src_ref=rows(a, block) if src is None else src, dst_ref=rows(a, block),
                send_sem=send_sems.at[7 * a + k], recv_sem=recv_sems.at[7 * a + k],
                device_id=to, device_id_type=MESH_ID)

        mine = [pltpu.make_async_copy(ins[a], rows(a, me), local_sems.at[a]) for a in range(n)]
        for cp in mine:
            cp.start()
        first = []
        for a in range(n):
            first.append(copy(a, 0, me, sibling, src=ins[a]))
            first += [copy(a, 1 + j, me, _flip(me, k), src=ins[a]) for j, k in enumerate(chips)]
        for cp in first:
            cp.start()
        passed = []
        for j, k in enumerate(chips):
            for a in range(n):
                copy(a, 1 + j, _flip(me, k), me).wait_recv()
                fwd = copy(a, 4 + j, _flip(me, k), sibling)
                fwd.start()
                passed.append(fwd)
        for a in range(n):
            copy(a, 0, sibling, me).wait_recv()
            for j, k in enumerate(chips):
                copy(a, 4 + j, _flip(sibling, k), me).wait_recv()
        for cp in first + passed:
            cp.wait_send()
        for cp in mine:
            cp.wait()

    any_spec = pl.BlockSpec(memory_space=pl.ANY)
    return pl.pallas_call(
        body,
        name="gather_weights",
        in_specs=[any_spec] * n,
        out_specs=[any_spec] * n,
        out_shape=[jax.ShapeDtypeStruct((s.shape[0], N_DEV * s.shape[1], s.shape[2]), s.dtype) for s in shards],
        scratch_shapes=[pltpu.SemaphoreType.DMA((7 * n,)), pltpu.SemaphoreType.DMA((7 * n,)), pltpu.SemaphoreType.DMA((n,))],
        compiler_params=pltpu.CompilerParams(has_side_effects=True),
    )(*shards)


def _scatter_grads(grads):
    n = len(grads)

    def body(*refs):
        ins, outs = refs[:n], refs[n:2 * n]
        send_sems, recv_sems, local_sems = refs[2 * n:]
        me = _mesh_pos()

        def block(a, pos):
            r = grads[a].shape[1] // N_DEV
            return ins[a].at[:, pl.ds(_index(pos) * r, r), :]

        def copy(a, k, slab):
            peer = _flip(me, k)
            return pltpu.make_async_remote_copy(
                src_ref=block(a, peer), dst_ref=outs[a].at[_index(slab)],
                send_sem=send_sems.at[7 * a + k - 1], recv_sem=recv_sems.at[7 * a + k - 1],
                device_id=peer, device_id_type=MESH_ID)

        mine = [pltpu.make_async_copy(block(a, me), outs[a].at[_index(me)], local_sems.at[a]) for a in range(n)]
        sends = [copy(a, k, me) for a in range(n) for k in range(1, N_DEV)]
        for cp in mine + sends:
            cp.start()
        for a in range(n):
            for k in range(1, N_DEV):
                copy(a, k, _flip(me, k)).wait_recv()
        for cp in sends:
            cp.wait_send()
        for cp in mine:
            cp.wait()

    any_spec = pl.BlockSpec(memory_space=pl.ANY)
    return pl.pallas_call(
        body,
        name="scatter_grads",
        in_specs=[any_spec] * n,
        out_specs=[any_spec] * n,
        out_shape=[jax.ShapeDtypeStruct((N_DEV, g.shape[0], g.shape[1] // N_DEV, g.shape[2]), g.dtype) for g in grads],
        scratch_shapes=[pltpu.SemaphoreType.DMA((7 * n,)), pltpu.SemaphoreType.DMA((7 * n,)), pltpu.SemaphoreType.DMA((n,))],
        compiler_params=pltpu.CompilerParams(has_side_effects=True),
    )(*grads)


def _allreduce_small(vec):
    R, C = vec.shape

    def body(v_ref, o_ref, buf, send_sems, recv_sems):
        me = _mesh_pos()
        buf[_index(me)] = v_ref[...]
        sends = []
        for k in range(1, N_DEV):
            sends.append(pltpu.make_async_remote_copy(
                src_ref=buf.at[_index(me)], dst_ref=buf.at[_index(me)],
                send_sem=send_sems.at[k - 1], recv_sem=recv_sems.at[k - 1],
                device_id=_flip(me, k), device_id_type=MESH_ID))
        for cp in sends:
            cp.start()
        for cp in sends:
            cp.wait_recv()
        for cp in sends:
            cp.wait_send()
        acc = buf[0]
        for s in range(1, N_DEV):
            acc = acc + buf[s]
        o_ref[...] = acc

    return pl.pallas_call(
        body,
        name="allreduce_small",
        in_specs=[pl.BlockSpec(memory_space=pltpu.VMEM)],
        out_specs=pl.BlockSpec(memory_space=pltpu.VMEM),
        out_shape=jax.ShapeDtypeStruct((R, C), F32),
        scratch_shapes=[pltpu.VMEM((N_DEV, R, C), F32), pltpu.SemaphoreType.DMA((7,)), pltpu.SemaphoreType.DMA((7,))],
        compiler_params=pltpu.CompilerParams(has_side_effects=True, vmem_limit_bytes=VMEM_LIMIT_BYTES),
    )(vec)


def _pack_small(parts):
    flat = jnp.concatenate([p.reshape(-1) for p in parts])
    n = flat.shape[0]
    rows = -(-n // SMALL_COLS)
    rows = -(-rows // 8) * 8
    return jnp.pad(flat, (0, rows * SMALL_COLS - n)).reshape(rows, SMALL_COLS)


def _unpack_small(packed, like):
    flat = packed.reshape(-1)
    out, pos = [], 0
    for p in like:
        out.append(flat[pos:pos + p.size].reshape(p.shape))
        pos += p.size
    return out


def kernel(x, p, w_in, b_in, w_out, attn_sinks, rel_bias, w_pool, pool_scale, w_ple, w_gate_ple, ln_gain, ln_bias, loss_target, m_w_in, m_b_in, m_w_out, m_attn_sinks, m_rel_bias, m_w_pool, m_pool_scale, m_w_ple, m_w_gate_ple, m_ln_gain, m_ln_bias, v_w_in, v_b_in, v_w_out, v_attn_sinks, v_rel_bias, v_w_pool, v_pool_scale, v_w_ple, v_w_gate_ple, v_ln_gain, v_ln_bias):
    L = w_in.shape[0]
    S = x.shape[1]
    alpha = (2.0 * L) ** 0.25
    bucket_np, masks_np, window_np = _band_constants()
    bucket, masks, window = jnp.asarray(bucket_np), jnp.asarray(masks_np), jnp.asarray(window_np)

    w_in_t, w_out_g, w_gate_g, w_ple_t, w_pool_g = _gather_weights([
        jnp.swapaxes(w_in, 1, 2).astype(BF16),
        w_out.astype(BF16),
        w_gate_ple.astype(BF16),
        jnp.swapaxes(w_ple, 1, 2).astype(BF16),
        w_pool.astype(BF16).reshape(L * 4, 32, 256),
    ])
    w_pool_g = w_pool_g.reshape(L, 4, 256, 256)

    bias = _bias_build(rel_bias, bucket).reshape(N_KV_HEADS, KV_GROUP * BLOCK, 2 * BLOCK)

    xs = x[0]
    xb = xs.astype(BF16)
    saved = []
    for l in range(L):
        pb = p[l, 0].astype(BF16)
        sink_rows = jnp.repeat(attn_sinks[l], BLOCK).reshape(N_KV_HEADS, KV_GROUP * BLOCK, 1)
        scale_l = pool_scale[l].reshape(1, POOL_WIDTH)
        h = _matmul(xb, w_in_t[l], tb=True, tm=512, tn=2176, tk=2048, out_dtype=F32,
                    bias=b_in[l].reshape(1, IN_COLS), name=f"in_proj_{l}")
        gp = _matmul(xb, w_gate_g[l], tm=512, tn=2048, tk=2048, out_dtype=F32, name=f"gate_proj_{l}")
        pe = _matmul(pb, w_ple_t[l], tb=True, tm=1024, tn=2048, tk=256, out_dtype=F32, name=f"ple_proj_{l}")
        a = _attn_fwd(h, bias, masks, sink_rows, name=f"attn_fwd_{l}")
        b = _pool_fwd(h, w_pool_g[l], scale_l, name=f"pool_fwd_{l}")
        ab = jnp.concatenate([a, b], axis=1)
        mix = _matmul(ab, w_out_g[l], tm=512, tn=2048, tk=2048, out_dtype=F32, name=f"out_proj_{l}")
        y, yb, xhat, rstd = _ln_fwd(xs, mix, gp, pe, ln_gain[l].reshape(1, D_MODEL), ln_bias[l].reshape(1, D_MODEL),
                                    alpha, name=f"ln_fwd_{l}")
        saved.append((xb, pb, h, gp, pe, ab, xhat, rstd, sink_rows, scale_l))
        xs, xb = y, yb

    dy, loss_tile = _loss_head(xs, loss_target[0])

    dbias = jnp.zeros((N_KV_HEADS, KV_GROUP * BLOCK, 2 * BLOCK), F32)
    g_w_in_t, g_w_out, g_w_gate, g_w_ple_t, g_w_pool = [], [], [], [], []
    g_b_in, g_sinks, g_scale, g_gain, g_beta = [], [], [], [], []
    for l in reversed(range(L)):
        xb, pb, h, gp, pe, ab, xhat, rstd, sink_rows, scale_l = saved[l]
        dz, dzb, dpe, dgp, dgain, dbeta = _ln_bwd(dy, xhat, rstd, ln_gain[l].reshape(1, D_MODEL), gp, pe, name=f"ln_bwd_{l}")
        dab = _matmul(dzb, w_out_g[l], tb=True, tm=512, tn=2048, tk=2048, out_dtype=F32, name=f"dmix_{l}")
        g_w_out.append(_matmul(ab, dzb, ta=True, tm=1024, tn=2048, tk=1024, out_dtype=BF16, name=f"dw_out_{l}"))
        g_w_gate.append(_matmul(xb, dgp, ta=True, tm=1024, tn=2048, tk=1024, out_dtype=BF16, name=f"dw_gate_{l}"))
        g_w_ple_t.append(_matmul(dpe, pb, ta=True, tm=1024, tn=256, tk=1024, out_dtype=BF16, name=f"dw_ple_{l}"))
        dq, dkv, dga, dbias, dsink = _attn_bwd(h, dab, bias, masks, sink_rows, dbias, name=f"attn_bwd_{l}")
        du, dgb, dwp, dsc = _pool_bwd(h, dab, w_pool_g[l], scale_l, name=f"pool_bwd_{l}")
        dh = jnp.concatenate([dq, dkv, dga, du, dgb], axis=1)
        g_w_in_t.append(_matmul(dh, xb, ta=True, tm=2176, tn=1024, tk=1024, out_dtype=BF16, name=f"dw_in_{l}"))
        g_b_in.append(_col_sum(dh, name=f"db_in_{l}"))
        dx = _matmul(dgp, w_gate_g[l], tb=True, tm=512, tn=2048, tk=2048, out_dtype=F32, add=dz, add_scale=alpha,
                     name=f"dx_gate_{l}")
        dy = _matmul(dh, w_in_t[l], tm=512, tn=1024, tk=4352, out_dtype=F32, add=dx, name=f"dx_in_{l}")
        g_w_pool.append(dwp.astype(BF16))
        g_sinks.append(dsink[:, :, 0].reshape(N_HEADS))
        g_scale.append(dsc.reshape(POOL_WIDTH))
        g_gain.append(dgain.reshape(D_MODEL))
        g_beta.append(dbeta.reshape(D_MODEL))
    grad_x = dy[None]
    for lst in (g_w_in_t, g_w_out, g_w_gate, g_w_ple_t, g_w_pool, g_b_in, g_sinks, g_scale, g_gain, g_beta):
        lst.reverse()
    g_rel = _bias_bwd(dbias.reshape(N_HEADS, BLOCK, 2 * BLOCK), bucket, window)[:, :N_HEADS]

    r_in, r_out, r_gate, r_ple, r_pool = _scatter_grads([
        jnp.stack(g_w_in_t), jnp.stack(g_w_out), jnp.stack(g_w_gate), jnp.stack(g_w_ple_t),
        jnp.stack(g_w_pool).reshape(L * 4, 256, 256)])
    small_like = [b_in, attn_sinks, rel_bias, pool_scale, ln_gain, ln_bias]
    small_g = _allreduce_small(_pack_small([
        jnp.stack(g_b_in).reshape(L, IN_COLS), jnp.stack(g_sinks), g_rel, jnp.stack(g_scale), jnp.stack(g_gain),
        jnp.stack(g_beta), loss_tile[0, :1]]))

    gt_in = _sum_slabs(r_in.reshape(N_DEV, L * 544, D_MODEL), name="sum_w_in")
    grad_w_in = jnp.swapaxes(gt_in.reshape(L, 544, D_MODEL), 1, 2)
    grad_w_out = _sum_slabs(r_out.reshape(N_DEV, L * 256, D_MODEL), name="sum_w_out").reshape(L, 256, D_MODEL)
    grad_w_gate = _sum_slabs(r_gate.reshape(N_DEV, L * 256, D_MODEL), name="sum_w_gate").reshape(L, 256, D_MODEL)
    gt_ple = _sum_slabs(r_ple.reshape(N_DEV, L * 256, PLE_DIM), name="sum_w_ple")
    grad_w_ple = jnp.swapaxes(gt_ple.reshape(L, 256, PLE_DIM), 1, 2)
    grad_w_pool = _sum_slabs(r_pool.reshape(N_DEV, L * 4 * 32, 256), name="sum_w_pool").reshape(L, 4, 32, 256)

    def big(w, g, m, v, name):
        shape = w.shape
        two_d = (shape[0] * shape[1], shape[2]) if len(shape) == 3 else (shape[0] * shape[1] * shape[2], shape[3])
        d, nm, nv = _adamw(w.reshape(two_d), g.reshape(two_d), m.reshape(two_d), v.reshape(two_d), name=name)
        return d.reshape(shape), nm.reshape(shape), nv.reshape(shape)

    upd_in = big(w_in, grad_w_in, m_w_in, v_w_in, "adamw_w_in")
    upd_out = big(w_out, grad_w_out, m_w_out, v_w_out, "adamw_w_out")
    upd_pool = big(w_pool, grad_w_pool, m_w_pool, v_w_pool, "adamw_w_pool")
    upd_ple = big(w_ple, grad_w_ple, m_w_ple, v_w_ple, "adamw_w_ple")
    upd_gate = big(w_gate_ple, grad_w_gate, m_w_gate_ple, v_w_gate_ple, "adamw_w_gate")

    zero1 = jnp.zeros((1,), F32)
    sw = _pack_small(small_like + [zero1])
    sm = _pack_small([m_b_in, m_attn_sinks, m_rel_bias, m_pool_scale, m_ln_gain, m_ln_bias, zero1])
    sv = _pack_small([v_b_in, v_attn_sinks, v_rel_bias, v_pool_scale, v_ln_gain, v_ln_bias, zero1])
    sd, snm, snv = _adamw(sw, small_g, sm, sv, name="adamw_small")
    like = small_like + [zero1]
    sg_parts = _unpack_small(small_g, like)
    sd_parts, snm_parts, snv_parts = _unpack_small(sd, like), _unpack_small(snm, like), _unpack_small(snv, like)
    loss = sg_parts[6][0]

    def assemble(big_parts, small_parts):
        w_in_, w_out_, w_pool_, w_ple_, w_gate_ = big_parts
        b_in_, sinks_, rel_, scale_, gain_, beta_ = small_parts[:6]
        return [w_in_, b_in_, w_out_, sinks_, rel_, w_pool_, scale_, w_ple_, w_gate_, gain_, beta_]

    grads = assemble([grad_w_in, grad_w_out, grad_w_pool, grad_w_ple, grad_w_gate], sg_parts)
    ups = [upd_in, upd_out, upd_pool, upd_ple, upd_gate]
    deltas = assemble([u[0] for u in ups], sd_parts)
    new_m = assemble([u[1] for u in ups], snm_parts)
    new_v = assemble([u[2] for u in ups], snv_parts)
    return (loss, grad_x, *grads, *deltas, *new_m, *new_v)
```

```python
import functools
import math

import numpy as np
import jax
import jax.numpy as jnp
from jax import lax
from jax.experimental import pallas as pl
from jax.experimental.pallas import tpu as pltpu

F32 = jnp.float32
BF16 = jnp.bfloat16

D_MODEL = 2048
PLE_DIM = 256
ATTN_WIDTH = 1024
POOL_WIDTH = 1024
HEAD_DIM = 64
N_HEADS = 16
N_KV_HEADS = 2
KV_GROUP = 8
WINDOW = 128
BLOCK = 128
POOL_WINDOWS = (2, 4, 8, 16)
POOL_GROUP_DIM = 256
POOL_HALO = 16
REL_BUCKETS = 32
REL_MAX_DIST = 128
LN_EPS = 1e-5
KV_COLS = N_KV_HEADS * HEAD_DIM
IN_COLS = 4352
Q_OFF, KV_OFF, GA_OFF, U_OFF, GB_OFF = 0, 1024, 1280, 2304, 3328
ATTN_SCALE = 1.0 / math.sqrt(HEAD_DIM)
NEG_BIG = -1e30
LANES = 128

ADAM_LR = 0.001
ADAM_B1 = 0.9
ADAM_B2 = 0.999
ADAM_EPS = 1e-08
ADAM_WD = 0.01
ADAM_STEP = 10

N_DEV = 8
MESH_ID = pl.DeviceIdType.MESH
VMEM_LIMIT_BYTES = 52 * 1024 * 1024
SMALL_COLS = 1024


def _params(sem=None):
    return pltpu.CompilerParams(dimension_semantics=sem, vmem_limit_bytes=VMEM_LIMIT_BYTES)


def _sigmoid(x):
    return 1.0 / (1.0 + jnp.exp(-x))


def _tile(n, pref, unit=16):
    if n <= pref:
        return n
    t = pref - pref % unit
    while n % t:
        t -= unit
    assert t > 0, (n, pref)
    return t


def _matmul(a, b, *, name, ta=False, tb=False, tm, tn, tk, out_dtype, b_layer=None, bias=None, add=None, add_scale=1.0):
    M, K = (a.shape[1], a.shape[0]) if ta else a.shape
    b_shape = b.shape if b_layer is None else b.shape[1:]
    N = b_shape[0] if tb else b_shape[1]
    assert (b_shape[1] if tb else b_shape[0]) == K
    tm, tn, tk = _tile(M, tm), _tile(N, tn), _tile(K, tk)
    nm, nn, nk = M // tm, N // tn, K // tk
    a_spec = pl.BlockSpec((tk, tm), lambda j, i, k: (k, i)) if ta else pl.BlockSpec((tm, tk), lambda j, i, k: (i, k))
    b_block = (tn, tk) if tb else (tk, tn)
    b_index = (lambda j, i, k: (j, k)) if tb else (lambda j, i, k: (k, j))
    if b_layer is None:
        b_spec = pl.BlockSpec(b_block, b_index)
    else:
        b_spec = pl.BlockSpec((None,) + b_block, lambda j, i, k: (b_layer,) + b_index(j, i, k))
    dims = (((0 if ta else 1,), (1 if tb else 0,)), ((), ()))
    operands, in_specs = [a, b], [a_spec, b_spec]
    if bias is not None:
        operands.append(bias)
        in_specs.append(pl.BlockSpec((1, tn), lambda j, i, k: (0, j)))
    if add is not None:
        operands.append(add)
        in_specs.append(pl.BlockSpec((tm, tn), lambda j, i, k: (i, j)))

    def body(*refs):
        a_ref, b_ref = refs[0], refs[1]
        pos = 2
        bias_ref = add_ref = None
        if bias is not None:
            bias_ref = refs[pos]
            pos += 1
        if add is not None:
            add_ref = refs[pos]
            pos += 1
        o_ref = refs[pos]
        part = lax.dot_general(a_ref[...].astype(BF16), b_ref[...].astype(BF16), dims, preferred_element_type=F32)

        def finish(acc):
            if bias_ref is not None:
                acc = acc + bias_ref[...]
            if add_ref is not None:
                acc = acc + add_scale * add_ref[...].astype(F32)
            o_ref[...] = acc.astype(out_dtype)

        if nk == 1:
            finish(part)
        else:
            acc_ref = refs[pos + 1]
            k = pl.program_id(2)

            @pl.when(k == 0)
            def _():
                acc_ref[...] = part

            @pl.when(k > 0)
            def _():
                acc_ref[...] += part

            @pl.when(k == nk - 1)
            def _():
                finish(acc_ref[...])

    return pl.pallas_call(
        body,
        name=name,
        grid=(nn, nm, nk),
        in_specs=in_specs,
        out_specs=pl.BlockSpec((tm, tn), lambda j, i, k: (i, j)),
        out_shape=jax.ShapeDtypeStruct((M, N), out_dtype),
        scratch_shapes=[pltpu.VMEM((tm, tn), F32)] if nk > 1 else [],
        compiler_params=_params(("parallel", "parallel", "arbitrary")),
    )(*operands)


def _band_constants():
    qq = np.arange(BLOCK)[:, None]
    kk = np.arange(2 * BLOCK)[None, :]
    dist = qq + BLOCK - kk
    in_window = (dist >= 0) & (dist < WINDOW)
    max_exact = REL_BUCKETS // 2
    d = np.maximum(dist, 0)
    d_f = np.maximum(d, 1).astype(np.float32)
    large = max_exact + (
        np.log(d_f / np.float32(max_exact)) / np.float32(math.log(REL_MAX_DIST / max_exact)) * np.float32(REL_BUCKETS - max_exact)
    ).astype(np.int32)
    large = np.minimum(large, REL_BUCKETS - 1)
    bucket = np.where(d < max_exact, d, large).astype(np.int32)
    bucket = np.where(in_window, bucket, 0).astype(np.int32)
    first = in_window & (kk >= BLOCK)
    masks = np.stack([first, in_window]).astype(np.float32)
    return bucket, masks, in_window.astype(np.float32)


def _bias_build(rel_bias, bucket):
    def body(rb_ref, bkt_ref, o_ref):
        h = pl.program_id(0)
        bkt = bkt_ref[...]

        def step(b, acc):
            return jnp.where(bkt == b, rb_ref[b, h], acc)

        o_ref[0] = lax.fori_loop(0, REL_BUCKETS, step, jnp.zeros((BLOCK, 2 * BLOCK), F32))

    return pl.pallas_call(
        body,
        name="bias_build",
        grid=(N_HEADS,),
        in_specs=[pl.BlockSpec(memory_space=pltpu.SMEM), pl.BlockSpec((BLOCK, 2 * BLOCK), lambda h: (0, 0))],
        out_specs=pl.BlockSpec((1, BLOCK, 2 * BLOCK), lambda h: (h, 0, 0)),
        out_shape=jax.ShapeDtypeStruct((N_HEADS, BLOCK, 2 * BLOCK), F32),
        compiler_params=_params(("arbitrary",)),
    )(rel_bias, bucket)


def _bias_bwd(dbias, bucket, window):
    def body(db_ref, bkt_ref, win_ref, o_ref):
        h = pl.program_id(0)

        @pl.when(h == 0)
        def _():
            o_ref[...] = jnp.zeros_like(o_ref)

        bkt = bkt_ref[...]
        x = jnp.where(win_ref[...] > 0.5, db_ref[0], 0.0)
        row = lax.broadcasted_iota(jnp.int32, (REL_BUCKETS, LANES), 0)
        col = lax.broadcasted_iota(jnp.int32, (REL_BUCKETS, LANES), 1)

        def step(b, acc):
            s = jnp.sum(jnp.where(bkt == b, x, 0.0), axis=1, keepdims=True)
            s = jnp.sum(s, axis=0, keepdims=True)
            return acc + jnp.where((row == b) & (col == h), s, 0.0)

        o_ref[...] += lax.fori_loop(0, REL_BUCKETS, step, jnp.zeros((REL_BUCKETS, LANES), F32))

    return pl.pallas_call(
        body,
        name="bias_bwd",
        grid=(N_HEADS,),
        in_specs=[
            pl.BlockSpec((1, BLOCK, 2 * BLOCK), lambda h: (h, 0, 0)),
            pl.BlockSpec((BLOCK, 2 * BLOCK), lambda h: (0, 0)),
            pl.BlockSpec((BLOCK, 2 * BLOCK), lambda h: (0, 0)),
        ],
        out_specs=pl.BlockSpec((REL_BUCKETS, LANES), lambda h: (0, 0)),
        out_shape=jax.ShapeDtypeStruct((REL_BUCKETS, LANES), F32),
        compiler_params=_params(("arbitrary",)),
    )(dbias, bucket, window)


def _lane_lo(shape):
    return lax.broadcasted_iota(jnp.int32, shape, 1) < HEAD_DIM


def _dup_heads(x):
    r = pltpu.roll(x, HEAD_DIM, axis=1)
    lo = _lane_lo(x.shape)
    return jnp.where(lo, x, r), jnp.where(lo, r, x)


def _kv_operands(kvp_ref, kvc_ref):
    kvp, kvc = kvp_ref[...], kvc_ref[...]
    k2 = jnp.concatenate([kvp[:, :KV_COLS], kvc[:, :KV_COLS]], axis=0)
    v2 = jnp.concatenate([kvp[:, KV_COLS:], kvc[:, KV_COLS:]], axis=0)
    return _dup_heads(k2), _dup_heads(v2)


def _head_probs(qs, k_t, bias, mask, sink):
    s = jnp.dot(qs, k_t, preferred_element_type=F32) * ATTN_SCALE + bias
    s = jnp.where(mask, s, NEG_BIG)
    m = jnp.maximum(jnp.max(s, axis=-1, keepdims=True), sink)
    e = jnp.exp(s - m)
    e_sink = jnp.exp(sink - m)
    inv = 1.0 / (jnp.sum(e, axis=-1, keepdims=True) + e_sink)
    return e * inv, e_sink * inv


def _gate_cols(ga_refs, pair):
    off = LANES * (pair % 2)
    return ga_refs[pair // 2][:, off:off + LANES]


def _attn_specs(order):
    return [
        pl.BlockSpec((BLOCK, ATTN_WIDTH), lambda t: (order(t), Q_OFF // ATTN_WIDTH)),
        pl.BlockSpec((BLOCK, 2 * KV_COLS), lambda t: (order(t), KV_OFF // (2 * KV_COLS))),
        pl.BlockSpec((BLOCK, 2 * KV_COLS), lambda t: (jnp.maximum(order(t) - 1, 0), KV_OFF // (2 * KV_COLS))),
    ] + [
        pl.BlockSpec((BLOCK, 256), functools.partial(lambda t, c: (order(t), GA_OFF // 256 + c), c=c)) for c in range(4)
    ] + [
        pl.BlockSpec((N_KV_HEADS, KV_GROUP * BLOCK, 2 * BLOCK), lambda t: (0, 0, 0)),
        pl.BlockSpec((None, BLOCK, 2 * BLOCK), lambda t: (jnp.minimum(order(t), 1), 0, 0)),
        pl.BlockSpec((N_KV_HEADS, KV_GROUP * BLOCK, 1), lambda t: (0, 0, 0)),
    ]


def _attn_fwd(h, bias, masks, sink_rows, name):
    S = h.shape[0]
    nb = S // BLOCK

    def body(q_ref, kvc_ref, kvp_ref, ga0, ga1, ga2, ga3, bias_ref, mask_ref, sink_ref, o_ref):
        kd, vd = _kv_operands(kvp_ref, kvc_ref)
        mask = mask_ref[...] > 0.5
        lo = _lane_lo((BLOCK, LANES))
        for g in range(N_KV_HEADS):
            k_t = kd[g].T.astype(BF16)
            v_r = vd[g].astype(BF16)
            for pr in range(KV_GROUP // 2):
                pair = (KV_GROUP // 2) * g + pr
                qp = q_ref[:, LANES * pair:LANES * (pair + 1)]
                outs = []
                for hh in range(2):
                    rows = slice(BLOCK * (2 * pr + hh), BLOCK * (2 * pr + hh + 1))
                    qs = jnp.where(lo if hh == 0 else ~lo, qp, 0.0).astype(BF16)
                    p, _ = _head_probs(qs, k_t, bias_ref[g, rows, :], mask, sink_ref[g, rows, :])
                    outs.append(jnp.dot(p.astype(BF16), v_r, preferred_element_type=F32))
                ga = _gate_cols((ga0, ga1, ga2, ga3), pair)
                o_ref[:, LANES * pair:LANES * (pair + 1)] = (
                    jnp.where(lo, outs[0], outs[1]) * (ga * _sigmoid(ga))).astype(BF16)

    return pl.pallas_call(
        body,
        name=name,
        grid=(nb,),
        in_specs=_attn_specs(lambda t: t),
        out_specs=pl.BlockSpec((BLOCK, ATTN_WIDTH), lambda t: (t, 0)),
        out_shape=jax.ShapeDtypeStruct((S, ATTN_WIDTH + POOL_WIDTH), BF16),
        compiler_params=_params(("arbitrary",)),
    )(h, h, h, h, h, h, h, bias, masks, sink_rows)


DH_ATTN_COLS = U_OFF


def _attn_bwd(h, dab, dh, bias, masks, sink_rows, dbias_in, name):
    S = h.shape[0]
    nb = S // BLOCK

    def order(t):
        return nb - 1 - t

    def body(q_ref, kvc_ref, kvp_ref, ga0, ga1, ga2, ga3, bias_ref, mask_ref, sink_ref, da_ref, dbin_ref, dh_in_ref,
             dh_ref, dbias_ref, dsink_ref, carry_scr):
        del dh_in_ref
        t = pl.program_id(0)

        @pl.when(t == 0)
        def _():
            dbias_ref[...] = dbin_ref[...]
            dsink_ref[...] = jnp.zeros_like(dsink_ref)
            carry_scr[...] = jnp.zeros_like(carry_scr)

        kd, vd = _kv_operands(kvp_ref, kvc_ref)
        mask = mask_ref[...] > 0.5
        lo = _lane_lo((BLOCK, LANES))
        dk_tot, dv_tot = [], []
        for g in range(N_KV_HEADS):
            k_t, k_r = kd[g].T.astype(BF16), kd[g].astype(BF16)
            v_t, v_r = vd[g].T.astype(BF16), vd[g].astype(BF16)
            dk_t = jnp.zeros((LANES, 2 * BLOCK), F32)
            dv_t = jnp.zeros((LANES, 2 * BLOCK), F32)
            for pr in range(KV_GROUP // 2):
                pair = (KV_GROUP // 2) * g + pr
                cols = slice(LANES * pair, LANES * (pair + 1))
                qp = q_ref[:, cols]
                ga = _gate_cols((ga0, ga1, ga2, ga3), pair)
                sg = _sigmoid(ga)
                da = da_ref[:, cols]
                do_p = da * (ga * sg)
                outs, dqs = [], []
                for hh in range(2):
                    j = 2 * pr + hh
                    rows = slice(BLOCK * j, BLOCK * (j + 1))
                    half = lo if hh == 0 else ~lo
                    qs = jnp.where(half, qp, 0.0).astype(BF16)
                    p, p_sink = _head_probs(qs, k_t, bias_ref[g, rows, :], mask, sink_ref[g, rows, :])
                    pb = p.astype(BF16)
                    outs.append(jnp.dot(pb, v_r, preferred_element_type=F32))
                    dos = jnp.where(half, do_p, 0.0).astype(BF16)
                    dp = jnp.dot(dos, v_t, preferred_element_type=F32)
                    dsum = jnp.sum(p * dp, axis=-1, keepdims=True)
                    ds = p * (dp - dsum)
                    dbias_ref[g, rows, :] += ds
                    tot = jnp.sum(-(p_sink * dsum), axis=0, keepdims=True)
                    dsink_ref[g, j:j + 1, :] += jnp.broadcast_to(tot, (1, LANES))
                    dsb = ds.astype(BF16)
                    dqs.append(jnp.dot(dsb, k_r, preferred_element_type=F32))
                    dk_t = dk_t + lax.dot_general(qs, dsb, (((0,), (0,)), ((), ())), preferred_element_type=F32)
                    dv_t = dv_t + lax.dot_general(dos, pb, (((0,), (0,)), ((), ())), preferred_element_type=F32)
                attn = jnp.where(lo, outs[0], outs[1])
                dh_ref[:, cols] = (jnp.where(lo, dqs[0], dqs[1]) * ATTN_SCALE).astype(BF16)
                dh_ref[:, GA_OFF + LANES * pair:GA_OFF + LANES * (pair + 1)] = (
                    da * attn * (sg * (1.0 + ga * (1.0 - sg)))).astype(BF16)
            dk = (dk_t * ATTN_SCALE).T
            dv = dv_t.T
            dk_tot.append(dk + pltpu.roll(dk, HEAD_DIM, axis=1))
            dv_tot.append(dv + pltpu.roll(dv, HEAD_DIM, axis=1))
        lo2 = _lane_lo((2 * BLOCK, LANES))
        dkv = jnp.concatenate([jnp.where(lo2, dk_tot[0], dk_tot[1]), jnp.where(lo2, dv_tot[0], dv_tot[1])], axis=1)
        dh_ref[:, KV_OFF:KV_OFF + 2 * KV_COLS] = (dkv[BLOCK:, :] + carry_scr[...]).astype(BF16)
        carry_scr[...] = dkv[:BLOCK, :]

    n_in = 12
    return pl.pallas_call(
        body,
        name=name,
        grid=(nb,),
        in_specs=_attn_specs(order) + [
            pl.BlockSpec((BLOCK, ATTN_WIDTH), lambda t: (order(t), 0)),
            pl.BlockSpec((N_KV_HEADS, KV_GROUP * BLOCK, 2 * BLOCK), lambda t: (0, 0, 0)),
            pl.BlockSpec(memory_space=pl.ANY),
        ],
        out_specs=[
            pl.BlockSpec((BLOCK, DH_ATTN_COLS), lambda t: (order(t), 0)),
            pl.BlockSpec((N_KV_HEADS, KV_GROUP * BLOCK, 2 * BLOCK), lambda t: (0, 0, 0)),
            pl.BlockSpec((N_KV_HEADS, KV_GROUP, LANES), lambda t: (0, 0, 0)),
        ],
        out_shape=[
            jax.ShapeDtypeStruct((S, IN_COLS), BF16),
            jax.ShapeDtypeStruct((N_KV_HEADS, KV_GROUP * BLOCK, 2 * BLOCK), F32),
            jax.ShapeDtypeStruct((N_KV_HEADS, KV_GROUP, LANES), F32),
        ],
        scratch_shapes=[pltpu.VMEM((BLOCK, 2 * KV_COLS), F32)],
        input_output_aliases={n_in: 0},
        compiler_params=_params(("arbitrary",)),
    )(h, h, h, h, h, h, h, bias, masks, sink_rows, dab, dbias_in, dh)


def _window_sum(x, w, back):
    n = x.shape[0]
    s, sh = x, 1
    while sh < w:
        s = s + pltpu.roll(s, sh if back else n - sh, axis=0)
        sh *= 2
    return s


def _pool_counts(first_row, n, w):
    t = first_row + lax.broadcasted_iota(jnp.int32, (n, 1), 0)
    return jnp.minimum(t + 1, w).astype(F32)


def _pool_diff(u_ref, uh_ref, i, T, g):
    u = u_ref[...]
    halo = jnp.where(i > 0, uh_ref[...], 0.0)
    ext = jnp.concatenate([halo, u], axis=0)
    w = POOL_WINDOWS[g]
    s = _window_sum(ext, w, back=True)[POOL_HALO:, :]
    return s / _pool_counts(i * T, T, w) - u


def _pool_in_specs(T):
    hb = T // POOL_HALO
    specs = []
    for g in range(4):
        specs.append(pl.BlockSpec((T, 256), functools.partial(lambda i, g: (i, U_OFF // 256 + g), g=g)))
        specs.append(pl.BlockSpec((POOL_HALO, 256), functools.partial(
            lambda i, g: (jnp.maximum(i * hb - 1, 0), U_OFF // 256 + g), g=g)))
    return specs


def _pool_weight_specs(layer):
    return [pl.BlockSpec((None, 4, 256, 256), lambda i: (layer, 0, 0, 0)), pl.BlockSpec((1, POOL_WIDTH), lambda i: (0, 0))]


def _pool_fwd(h, ab, w_pool, layer, pool_scale, name):
    S = h.shape[0]
    T = _tile(S, 512)

    def body(*refs):
        u_refs = refs[0:8]
        gb_refs = refs[8:12]
        wp_ref, sc_ref, o_ref = refs[12], refs[13], refs[15]
        i = pl.program_id(0)
        for g in range(4):
            diff = _pool_diff(u_refs[2 * g], u_refs[2 * g + 1], i, T, g).astype(BF16)
            mixed = jnp.dot(diff, wp_ref[g], preferred_element_type=F32) * sc_ref[:, 256 * g:256 * (g + 1)]
            gb = gb_refs[g][...]
            o_ref[:, 256 * g:256 * (g + 1)] = (mixed * (gb * _sigmoid(gb))).astype(BF16)

    in_specs = _pool_in_specs(T) + [
        pl.BlockSpec((T, 256), functools.partial(lambda i, g: (i, GB_OFF // 256 + g), g=g)) for g in range(4)
    ] + _pool_weight_specs(layer) + [pl.BlockSpec(memory_space=pl.ANY)]
    return pl.pallas_call(
        body,
        name=name,
        grid=(S // T,),
        in_specs=in_specs,
        out_specs=pl.BlockSpec((T, POOL_WIDTH), lambda i: (i, 1)),
        out_shape=jax.ShapeDtypeStruct(ab.shape, BF16),
        input_output_aliases={14: 0},
        compiler_params=_params(("arbitrary",)),
    )(*([h] * 12), w_pool, pool_scale, ab)


DH_POOL_COLS = IN_COLS // 2


def _pool_bwd(h, dab, w_pool, layer, pool_scale, name):
    S = h.shape[0]
    T = _tile(S, 512)
    nt = S // T
    hb = T // POOL_HALO
    E = T + POOL_HALO
    lead = U_OFF - DH_POOL_COLS

    def body(*refs):
        u_refs = refs[0:8]
        gb_refs = refs[8:16]
        db_refs = refs[16:24]
        wp_ref, sc_ref = refs[24], refs[25]
        dh_ref, dwp_ref, dsc_ref = refs[26:29]
        i = pl.program_id(0)

        @pl.when(i == 0)
        def _():
            dwp_ref[...] = jnp.zeros_like(dwp_ref)
            dsc_ref[...] = jnp.zeros_like(dsc_ref)

        dh_ref[:, 0:lead] = jnp.zeros((T, lead), BF16)
        for g in range(4):
            w = POOL_WINDOWS[g]
            cols = slice(256 * g, 256 * (g + 1))
            scale = sc_ref[:, cols]
            wp = wp_ref[g]
            diff = _pool_diff(u_refs[2 * g], u_refs[2 * g + 1], i, T, g).astype(BF16)
            mixed = jnp.dot(diff, wp, preferred_element_type=F32)
            keep = i < nt - 1
            gb = jnp.concatenate([gb_refs[2 * g][...], jnp.where(keep, gb_refs[2 * g + 1][...], 0.0)], axis=0)
            db = jnp.concatenate([db_refs[2 * g][...], jnp.where(keep, db_refs[2 * g + 1][...], 0.0)], axis=0)
            sg = _sigmoid(gb)
            dms = db * (gb * sg)
            dmixed = (dms * scale).astype(BF16)
            ddiff = lax.dot_general(dmixed, wp, (((1,), (1,)), ((), ())), preferred_element_type=F32)
            r = ddiff / _pool_counts(i * T, E, w)
            du = _window_sum(r, w, back=False)[:T, :] - ddiff[:T, :]
            dh_ref[:, lead + 256 * g:lead + 256 * (g + 1)] = du.astype(BF16)
            dh_ref[:, lead + POOL_WIDTH + 256 * g:lead + POOL_WIDTH + 256 * (g + 1)] = (
                db[:T, :] * (mixed * scale) * (sg[:T, :] * (1.0 + gb[:T, :] * (1.0 - sg[:T, :])))).astype(BF16)
            dsc_ref[:, cols] += jnp.sum(dms[:T, :] * mixed, axis=0, keepdims=True)
            dwp_ref[g] += lax.dot_general(diff, dmixed[:T, :], (((0,), (0,)), ((), ())), preferred_element_type=F32)

    def after(i):
        return jnp.minimum((i + 1) * hb, S // POOL_HALO - 1)

    in_specs = _pool_in_specs(T)
    for off in (GB_OFF // 256, ATTN_WIDTH // 256):
        for g in range(4):
            in_specs.append(pl.BlockSpec((T, 256), functools.partial(lambda i, c: (i, c), c=off + g)))
            in_specs.append(pl.BlockSpec((POOL_HALO, 256), functools.partial(lambda i, c: (after(i), c), c=off + g)))
    in_specs += _pool_weight_specs(layer)
    return pl.pallas_call(
        body,
        name=name,
        grid=(nt,),
        in_specs=in_specs,
        out_specs=[
            pl.BlockSpec((T, DH_POOL_COLS), lambda i: (i, 1)),
            pl.BlockSpec((4, 256, 256), lambda i: (0, 0, 0)),
            pl.BlockSpec((1, POOL_WIDTH), lambda i: (0, 0)),
        ],
        out_shape=[
            jax.ShapeDtypeStruct((S, IN_COLS), BF16),
            jax.ShapeDtypeStruct((4, 256, 256), F32),
            jax.ShapeDtypeStruct((1, POOL_WIDTH), F32),
        ],
        compiler_params=_params(("arbitrary",)),
    )(*([h] * 16), *([dab] * 8), w_pool, pool_scale)


def _ln_fwd(x, mix, gp, pe, gain, bias, alpha, name):
    S = x.shape[0]
    T = _tile(S, 256)

    def body(x_ref, mix_ref, gp_ref, pe_ref, g_ref, b_ref, y_ref, yb_ref, xh_ref, rs_ref):
        z = alpha * x_ref[...] + mix_ref[...] + _sigmoid(gp_ref[...]) * pe_ref[...]
        mu = jnp.mean(z, axis=-1, keepdims=True)
        zc = z - mu
        var = jnp.mean(zc * zc, axis=-1, keepdims=True)
        rstd = lax.rsqrt(var + LN_EPS)
        xhat = zc * rstd
        y = xhat * g_ref[...] + b_ref[...]
        y_ref[...] = y
        yb_ref[...] = y.astype(BF16)
        xh_ref[...] = xhat
        rs_ref[...] = rstd

    row = pl.BlockSpec((T, D_MODEL), lambda i: (i, 0))
    vec = pl.BlockSpec((1, D_MODEL), lambda i: (0, 0))
    return pl.pallas_call(
        body,
        name=name,
        grid=(S // T,),
        in_specs=[row, row, row, row, vec, vec],
        out_specs=[row, row, row, pl.BlockSpec((T, 1), lambda i: (i, 0))],
        out_shape=[
            jax.ShapeDtypeStruct((S, D_MODEL), F32),
            jax.ShapeDtypeStruct((S, D_MODEL), BF16),
            jax.ShapeDtypeStruct((S, D_MODEL), F32),
            jax.ShapeDtypeStruct((S, 1), F32),
        ],
        compiler_params=_params(("parallel",)),
    )(x, mix, gp, pe, gain, bias)


def _ln_bwd(dy, xhat, rstd, gain, gp, pe, name):
    S = dy.shape[0]
    T = _tile(S, 256)

    def body(dy_ref, xh_ref, rs_ref, g_ref, gp_ref, pe_ref, dz_ref, dzb_ref, dpe_ref, dgp_ref, dgain_ref, dbias_ref):
        @pl.when(pl.program_id(0) == 0)
        def _():
            dgain_ref[...] = jnp.zeros_like(dgain_ref)
            dbias_ref[...] = jnp.zeros_like(dbias_ref)

        dy = dy_ref[...]
        xhat = xh_ref[...]
        dyg = dy * g_ref[...]
        c1 = jnp.mean(dyg, axis=-1, keepdims=True)
        c2 = jnp.mean(dyg * xhat, axis=-1, keepdims=True)
        dz = rs_ref[...] * (dyg - c1 - xhat * c2)
        dgain_ref[...] += jnp.sum(dy * xhat, axis=0, keepdims=True)
        dbias_ref[...] += jnp.sum(dy, axis=0, keepdims=True)
        sg = _sigmoid(gp_ref[...])
        dz_ref[...] = dz
        dzb_ref[...] = dz.astype(BF16)
        dpe_ref[...] = (dz * sg).astype(BF16)
        dgp_ref[...] = (dz * pe_ref[...] * (sg * (1.0 - sg))).astype(BF16)

    row = pl.BlockSpec((T, D_MODEL), lambda i: (i, 0))
    vec = pl.BlockSpec((1, D_MODEL), lambda i: (0, 0))
    return pl.pallas_call(
        body,
        name=name,
        grid=(S // T,),
        in_specs=[row, row, pl.BlockSpec((T, 1), lambda i: (i, 0)), vec, row, row],
        out_specs=[row, row, row, row, vec, vec],
        out_shape=[
            jax.ShapeDtypeStruct((S, D_MODEL), F32),
            jax.ShapeDtypeStruct((S, D_MODEL), BF16),
            jax.ShapeDtypeStruct((S, D_MODEL), BF16),
            jax.ShapeDtypeStruct((S, D_MODEL), BF16),
            jax.ShapeDtypeStruct((1, D_MODEL), F32),
            jax.ShapeDtypeStruct((1, D_MODEL), F32),
        ],
        compiler_params=_params(("arbitrary",)),
    )(dy, xhat, rstd, gain, gp, pe)


def _loss_head(y, target):
    S = y.shape[0]
    T = _tile(S, 256)

    def body(y_ref, t_ref, dy_ref, l_ref):
        @pl.when(pl.program_id(0) == 0)
        def _():
            l_ref[...] = jnp.zeros_like(l_ref)

        err = y_ref[...] - t_ref[...]
        dy_ref[...] = err * (1.0 / D_MODEL)
        per_token = jnp.mean(err * err, axis=-1, keepdims=True)
        l_ref[...] += 0.5 * jnp.sum(per_token, axis=0, keepdims=True)

    row = pl.BlockSpec((T, D_MODEL), lambda i: (i, 0))
    return pl.pallas_call(
        body,
        name="loss_head",
        grid=(S // T,),
        in_specs=[row, row],
        out_specs=[row, pl.BlockSpec((8, LANES), lambda i: (0, 0))],
        out_shape=[jax.ShapeDtypeStruct((S, D_MODEL), F32), jax.ShapeDtypeStruct((8, LANES), F32)],
        compiler_params=_params(("arbitrary",)),
    )(y, target)


def _col_sum(a, name):
    S, C = a.shape
    T = _tile(S, 512)

    def body(a_ref, o_ref):
        @pl.when(pl.program_id(0) == 0)
        def _():
            o_ref[...] = jnp.zeros_like(o_ref)

        o_ref[...] += jnp.sum(a_ref[...].astype(F32), axis=0, keepdims=True)

    return pl.pallas_call(
        body,
        name=name,
        grid=(S // T,),
        in_specs=[pl.BlockSpec((T, C), lambda i: (i, 0))],
        out_specs=pl.BlockSpec((1, C), lambda i: (0, 0)),
        out_shape=jax.ShapeDtypeStruct((1, C), F32),
        compiler_params=_params(("arbitrary",)),
    )(a)


def _sum_slabs(r, name):
    _, R, C = r.shape
    T = _tile(R, 256)

    def body(r_ref, o_ref):
        acc = r_ref[0].astype(F32)
        for s in range(1, N_DEV):
            acc = acc + r_ref[s].astype(F32)
        o_ref[...] = acc

    return pl.pallas_call(
        body,
        name=name,
        grid=(R // T,),
        in_specs=[pl.BlockSpec((N_DEV, T, C), lambda i: (0, i, 0))],
        out_specs=pl.BlockSpec((T, C), lambda i: (i, 0)),
        out_shape=jax.ShapeDtypeStruct((R, C), F32),
        compiler_params=_params(("parallel",)),
    )(r)


def _adamw(w, g, m, v, name):
    R, C = w.shape
    T = _tile(R, 256)

    def body(w_ref, g_ref, m_ref, v_ref, d_ref, nm_ref, nv_ref):
        g = g_ref[...]
        m = ADAM_B1 * m_ref[...] + (1.0 - ADAM_B1) * g
        v = ADAM_B2 * v_ref[...] + (1.0 - ADAM_B2) * jnp.square(g)
        m_hat = m / (1.0 - ADAM_B1 ** ADAM_STEP)
        v_hat = v / (1.0 - ADAM_B2 ** ADAM_STEP)
        d_ref[...] = -ADAM_LR * (m_hat / (jnp.sqrt(v_hat) + ADAM_EPS) + ADAM_WD * w_ref[...])
        nm_ref[...] = m
        nv_ref[...] = v

    blk = pl.BlockSpec((T, C), lambda i: (i, 0))
    shp = jax.ShapeDtypeStruct((R, C), F32)
    return pl.pallas_call(
        body,
        name=name,
        grid=(R // T,),
        in_specs=[blk] * 4,
        out_specs=[blk] * 3,
        out_shape=[shp] * 3,
        compiler_params=_params(("parallel",)),
    )(w, g, m, v)


def _mesh_pos():
    return lax.axis_index("x"), lax.axis_index("y"), lax.axis_index("c")


def _flip(pos, k):
    x, y, c = pos
    return (1 - x if k & 4 else x, 1 - y if k & 2 else y, 1 - c if k & 1 else c)


def _index(pos):
    return 4 * pos[0] + 2 * pos[1] + pos[2]


def _gather_weights(shards):
    n = len(shards)
    chips = (4, 2, 6)

    def body(*refs):
        ins, outs = refs[:n], refs[n:2 * n]
        send_sems, recv_sems, local_sems = refs[2 * n:]
        me = _mesh_pos()
        sibling = _flip(me, 1)

        def rows(a, pos):
            r = shards[a].shape[1]
            return outs[a].at[:, pl.ds(_index(pos) * r, r), :]

        def copy(a, k, block, to, src=None):
            return pltpu.make_async_remote_copy(
                src_ref=rows(a, block) if src is None else src, dst_ref=rows(a, block),
                send_sem=send_sems.at[7 * a + k], recv_sem=recv_sems.at[7 * a + k],
                device_id=to, device_id_type=MESH_ID)

        mine = [pltpu.make_async_copy(ins[a], rows(a, me), local_sems.at[a]) for a in range(n)]
        for cp in mine:
            cp.start()
        first = []
        for a in range(n):
            first.append(copy(a, 0, me, sibling, src=ins[a]))
            first += [copy(a, 1 + j, me, _flip(me, k), src=ins[a]) for j, k in enumerate(chips)]
        for cp in first:
            cp.start()
        passed = []
        for j, k in enumerate(chips):
            for a in range(n):
                copy(a, 1 + j, _flip(me, k), me).wait_recv()
                fwd = copy(a, 4 + j, _flip(me, k), sibling)
                fwd.start()
                passed.append(fwd)
        for a in range(n):
            copy(a, 0, sibling, me).wait_recv()
            for j, k in enumerate(chips):
                copy(a, 4 + j, _flip(sibling, k), me).wait_recv()
        for cp in first + passed:
            cp.wait_send()
        for cp in mine:
            cp.wait()

    any_spec = pl.BlockSpec(memory_space=pl.ANY)
    return pl.pallas_call(
        body,
        name="gather_weights",
        in_specs=[any_spec] * n,
        out_specs=[any_spec] * n,
        out_shape=[jax.ShapeDtypeStruct((s.shape[0], N_DEV * s.shape[1], s.shape[2]), s.dtype) for s in shards],
        scratch_shapes=[pltpu.SemaphoreType.DMA((7 * n,)), pltpu.SemaphoreType.DMA((7 * n,)), pltpu.SemaphoreType.DMA((n,))],
        compiler_params=pltpu.CompilerParams(has_side_effects=True),
    )(*shards)


def _scatter_grads(grads, name):
    n = len(grads)
    L = len(grads[0])

    def body(*refs):
        ins = [refs[a * L:(a + 1) * L] for a in range(n)]
        outs = refs[n * L:n * L + n]
        send_sems, recv_sems, local_sems = refs[n * L + n:]
        me = _mesh_pos()

        def block(a, l, pos):
            r = grads[a][l].shape[0] // N_DEV
            return ins[a][l].at[pl.ds(_index(pos) * r, r), :]

        def copy(a, l, k):
            peer = _flip(me, k)
            return pltpu.make_async_remote_copy(
                src_ref=block(a, l, peer), dst_ref=outs[a].at[_index(me), l],
                send_sem=send_sems.at[7 * a + k - 1], recv_sem=recv_sems.at[7 * a + k - 1],
                device_id=peer, device_id_type=MESH_ID)

        def all_layers(a, k, slab):
            whole = outs[a].at[_index(slab)]
            return pltpu.make_async_remote_copy(
                src_ref=whole, dst_ref=whole, send_sem=send_sems.at[7 * a + k - 1], recv_sem=recv_sems.at[7 * a + k - 1],
                device_id=_flip(me, k), device_id_type=MESH_ID)

        mine = [pltpu.make_async_copy(block(a, l, me), outs[a].at[_index(me), l], local_sems.at[a * L + l])
                for a in range(n) for l in range(L)]
        for cp in mine:
            cp.start()
        for a in range(n):
            for k in range(1, N_DEV):
                for l in range(L):
                    copy(a, l, k).start()
        for a in range(n):
            for k in range(1, N_DEV):
                all_layers(a, k, _flip(me, k)).wait_recv()
        for a in range(n):
            for k in range(1, N_DEV):
                all_layers(a, k, me).wait_send()
        for cp in mine:
            cp.wait()

    any_spec = pl.BlockSpec(memory_space=pl.ANY)
    flat = [g for per_weight in grads for g in per_weight]
    return pl.pallas_call(
        body,
        name=name,
        in_specs=[any_spec] * (n * L),
        out_specs=[any_spec] * n,
        out_shape=[jax.ShapeDtypeStruct((N_DEV, L, g[0].shape[0] // N_DEV, g[0].shape[1]), g[0].dtype) for g in grads],
        scratch_shapes=[pltpu.SemaphoreType.DMA((7 * n,)), pltpu.SemaphoreType.DMA((7 * n,)),
                        pltpu.SemaphoreType.DMA((n * L,))],
        compiler_params=pltpu.CompilerParams(has_side_effects=True),
    )(*flat)


def _allreduce_small(vec):
    R, C = vec.shape

    def body(v_ref, o_ref, buf, send_sems, recv_sems):
        me = _mesh_pos()
        buf[_index(me)] = v_ref[...]
        sends = []
        for k in range(1, N_DEV):
            sends.append(pltpu.make_async_remote_copy(
                src_ref=buf.at[_index(me)], dst_ref=buf.at[_index(me)],
                send_sem=send_sems.at[k - 1], recv_sem=recv_sems.at[k - 1],
                device_id=_flip(me, k), device_id_type=MESH_ID))
        for cp in sends:
            cp.start()
        for cp in sends:
            cp.wait_recv()
        for cp in sends:
            cp.wait_send()
        acc = buf[0]
        for s in range(1, N_DEV):
            acc = acc + buf[s]
        o_ref[...] = acc

    return pl.pallas_call(
        body,
        name="allreduce_small",
        in_specs=[pl.BlockSpec(memory_space=pltpu.VMEM)],
        out_specs=pl.BlockSpec(memory_space=pltpu.VMEM),
        out_shape=jax.ShapeDtypeStruct((R, C), F32),
        scratch_shapes=[pltpu.VMEM((N_DEV, R, C), F32), pltpu.SemaphoreType.DMA((7,)), pltpu.SemaphoreType.DMA((7,))],
        compiler_params=pltpu.CompilerParams(has_side_effects=True, vmem_limit_bytes=VMEM_LIMIT_BYTES),
    )(vec)


def _pack_small(parts):
    flat = jnp.concatenate([p.reshape(-1) for p in parts])
    n = flat.shape[0]
    rows = -(-n // SMALL_COLS)
    rows = -(-rows // 8) * 8
    return jnp.pad(flat, (0, rows * SMALL_COLS - n)).reshape(rows, SMALL_COLS)


def _unpack_small(packed, like):
    flat = packed.reshape(-1)
    out, pos = [], 0
    for p in like:
        out.append(flat[pos:pos + p.size].reshape(p.shape))
        pos += p.size
    return out


def kernel(x, p, w_in, b_in, w_out, attn_sinks, rel_bias, w_pool, pool_scale, w_ple, w_gate_ple, ln_gain, ln_bias, loss_target, m_w_in, m_b_in, m_w_out, m_attn_sinks, m_rel_bias, m_w_pool, m_pool_scale, m_w_ple, m_w_gate_ple, m_ln_gain, m_ln_bias, v_w_in, v_b_in, v_w_out, v_attn_sinks, v_rel_bias, v_w_pool, v_pool_scale, v_w_ple, v_w_gate_ple, v_ln_gain, v_ln_bias):
    L = w_in.shape[0]
    S = x.shape[1]
    alpha = (2.0 * L) ** 0.25
    bucket_np, masks_np, window_np = _band_constants()
    bucket, masks, window = jnp.asarray(bucket_np), jnp.asarray(masks_np), jnp.asarray(window_np)

    w_in_t, w_out_g, w_gate_g, w_ple_t, w_pool_g = _gather_weights([
        jnp.swapaxes(w_in, 1, 2).astype(BF16),
        w_out.astype(BF16),
        w_gate_ple.astype(BF16),
        jnp.swapaxes(w_ple, 1, 2).astype(BF16),
        w_pool.astype(BF16).reshape(L * 4, 32, 256),
    ])
    w_pool_g = w_pool_g.reshape(L, 4, 256, 256)

    bias = _bias_build(rel_bias, bucket).reshape(N_KV_HEADS, KV_GROUP * BLOCK, 2 * BLOCK)

    xs = x[0]
    xb = xs.astype(BF16)
    saved = []
    for l in range(L):
        pb = p[l, 0].astype(BF16)
        sink_rows = jnp.repeat(attn_sinks[l], BLOCK).reshape(N_KV_HEADS, KV_GROUP * BLOCK, 1)
        scale_l = pool_scale[l].reshape(1, POOL_WIDTH)
        h = _matmul(xb, w_in_t, b_layer=l, tb=True, tm=512, tn=2176, tk=2048, out_dtype=F32,
                    bias=b_in[l].reshape(1, IN_COLS), name=f"in_proj_{l}")
        gp = _matmul(xb, w_gate_g, b_layer=l, tm=512, tn=2048, tk=2048, out_dtype=F32, name=f"gate_proj_{l}")
        pe = _matmul(pb, w_ple_t, b_layer=l, tb=True, tm=1024, tn=2048, tk=256, out_dtype=F32, name=f"ple_proj_{l}")
        ab = _attn_fwd(h, bias, masks, sink_rows, name=f"attn_fwd_{l}")
        ab = _pool_fwd(h, ab, w_pool_g, l, scale_l, name=f"pool_fwd_{l}")
        mix = _matmul(ab, w_out_g, b_layer=l, tm=512, tn=2048, tk=2048, out_dtype=F32, name=f"out_proj_{l}")
        y, yb, xhat, rstd = _ln_fwd(xs, mix, gp, pe, ln_gain[l].reshape(1, D_MODEL), ln_bias[l].reshape(1, D_MODEL),
                                    alpha, name=f"ln_fwd_{l}")
        saved.append((xb, pb, h, gp, pe, ab, xhat, rstd, sink_rows, scale_l))
        xs, xb = y, yb

    dy, loss_tile = _loss_head(xs, loss_target[0])

    dbias = jnp.zeros((N_KV_HEADS, KV_GROUP * BLOCK, 2 * BLOCK), F32)
    g_w_in_t, g_w_out, g_w_gate, g_w_ple_t, g_w_pool = [], [], [], [], []
    g_b_in, g_sinks, g_scale, g_gain, g_beta = [], [], [], [], []
    for l in reversed(range(L)):
        xb, pb, h, gp, pe, ab, xhat, rstd, sink_rows, scale_l = saved[l]
        dz, dzb, dpe, dgp, dgain, dbeta = _ln_bwd(dy, xhat, rstd, ln_gain[l].reshape(1, D_MODEL), gp, pe, name=f"ln_bwd_{l}")
        dab = _matmul(dzb, w_out_g, b_layer=l, tb=True, tm=512, tn=2048, tk=2048, out_dtype=F32, name=f"dmix_{l}")
        g_w_out.append(_matmul(ab, dzb, ta=True, tm=1024, tn=2048, tk=1024, out_dtype=BF16, name=f"dw_out_{l}"))
        g_w_gate.append(_matmul(xb, dgp, ta=True, tm=1024, tn=2048, tk=1024, out_dtype=BF16, name=f"dw_gate_{l}"))
        g_w_ple_t.append(_matmul(dpe, pb, ta=True, tm=1024, tn=256, tk=1024, out_dtype=BF16, name=f"dw_ple_{l}"))
        dh, dwp, dsc = _pool_bwd(h, dab, w_pool_g, l, scale_l, name=f"pool_bwd_{l}")
        dh, dbias, dsink = _attn_bwd(h, dab, dh, bias, masks, sink_rows, dbias, name=f"attn_bwd_{l}")
        g_w_in_t.append(_matmul(dh, xb, ta=True, tm=2176, tn=1024, tk=1024, out_dtype=BF16, name=f"dw_in_{l}"))
        g_b_in.append(_col_sum(dh, name=f"db_in_{l}"))
        dx = _matmul(dgp, w_gate_g, b_layer=l, tb=True, tm=512, tn=2048, tk=2048, out_dtype=F32, add=dz, add_scale=alpha,
                     name=f"dx_gate_{l}")
        dy = _matmul(dh, w_in_t, b_layer=l, tm=512, tn=1024, tk=4352, out_dtype=F32, add=dx, name=f"dx_in_{l}")
        g_w_pool.append(dwp.astype(BF16).reshape(4 * 256, 256))
        g_sinks.append(dsink[:, :, 0].reshape(N_HEADS))
        g_scale.append(dsc.reshape(POOL_WIDTH))
        g_gain.append(dgain.reshape(D_MODEL))
        g_beta.append(dbeta.reshape(D_MODEL))
    grad_x = dy[None]
    for lst in (g_w_in_t, g_w_out, g_w_gate, g_w_ple_t, g_w_pool, g_b_in, g_sinks, g_scale, g_gain, g_beta):
        lst.reverse()
    g_rel = _bias_bwd(dbias.reshape(N_HEADS, BLOCK, 2 * BLOCK), bucket, window)[:, :N_HEADS]

    r_in, r_out, r_gate, r_ple = _scatter_grads([g_w_in_t, g_w_out, g_w_gate, g_w_ple_t], name="scatter_grads")
    (r_pool,) = _scatter_grads([[g[256 * q:256 * (q + 1)] for g in g_w_pool for q in range(4)]], name="scatter_pool")
    small_like = [b_in, attn_sinks, rel_bias, pool_scale, ln_gain, ln_bias]
    small_g = _allreduce_small(_pack_small([
        jnp.stack(g_b_in).reshape(L, IN_COLS), jnp.stack(g_sinks), g_rel, jnp.stack(g_scale), jnp.stack(g_gain),
        jnp.stack(g_beta), loss_tile[0, :1]]))

    gt_in = _sum_slabs(r_in.reshape(N_DEV, L * 544, D_MODEL), name="sum_w_in")
    grad_w_in = jnp.swapaxes(gt_in.reshape(L, 544, D_MODEL), 1, 2)
    grad_w_out = _sum_slabs(r_out.reshape(N_DEV, L * 256, D_MODEL), name="sum_w_out").reshape(L, 256, D_MODEL)
    grad_w_gate = _sum_slabs(r_gate.reshape(N_DEV, L * 256, D_MODEL), name="sum_w_gate").reshape(L, 256, D_MODEL)
    gt_ple = _sum_slabs(r_ple.reshape(N_DEV, L * 256, PLE_DIM), name="sum_w_ple")
    grad_w_ple = jnp.swapaxes(gt_ple.reshape(L, 256, PLE_DIM), 1, 2)
    grad_w_pool = _sum_slabs(r_pool.reshape(N_DEV, L * 4 * 32, 256), name="sum_w_pool").reshape(L, 4, 32, 256)

    def big(w, g, m, v, name):
        shape = w.shape
        two_d = (shape[0] * shape[1], shape[2]) if len(shape) == 3 else (shape[0] * shape[1] * shape[2], shape[3])
        d, nm, nv = _adamw(w.reshape(two_d), g.reshape(two_d), m.reshape(two_d), v.reshape(two_d), name=name)
        return d.reshape(shape), nm.reshape(shape), nv.reshape(shape)

    upd_in = big(w_in, grad_w_in, m_w_in, v_w_in, "adamw_w_in")
    upd_out = big(w_out, grad_w_out, m_w_out, v_w_out, "adamw_w_out")
    upd_pool = big(w_pool, grad_w_pool, m_w_pool, v_w_pool, "adamw_w_pool")
    upd_ple = big(w_ple, grad_w_ple, m_w_ple, v_w_ple, "adamw_w_ple")
    upd_gate = big(w_gate_ple, grad_w_gate, m_w_gate_ple, v_w_gate_ple, "adamw_w_gate")

    zero1 = jnp.zeros((1,), F32)
    sw = _pack_small(small_like + [zero1])
    sm = _pack_small([m_b_in, m_attn_sinks, m_rel_bias, m_pool_scale, m_ln_gain, m_ln_bias, zero1])
    sv = _pack_small([v_b_in, v_attn_sinks, v_rel_bias, v_pool_scale, v_ln_gain, v_ln_bias, zero1])
    sd, snm, snv = _adamw(sw, small_g, sm, sv, name="adamw_small")
    like = small_like + [zero1]
    sg_parts = _unpack_small(small_g, like)
    sd_parts, snm_parts, snv_parts = _unpack_small(sd, like), _unpack_small(snm, like), _unpack_small(snv, like)
    loss = sg_parts[6][0]

    def assemble(big_parts, small_parts):
        w_in_, w_out_, w_pool_, w_ple_, w_gate_ = big_parts
        b_in_, sinks_, rel_, scale_, gain_, beta_ = small_parts[:6]
        return [w_in_, b_in_, w_out_, sinks_, rel_, w_pool_, scale_, w_ple_, w_gate_, gain_, beta_]

    grads = assemble([grad_w_in, grad_w_out, grad_w_pool, grad_w_ple, grad_w_gate], sg_parts)
    ups = [upd_in, upd_out, upd_pool, upd_ple, upd_gate]
    deltas = assemble([u[0] for u in ups], sd_parts)
    new_m = assemble([u[1] for u in ups], snm_parts)
    new_v = assemble([u[2] for u in ups], snv_parts)
    return (loss, grad_x, *grads, *deltas, *new_m, *new_v)
```

```python
import functools
import math

import numpy as np
import jax
import jax.numpy as jnp
from jax import lax
from jax.experimental import pallas as pl
from jax.experimental.pallas import tpu as pltpu

F32 = jnp.float32
BF16 = jnp.bfloat16

D_MODEL = 2048
PLE_DIM = 256
ATTN_WIDTH = 1024
POOL_WIDTH = 1024
HEAD_DIM = 64
N_HEADS = 16
N_KV_HEADS = 2
KV_GROUP = 8
WINDOW = 128
BLOCK = 128
POOL_WINDOWS = (2, 4, 8, 16)
POOL_GROUP_DIM = 256
POOL_HALO = 16
REL_BUCKETS = 32
REL_MAX_DIST = 128
LN_EPS = 1e-5
KV_COLS = N_KV_HEADS * HEAD_DIM
IN_COLS = 4352
Q_OFF, KV_OFF, GA_OFF, U_OFF, GB_OFF = 0, 1024, 1280, 2304, 3328
ATTN_SCALE = 1.0 / math.sqrt(HEAD_DIM)
NEG_BIG = -1e30
LANES = 128

ADAM_LR = 0.001
ADAM_B1 = 0.9
ADAM_B2 = 0.999
ADAM_EPS = 1e-08
ADAM_WD = 0.01
ADAM_STEP = 10

N_DEV = 8
MESH_ID = pl.DeviceIdType.MESH
VMEM_LIMIT_BYTES = 52 * 1024 * 1024
SMALL_COLS = 1024


def _params(sem=None):
    return pltpu.CompilerParams(dimension_semantics=sem, vmem_limit_bytes=VMEM_LIMIT_BYTES)


def _sigmoid(x):
    return 1.0 / (1.0 + jnp.exp(-x))


def _tile(n, pref, unit=16):
    if n <= pref:
        return n
    t = pref - pref % unit
    while n % t:
        t -= unit
    assert t > 0, (n, pref)
    return t


def _matmul(a, b, *, name, ta=False, tb=False, tm, tn, tk, out_dtype, bias=None, add=None, add_scale=1.0, after=None):
    M, K = (a.shape[1], a.shape[0]) if ta else a.shape
    N = b.shape[0] if tb else b.shape[1]
    assert (b.shape[1] if tb else b.shape[0]) == K
    tm, tn, tk = _tile(M, tm), _tile(N, tn), _tile(K, tk)
    nm, nn, nk = M // tm, N // tn, K // tk
    a_spec = pl.BlockSpec((tk, tm), lambda j, i, k: (k, i)) if ta else pl.BlockSpec((tm, tk), lambda j, i, k: (i, k))
    b_spec = pl.BlockSpec((tn, tk), lambda j, i, k: (j, k)) if tb else pl.BlockSpec((tk, tn), lambda j, i, k: (k, j))
    dims = (((0 if ta else 1,), (1 if tb else 0,)), ((), ()))
    operands, in_specs = [a, b], [a_spec, b_spec]
    if bias is not None:
        operands.append(bias)
        in_specs.append(pl.BlockSpec((1, tn), lambda j, i, k: (0, j)))
    if add is not None:
        operands.append(add)
        in_specs.append(pl.BlockSpec((tm, tn), lambda j, i, k: (i, j)))
    if after is not None:
        operands.append(after)
        in_specs.append(pl.BlockSpec(memory_space=pl.ANY))

    def body(*refs):
        a_ref, b_ref = refs[0], refs[1]
        pos = 2
        bias_ref = add_ref = None
        if bias is not None:
            bias_ref = refs[pos]
            pos += 1
        if add is not None:
            add_ref = refs[pos]
            pos += 1
        if after is not None:
            pos += 1
        o_ref = refs[pos]
        part = lax.dot_general(a_ref[...].astype(BF16), b_ref[...].astype(BF16), dims, preferred_element_type=F32)

        def finish(acc):
            if bias_ref is not None:
                acc = acc + bias_ref[...]
            if add_ref is not None:
                acc = acc + add_scale * add_ref[...].astype(F32)
            o_ref[...] = acc.astype(out_dtype)

        if nk == 1:
            finish(part)
        else:
            acc_ref = refs[pos + 1]
            k = pl.program_id(2)

            @pl.when(k == 0)
            def _():
                acc_ref[...] = part

            @pl.when(k > 0)
            def _():
                acc_ref[...] += part

            @pl.when(k == nk - 1)
            def _():
                finish(acc_ref[...])

    return pl.pallas_call(
        body,
        name=name,
        grid=(nn, nm, nk),
        in_specs=in_specs,
        out_specs=pl.BlockSpec((tm, tn), lambda j, i, k: (i, j)),
        out_shape=jax.ShapeDtypeStruct((M, N), out_dtype),
        scratch_shapes=[pltpu.VMEM((tm, tn), F32)] if nk > 1 else [],
        compiler_params=_params(("parallel", "parallel", "arbitrary")),
    )(*operands)


def _band_constants():
    qq = np.arange(BLOCK)[:, None]
    kk = np.arange(2 * BLOCK)[None, :]
    dist = qq + BLOCK - kk
    in_window = (dist >= 0) & (dist < WINDOW)
    max_exact = REL_BUCKETS // 2
    d = np.maximum(dist, 0)
    d_f = np.maximum(d, 1).astype(np.float32)
    large = max_exact + (
        np.log(d_f / np.float32(max_exact)) / np.float32(math.log(REL_MAX_DIST / max_exact)) * np.float32(REL_BUCKETS - max_exact)
    ).astype(np.int32)
    large = np.minimum(large, REL_BUCKETS - 1)
    bucket = np.where(d < max_exact, d, large).astype(np.int32)
    bucket = np.where(in_window, bucket, 0).astype(np.int32)
    first = in_window & (kk >= BLOCK)
    masks = np.stack([first, in_window]).astype(np.float32)
    return bucket, masks, in_window.astype(np.float32)


def _bias_build(rel_bias, bucket):
    def body(rb_ref, bkt_ref, o_ref):
        h = pl.program_id(0)
        bkt = bkt_ref[...]

        def step(b, acc):
            return jnp.where(bkt == b, rb_ref[b, h], acc)

        o_ref[0] = lax.fori_loop(0, REL_BUCKETS, step, jnp.zeros((BLOCK, 2 * BLOCK), F32))

    return pl.pallas_call(
        body,
        name="bias_build",
        grid=(N_HEADS,),
        in_specs=[pl.BlockSpec(memory_space=pltpu.SMEM), pl.BlockSpec((BLOCK, 2 * BLOCK), lambda h: (0, 0))],
        out_specs=pl.BlockSpec((1, BLOCK, 2 * BLOCK), lambda h: (h, 0, 0)),
        out_shape=jax.ShapeDtypeStruct((N_HEADS, BLOCK, 2 * BLOCK), F32),
        compiler_params=_params(("arbitrary",)),
    )(rel_bias, bucket)


def _bias_bwd(dbias, bucket, window):
    def body(db_ref, bkt_ref, win_ref, o_ref):
        h = pl.program_id(0)

        @pl.when(h == 0)
        def _():
            o_ref[...] = jnp.zeros_like(o_ref)

        bkt = bkt_ref[...]
        x = jnp.where(win_ref[...] > 0.5, db_ref[0], 0.0)
        row = lax.broadcasted_iota(jnp.int32, (REL_BUCKETS, LANES), 0)
        col = lax.broadcasted_iota(jnp.int32, (REL_BUCKETS, LANES), 1)

        def step(b, acc):
            s = jnp.sum(jnp.where(bkt == b, x, 0.0), axis=1, keepdims=True)
            s = jnp.sum(s, axis=0, keepdims=True)
            return acc + jnp.where((row == b) & (col == h), s, 0.0)

        o_ref[...] += lax.fori_loop(0, REL_BUCKETS, step, jnp.zeros((REL_BUCKETS, LANES), F32))

    return pl.pallas_call(
        body,
        name="bias_bwd",
        grid=(N_HEADS,),
        in_specs=[
            pl.BlockSpec((1, BLOCK, 2 * BLOCK), lambda h: (h, 0, 0)),
            pl.BlockSpec((BLOCK, 2 * BLOCK), lambda h: (0, 0)),
            pl.BlockSpec((BLOCK, 2 * BLOCK), lambda h: (0, 0)),
        ],
        out_specs=pl.BlockSpec((REL_BUCKETS, LANES), lambda h: (0, 0)),
        out_shape=jax.ShapeDtypeStruct((REL_BUCKETS, LANES), F32),
        compiler_params=_params(("arbitrary",)),
    )(dbias, bucket, window)


def _lane_lo(shape):
    return lax.broadcasted_iota(jnp.int32, shape, 1) < HEAD_DIM


def _dup_heads(x):
    r = pltpu.roll(x, HEAD_DIM, axis=1)
    lo = _lane_lo(x.shape)
    return jnp.where(lo, x, r), jnp.where(lo, r, x)


def _kv_operands(kvp_ref, kvc_ref):
    kvp, kvc = kvp_ref[...], kvc_ref[...]
    k2 = jnp.concatenate([kvp[:, :KV_COLS], kvc[:, :KV_COLS]], axis=0)
    v2 = jnp.concatenate([kvp[:, KV_COLS:], kvc[:, KV_COLS:]], axis=0)
    return _dup_heads(k2), _dup_heads(v2)


def _head_probs(qs, k_t, bias, mask, sink):
    s = jnp.dot(qs, k_t, preferred_element_type=F32) * ATTN_SCALE + bias
    s = jnp.where(mask, s, NEG_BIG)
    m = jnp.maximum(jnp.max(s, axis=-1, keepdims=True), sink)
    e = jnp.exp(s - m)
    e_sink = jnp.exp(sink - m)
    inv = 1.0 / (jnp.sum(e, axis=-1, keepdims=True) + e_sink)
    return e * inv, e_sink * inv


def _gate_cols(ga_refs, pair):
    off = LANES * (pair % 2)
    return ga_refs[pair // 2][:, off:off + LANES]


def _attn_specs(order):
    return [
        pl.BlockSpec((BLOCK, ATTN_WIDTH), lambda t: (order(t), Q_OFF // ATTN_WIDTH)),
        pl.BlockSpec((BLOCK, 2 * KV_COLS), lambda t: (order(t), KV_OFF // (2 * KV_COLS))),
        pl.BlockSpec((BLOCK, 2 * KV_COLS), lambda t: (jnp.maximum(order(t) - 1, 0), KV_OFF // (2 * KV_COLS))),
    ] + [
        pl.BlockSpec((BLOCK, 256), functools.partial(lambda t, c: (order(t), GA_OFF // 256 + c), c=c)) for c in range(4)
    ] + [
        pl.BlockSpec((N_KV_HEADS, KV_GROUP * BLOCK, 2 * BLOCK), lambda t: (0, 0, 0)),
        pl.BlockSpec((None, BLOCK, 2 * BLOCK), lambda t: (jnp.minimum(order(t), 1), 0, 0)),
        pl.BlockSpec((N_KV_HEADS, KV_GROUP * BLOCK, 1), lambda t: (0, 0, 0)),
    ]


def _attn_fwd(h, bias, masks, sink_rows, name):
    S = h.shape[0]
    nb = S // BLOCK

    def body(q_ref, kvc_ref, kvp_ref, ga0, ga1, ga2, ga3, bias_ref, mask_ref, sink_ref, o_ref):
        kd, vd = _kv_operands(kvp_ref, kvc_ref)
        mask = mask_ref[...] > 0.5
        lo = _lane_lo((BLOCK, LANES))
        for g in range(N_KV_HEADS):
            k_t = kd[g].T.astype(BF16)
            v_r = vd[g].astype(BF16)
            for pr in range(KV_GROUP // 2):
                pair = (KV_GROUP // 2) * g + pr
                qp = q_ref[:, LANES * pair:LANES * (pair + 1)]
                outs = []
                for hh in range(2):
                    rows = slice(BLOCK * (2 * pr + hh), BLOCK * (2 * pr + hh + 1))
                    qs = jnp.where(lo if hh == 0 else ~lo, qp, 0.0).astype(BF16)
                    p, _ = _head_probs(qs, k_t, bias_ref[g, rows, :], mask, sink_ref[g, rows, :])
                    outs.append(jnp.dot(p.astype(BF16), v_r, preferred_element_type=F32))
                ga = _gate_cols((ga0, ga1, ga2, ga3), pair)
                o_ref[:, LANES * pair:LANES * (pair + 1)] = (
                    jnp.where(lo, outs[0], outs[1]) * (ga * _sigmoid(ga))).astype(BF16)

    return pl.pallas_call(
        body,
        name=name,
        grid=(nb,),
        in_specs=_attn_specs(lambda t: t),
        out_specs=pl.BlockSpec((BLOCK, ATTN_WIDTH), lambda t: (t, 0)),
        out_shape=jax.ShapeDtypeStruct((S, ATTN_WIDTH + POOL_WIDTH), BF16),
        compiler_params=_params(("arbitrary",)),
    )(h, h, h, h, h, h, h, bias, masks, sink_rows)


DH_ATTN_COLS = U_OFF


def _attn_bwd(h, dab, dh, bias, masks, sink_rows, dbias_in, name):
    S = h.shape[0]
    nb = S // BLOCK

    def order(t):
        return nb - 1 - t

    def body(q_ref, kvc_ref, kvp_ref, ga0, ga1, ga2, ga3, bias_ref, mask_ref, sink_ref, da_ref, dbin_ref, dh_in_ref,
             dh_ref, dbias_ref, dsink_ref, carry_scr):
        del dh_in_ref
        t = pl.program_id(0)

        @pl.when(t == 0)
        def _():
            dbias_ref[...] = dbin_ref[...]
            dsink_ref[...] = jnp.zeros_like(dsink_ref)
            carry_scr[...] = jnp.zeros_like(carry_scr)

        kd, vd = _kv_operands(kvp_ref, kvc_ref)
        mask = mask_ref[...] > 0.5
        lo = _lane_lo((BLOCK, LANES))
        dk_tot, dv_tot = [], []
        for g in range(N_KV_HEADS):
            k_t, k_r = kd[g].T.astype(BF16), kd[g].astype(BF16)
            v_t, v_r = vd[g].T.astype(BF16), vd[g].astype(BF16)
            dk_t = jnp.zeros((LANES, 2 * BLOCK), F32)
            dv_t = jnp.zeros((LANES, 2 * BLOCK), F32)
            for pr in range(KV_GROUP // 2):
                pair = (KV_GROUP // 2) * g + pr
                cols = slice(LANES * pair, LANES * (pair + 1))
                qp = q_ref[:, cols]
                ga = _gate_cols((ga0, ga1, ga2, ga3), pair)
                sg = _sigmoid(ga)
                da = da_ref[:, cols]
                do_p = da * (ga * sg)
                outs, dqs = [], []
                for hh in range(2):
                    j = 2 * pr + hh
                    rows = slice(BLOCK * j, BLOCK * (j + 1))
                    half = lo if hh == 0 else ~lo
                    qs = jnp.where(half, qp, 0.0).astype(BF16)
                    p, p_sink = _head_probs(qs, k_t, bias_ref[g, rows, :], mask, sink_ref[g, rows, :])
                    pb = p.astype(BF16)
                    outs.append(jnp.dot(pb, v_r, preferred_element_type=F32))
                    dos = jnp.where(half, do_p, 0.0).astype(BF16)
                    dp = jnp.dot(dos, v_t, preferred_element_type=F32)
                    dsum = jnp.sum(p * dp, axis=-1, keepdims=True)
                    ds = p * (dp - dsum)
                    dbias_ref[g, rows, :] += ds
                    tot = jnp.sum(-(p_sink * dsum), axis=0, keepdims=True)
                    dsink_ref[g, j:j + 1, :] += jnp.broadcast_to(tot, (1, LANES))
                    dsb = ds.astype(BF16)
                    dqs.append(jnp.dot(dsb, k_r, preferred_element_type=F32))
                    dk_t = dk_t + lax.dot_general(qs, dsb, (((0,), (0,)), ((), ())), preferred_element_type=F32)
                    dv_t = dv_t + lax.dot_general(dos, pb, (((0,), (0,)), ((), ())), preferred_element_type=F32)
                attn = jnp.where(lo, outs[0], outs[1])
                dh_ref[:, cols] = (jnp.where(lo, dqs[0], dqs[1]) * ATTN_SCALE).astype(BF16)
                dh_ref[:, GA_OFF + LANES * pair:GA_OFF + LANES * (pair + 1)] = (
                    da * attn * (sg * (1.0 + ga * (1.0 - sg)))).astype(BF16)
            dk = (dk_t * ATTN_SCALE).T
            dv = dv_t.T
            dk_tot.append(dk + pltpu.roll(dk, HEAD_DIM, axis=1))
            dv_tot.append(dv + pltpu.roll(dv, HEAD_DIM, axis=1))
        lo2 = _lane_lo((2 * BLOCK, LANES))
        dkv = jnp.concatenate([jnp.where(lo2, dk_tot[0], dk_tot[1]), jnp.where(lo2, dv_tot[0], dv_tot[1])], axis=1)
        dh_ref[:, KV_OFF:KV_OFF + 2 * KV_COLS] = (dkv[BLOCK:, :] + carry_scr[...]).astype(BF16)
        carry_scr[...] = dkv[:BLOCK, :]

    n_in = 12
    return pl.pallas_call(
        body,
        name=name,
        grid=(nb,),
        in_specs=_attn_specs(order) + [
            pl.BlockSpec((BLOCK, ATTN_WIDTH), lambda t: (order(t), 0)),
            pl.BlockSpec((N_KV_HEADS, KV_GROUP * BLOCK, 2 * BLOCK), lambda t: (0, 0, 0)),
            pl.BlockSpec(memory_space=pl.ANY),
        ],
        out_specs=[
            pl.BlockSpec((BLOCK, DH_ATTN_COLS), lambda t: (order(t), 0)),
            pl.BlockSpec((N_KV_HEADS, KV_GROUP * BLOCK, 2 * BLOCK), lambda t: (0, 0, 0)),
            pl.BlockSpec((N_KV_HEADS, KV_GROUP, LANES), lambda t: (0, 0, 0)),
        ],
        out_shape=[
            jax.ShapeDtypeStruct((S, IN_COLS), BF16),
            jax.ShapeDtypeStruct((N_KV_HEADS, KV_GROUP * BLOCK, 2 * BLOCK), F32),
            jax.ShapeDtypeStruct((N_KV_HEADS, KV_GROUP, LANES), F32),
        ],
        scratch_shapes=[pltpu.VMEM((BLOCK, 2 * KV_COLS), F32)],
        input_output_aliases={n_in: 0},
        compiler_params=_params(("arbitrary",)),
    )(h, h, h, h, h, h, h, bias, masks, sink_rows, dab, dbias_in, dh)


def _window_sum(x, w, back):
    n = x.shape[0]
    s, sh = x, 1
    while sh < w:
        s = s + pltpu.roll(s, sh if back else n - sh, axis=0)
        sh *= 2
    return s


def _pool_counts(first_row, n, w):
    t = first_row + lax.broadcasted_iota(jnp.int32, (n, 1), 0)
    return jnp.minimum(t + 1, w).astype(F32)


def _pool_diff(u_ref, uh_ref, i, T, g):
    u = u_ref[...]
    halo = jnp.where(i > 0, uh_ref[...], 0.0)
    ext = jnp.concatenate([halo, u], axis=0)
    w = POOL_WINDOWS[g]
    s = _window_sum(ext, w, back=True)[POOL_HALO:, :]
    return s / _pool_counts(i * T, T, w) - u


def _pool_in_specs(T):
    hb = T // POOL_HALO
    specs = []
    for g in range(4):
        specs.append(pl.BlockSpec((T, 256), functools.partial(lambda i, g: (i, U_OFF // 256 + g), g=g)))
        specs.append(pl.BlockSpec((POOL_HALO, 256), functools.partial(
            lambda i, g: (jnp.maximum(i * hb - 1, 0), U_OFF // 256 + g), g=g)))
    return specs


def _pool_weight_specs():
    return [pl.BlockSpec((4, 256, 256), lambda i: (0, 0, 0)), pl.BlockSpec((1, POOL_WIDTH), lambda i: (0, 0))]


def _pool_fwd(h, ab, w_pool, pool_scale, name):
    S = h.shape[0]
    T = _tile(S, 512)

    def body(*refs):
        u_refs = refs[0:8]
        gb_refs = refs[8:12]
        wp_ref, sc_ref, o_ref = refs[12], refs[13], refs[15]
        i = pl.program_id(0)
        for g in range(4):
            diff = _pool_diff(u_refs[2 * g], u_refs[2 * g + 1], i, T, g).astype(BF16)
            mixed = jnp.dot(diff, wp_ref[g], preferred_element_type=F32) * sc_ref[:, 256 * g:256 * (g + 1)]
            gb = gb_refs[g][...]
            o_ref[:, 256 * g:256 * (g + 1)] = (mixed * (gb * _sigmoid(gb))).astype(BF16)

    in_specs = _pool_in_specs(T) + [
        pl.BlockSpec((T, 256), functools.partial(lambda i, g: (i, GB_OFF // 256 + g), g=g)) for g in range(4)
    ] + _pool_weight_specs() + [pl.BlockSpec(memory_space=pl.ANY)]
    return pl.pallas_call(
        body,
        name=name,
        grid=(S // T,),
        in_specs=in_specs,
        out_specs=pl.BlockSpec((T, POOL_WIDTH), lambda i: (i, 1)),
        out_shape=jax.ShapeDtypeStruct(ab.shape, BF16),
        input_output_aliases={14: 0},
        compiler_params=_params(("arbitrary",)),
    )(*([h] * 12), w_pool, pool_scale, ab)


DH_POOL_COLS = IN_COLS // 2


def _pool_bwd(h, dab, w_pool, pool_scale, name):
    S = h.shape[0]
    T = _tile(S, 512)
    nt = S // T
    hb = T // POOL_HALO
    E = T + POOL_HALO
    lead = U_OFF - DH_POOL_COLS

    def body(*refs):
        u_refs = refs[0:8]
        gb_refs = refs[8:16]
        db_refs = refs[16:24]
        wp_ref, sc_ref = refs[24], refs[25]
        dh_ref, dwp_ref, dsc_ref = refs[26:29]
        i = pl.program_id(0)

        @pl.when(i == 0)
        def _():
            dwp_ref[...] = jnp.zeros_like(dwp_ref)
            dsc_ref[...] = jnp.zeros_like(dsc_ref)

        dh_ref[:, 0:lead] = jnp.zeros((T, lead), BF16)
        for g in range(4):
            w = POOL_WINDOWS[g]
            cols = slice(256 * g, 256 * (g + 1))
            scale = sc_ref[:, cols]
            wp = wp_ref[g]
            diff = _pool_diff(u_refs[2 * g], u_refs[2 * g + 1], i, T, g).astype(BF16)
            mixed = jnp.dot(diff, wp, preferred_element_type=F32)
            keep = i < nt - 1
            gb = jnp.concatenate([gb_refs[2 * g][...], jnp.where(keep, gb_refs[2 * g + 1][...], 0.0)], axis=0)
            db = jnp.concatenate([db_refs[2 * g][...], jnp.where(keep, db_refs[2 * g + 1][...], 0.0)], axis=0)
            sg = _sigmoid(gb)
            dms = db * (gb * sg)
            dmixed = (dms * scale).astype(BF16)
            ddiff = lax.dot_general(dmixed, wp, (((1,), (1,)), ((), ())), preferred_element_type=F32)
            r = ddiff / _pool_counts(i * T, E, w)
            du = _window_sum(r, w, back=False)[:T, :] - ddiff[:T, :]
            dh_ref[:, lead + 256 * g:lead + 256 * (g + 1)] = du.astype(BF16)
            dh_ref[:, lead + POOL_WIDTH + 256 * g:lead + POOL_WIDTH + 256 * (g + 1)] = (
                db[:T, :] * (mixed * scale) * (sg[:T, :] * (1.0 + gb[:T, :] * (1.0 - sg[:T, :])))).astype(BF16)
            dsc_ref[:, cols] += jnp.sum(dms[:T, :] * mixed, axis=0, keepdims=True)
            dwp_ref[g] += lax.dot_general(diff, dmixed[:T, :], (((0,), (0,)), ((), ())), preferred_element_type=F32)

    def after(i):
        return jnp.minimum((i + 1) * hb, S // POOL_HALO - 1)

    in_specs = _pool_in_specs(T)
    for off in (GB_OFF // 256, ATTN_WIDTH // 256):
        for g in range(4):
            in_specs.append(pl.BlockSpec((T, 256), functools.partial(lambda i, c: (i, c), c=off + g)))
            in_specs.append(pl.BlockSpec((POOL_HALO, 256), functools.partial(lambda i, c: (after(i), c), c=off + g)))
    in_specs += _pool_weight_specs()
    return pl.pallas_call(
        body,
        name=name,
        grid=(nt,),
        in_specs=in_specs,
        out_specs=[
            pl.BlockSpec((T, DH_POOL_COLS), lambda i: (i, 1)),
            pl.BlockSpec((4, 256, 256), lambda i: (0, 0, 0)),
            pl.BlockSpec((1, POOL_WIDTH), lambda i: (0, 0)),
        ],
        out_shape=[
            jax.ShapeDtypeStruct((S, IN_COLS), BF16),
            jax.ShapeDtypeStruct((4, 256, 256), F32),
            jax.ShapeDtypeStruct((1, POOL_WIDTH), F32),
        ],
        compiler_params=_params(("arbitrary",)),
    )(*([h] * 16), *([dab] * 8), w_pool, pool_scale)


def _ln_fwd(x, mix, gp, pe, gain, bias, alpha, name):
    S = x.shape[0]
    T = _tile(S, 256)

    def body(x_ref, mix_ref, gp_ref, pe_ref, g_ref, b_ref, y_ref, yb_ref, xh_ref, rs_ref):
        z = alpha * x_ref[...] + mix_ref[...] + _sigmoid(gp_ref[...]) * pe_ref[...]
        mu = jnp.mean(z, axis=-1, keepdims=True)
        zc = z - mu
        var = jnp.mean(zc * zc, axis=-1, keepdims=True)
        rstd = lax.rsqrt(var + LN_EPS)
        xhat = zc * rstd
        y = xhat * g_ref[...] + b_ref[...]
        y_ref[...] = y
        yb_ref[...] = y.astype(BF16)
        xh_ref[...] = xhat
        rs_ref[...] = rstd

    row = pl.BlockSpec((T, D_MODEL), lambda i: (i, 0))
    vec = pl.BlockSpec((1, D_MODEL), lambda i: (0, 0))
    return pl.pallas_call(
        body,
        name=name,
        grid=(S // T,),
        in_specs=[row, row, row, row, vec, vec],
        out_specs=[row, row, row, pl.BlockSpec((T, 1), lambda i: (i, 0))],
        out_shape=[
            jax.ShapeDtypeStruct((S, D_MODEL), F32),
            jax.ShapeDtypeStruct((S, D_MODEL), BF16),
            jax.ShapeDtypeStruct((S, D_MODEL), F32),
            jax.ShapeDtypeStruct((S, 1), F32),
        ],
        compiler_params=_params(("parallel",)),
    )(x, mix, gp, pe, gain, bias)


def _ln_bwd(dy, xhat, rstd, gain, gp, pe, after, name):
    S = dy.shape[0]
    T = _tile(S, 256)

    def body(dy_ref, xh_ref, rs_ref, g_ref, gp_ref, pe_ref, after_ref, dz_ref, dzb_ref, dpe_ref, dgp_ref, dgain_ref, dbias_ref):
        del after_ref
        @pl.when(pl.program_id(0) == 0)
        def _():
            dgain_ref[...] = jnp.zeros_like(dgain_ref)
            dbias_ref[...] = jnp.zeros_like(dbias_ref)

        dy = dy_ref[...]
        xhat = xh_ref[...]
        dyg = dy * g_ref[...]
        c1 = jnp.mean(dyg, axis=-1, keepdims=True)
        c2 = jnp.mean(dyg * xhat, axis=-1, keepdims=True)
        dz = rs_ref[...] * (dyg - c1 - xhat * c2)
        dgain_ref[...] += jnp.sum(dy * xhat, axis=0, keepdims=True)
        dbias_ref[...] += jnp.sum(dy, axis=0, keepdims=True)
        sg = _sigmoid(gp_ref[...])
        dz_ref[...] = dz
        dzb_ref[...] = dz.astype(BF16)
        dpe_ref[...] = (dz * sg).astype(BF16)
        dgp_ref[...] = (dz * pe_ref[...] * (sg * (1.0 - sg))).astype(BF16)

    row = pl.BlockSpec((T, D_MODEL), lambda i: (i, 0))
    vec = pl.BlockSpec((1, D_MODEL), lambda i: (0, 0))
    return pl.pallas_call(
        body,
        name=name,
        grid=(S // T,),
        in_specs=[row, row, pl.BlockSpec((T, 1), lambda i: (i, 0)), vec, row, row, pl.BlockSpec(memory_space=pl.ANY)],
        out_specs=[row, row, row, row, vec, vec],
        out_shape=[
            jax.ShapeDtypeStruct((S, D_MODEL), F32),
            jax.ShapeDtypeStruct((S, D_MODEL), BF16),
            jax.ShapeDtypeStruct((S, D_MODEL), BF16),
            jax.ShapeDtypeStruct((S, D_MODEL), BF16),
            jax.ShapeDtypeStruct((1, D_MODEL), F32),
            jax.ShapeDtypeStruct((1, D_MODEL), F32),
        ],
        compiler_params=_params(("arbitrary",)),
    )(dy, xhat, rstd, gain, gp, pe, after)


def _loss_head(y, target):
    S = y.shape[0]
    T = _tile(S, 256)

    def body(y_ref, t_ref, dy_ref, l_ref):
        @pl.when(pl.program_id(0) == 0)
        def _():
            l_ref[...] = jnp.zeros_like(l_ref)

        err = y_ref[...] - t_ref[...]
        dy_ref[...] = err * (1.0 / D_MODEL)
        per_token = jnp.mean(err * err, axis=-1, keepdims=True)
        l_ref[...] += 0.5 * jnp.sum(per_token, axis=0, keepdims=True)

    row = pl.BlockSpec((T, D_MODEL), lambda i: (i, 0))
    return pl.pallas_call(
        body,
        name="loss_head",
        grid=(S // T,),
        in_specs=[row, row],
        out_specs=[row, pl.BlockSpec((8, LANES), lambda i: (0, 0))],
        out_shape=[jax.ShapeDtypeStruct((S, D_MODEL), F32), jax.ShapeDtypeStruct((8, LANES), F32)],
        compiler_params=_params(("arbitrary",)),
    )(y, target)


def _col_sum(a, name):
    S, C = a.shape
    T = _tile(S, 512)

    def body(a_ref, o_ref):
        @pl.when(pl.program_id(0) == 0)
        def _():
            o_ref[...] = jnp.zeros_like(o_ref)

        o_ref[...] += jnp.sum(a_ref[...].astype(F32), axis=0, keepdims=True)

    return pl.pallas_call(
        body,
        name=name,
        grid=(S // T,),
        in_specs=[pl.BlockSpec((T, C), lambda i: (i, 0))],
        out_specs=pl.BlockSpec((1, C), lambda i: (0, 0)),
        out_shape=jax.ShapeDtypeStruct((1, C), F32),
        compiler_params=_params(("arbitrary",)),
    )(a)


def _sum_slabs(r, name):
    _, R, C = r.shape
    T = _tile(R, 256)

    def body(r_ref, o_ref):
        acc = r_ref[0].astype(F32)
        for s in range(1, N_DEV):
            acc = acc + r_ref[s].astype(F32)
        o_ref[...] = acc

    return pl.pallas_call(
        body,
        name=name,
        grid=(R // T,),
        in_specs=[pl.BlockSpec((N_DEV, T, C), lambda i: (0, i, 0))],
        out_specs=pl.BlockSpec((T, C), lambda i: (i, 0)),
        out_shape=jax.ShapeDtypeStruct((R, C), F32),
        compiler_params=_params(("parallel",)),
    )(r)


def _adamw(w, g, m, v, name):
    R, C = w.shape
    T = _tile(R, 256)

    def body(w_ref, g_ref, m_ref, v_ref, d_ref, nm_ref, nv_ref):
        g = g_ref[...]
        m = ADAM_B1 * m_ref[...] + (1.0 - ADAM_B1) * g
        v = ADAM_B2 * v_ref[...] + (1.0 - ADAM_B2) * jnp.square(g)
        m_hat = m / (1.0 - ADAM_B1 ** ADAM_STEP)
        v_hat = v / (1.0 - ADAM_B2 ** ADAM_STEP)
        d_ref[...] = -ADAM_LR * (m_hat / (jnp.sqrt(v_hat) + ADAM_EPS) + ADAM_WD * w_ref[...])
        nm_ref[...] = m
        nv_ref[...] = v

    blk = pl.BlockSpec((T, C), lambda i: (i, 0))
    shp = jax.ShapeDtypeStruct((R, C), F32)
    return pl.pallas_call(
        body,
        name=name,
        grid=(R // T,),
        in_specs=[blk] * 4,
        out_specs=[blk] * 3,
        out_shape=[shp] * 3,
        compiler_params=_params(("parallel",)),
    )(w, g, m, v)


def _mesh_pos():
    return lax.axis_index("x"), lax.axis_index("y"), lax.axis_index("c")


def _flip(pos, k):
    x, y, c = pos
    return (1 - x if k & 4 else x, 1 - y if k & 2 else y, 1 - c if k & 1 else c)


def _index(pos):
    return 4 * pos[0] + 2 * pos[1] + pos[2]


HBM_SPEC = pl.BlockSpec(memory_space=pltpu.HBM)
SEM_SPEC = pl.BlockSpec(memory_space=pltpu.SEMAPHORE)
ANY_SPEC = pl.BlockSpec(memory_space=pl.ANY)
SPLIT_EFFECT = pltpu.SideEffectType.DATAFLOW_SIDE_EFFECTING
GATHER_FLIPS = (1, 4, 2, 6)
CHIP_FLIPS = (4, 2, 6)
TOKEN = jax.ShapeDtypeStruct((8, LANES), F32)


def _hbm(a):
    return pltpu.with_memory_space_constraint(a, pltpu.HBM)


def _hbm_like(a):
    return pltpu.HBM(a.shape, a.dtype)


def _block_rows(ref, pos, r):
    return ref.at[:, pl.ds(_index(pos) * r, r), :]


def _gather_start(shards, after, name):
    n = len(shards)
    lands = [lax.empty((s.shape[0], N_DEV * s.shape[1], s.shape[2]), s.dtype) for s in shards]

    def body(*refs):
        ins, bufs = refs[:n], refs[n:2 * n]
        send_sems, recv_sems = refs[2 * n + 1], refs[2 * n + 2]
        token = refs[4 * n + 3]
        me = _mesh_pos()
        for a in range(n):
            for j, k in enumerate(GATHER_FLIPS):
                pltpu.make_async_remote_copy(
                    src_ref=ins[a], dst_ref=_block_rows(bufs[a], me, shards[a].shape[1]),
                    send_sem=send_sems.at[4 * a + j], recv_sem=recv_sems.at[4 * a + j],
                    device_id=_flip(me, k), device_id_type=MESH_ID).start()
        token[...] = jnp.zeros_like(token)

    outs = pl.pallas_call(
        body,
        name=name,
        in_specs=[HBM_SPEC] * (2 * n) + [ANY_SPEC],
        out_specs=[SEM_SPEC, SEM_SPEC] + [HBM_SPEC] * (2 * n) + [pl.BlockSpec(memory_space=pltpu.VMEM)],
        out_shape=[pltpu.SemaphoreType.DMA((4 * n,)), pltpu.SemaphoreType.DMA((4 * n,))]
        + [_hbm_like(s) for s in shards] + [_hbm_like(b) for b in lands] + [TOKEN],
        input_output_aliases={i: 2 + i for i in range(2 * n)},
        compiler_params=pltpu.CompilerParams(has_side_effects=SPLIT_EFFECT),
    )(*[_hbm(s) for s in shards], *[_hbm(b) for b in lands], after)
    return outs[0], outs[1], outs[2:2 + n], outs[2 + n:2 + 2 * n], outs[2 + 2 * n]


def _gather_wait(started, after, name):
    send_sems, recv_sems, shards, lands, _ = started
    n = len(shards)

    def body(*refs):
        ins, bufs = refs[:n], refs[n:2 * n]
        send_sems, recv_sems = refs[2 * n], refs[2 * n + 1]
        me = _mesh_pos()
        for a in range(n):
            for j, k in enumerate(GATHER_FLIPS):
                cp = pltpu.make_async_remote_copy(
                    src_ref=ins[a], dst_ref=_block_rows(bufs[a], _flip(me, k), shards[a].shape[1]),
                    send_sem=send_sems.at[4 * a + j], recv_sem=recv_sems.at[4 * a + j],
                    device_id=_flip(me, k), device_id_type=MESH_ID)
                cp.wait_send()
                cp.wait_recv()

    outs = pl.pallas_call(
        body,
        name=name,
        in_specs=[HBM_SPEC] * (2 * n) + [SEM_SPEC, SEM_SPEC, ANY_SPEC],
        out_specs=[HBM_SPEC] * (2 * n),
        out_shape=[_hbm_like(s) for s in shards] + [_hbm_like(b) for b in lands],
        input_output_aliases={i: i for i in range(2 * n)},
        compiler_params=pltpu.CompilerParams(has_side_effects=SPLIT_EFFECT),
    )(*shards, *lands, send_sems, recv_sems, after)
    return outs[:n], outs[n:]


def _gather_pass(shards, lands, name):
    n = len(shards)

    def body(*refs):
        ins, bufs = refs[:n], refs[n:2 * n]
        token = refs[3 * n]
        send_sems, recv_sems, local_sems = refs[3 * n + 1:]
        me = _mesh_pos()
        sibling = _flip(me, 1)

        def copy(a, j, block):
            rows = _block_rows(bufs[a], block, shards[a].shape[1])
            return pltpu.make_async_remote_copy(
                src_ref=rows, dst_ref=rows, send_sem=send_sems.at[3 * a + j], recv_sem=recv_sems.at[3 * a + j],
                device_id=sibling, device_id_type=MESH_ID)

        mine = [pltpu.make_async_copy(ins[a], _block_rows(bufs[a], me, shards[a].shape[1]), local_sems.at[a])
                for a in range(n)]
        sends = [copy(a, j, _flip(me, k)) for a in range(n) for j, k in enumerate(CHIP_FLIPS)]
        for cp in mine + sends:
            cp.start()
        for a in range(n):
            for j, k in enumerate(CHIP_FLIPS):
                copy(a, j, _flip(sibling, k)).wait_recv()
        for cp in sends:
            cp.wait_send()
        for cp in mine:
            cp.wait()
        token[...] = jnp.zeros_like(token)

    outs = pl.pallas_call(
        body,
        name=name,
        in_specs=[ANY_SPEC] * (2 * n),
        out_specs=[ANY_SPEC] * n + [pl.BlockSpec(memory_space=pltpu.VMEM)],
        out_shape=[jax.ShapeDtypeStruct(b.shape, b.dtype) for b in lands] + [TOKEN],
        scratch_shapes=[pltpu.SemaphoreType.DMA((3 * n,)), pltpu.SemaphoreType.DMA((3 * n,)), pltpu.SemaphoreType.DMA((n,))],
        input_output_aliases={n + i: i for i in range(n)},
        compiler_params=pltpu.CompilerParams(has_side_effects=True),
    )(*shards, *lands)
    return outs[:n], outs[n]


def _place_own(grads, lands, layer, name):
    n = len(grads)

    def body(*refs):
        ins, bufs, local_sems = refs[:n], refs[n:2 * n], refs[3 * n]
        me = _mesh_pos()
        cps = [pltpu.make_async_copy(_block_rows(ins[a], me, grads[a].shape[1] // N_DEV),
                                     bufs[a].at[_index(me), layer], local_sems.at[a]) for a in range(n)]
        for cp in cps:
            cp.start()
        for cp in cps:
            cp.wait()

    return pl.pallas_call(
        body,
        name=name,
        in_specs=[ANY_SPEC] * (2 * n),
        out_specs=[ANY_SPEC] * n,
        out_shape=[jax.ShapeDtypeStruct(b.shape, b.dtype) for b in lands],
        scratch_shapes=[pltpu.SemaphoreType.DMA((n,))],
        input_output_aliases={n + i: i for i in range(n)},
        compiler_params=pltpu.CompilerParams(has_side_effects=True),
    )(*grads, *lands)


def _scatter_copy(ins, bufs, send_sems, recv_sems, a, k, r, layer, me, slab):
    peer = _flip(me, k)
    return pltpu.make_async_remote_copy(
        src_ref=_block_rows(ins[a], peer, r), dst_ref=bufs[a].at[_index(slab), layer],
        send_sem=send_sems.at[7 * a + k - 1], recv_sem=recv_sems.at[7 * a + k - 1],
        device_id=peer, device_id_type=MESH_ID)


def _scatter_start(grads, lands, layer, name):
    n = len(grads)

    def body(*refs):
        ins, bufs = refs[:n], refs[n:2 * n]
        send_sems, recv_sems = refs[2 * n], refs[2 * n + 1]
        token = refs[4 * n + 2]
        me = _mesh_pos()
        for a in range(n):
            for k in range(1, N_DEV):
                _scatter_copy(ins, bufs, send_sems, recv_sems, a, k, grads[a].shape[1] // N_DEV, layer, me, me).start()
        token[...] = jnp.zeros_like(token)

    outs = pl.pallas_call(
        body,
        name=name,
        in_specs=[HBM_SPEC] * (2 * n),
        out_specs=[SEM_SPEC, SEM_SPEC] + [HBM_SPEC] * (2 * n) + [pl.BlockSpec(memory_space=pltpu.VMEM)],
        out_shape=[pltpu.SemaphoreType.DMA((7 * n,)), pltpu.SemaphoreType.DMA((7 * n,))]
        + [_hbm_like(g) for g in grads] + [_hbm_like(b) for b in lands] + [TOKEN],
        input_output_aliases={i: 2 + i for i in range(2 * n)},
        compiler_params=pltpu.CompilerParams(has_side_effects=SPLIT_EFFECT),
    )(*[_hbm(g) for g in grads], *[_hbm(b) for b in lands])
    return outs[0], outs[1], outs[2:2 + n], outs[2 + n:2 + 2 * n], outs[2 + 2 * n]


def _scatter_wait(started, layer, after, name):
    send_sems, recv_sems, grads, lands, _ = started
    n = len(grads)

    def body(*refs):
        ins, bufs = refs[:n], refs[n:2 * n]
        send_sems, recv_sems = refs[2 * n], refs[2 * n + 1]
        me = _mesh_pos()
        for a in range(n):
            for k in range(1, N_DEV):
                cp = _scatter_copy(ins, bufs, send_sems, recv_sems, a, k, grads[a].shape[1] // N_DEV, layer, me, _flip(me, k))
                cp.wait_send()
                cp.wait_recv()

    outs = pl.pallas_call(
        body,
        name=name,
        in_specs=[HBM_SPEC] * (2 * n) + [SEM_SPEC, SEM_SPEC, ANY_SPEC],
        out_specs=[HBM_SPEC] * (2 * n),
        out_shape=[_hbm_like(g) for g in grads] + [_hbm_like(b) for b in lands],
        input_output_aliases={i: i for i in range(2 * n)},
        compiler_params=pltpu.CompilerParams(has_side_effects=SPLIT_EFFECT),
    )(*grads, *lands, send_sems, recv_sems, after)
    return outs[n:]


def _allreduce_small(vec):
    R, C = vec.shape

    def body(v_ref, o_ref, buf, send_sems, recv_sems):
        me = _mesh_pos()
        buf[_index(me)] = v_ref[...]
        sends = []
        for k in range(1, N_DEV):
            sends.append(pltpu.make_async_remote_copy(
                src_ref=buf.at[_index(me)], dst_ref=buf.at[_index(me)],
                send_sem=send_sems.at[k - 1], recv_sem=recv_sems.at[k - 1],
                device_id=_flip(me, k), device_id_type=MESH_ID))
        for cp in sends:
            cp.start()
        for cp in sends:
            cp.wait_recv()
        for cp in sends:
            cp.wait_send()
        acc = buf[0]
        for s in range(1, N_DEV):
            acc = acc + buf[s]
        o_ref[...] = acc

    return pl.pallas_call(
        body,
        name="allreduce_small",
        in_specs=[pl.BlockSpec(memory_space=pltpu.VMEM)],
        out_specs=pl.BlockSpec(memory_space=pltpu.VMEM),
        out_shape=jax.ShapeDtypeStruct((R, C), F32),
        scratch_shapes=[pltpu.VMEM((N_DEV, R, C), F32), pltpu.SemaphoreType.DMA((7,)), pltpu.SemaphoreType.DMA((7,))],
        compiler_params=pltpu.CompilerParams(has_side_effects=True, vmem_limit_bytes=VMEM_LIMIT_BYTES),
    )(vec)


def _pack_small(parts):
    flat = jnp.concatenate([p.reshape(-1) for p in parts])
    n = flat.shape[0]
    rows = -(-n // SMALL_COLS)
    rows = -(-rows // 8) * 8
    return jnp.pad(flat, (0, rows * SMALL_COLS - n)).reshape(rows, SMALL_COLS)


def _unpack_small(packed, like):
    flat = packed.reshape(-1)
    out, pos = [], 0
    for p in like:
        out.append(flat[pos:pos + p.size].reshape(p.shape))
        pos += p.size
    return out


def kernel(x, p, w_in, b_in, w_out, attn_sinks, rel_bias, w_pool, pool_scale, w_ple, w_gate_ple, ln_gain, ln_bias, loss_target, m_w_in, m_b_in, m_w_out, m_attn_sinks, m_rel_bias, m_w_pool, m_pool_scale, m_w_ple, m_w_gate_ple, m_ln_gain, m_ln_bias, v_w_in, v_b_in, v_w_out, v_attn_sinks, v_rel_bias, v_w_pool, v_pool_scale, v_w_ple, v_w_gate_ple, v_ln_gain, v_ln_bias):
    L = w_in.shape[0]
    S = x.shape[1]
    alpha = (2.0 * L) ** 0.25
    bucket_np, masks_np, window_np = _band_constants()
    bucket, masks, window = jnp.asarray(bucket_np), jnp.asarray(masks_np), jnp.asarray(window_np)

    w_in_s = jnp.swapaxes(w_in, 1, 2).astype(BF16)
    w_out_s = w_out.astype(BF16)
    w_gate_s = w_gate_ple.astype(BF16)
    w_ple_s = jnp.swapaxes(w_ple, 1, 2).astype(BF16)
    w_pool_s = w_pool.astype(BF16)

    def shards_of(l):
        return [w_in_s[l][None], w_out_s[l][None], w_gate_s[l][None], w_ple_s[l][None], w_pool_s[l]]

    def gathered(started, after, l):
        shards, lands = _gather_wait(started, after, name=f"gather_wait_{l}")
        full, token = _gather_pass(shards, lands, name=f"gather_pass_{l}")
        return (full[0][0], full[1][0], full[2][0], full[3][0], full[4]), token

    bias = _bias_build(rel_bias, bucket).reshape(N_KV_HEADS, KV_GROUP * BLOCK, 2 * BLOCK)

    xs = x[0]
    xb = xs.astype(BF16)
    weights, pass_token = gathered(_gather_start(shards_of(0), rel_bias, name="gather_start_0"), xb, 0)
    saved = []
    for l in range(L):
        w_in_t, w_out_g, w_gate_g, w_ple_t, w_pool_g = weights
        if l + 1 < L:
            started = _gather_start(shards_of(l + 1), pass_token, name=f"gather_start_{l + 1}")
        pb = p[l, 0].astype(BF16)
        sink_rows = jnp.repeat(attn_sinks[l], BLOCK).reshape(N_KV_HEADS, KV_GROUP * BLOCK, 1)
        scale_l = pool_scale[l].reshape(1, POOL_WIDTH)
        h = _matmul(xb, w_in_t, tb=True, tm=512, tn=2176, tk=2048, out_dtype=F32, bias=b_in[l].reshape(1, IN_COLS),
                    after=started[4] if l + 1 < L else None, name=f"in_proj_{l}")
        gp = _matmul(xb, w_gate_g, tm=512, tn=2048, tk=2048, out_dtype=F32, name=f"gate_proj_{l}")
        pe = _matmul(pb, w_ple_t, tb=True, tm=1024, tn=2048, tk=256, out_dtype=F32, name=f"ple_proj_{l}")
        ab = _attn_fwd(h, bias, masks, sink_rows, name=f"attn_fwd_{l}")
        ab = _pool_fwd(h, ab, w_pool_g, scale_l, name=f"pool_fwd_{l}")
        mix = _matmul(ab, w_out_g, tm=512, tn=2048, tk=2048, out_dtype=F32, name=f"out_proj_{l}")
        y, yb, xhat, rstd = _ln_fwd(xs, mix, gp, pe, ln_gain[l].reshape(1, D_MODEL), ln_bias[l].reshape(1, D_MODEL),
                                    alpha, name=f"ln_fwd_{l}")
        saved.append((xb, pb, h, gp, pe, ab, xhat, rstd, sink_rows, scale_l, weights))
        xs, xb = y, yb
        if l + 1 < L:
            weights, pass_token = gathered(started, yb, l + 1)

    dy, loss_tile = _loss_head(xs, loss_target[0])

    dbias = jnp.zeros((N_KV_HEADS, KV_GROUP * BLOCK, 2 * BLOCK), F32)
    lands = [lax.empty((N_DEV, L) + s.shape, BF16) for s in shards_of(0)]
    g_b_in, g_sinks, g_scale, g_gain, g_beta = [], [], [], [], []
    pending = None
    for l in reversed(range(L)):
        xb, pb, h, gp, pe, ab, xhat, rstd, sink_rows, scale_l, weights = saved[l]
        w_in_t, w_out_g, w_gate_g, w_ple_t, w_pool_g = weights
        dz, dzb, dpe, dgp, dgain, dbeta = _ln_bwd(dy, xhat, rstd, ln_gain[l].reshape(1, D_MODEL), gp, pe,
                                                  pending[0][4] if pending else rel_bias, name=f"ln_bwd_{l}")
        dab = _matmul(dzb, w_out_g, tb=True, tm=512, tn=2048, tk=2048, out_dtype=F32, name=f"dmix_{l}")
        g_w_out = _matmul(ab, dzb, ta=True, tm=1024, tn=2048, tk=1024, out_dtype=BF16, name=f"dw_out_{l}")
        g_w_gate = _matmul(xb, dgp, ta=True, tm=1024, tn=2048, tk=1024, out_dtype=BF16, name=f"dw_gate_{l}")
        g_w_ple_t = _matmul(dpe, pb, ta=True, tm=1024, tn=256, tk=1024, out_dtype=BF16, name=f"dw_ple_{l}")
        dh, dwp, dsc = _pool_bwd(h, dab, w_pool_g, scale_l, name=f"pool_bwd_{l}")
        dh, dbias, dsink = _attn_bwd(h, dab, dh, bias, masks, sink_rows, dbias, name=f"attn_bwd_{l}")
        g_w_in_t = _matmul(dh, xb, ta=True, tm=2176, tn=1024, tk=1024, out_dtype=BF16, name=f"dw_in_{l}")
        if pending:
            lands = _scatter_wait(pending[0], pending[1], g_w_in_t, name=f"scatter_wait_{pending[1]}")
        grads = [g_w_in_t[None], g_w_out[None], g_w_gate[None], g_w_ple_t[None], dwp.astype(BF16)]
        lands = _place_own(grads, lands, l, name=f"place_own_{l}")
        pending = (_scatter_start(grads, lands, l, name=f"scatter_start_{l}"), l)
        g_b_in.append(_col_sum(dh, name=f"db_in_{l}"))
        dx = _matmul(dgp, w_gate_g, tb=True, tm=512, tn=2048, tk=2048, out_dtype=F32, add=dz, add_scale=alpha,
                     after=pending[0][4], name=f"dx_gate_{l}")
        dy = _matmul(dh, w_in_t, tm=512, tn=1024, tk=4352, out_dtype=F32, add=dx, name=f"dx_in_{l}")
        g_sinks.append(dsink[:, :, 0].reshape(N_HEADS))
        g_scale.append(dsc.reshape(POOL_WIDTH))
        g_gain.append(dgain.reshape(D_MODEL))
        g_beta.append(dbeta.reshape(D_MODEL))
    grad_x = dy[None]
    for lst in (g_b_in, g_sinks, g_scale, g_gain, g_beta):
        lst.reverse()
    g_rel = _bias_bwd(dbias.reshape(N_HEADS, BLOCK, 2 * BLOCK), bucket, window)[:, :N_HEADS]

    r_in, r_out, r_gate, r_ple, r_pool = _scatter_wait(pending[0], pending[1], dy, name=f"scatter_wait_{pending[1]}")
    small_like = [b_in, attn_sinks, rel_bias, pool_scale, ln_gain, ln_bias]
    small_g = _allreduce_small(_pack_small([
        jnp.stack(g_b_in).reshape(L, IN_COLS), jnp.stack(g_sinks), g_rel, jnp.stack(g_scale), jnp.stack(g_gain),
        jnp.stack(g_beta), loss_tile[0, :1]]))

    gt_in = _sum_slabs(r_in.reshape(N_DEV, L * 544, D_MODEL), name="sum_w_in")
    grad_w_in = jnp.swapaxes(gt_in.reshape(L, 544, D_MODEL), 1, 2)
    grad_w_out = _sum_slabs(r_out.reshape(N_DEV, L * 256, D_MODEL), name="sum_w_out").reshape(L, 256, D_MODEL)
    grad_w_gate = _sum_slabs(r_gate.reshape(N_DEV, L * 256, D_MODEL), name="sum_w_gate").reshape(L, 256, D_MODEL)
    gt_ple = _sum_slabs(r_ple.reshape(N_DEV, L * 256, PLE_DIM), name="sum_w_ple")
    grad_w_ple = jnp.swapaxes(gt_ple.reshape(L, 256, PLE_DIM), 1, 2)
    grad_w_pool = _sum_slabs(r_pool.reshape(N_DEV, L * 4 * 32, 256), name="sum_w_pool").reshape(L, 4, 32, 256)

    def big(w, g, m, v, name):
        shape = w.shape
        two_d = (shape[0] * shape[1], shape[2]) if len(shape) == 3 else (shape[0] * shape[1] * shape[2], shape[3])
        d, nm, nv = _adamw(w.reshape(two_d), g.reshape(two_d), m.reshape(two_d), v.reshape(two_d), name=name)
        return d.reshape(shape), nm.reshape(shape), nv.reshape(shape)

    upd_in = big(w_in, grad_w_in, m_w_in, v_w_in, "adamw_w_in")
    upd_out = big(w_out, grad_w_out, m_w_out, v_w_out, "adamw_w_out")
    upd_pool = big(w_pool, grad_w_pool, m_w_pool, v_w_pool, "adamw_w_pool")
    upd_ple = big(w_ple, grad_w_ple, m_w_ple, v_w_ple, "adamw_w_ple")
    upd_gate = big(w_gate_ple, grad_w_gate, m_w_gate_ple, v_w_gate_ple, "adamw_w_gate")

    zero1 = jnp.zeros((1,), F32)
    sw = _pack_small(small_like + [zero1])
    sm = _pack_small([m_b_in, m_attn_sinks, m_rel_bias, m_pool_scale, m_ln_gain, m_ln_bias, zero1])
    sv = _pack_small([v_b_in, v_attn_sinks, v_rel_bias, v_pool_scale, v_ln_gain, v_ln_bias, zero1])
    sd, snm, snv = _adamw(sw, small_g, sm, sv, name="adamw_small")
    like = small_like + [zero1]
    sg_parts = _unpack_small(small_g, like)
    sd_parts, snm_parts, snv_parts = _unpack_small(sd, like), _unpack_small(snm, like), _unpack_small(snv, like)
    loss = sg_parts[6][0]

    def assemble(big_parts, small_parts):
        w_in_, w_out_, w_pool_, w_ple_, w_gate_ = big_parts
        b_in_, sinks_, rel_, scale_, gain_, beta_ = small_parts[:6]
        return [w_in_, b_in_, w_out_, sinks_, rel_, w_pool_, scale_, w_ple_, w_gate_, gain_, beta_]

    grads = assemble([grad_w_in, grad_w_out, grad_w_pool, grad_w_ple, grad_w_gate], sg_parts)
    ups = [upd_in, upd_out, upd_pool, upd_ple, upd_gate]
    deltas = assemble([u[0] for u in ups], sd_parts)
    new_m = assemble([u[1] for u in ups], snm_parts)
    new_v = assemble([u[2] for u in ups], snv_parts)
    return (loss, grad_x, *grads, *deltas, *new_m, *new_v)
```

```python
import functools
import math

import numpy as np
import jax
import jax.numpy as jnp
from jax import lax
from jax.experimental import pallas as pl
from jax.experimental.pallas import tpu as pltpu

F32 = jnp.float32
BF16 = jnp.bfloat16

D_MODEL = 2048
PLE_DIM = 256
ATTN_WIDTH = 1024
POOL_WIDTH = 1024
HEAD_DIM = 64
N_HEADS = 16
N_KV_HEADS = 2
KV_GROUP = 8
WINDOW = 128
BLOCK = 128
POOL_WINDOWS = (2, 4, 8, 16)
POOL_GROUP_DIM = 256
POOL_HALO = 16
REL_BUCKETS = 32
REL_MAX_DIST = 128
LN_EPS = 1e-5
KV_COLS = N_KV_HEADS * HEAD_DIM
IN_COLS = 4352
Q_OFF, KV_OFF, GA_OFF, U_OFF, GB_OFF = 0, 1024, 1280, 2304, 3328
ATTN_SCALE = 1.0 / math.sqrt(HEAD_DIM)
NEG_BIG = -1e30
LANES = 128

ADAM_LR = 0.001
ADAM_B1 = 0.9
ADAM_B2 = 0.999
ADAM_EPS = 1e-08
ADAM_WD = 0.01
ADAM_STEP = 10

N_DEV = 8
MESH_ID = pl.DeviceIdType.MESH
VMEM_LIMIT_BYTES = 52 * 1024 * 1024
SMALL_COLS = 1024


def _params(sem=None):
    return pltpu.CompilerParams(dimension_semantics=sem, vmem_limit_bytes=VMEM_LIMIT_BYTES)


def _sigmoid(x):
    return 1.0 / (1.0 + jnp.exp(-x))


def _tile(n, pref, unit=16):
    if n <= pref:
        return n
    t = pref - pref % unit
    while n % t:
        t -= unit
    assert t > 0, (n, pref)
    return t


def _matmul(a, b, *, name, ta=False, tb=False, tm, tn, tk, out_dtype, bias=None, add=None, add_scale=1.0, after=None):
    M, K = (a.shape[1], a.shape[0]) if ta else a.shape
    N = b.shape[0] if tb else b.shape[1]
    assert (b.shape[1] if tb else b.shape[0]) == K
    tm, tn, tk = _tile(M, tm), _tile(N, tn), _tile(K, tk)
    nm, nn, nk = M // tm, N // tn, K // tk
    a_spec = pl.BlockSpec((tk, tm), lambda j, i, k: (k, i)) if ta else pl.BlockSpec((tm, tk), lambda j, i, k: (i, k))
    b_spec = pl.BlockSpec((tn, tk), lambda j, i, k: (j, k)) if tb else pl.BlockSpec((tk, tn), lambda j, i, k: (k, j))
    dims = (((0 if ta else 1,), (1 if tb else 0,)), ((), ()))
    operands, in_specs = [a, b], [a_spec, b_spec]
    if bias is not None:
        operands.append(bias)
        in_specs.append(pl.BlockSpec((1, tn), lambda j, i, k: (0, j)))
    if add is not None:
        operands.append(add)
        in_specs.append(pl.BlockSpec((tm, tn), lambda j, i, k: (i, j)))
    if after is not None:
        operands.append(after)
        in_specs.append(pl.BlockSpec(memory_space=pl.ANY))

    def body(*refs):
        a_ref, b_ref = refs[0], refs[1]
        pos = 2
        bias_ref = add_ref = None
        if bias is not None:
            bias_ref = refs[pos]
            pos += 1
        if add is not None:
            add_ref = refs[pos]
            pos += 1
        if after is not None:
            pos += 1
        o_ref = refs[pos]
        part = lax.dot_general(a_ref[...].astype(BF16), b_ref[...].astype(BF16), dims, preferred_element_type=F32)

        def finish(acc):
            if bias_ref is not None:
                acc = acc + bias_ref[...]
            if add_ref is not None:
                acc = acc + add_scale * add_ref[...].astype(F32)
            o_ref[...] = acc.astype(out_dtype)

        if nk == 1:
            finish(part)
        else:
            acc_ref = refs[pos + 1]
            k = pl.program_id(2)

            @pl.when(k == 0)
            def _():
                acc_ref[...] = part

            @pl.when(k > 0)
            def _():
                acc_ref[...] += part

            @pl.when(k == nk - 1)
            def _():
                finish(acc_ref[...])

    return pl.pallas_call(
        body,
        name=name,
        grid=(nn, nm, nk),
        in_specs=in_specs,
        out_specs=pl.BlockSpec((tm, tn), lambda j, i, k: (i, j)),
        out_shape=jax.ShapeDtypeStruct((M, N), out_dtype),
        scratch_shapes=[pltpu.VMEM((tm, tn), F32)] if nk > 1 else [],
        compiler_params=_params(("parallel", "parallel", "arbitrary")),
    )(*operands)


BAND = (2 * BLOCK, BLOCK)


def _band_constants():
    qq = np.arange(BLOCK)[None, :]
    kk = np.arange(2 * BLOCK)[:, None]
    dist = qq + BLOCK - kk
    in_window = (dist >= 0) & (dist < WINDOW)
    max_exact = REL_BUCKETS // 2
    d = np.maximum(dist, 0)
    d_f = np.maximum(d, 1).astype(np.float32)
    large = max_exact + (
        np.log(d_f / np.float32(max_exact)) / np.float32(math.log(REL_MAX_DIST / max_exact)) * np.float32(REL_BUCKETS - max_exact)
    ).astype(np.int32)
    large = np.minimum(large, REL_BUCKETS - 1)
    bucket = np.where(d < max_exact, d, large).astype(np.int32)
    bucket = np.where(in_window, bucket, 0).astype(np.int32)
    first = in_window & (kk >= BLOCK)
    masks = np.stack([first, in_window]).astype(np.float32)
    return bucket, masks, in_window.astype(np.float32)


def _bias_build(rel_bias, bucket):
    def body(rb_ref, bkt_ref, o_ref):
        h = pl.program_id(0)
        bkt = bkt_ref[...]

        def step(b, acc):
            return jnp.where(bkt == b, rb_ref[b, h], acc)

        o_ref[0] = lax.fori_loop(0, REL_BUCKETS, step, jnp.zeros(BAND, F32))

    return pl.pallas_call(
        body,
        name="bias_build",
        grid=(N_HEADS,),
        in_specs=[pl.BlockSpec(memory_space=pltpu.SMEM), pl.BlockSpec(BAND, lambda h: (0, 0))],
        out_specs=pl.BlockSpec((1,) + BAND, lambda h: (h, 0, 0)),
        out_shape=jax.ShapeDtypeStruct((N_HEADS,) + BAND, F32),
        compiler_params=_params(("arbitrary",)),
    )(rel_bias, bucket)


def _bias_bwd(dbias, bucket, window):
    def body(db_ref, bkt_ref, win_ref, o_ref):
        h = pl.program_id(0)

        @pl.when(h == 0)
        def _():
            o_ref[...] = jnp.zeros_like(o_ref)

        bkt = bkt_ref[...]
        x = jnp.where(win_ref[...] > 0.5, db_ref[0], 0.0)
        row = lax.broadcasted_iota(jnp.int32, (REL_BUCKETS, LANES), 0)
        col = lax.broadcasted_iota(jnp.int32, (REL_BUCKETS, LANES), 1)

        def step(b, acc):
            s = jnp.sum(jnp.where(bkt == b, x, 0.0), axis=1, keepdims=True)
            s = jnp.sum(s, axis=0, keepdims=True)
            return acc + jnp.where((row == b) & (col == h), s, 0.0)

        o_ref[...] += lax.fori_loop(0, REL_BUCKETS, step, jnp.zeros((REL_BUCKETS, LANES), F32))

    return pl.pallas_call(
        body,
        name="bias_bwd",
        grid=(N_HEADS,),
        in_specs=[
            pl.BlockSpec((1,) + BAND, lambda h: (h, 0, 0)),
            pl.BlockSpec(BAND, lambda h: (0, 0)),
            pl.BlockSpec(BAND, lambda h: (0, 0)),
        ],
        out_specs=pl.BlockSpec((REL_BUCKETS, LANES), lambda h: (0, 0)),
        out_shape=jax.ShapeDtypeStruct((REL_BUCKETS, LANES), F32),
        compiler_params=_params(("arbitrary",)),
    )(dbias, bucket, window)


def _lane_lo(shape):
    return lax.broadcasted_iota(jnp.int32, shape, 1) < HEAD_DIM


def _row_lo(shape):
    return lax.broadcasted_iota(jnp.int32, shape, 0) < HEAD_DIM


def _dup_heads(x):
    r = pltpu.roll(x, HEAD_DIM, axis=1)
    lo = _lane_lo(x.shape)
    return jnp.where(lo, x, r), jnp.where(lo, r, x)


def _kv_operands(kvp_ref, kvc_ref):
    kvp, kvc = kvp_ref[...], kvc_ref[...]
    k2 = jnp.concatenate([kvp[:, :KV_COLS], kvc[:, :KV_COLS]], axis=0)
    v2 = jnp.concatenate([kvp[:, KV_COLS:], kvc[:, KV_COLS:]], axis=0)
    return _dup_heads(k2), _dup_heads(v2)


def _head_probs(k_r, qs_t, bias, mask, sink):
    s = jnp.dot(k_r, qs_t, preferred_element_type=F32) * ATTN_SCALE + bias
    s = jnp.where(mask, s, NEG_BIG)
    m = jnp.maximum(jnp.max(s, axis=0, keepdims=True), sink)
    e = jnp.exp(s - m)
    e_sink = jnp.exp(sink - m)
    inv = 1.0 / (jnp.sum(e, axis=0, keepdims=True) + e_sink)
    return e * inv, e_sink * inv


def _gate_cols(ga_refs, pair):
    off = LANES * (pair % 2)
    return ga_refs[pair // 2][:, off:off + LANES]


def _attn_specs(order):
    return [
        pl.BlockSpec((BLOCK, ATTN_WIDTH), lambda t: (order(t), Q_OFF // ATTN_WIDTH)),
        pl.BlockSpec((BLOCK, 2 * KV_COLS), lambda t: (order(t), KV_OFF // (2 * KV_COLS))),
        pl.BlockSpec((BLOCK, 2 * KV_COLS), lambda t: (jnp.maximum(order(t) - 1, 0), KV_OFF // (2 * KV_COLS))),
    ] + [
        pl.BlockSpec((BLOCK, 256), functools.partial(lambda t, c: (order(t), GA_OFF // 256 + c), c=c)) for c in range(4)
    ] + [
        pl.BlockSpec((N_KV_HEADS, KV_GROUP * 2 * BLOCK, BLOCK), lambda t: (0, 0, 0)),
        pl.BlockSpec((None,) + BAND, lambda t: (jnp.minimum(order(t), 1), 0, 0)),
        pl.BlockSpec(memory_space=pltpu.SMEM),
    ]


def _attn_fwd(h, bias, masks, sinks, name):
    S = h.shape[0]
    nb = S // BLOCK

    def body(q_ref, kvc_ref, kvp_ref, ga0, ga1, ga2, ga3, bias_ref, mask_ref, sink_ref, o_ref):
        kd, vd = _kv_operands(kvp_ref, kvc_ref)
        mask = mask_ref[...] > 0.5
        lo = _row_lo((LANES, BLOCK))
        for g in range(N_KV_HEADS):
            k_r = kd[g].astype(BF16)
            v_t = vd[g].T.astype(BF16)
            for pr in range(KV_GROUP // 2):
                pair = (KV_GROUP // 2) * g + pr
                qp_t = q_ref[:, LANES * pair:LANES * (pair + 1)].T
                outs = []
                for hh in range(2):
                    j = 2 * pr + hh
                    qs_t = jnp.where(lo if hh == 0 else ~lo, qp_t, 0.0).astype(BF16)
                    p, _ = _head_probs(k_r, qs_t, bias_ref[g, 2 * BLOCK * j:2 * BLOCK * (j + 1), :], mask,
                                       sink_ref[KV_GROUP * g + j])
                    outs.append(jnp.dot(v_t, p.astype(BF16), preferred_element_type=F32))
                ga = _gate_cols((ga0, ga1, ga2, ga3), pair)
                o_ref[:, LANES * pair:LANES * (pair + 1)] = (
                    jnp.where(lo, outs[0], outs[1]).T * (ga * _sigmoid(ga))).astype(BF16)

    return pl.pallas_call(
        body,
        name=name,
        grid=(nb,),
        in_specs=_attn_specs(lambda t: t),
        out_specs=pl.BlockSpec((BLOCK, ATTN_WIDTH), lambda t: (t, 0)),
        out_shape=jax.ShapeDtypeStruct((S, ATTN_WIDTH + POOL_WIDTH), BF16),
        compiler_params=_params(("arbitrary",)),
    )(h, h, h, h, h, h, h, bias, masks, sinks)


DH_ATTN_COLS = U_OFF
BIAS_ROWS = KV_GROUP * 2 * BLOCK


def _attn_bwd(h, dab, dh, bias, masks, sinks, dbias_in, name):
    S = h.shape[0]
    nb = S // BLOCK

    def order(t):
        return nb - 1 - t

    def body(q_ref, kvc_ref, kvp_ref, ga0, ga1, ga2, ga3, bias_ref, mask_ref, sink_ref, da_ref, dbin_ref, dh_in_ref,
             dh_ref, dbias_ref, dsink_ref, carry_scr):
        del dh_in_ref
        t = pl.program_id(0)

        @pl.when(t == 0)
        def _():
            dbias_ref[...] = dbin_ref[...]
            dsink_ref[...] = jnp.zeros_like(dsink_ref)
            carry_scr[...] = jnp.zeros_like(carry_scr)

        kd, vd = _kv_operands(kvp_ref, kvc_ref)
        mask = mask_ref[...] > 0.5
        lo = _lane_lo((BLOCK, LANES))
        lo_t = _row_lo((LANES, BLOCK))
        dk_tot, dv_tot = [], []
        for g in range(N_KV_HEADS):
            k_t, k_r = kd[g].T.astype(BF16), kd[g].astype(BF16)
            v_t, v_r = vd[g].T.astype(BF16), vd[g].astype(BF16)
            dk = jnp.zeros((2 * BLOCK, LANES), F32)
            dv = jnp.zeros((2 * BLOCK, LANES), F32)
            for pr in range(KV_GROUP // 2):
                pair = (KV_GROUP // 2) * g + pr
                cols = slice(LANES * pair, LANES * (pair + 1))
                qp = q_ref[:, cols]
                qp_t = qp.T
                ga = _gate_cols((ga0, ga1, ga2, ga3), pair)
                sg = _sigmoid(ga)
                da = da_ref[:, cols]
                do_p = da * (ga * sg)
                do_t = do_p.T
                outs, dqs = [], []
                for hh in range(2):
                    j = 2 * pr + hh
                    rows = slice(2 * BLOCK * j, 2 * BLOCK * (j + 1))
                    half, half_t = (lo, lo_t) if hh == 0 else (~lo, ~lo_t)
                    qs_t = jnp.where(half_t, qp_t, 0.0).astype(BF16)
                    p, p_sink = _head_probs(k_r, qs_t, bias_ref[g, rows, :], mask, sink_ref[KV_GROUP * g + j])
                    pb = p.astype(BF16)
                    outs.append(jnp.dot(v_t, pb, preferred_element_type=F32))
                    dos_t = jnp.where(half_t, do_t, 0.0).astype(BF16)
                    dp = jnp.dot(v_r, dos_t, preferred_element_type=F32)
                    dsum = jnp.sum(p * dp, axis=0, keepdims=True)
                    ds = p * (dp - dsum)
                    dbias_ref[g, rows, :] += ds
                    tot = jnp.sum(-(p_sink * dsum), axis=1, keepdims=True)
                    dsink_ref[g, j:j + 1, :] += jnp.broadcast_to(tot, (1, LANES))
                    dsb = ds.astype(BF16)
                    dqs.append(jnp.dot(k_t, dsb, preferred_element_type=F32))
                    dk = dk + jnp.dot(dsb, jnp.where(half, qp, 0.0).astype(BF16), preferred_element_type=F32)
                    dv = dv + jnp.dot(pb, jnp.where(half, do_p, 0.0).astype(BF16), preferred_element_type=F32)
                attn = jnp.where(lo_t, outs[0], outs[1]).T
                dh_ref[:, cols] = (jnp.where(lo_t, dqs[0], dqs[1]).T * ATTN_SCALE).astype(BF16)
                dh_ref[:, GA_OFF + LANES * pair:GA_OFF + LANES * (pair + 1)] = (
                    da * attn * (sg * (1.0 + ga * (1.0 - sg)))).astype(BF16)
            dk = dk * ATTN_SCALE
            dk_tot.append(dk + pltpu.roll(dk, HEAD_DIM, axis=1))
            dv_tot.append(dv + pltpu.roll(dv, HEAD_DIM, axis=1))
        lo2 = _lane_lo((2 * BLOCK, LANES))
        dkv = jnp.concatenate([jnp.where(lo2, dk_tot[0], dk_tot[1]), jnp.where(lo2, dv_tot[0], dv_tot[1])], axis=1)
        dh_ref[:, KV_OFF:KV_OFF + 2 * KV_COLS] = (dkv[BLOCK:, :] + carry_scr[...]).astype(BF16)
        carry_scr[...] = dkv[:BLOCK, :]

    n_in = 12
    return pl.pallas_call(
        body,
        name=name,
        grid=(nb,),
        in_specs=_attn_specs(order) + [
            pl.BlockSpec((BLOCK, ATTN_WIDTH), lambda t: (order(t), 0)),
            pl.BlockSpec((N_KV_HEADS, BIAS_ROWS, BLOCK), lambda t: (0, 0, 0)),
            pl.BlockSpec(memory_space=pl.ANY),
        ],
        out_specs=[
            pl.BlockSpec((BLOCK, DH_ATTN_COLS), lambda t: (order(t), 0)),
            pl.BlockSpec((N_KV_HEADS, BIAS_ROWS, BLOCK), lambda t: (0, 0, 0)),
            pl.BlockSpec((N_KV_HEADS, KV_GROUP, LANES), lambda t: (0, 0, 0)),
        ],
        out_shape=[
            jax.ShapeDtypeStruct((S, IN_COLS), BF16),
            jax.ShapeDtypeStruct((N_KV_HEADS, BIAS_ROWS, BLOCK), F32),
            jax.ShapeDtypeStruct((N_KV_HEADS, KV_GROUP, LANES), F32),
        ],
        scratch_shapes=[pltpu.VMEM((BLOCK, 2 * KV_COLS), F32)],
        input_output_aliases={n_in: 0},
        compiler_params=_params(("arbitrary",)),
    )(h, h, h, h, h, h, h, bias, masks, sinks, dab, dbias_in, dh)


def _window_sum(x, w, back):
    n = x.shape[0]
    s, sh = x, 1
    while sh < w:
        s = s + pltpu.roll(s, sh if back else n - sh, axis=0)
        sh *= 2
    return s


def _pool_counts(first_row, n, w):
    t = first_row + lax.broadcasted_iota(jnp.int32, (n, 1), 0)
    return jnp.minimum(t + 1, w).astype(F32)


def _pool_diff(u_ref, uh_ref, i, T, g):
    u = u_ref[...]
    halo = jnp.where(i > 0, uh_ref[...], 0.0)
    ext = jnp.concatenate([halo, u], axis=0)
    w = POOL_WINDOWS[g]
    s = _window_sum(ext, w, back=True)[POOL_HALO:, :]
    return s / _pool_counts(i * T, T, w) - u


def _pool_in_specs(T):
    hb = T // POOL_HALO
    specs = []
    for g in range(4):
        specs.append(pl.BlockSpec((T, 256), functools.partial(lambda i, g: (i, U_OFF // 256 + g), g=g)))
        specs.append(pl.BlockSpec((POOL_HALO, 256), functools.partial(
            lambda i, g: (jnp.maximum(i * hb - 1, 0), U_OFF // 256 + g), g=g)))
    return specs


def _pool_weight_specs():
    return [pl.BlockSpec((4, 256, 256), lambda i: (0, 0, 0)), pl.BlockSpec((1, POOL_WIDTH), lambda i: (0, 0))]


def _pool_fwd(h, ab, w_pool, pool_scale, name):
    S = h.shape[0]
    T = _tile(S, 512)

    def body(*refs):
        u_refs = refs[0:8]
        gb_refs = refs[8:12]
        wp_ref, sc_ref, o_ref = refs[12], refs[13], refs[15]
        i = pl.program_id(0)
        for g in range(4):
            diff = _pool_diff(u_refs[2 * g], u_refs[2 * g + 1], i, T, g).astype(BF16)
            mixed = jnp.dot(diff, wp_ref[g], preferred_element_type=F32) * sc_ref[:, 256 * g:256 * (g + 1)]
            gb = gb_refs[g][...]
            o_ref[:, 256 * g:256 * (g + 1)] = (mixed * (gb * _sigmoid(gb))).astype(BF16)

    in_specs = _pool_in_specs(T) + [
        pl.BlockSpec((T, 256), functools.partial(lambda i, g: (i, GB_OFF // 256 + g), g=g)) for g in range(4)
    ] + _pool_weight_specs() + [pl.BlockSpec(memory_space=pl.ANY)]
    return pl.pallas_call(
        body,
        name=name,
        grid=(S // T,),
        in_specs=in_specs,
        out_specs=pl.BlockSpec((T, POOL_WIDTH), lambda i: (i, 1)),
        out_shape=jax.ShapeDtypeStruct(ab.shape, BF16),
        input_output_aliases={14: 0},
        compiler_params=_params(("arbitrary",)),
    )(*([h] * 12), w_pool, pool_scale, ab)


DH_POOL_COLS = IN_COLS // 2


def _pool_bwd(h, dab, w_pool, pool_scale, name):
    S = h.shape[0]
    T = _tile(S, 512)
    nt = S // T
    hb = T // POOL_HALO
    E = T + POOL_HALO
    lead = U_OFF - DH_POOL_COLS

    def body(*refs):
        u_refs = refs[0:8]
        gb_refs = refs[8:16]
        db_refs = refs[16:24]
        wp_ref, sc_ref = refs[24], refs[25]
        dh_ref, dwp_ref, dsc_ref = refs[26:29]
        i = pl.program_id(0)

        @pl.when(i == 0)
        def _():
            dwp_ref[...] = jnp.zeros_like(dwp_ref)
            dsc_ref[...] = jnp.zeros_like(dsc_ref)

        dh_ref[:, 0:lead] = jnp.zeros((T, lead), BF16)
        for g in range(4):
            w = POOL_WINDOWS[g]
            cols = slice(256 * g, 256 * (g + 1))
            scale = sc_ref[:, cols]
            wp = wp_ref[g]
            diff = _pool_diff(u_refs[2 * g], u_refs[2 * g + 1], i, T, g).astype(BF16)
            mixed = jnp.dot(diff, wp, preferred_element_type=F32)
            keep = i < nt - 1
            gb = jnp.concatenate([gb_refs[2 * g][...], jnp.where(keep, gb_refs[2 * g + 1][...], 0.0)], axis=0)
            db = jnp.concatenate([db_refs[2 * g][...], jnp.where(keep, db_refs[2 * g + 1][...], 0.0)], axis=0)
            sg = _sigmoid(gb)
            dms = db * (gb * sg)
            dmixed = (dms * scale).astype(BF16)
            ddiff = lax.dot_general(dmixed, wp, (((1,), (1,)), ((), ())), preferred_element_type=F32)
            r = ddiff / _pool_counts(i * T, E, w)
            du = _window_sum(r, w, back=False)[:T, :] - ddiff[:T, :]
            dh_ref[:, lead + 256 * g:lead + 256 * (g + 1)] = du.astype(BF16)
            dh_ref[:, lead + POOL_WIDTH + 256 * g:lead + POOL_WIDTH + 256 * (g + 1)] = (
                db[:T, :] * (mixed * scale) * (sg[:T, :] * (1.0 + gb[:T, :] * (1.0 - sg[:T, :])))).astype(BF16)
            dsc_ref[:, cols] += jnp.sum(dms[:T, :] * mixed, axis=0, keepdims=True)
            dwp_ref[g] += lax.dot_general(diff, dmixed[:T, :], (((0,), (0,)), ((), ())), preferred_element_type=F32)

    def after(i):
        return jnp.minimum((i + 1) * hb, S // POOL_HALO - 1)

    in_specs = _pool_in_specs(T)
    for off in (GB_OFF // 256, ATTN_WIDTH // 256):
        for g in range(4):
            in_specs.append(pl.BlockSpec((T, 256), functools.partial(lambda i, c: (i, c), c=off + g)))
            in_specs.append(pl.BlockSpec((POOL_HALO, 256), functools.partial(lambda i, c: (after(i), c), c=off + g)))
    in_specs += _pool_weight_specs()
    return pl.pallas_call(
        body,
        name=name,
        grid=(nt,),
        in_specs=in_specs,
        out_specs=[
            pl.BlockSpec((T, DH_POOL_COLS), lambda i: (i, 1)),
            pl.BlockSpec((4, 256, 256), lambda i: (0, 0, 0)),
            pl.BlockSpec((1, POOL_WIDTH), lambda i: (0, 0)),
        ],
        out_shape=[
            jax.ShapeDtypeStruct((S, IN_COLS), BF16),
            jax.ShapeDtypeStruct((4, 256, 256), F32),
            jax.ShapeDtypeStruct((1, POOL_WIDTH), F32),
        ],
        compiler_params=_params(("arbitrary",)),
    )(*([h] * 16), *([dab] * 8), w_pool, pool_scale)


def _ln_fwd(x, mix, gp, pe, gain, bias, alpha, name):
    S = x.shape[0]
    T = _tile(S, 256)

    def body(x_ref, mix_ref, gp_ref, pe_ref, g_ref, b_ref, y_ref, yb_ref, xh_ref, rs_ref):
        z = alpha * x_ref[...] + mix_ref[...] + _sigmoid(gp_ref[...]) * pe_ref[...]
        mu = jnp.mean(z, axis=-1, keepdims=True)
        zc = z - mu
        var = jnp.mean(zc * zc, axis=-1, keepdims=True)
        rstd = lax.rsqrt(var + LN_EPS)
        xhat = zc * rstd
        y = xhat * g_ref[...] + b_ref[...]
        y_ref[...] = y
        yb_ref[...] = y.astype(BF16)
        xh_ref[...] = xhat
        rs_ref[...] = rstd

    row = pl.BlockSpec((T, D_MODEL), lambda i: (i, 0))
    vec = pl.BlockSpec((1, D_MODEL), lambda i: (0, 0))
    return pl.pallas_call(
        body,
        name=name,
        grid=(S // T,),
        in_specs=[row, row, row, row, vec, vec],
        out_specs=[row, row, row, pl.BlockSpec((T, 1), lambda i: (i, 0))],
        out_shape=[
            jax.ShapeDtypeStruct((S, D_MODEL), F32),
            jax.ShapeDtypeStruct((S, D_MODEL), BF16),
            jax.ShapeDtypeStruct((S, D_MODEL), F32),
            jax.ShapeDtypeStruct((S, 1), F32),
        ],
        compiler_params=_params(("parallel",)),
    )(x, mix, gp, pe, gain, bias)


def _ln_bwd(dy, xhat, rstd, gain, gp, pe, after, name):
    S = dy.shape[0]
    T = _tile(S, 256)

    def body(dy_ref, xh_ref, rs_ref, g_ref, gp_ref, pe_ref, after_ref, dz_ref, dzb_ref, dpe_ref, dgp_ref, dgain_ref, dbias_ref):
        del after_ref
        @pl.when(pl.program_id(0) == 0)
        def _():
            dgain_ref[...] = jnp.zeros_like(dgain_ref)
            dbias_ref[...] = jnp.zeros_like(dbias_ref)

        dy = dy_ref[...]
        xhat = xh_ref[...]
        dyg = dy * g_ref[...]
        c1 = jnp.mean(dyg, axis=-1, keepdims=True)
        c2 = jnp.mean(dyg * xhat, axis=-1, keepdims=True)
        dz = rs_ref[...] * (dyg - c1 - xhat * c2)
        dgain_ref[...] += jnp.sum(dy * xhat, axis=0, keepdims=True)
        dbias_ref[...] += jnp.sum(dy, axis=0, keepdims=True)
        sg = _sigmoid(gp_ref[...])
        dz_ref[...] = dz
        dzb_ref[...] = dz.astype(BF16)
        dpe_ref[...] = (dz * sg).astype(BF16)
        dgp_ref[...] = (dz * pe_ref[...] * (sg * (1.0 - sg))).astype(BF16)

    row = pl.BlockSpec((T, D_MODEL), lambda i: (i, 0))
    vec = pl.BlockSpec((1, D_MODEL), lambda i: (0, 0))
    return pl.pallas_call(
        body,
        name=name,
        grid=(S // T,),
        in_specs=[row, row, pl.BlockSpec((T, 1), lambda i: (i, 0)), vec, row, row, pl.BlockSpec(memory_space=pl.ANY)],
        out_specs=[row, row, row, row, vec, vec],
        out_shape=[
            jax.ShapeDtypeStruct((S, D_MODEL), F32),
            jax.ShapeDtypeStruct((S, D_MODEL), BF16),
            jax.ShapeDtypeStruct((S, D_MODEL), BF16),
            jax.ShapeDtypeStruct((S, D_MODEL), BF16),
            jax.ShapeDtypeStruct((1, D_MODEL), F32),
            jax.ShapeDtypeStruct((1, D_MODEL), F32),
        ],
        compiler_params=_params(("arbitrary",)),
    )(dy, xhat, rstd, gain, gp, pe, after)


def _loss_head(y, target):
    S = y.shape[0]
    T = _tile(S, 256)

    def body(y_ref, t_ref, dy_ref, l_ref):
        @pl.when(pl.program_id(0) == 0)
        def _():
            l_ref[...] = jnp.zeros_like(l_ref)

        err = y_ref[...] - t_ref[...]
        dy_ref[...] = err * (1.0 / D_MODEL)
        per_token = jnp.mean(err * err, axis=-1, keepdims=True)
        l_ref[...] += 0.5 * jnp.sum(per_token, axis=0, keepdims=True)

    row = pl.BlockSpec((T, D_MODEL), lambda i: (i, 0))
    return pl.pallas_call(
        body,
        name="loss_head",
        grid=(S // T,),
        in_specs=[row, row],
        out_specs=[row, pl.BlockSpec((8, LANES), lambda i: (0, 0))],
        out_shape=[jax.ShapeDtypeStruct((S, D_MODEL), F32), jax.ShapeDtypeStruct((8, LANES), F32)],
        compiler_params=_params(("arbitrary",)),
    )(y, target)


def _col_sum(a, name):
    S, C = a.shape
    T = _tile(S, 512)

    def body(a_ref, o_ref):
        @pl.when(pl.program_id(0) == 0)
        def _():
            o_ref[...] = jnp.zeros_like(o_ref)

        o_ref[...] += jnp.sum(a_ref[...].astype(F32), axis=0, keepdims=True)

    return pl.pallas_call(
        body,
        name=name,
        grid=(S // T,),
        in_specs=[pl.BlockSpec((T, C), lambda i: (i, 0))],
        out_specs=pl.BlockSpec((1, C), lambda i: (0, 0)),
        out_shape=jax.ShapeDtypeStruct((1, C), F32),
        compiler_params=_params(("arbitrary",)),
    )(a)


def _sum_slabs(r, name):
    _, R, C = r.shape
    T = _tile(R, 256)

    def body(r_ref, o_ref):
        acc = r_ref[0].astype(F32)
        for s in range(1, N_DEV):
            acc = acc + r_ref[s].astype(F32)
        o_ref[...] = acc

    return pl.pallas_call(
        body,
        name=name,
        grid=(R // T,),
        in_specs=[pl.BlockSpec((N_DEV, T, C), lambda i: (0, i, 0))],
        out_specs=pl.BlockSpec((T, C), lambda i: (i, 0)),
        out_shape=jax.ShapeDtypeStruct((R, C), F32),
        compiler_params=_params(("parallel",)),
    )(r)


def _adamw(w, g, m, v, name):
    R, C = w.shape
    T = _tile(R, 256)

    def body(w_ref, g_ref, m_ref, v_ref, d_ref, nm_ref, nv_ref):
        g = g_ref[...]
        m = ADAM_B1 * m_ref[...] + (1.0 - ADAM_B1) * g
        v = ADAM_B2 * v_ref[...] + (1.0 - ADAM_B2) * jnp.square(g)
        m_hat = m / (1.0 - ADAM_B1 ** ADAM_STEP)
        v_hat = v / (1.0 - ADAM_B2 ** ADAM_STEP)
        d_ref[...] = -ADAM_LR * (m_hat / (jnp.sqrt(v_hat) + ADAM_EPS) + ADAM_WD * w_ref[...])
        nm_ref[...] = m
        nv_ref[...] = v

    blk = pl.BlockSpec((T, C), lambda i: (i, 0))
    shp = jax.ShapeDtypeStruct((R, C), F32)
    return pl.pallas_call(
        body,
        name=name,
        grid=(R // T,),
        in_specs=[blk] * 4,
        out_specs=[blk] * 3,
        out_shape=[shp] * 3,
        compiler_params=_params(("parallel",)),
    )(w, g, m, v)


def _mesh_pos():
    return lax.axis_index("x"), lax.axis_index("y"), lax.axis_index("c")


def _flip(pos, k):
    x, y, c = pos
    return (1 - x if k & 4 else x, 1 - y if k & 2 else y, 1 - c if k & 1 else c)


def _index(pos):
    return 4 * pos[0] + 2 * pos[1] + pos[2]


HBM_SPEC = pl.BlockSpec(memory_space=pltpu.HBM)
SEM_SPEC = pl.BlockSpec(memory_space=pltpu.SEMAPHORE)
ANY_SPEC = pl.BlockSpec(memory_space=pl.ANY)
SPLIT_EFFECT = pltpu.SideEffectType.DATAFLOW_SIDE_EFFECTING
GATHER_FLIPS = (1, 4, 2, 6)
CHIP_FLIPS = (4, 2, 6)
TOKEN = jax.ShapeDtypeStruct((8, LANES), F32)


def _hbm(a):
    return pltpu.with_memory_space_constraint(a, pltpu.HBM)


def _hbm_like(a):
    return pltpu.HBM(a.shape, a.dtype)


def _block_rows(ref, pos, r):
    return ref.at[:, pl.ds(_index(pos) * r, r), :]


def _gather_start(shards, after, name):
    n = len(shards)
    lands = [lax.empty((s.shape[0], N_DEV * s.shape[1], s.shape[2]), s.dtype) for s in shards]

    def body(*refs):
        ins, bufs = refs[:n], refs[n:2 * n]
        send_sems, recv_sems = refs[2 * n + 1], refs[2 * n + 2]
        token = refs[4 * n + 3]
        me = _mesh_pos()
        for a in range(n):
            for j, k in enumerate(GATHER_FLIPS):
                pltpu.make_async_remote_copy(
                    src_ref=ins[a], dst_ref=_block_rows(bufs[a], me, shards[a].shape[1]),
                    send_sem=send_sems.at[4 * a + j], recv_sem=recv_sems.at[4 * a + j],
                    device_id=_flip(me, k), device_id_type=MESH_ID).start()
        token[...] = jnp.zeros_like(token)

    outs = pl.pallas_call(
        body,
        name=name,
        in_specs=[HBM_SPEC] * (2 * n) + [ANY_SPEC],
        out_specs=[SEM_SPEC, SEM_SPEC] + [HBM_SPEC] * (2 * n) + [pl.BlockSpec(memory_space=pltpu.VMEM)],
        out_shape=[pltpu.SemaphoreType.DMA((4 * n,)), pltpu.SemaphoreType.DMA((4 * n,))]
        + [_hbm_like(s) for s in shards] + [_hbm_like(b) for b in lands] + [TOKEN],
        input_output_aliases={i: 2 + i for i in range(2 * n)},
        compiler_params=pltpu.CompilerParams(has_side_effects=SPLIT_EFFECT),
    )(*[_hbm(s) for s in shards], *[_hbm(b) for b in lands], after)
    return outs[0], outs[1], outs[2:2 + n], outs[2 + n:2 + 2 * n], outs[2 + 2 * n]


def _gather_wait(started, after, name):
    send_sems, recv_sems, shards, lands, _ = started
    n = len(shards)

    def body(*refs):
        ins, bufs = refs[:n], refs[n:2 * n]
        send_sems, recv_sems = refs[2 * n], refs[2 * n + 1]
        me = _mesh_pos()
        for a in range(n):
            for j, k in enumerate(GATHER_FLIPS):
                cp = pltpu.make_async_remote_copy(
                    src_ref=ins[a], dst_ref=_block_rows(bufs[a], _flip(me, k), shards[a].shape[1]),
                    send_sem=send_sems.at[4 * a + j], recv_sem=recv_sems.at[4 * a + j],
                    device_id=_flip(me, k), device_id_type=MESH_ID)
                cp.wait_send()
                cp.wait_recv()

    outs = pl.pallas_call(
        body,
        name=name,
        in_specs=[HBM_SPEC] * (2 * n) + [SEM_SPEC, SEM_SPEC, ANY_SPEC],
        out_specs=[HBM_SPEC] * (2 * n),
        out_shape=[_hbm_like(s) for s in shards] + [_hbm_like(b) for b in lands],
        input_output_aliases={i: i for i in range(2 * n)},
        compiler_params=pltpu.CompilerParams(has_side_effects=SPLIT_EFFECT),
    )(*shards, *lands, send_sems, recv_sems, after)
    return outs[:n], outs[n:]


def _gather_pass(shards, lands, name):
    n = len(shards)

    def body(*refs):
        ins, bufs = refs[:n], refs[n:2 * n]
        token = refs[3 * n]
        send_sems, recv_sems, local_sems = refs[3 * n + 1:]
        me = _mesh_pos()
        sibling = _flip(me, 1)

        def copy(a, j, block):
            rows = _block_rows(bufs[a], block, shards[a].shape[1])
            return pltpu.make_async_remote_copy(
                src_ref=rows, dst_ref=rows, send_sem=send_sems.at[3 * a + j], recv_sem=recv_sems.at[3 * a + j],
                device_id=sibling, device_id_type=MESH_ID)

        mine = [pltpu.make_async_copy(ins[a], _block_rows(bufs[a], me, shards[a].shape[1]), local_sems.at[a])
                for a in range(n)]
        sends = [copy(a, j, _flip(me, k)) for a in range(n) for j, k in enumerate(CHIP_FLIPS)]
        for cp in mine + sends:
            cp.start()
        for a in range(n):
            for j, k in enumerate(CHIP_FLIPS):
                copy(a, j, _flip(sibling, k)).wait_recv()
        for cp in sends:
            cp.wait_send()
        for cp in mine:
            cp.wait()
        token[...] = jnp.zeros_like(token)

    outs = pl.pallas_call(
        body,
        name=name,
        in_specs=[pl.BlockSpec(memory_space=pltpu.VMEM)] * n + [ANY_SPEC] * n,
        out_specs=[ANY_SPEC] * n + [pl.BlockSpec(memory_space=pltpu.VMEM)],
        out_shape=[jax.ShapeDtypeStruct(b.shape, b.dtype) for b in lands] + [TOKEN],
        scratch_shapes=[pltpu.SemaphoreType.DMA((3 * n,)), pltpu.SemaphoreType.DMA((3 * n,)), pltpu.SemaphoreType.DMA((n,))],
        input_output_aliases={n + i: i for i in range(n)},
        compiler_params=pltpu.CompilerParams(has_side_effects=True, vmem_limit_bytes=VMEM_LIMIT_BYTES),
    )(*shards, *lands)
    return outs[:n], outs[n]


def _place_own(grads, lands, layer, name):
    n = len(grads)
    blocks = [(g.shape[0], g.shape[1] // N_DEV, g.shape[2]) for g in grads]

    def body(*refs):
        ins, bufs = refs[:n], refs[n:2 * n]
        stage, in_sems, out_sems = refs[3 * n:4 * n], refs[4 * n], refs[4 * n + 1]
        me = _mesh_pos()
        loads = [pltpu.make_async_copy(_block_rows(ins[a], me, blocks[a][1]), stage[a], in_sems.at[a]) for a in range(n)]
        stores = [pltpu.make_async_copy(stage[a], bufs[a].at[_index(me), layer], out_sems.at[a]) for a in range(n)]
        for cp in loads:
            cp.start()
        for a in range(n):
            loads[a].wait()
            stores[a].start()
        for cp in stores:
            cp.wait()

    return pl.pallas_call(
        body,
        name=name,
        in_specs=[ANY_SPEC] * (2 * n),
        out_specs=[ANY_SPEC] * n,
        out_shape=[jax.ShapeDtypeStruct(b.shape, b.dtype) for b in lands],
        scratch_shapes=[pltpu.VMEM(blk, g.dtype) for blk, g in zip(blocks, grads)]
        + [pltpu.SemaphoreType.DMA((n,)), pltpu.SemaphoreType.DMA((n,))],
        input_output_aliases={n + i: i for i in range(n)},
        compiler_params=pltpu.CompilerParams(has_side_effects=True, vmem_limit_bytes=VMEM_LIMIT_BYTES),
    )(*grads, *lands)


def _scatter_copy(ins, bufs, send_sems, recv_sems, a, k, r, layer, me, slab):
    peer = _flip(me, k)
    return pltpu.make_async_remote_copy(
        src_ref=_block_rows(ins[a], peer, r), dst_ref=bufs[a].at[_index(slab), layer],
        send_sem=send_sems.at[7 * a + k - 1], recv_sem=recv_sems.at[7 * a + k - 1],
        device_id=peer, device_id_type=MESH_ID)


def _scatter_start(grads, lands, layer, name):
    n = len(grads)

    def body(*refs):
        ins, bufs = refs[:n], refs[n:2 * n]
        send_sems, recv_sems = refs[2 * n], refs[2 * n + 1]
        token = refs[4 * n + 2]
        me = _mesh_pos()
        for a in range(n):
            for k in range(1, N_DEV):
                _scatter_copy(ins, bufs, send_sems, recv_sems, a, k, grads[a].shape[1] // N_DEV, layer, me, me).start()
        token[...] = jnp.zeros_like(token)

    outs = pl.pallas_call(
        body,
        name=name,
        in_specs=[HBM_SPEC] * (2 * n),
        out_specs=[SEM_SPEC, SEM_SPEC] + [HBM_SPEC] * (2 * n) + [pl.BlockSpec(memory_space=pltpu.VMEM)],
        out_shape=[pltpu.SemaphoreType.DMA((7 * n,)), pltpu.SemaphoreType.DMA((7 * n,))]
        + [_hbm_like(g) for g in grads] + [_hbm_like(b) for b in lands] + [TOKEN],
        input_output_aliases={i: 2 + i for i in range(2 * n)},
        compiler_params=pltpu.CompilerParams(has_side_effects=SPLIT_EFFECT),
    )(*[_hbm(g) for g in grads], *[_hbm(b) for b in lands])
    return outs[0], outs[1], outs[2:2 + n], outs[2 + n:2 + 2 * n], outs[2 + 2 * n]


def _scatter_wait(started, layer, after, name):
    send_sems, recv_sems, grads, lands, _ = started
    n = len(grads)

    def body(*refs):
        ins, bufs = refs[:n], refs[n:2 * n]
        send_sems, recv_sems = refs[2 * n], refs[2 * n + 1]
        me = _mesh_pos()
        for a in range(n):
            for k in range(1, N_DEV):
                cp = _scatter_copy(ins, bufs, send_sems, recv_sems, a, k, grads[a].shape[1] // N_DEV, layer, me, _flip(me, k))
                cp.wait_send()
                cp.wait_recv()

    outs = pl.pallas_call(
        body,
        name=name,
        in_specs=[HBM_SPEC] * (2 * n) + [SEM_SPEC, SEM_SPEC, ANY_SPEC],
        out_specs=[HBM_SPEC] * (2 * n),
        out_shape=[_hbm_like(g) for g in grads] + [_hbm_like(b) for b in lands],
        input_output_aliases={i: i for i in range(2 * n)},
        compiler_params=pltpu.CompilerParams(has_side_effects=SPLIT_EFFECT),
    )(*grads, *lands, send_sems, recv_sems, after)
    return outs[n:]


def _allreduce_small(vec):
    R, C = vec.shape

    def body(v_ref, o_ref, buf, send_sems, recv_sems):
        me = _mesh_pos()
        buf[_index(me)] = v_ref[...]
        sends = []
        for k in range(1, N_DEV):
            sends.append(pltpu.make_async_remote_copy(
                src_ref=buf.at[_index(me)], dst_ref=buf.at[_index(me)],
                send_sem=send_sems.at[k - 1], recv_sem=recv_sems.at[k - 1],
                device_id=_flip(me, k), device_id_type=MESH_ID))
        for cp in sends:
            cp.start()
        for cp in sends:
            cp.wait_recv()
        for cp in sends:
            cp.wait_send()
        acc = buf[0]
        for s in range(1, N_DEV):
            acc = acc + buf[s]
        o_ref[...] = acc

    return pl.pallas_call(
        body,
        name="allreduce_small",
        in_specs=[pl.BlockSpec(memory_space=pltpu.VMEM)],
        out_specs=pl.BlockSpec(memory_space=pltpu.VMEM),
        out_shape=jax.ShapeDtypeStruct((R, C), F32),
        scratch_shapes=[pltpu.VMEM((N_DEV, R, C), F32), pltpu.SemaphoreType.DMA((7,)), pltpu.SemaphoreType.DMA((7,))],
        compiler_params=pltpu.CompilerParams(has_side_effects=True, vmem_limit_bytes=VMEM_LIMIT_BYTES),
    )(vec)


def _pack_small(parts):
    flat = jnp.concatenate([p.reshape(-1) for p in parts])
    n = flat.shape[0]
    rows = -(-n // SMALL_COLS)
    rows = -(-rows // 8) * 8
    return jnp.pad(flat, (0, rows * SMALL_COLS - n)).reshape(rows, SMALL_COLS)


def _unpack_small(packed, like):
    flat = packed.reshape(-1)
    out, pos = [], 0
    for p in like:
        out.append(flat[pos:pos + p.size].reshape(p.shape))
        pos += p.size
    return out


def kernel(x, p, w_in, b_in, w_out, attn_sinks, rel_bias, w_pool, pool_scale, w_ple, w_gate_ple, ln_gain, ln_bias, loss_target, m_w_in, m_b_in, m_w_out, m_attn_sinks, m_rel_bias, m_w_pool, m_pool_scale, m_w_ple, m_w_gate_ple, m_ln_gain, m_ln_bias, v_w_in, v_b_in, v_w_out, v_attn_sinks, v_rel_bias, v_w_pool, v_pool_scale, v_w_ple, v_w_gate_ple, v_ln_gain, v_ln_bias):
    L = w_in.shape[0]
    S = x.shape[1]
    alpha = (2.0 * L) ** 0.25
    bucket_np, masks_np, window_np = _band_constants()
    bucket, masks, window = jnp.asarray(bucket_np), jnp.asarray(masks_np), jnp.asarray(window_np)

    w_in_s = jnp.swapaxes(w_in, 1, 2).astype(BF16)
    w_out_s = w_out.astype(BF16)
    w_gate_s = w_gate_ple.astype(BF16)
    w_ple_s = jnp.swapaxes(w_ple, 1, 2).astype(BF16)
    w_pool_s = w_pool.astype(BF16)

    def shards_of(l):
        return [w_in_s[l][None], w_out_s[l][None], w_gate_s[l][None], w_ple_s[l][None], w_pool_s[l]]

    def gathered(started, after, l):
        shards, lands = _gather_wait(started, after, name=f"gather_wait_{l}")
        full, token = _gather_pass(shards, lands, name=f"gather_pass_{l}")
        return (full[0][0], full[1][0], full[2][0], full[3][0], full[4]), token

    bias = _bias_build(rel_bias, bucket).reshape(N_KV_HEADS, BIAS_ROWS, BLOCK)

    xs = x[0]
    xb = xs.astype(BF16)
    weights, pass_token = gathered(_gather_start(shards_of(0), rel_bias, name="gather_start_0"), xb, 0)
    saved = []
    for l in range(L):
        w_in_t, w_out_g, w_gate_g, w_ple_t, w_pool_g = weights
        if l + 1 < L:
            started = _gather_start(shards_of(l + 1), pass_token, name=f"gather_start_{l + 1}")
        pb = p[l, 0].astype(BF16)
        sinks_l = attn_sinks[l]
        scale_l = pool_scale[l].reshape(1, POOL_WIDTH)
        h = _matmul(xb, w_in_t, tb=True, tm=512, tn=2176, tk=2048, out_dtype=F32, bias=b_in[l].reshape(1, IN_COLS),
                    after=started[4] if l + 1 < L else None, name=f"in_proj_{l}")
        gp = _matmul(xb, w_gate_g, tm=512, tn=2048, tk=2048, out_dtype=F32, name=f"gate_proj_{l}")
        pe = _matmul(pb, w_ple_t, tb=True, tm=1024, tn=2048, tk=256, out_dtype=F32, name=f"ple_proj_{l}")
        ab = _attn_fwd(h, bias, masks, sinks_l, name=f"attn_fwd_{l}")
        ab = _pool_fwd(h, ab, w_pool_g, scale_l, name=f"pool_fwd_{l}")
        mix = _matmul(ab, w_out_g, tm=512, tn=2048, tk=2048, out_dtype=F32, name=f"out_proj_{l}")
        y, yb, xhat, rstd = _ln_fwd(xs, mix, gp, pe, ln_gain[l].reshape(1, D_MODEL), ln_bias[l].reshape(1, D_MODEL),
                                    alpha, name=f"ln_fwd_{l}")
        saved.append((xb, pb, h, gp, pe, ab, xhat, rstd, sinks_l, scale_l, weights))
        xs, xb = y, yb
        if l + 1 < L:
            weights, pass_token = gathered(started, yb, l + 1)

    dy, loss_tile = _loss_head(xs, loss_target[0])

    dbias = jnp.zeros((N_KV_HEADS, BIAS_ROWS, BLOCK), F32)
    lands = [lax.empty((N_DEV, L) + s.shape, BF16) for s in shards_of(0)]
    g_b_in, g_sinks, g_scale, g_gain, g_beta = [], [], [], [], []
    pending = None
    for l in reversed(range(L)):
        xb, pb, h, gp, pe, ab, xhat, rstd, sinks_l, scale_l, weights = saved[l]
        w_in_t, w_out_g, w_gate_g, w_ple_t, w_pool_g = weights
        dz, dzb, dpe, dgp, dgain, dbeta = _ln_bwd(dy, xhat, rstd, ln_gain[l].reshape(1, D_MODEL), gp, pe,
                                                  pending[0][4] if pending else rel_bias, name=f"ln_bwd_{l}")
        dab = _matmul(dzb, w_out_g, tb=True, tm=512, tn=2048, tk=2048, out_dtype=F32, name=f"dmix_{l}")
        g_w_out = _matmul(ab, dzb, ta=True, tm=1024, tn=2048, tk=1024, out_dtype=BF16, name=f"dw_out_{l}")
        g_w_gate = _matmul(xb, dgp, ta=True, tm=1024, tn=2048, tk=1024, out_dtype=BF16, name=f"dw_gate_{l}")
        g_w_ple_t = _matmul(dpe, pb, ta=True, tm=1024, tn=256, tk=1024, out_dtype=BF16, name=f"dw_ple_{l}")
        dh, dwp, dsc = _pool_bwd(h, dab, w_pool_g, scale_l, name=f"pool_bwd_{l}")
        dh, dbias, dsink = _attn_bwd(h, dab, dh, bias, masks, sinks_l, dbias, name=f"attn_bwd_{l}")
        g_w_in_t = _matmul(dh, xb, ta=True, tm=2176, tn=1024, tk=1024, out_dtype=BF16, name=f"dw_in_{l}")
        if pending:
            lands = _scatter_wait(pending[0], pending[1], g_w_in_t, name=f"scatter_wait_{pending[1]}")
        grads = [g_w_in_t[None], g_w_out[None], g_w_gate[None], g_w_ple_t[None], dwp.astype(BF16)]
        lands = _place_own(grads, lands, l, name=f"place_own_{l}")
        pending = (_scatter_start(grads, lands, l, name=f"scatter_start_{l}"), l)
        g_b_in.append(_col_sum(dh, name=f"db_in_{l}"))
        dx = _matmul(dgp, w_gate_g, tb=True, tm=512, tn=2048, tk=2048, out_dtype=F32, add=dz, add_scale=alpha,
                     after=pending[0][4], name=f"dx_gate_{l}")
        dy = _matmul(dh, w_in_t, tm=512, tn=1024, tk=4352, out_dtype=F32, add=dx, name=f"dx_in_{l}")
        g_sinks.append(dsink[:, :, 0].reshape(N_HEADS))
        g_scale.append(dsc.reshape(POOL_WIDTH))
        g_gain.append(dgain.reshape(D_MODEL))
        g_beta.append(dbeta.reshape(D_MODEL))
    grad_x = dy[None]
    for lst in (g_b_in, g_sinks, g_scale, g_gain, g_beta):
        lst.reverse()
    g_rel = _bias_bwd(dbias.reshape((N_HEADS,) + BAND), bucket, window)[:, :N_HEADS]

    r_in, r_out, r_gate, r_ple, r_pool = _scatter_wait(pending[0], pending[1], dy, name=f"scatter_wait_{pending[1]}")
    small_like = [b_in, attn_sinks, rel_bias, pool_scale, ln_gain, ln_bias]
    small_g = _allreduce_small(_pack_small([
        jnp.stack(g_b_in).reshape(L, IN_COLS), jnp.stack(g_sinks), g_rel, jnp.stack(g_scale), jnp.stack(g_gain),
        jnp.stack(g_beta), loss_tile[0, :1]]))

    gt_in = _sum_slabs(r_in.reshape(N_DEV, L * 544, D_MODEL), name="sum_w_in")
    grad_w_in = jnp.swapaxes(gt_in.reshape(L, 544, D_MODEL), 1, 2)
    grad_w_out = _sum_slabs(r_out.reshape(N_DEV, L * 256, D_MODEL), name="sum_w_out").reshape(L, 256, D_MODEL)
    grad_w_gate = _sum_slabs(r_gate.reshape(N_DEV, L * 256, D_MODEL), name="sum_w_gate").reshape(L, 256, D_MODEL)
    gt_ple = _sum_slabs(r_ple.reshape(N_DEV, L * 256, PLE_DIM), name="sum_w_ple")
    grad_w_ple = jnp.swapaxes(gt_ple.reshape(L, 256, PLE_DIM), 1, 2)
    grad_w_pool = _sum_slabs(r_pool.reshape(N_DEV, L * 4 * 32, 256), name="sum_w_pool").reshape(L, 4, 32, 256)

    def big(w, g, m, v, name):
        shape = w.shape
        two_d = (shape[0] * shape[1], shape[2]) if len(shape) == 3 else (shape[0] * shape[1] * shape[2], shape[3])
        d, nm, nv = _adamw(w.reshape(two_d), g.reshape(two_d), m.reshape(two_d), v.reshape(two_d), name=name)
        return d.reshape(shape), nm.reshape(shape), nv.reshape(shape)

    upd_in = big(w_in, grad_w_in, m_w_in, v_w_in, "adamw_w_in")
    upd_out = big(w_out, grad_w_out, m_w_out, v_w_out, "adamw_w_out")
    upd_pool = big(w_pool, grad_w_pool, m_w_pool, v_w_pool, "adamw_w_pool")
    upd_ple = big(w_ple, grad_w_ple, m_w_ple, v_w_ple, "adamw_w_ple")
    upd_gate = big(w_gate_ple, grad_w_gate, m_w_gate_ple, v_w_gate_ple, "adamw_w_gate")

    zero1 = jnp.zeros((1,), F32)
    sw = _pack_small(small_like + [zero1])
    sm = _pack_small([m_b_in, m_attn_sinks, m_rel_bias, m_pool_scale, m_ln_gain, m_ln_bias, zero1])
    sv = _pack_small([v_b_in, v_attn_sinks, v_rel_bias, v_pool_scale, v_ln_gain, v_ln_bias, zero1])
    sd, snm, snv = _adamw(sw, small_g, sm, sv, name="adamw_small")
    like = small_like + [zero1]
    sg_parts = _unpack_small(small_g, like)
    sd_parts, snm_parts, snv_parts = _unpack_small(sd, like), _unpack_small(snm, like), _unpack_small(snv, like)
    loss = sg_parts[6][0]

    def assemble(big_parts, small_parts):
        w_in_, w_out_, w_pool_, w_ple_, w_gate_ = big_parts
        b_in_, sinks_, rel_, scale_, gain_, beta_ = small_parts[:6]
        return [w_in_, b_in_, w_out_, sinks_, rel_, w_pool_, scale_, w_ple_, w_gate_, gain_, beta_]

    grads = assemble([grad_w_in, grad_w_out, grad_w_pool, grad_w_ple, grad_w_gate], sg_parts)
    ups = [upd_in, upd_out, upd_pool, upd_ple, upd_gate]
    deltas = assemble([u[0] for u in ups], sd_parts)
    new_m = assemble([u[1] for u in ups], snm_parts)
    new_v = assemble([u[2] for u in ups], snv_parts)
    return (loss, grad_x, *grads, *deltas, *new_m, *new_v)
```

```python
import functools
import math

import numpy as np
import jax
import jax.numpy as jnp
from jax import lax
from jax.experimental import pallas as pl
from jax.experimental.pallas import tpu as pltpu

F32 = jnp.float32
BF16 = jnp.bfloat16

D_MODEL = 2048
PLE_DIM = 256
ATTN_WIDTH = 1024
POOL_WIDTH = 1024
HEAD_DIM = 64
N_HEADS = 16
N_KV_HEADS = 2
KV_GROUP = 8
WINDOW = 128
BLOCK = 128
POOL_WINDOWS = (2, 4, 8, 16)
POOL_GROUP_DIM = 256
POOL_HALO = 16
REL_BUCKETS = 32
REL_MAX_DIST = 128
LN_EPS = 1e-5
KV_COLS = N_KV_HEADS * HEAD_DIM
IN_COLS = 4352
Q_OFF, KV_OFF, GA_OFF, U_OFF, GB_OFF = 0, 1024, 1280, 2304, 3328
ATTN_SCALE = 1.0 / math.sqrt(HEAD_DIM)
NEG_BIG = -1e30
LANES = 128

ADAM_LR = 0.001
ADAM_B1 = 0.9
ADAM_B2 = 0.999
ADAM_EPS = 1e-08
ADAM_WD = 0.01
ADAM_STEP = 10

N_DEV = 8
MESH_ID = pl.DeviceIdType.MESH
VMEM_LIMIT_BYTES = 52 * 1024 * 1024
SMALL_COLS = 1024


def _params(sem=None):
    return pltpu.CompilerParams(dimension_semantics=sem, vmem_limit_bytes=VMEM_LIMIT_BYTES)


def _sigmoid(x):
    return 1.0 / (1.0 + jnp.exp(-x))


def _tile(n, pref, unit=16):
    if n <= pref:
        return n
    t = pref - pref % unit
    while n % t:
        t -= unit
    assert t > 0, (n, pref)
    return t


def _matmul(a, b, *, name, ta=False, tb=False, tm, tn, tk, out_dtype, bias=None, add=None, add_scale=1.0, after=None):
    M, K = (a.shape[1], a.shape[0]) if ta else a.shape
    N = b.shape[0] if tb else b.shape[1]
    assert (b.shape[1] if tb else b.shape[0]) == K
    tm, tn, tk = _tile(M, tm), _tile(N, tn), _tile(K, tk)
    nm, nn, nk = M // tm, N // tn, K // tk
    a_spec = pl.BlockSpec((tk, tm), lambda j, i, k: (k, i)) if ta else pl.BlockSpec((tm, tk), lambda j, i, k: (i, k))
    b_spec = pl.BlockSpec((tn, tk), lambda j, i, k: (j, k)) if tb else pl.BlockSpec((tk, tn), lambda j, i, k: (k, j))
    dims = (((0 if ta else 1,), (1 if tb else 0,)), ((), ()))
    operands, in_specs = [a, b], [a_spec, b_spec]
    if bias is not None:
        operands.append(bias)
        in_specs.append(pl.BlockSpec((1, tn), lambda j, i, k: (0, j)))
    if add is not None:
        operands.append(add)
        in_specs.append(pl.BlockSpec((tm, tn), lambda j, i, k: (i, j)))
    if after is not None:
        operands.append(after)
        in_specs.append(pl.BlockSpec(memory_space=pl.ANY))

    def body(*refs):
        a_ref, b_ref = refs[0], refs[1]
        pos = 2
        bias_ref = add_ref = None
        if bias is not None:
            bias_ref = refs[pos]
            pos += 1
        if add is not None:
            add_ref = refs[pos]
            pos += 1
        if after is not None:
            pos += 1
        o_ref = refs[pos]
        part = lax.dot_general(a_ref[...].astype(BF16), b_ref[...].astype(BF16), dims, preferred_element_type=F32)

        def finish(acc):
            if bias_ref is not None:
                acc = acc + bias_ref[...]
            if add_ref is not None:
                acc = acc + add_scale * add_ref[...].astype(F32)
            o_ref[...] = acc.astype(out_dtype)

        if nk == 1:
            finish(part)
        else:
            acc_ref = refs[pos + 1]
            k = pl.program_id(2)

            @pl.when(k == 0)
            def _():
                acc_ref[...] = part

            @pl.when(k > 0)
            def _():
                acc_ref[...] += part

            @pl.when(k == nk - 1)
            def _():
                finish(acc_ref[...])

    return pl.pallas_call(
        body,
        name=name,
        grid=(nn, nm, nk),
        in_specs=in_specs,
        out_specs=pl.BlockSpec((tm, tn), lambda j, i, k: (i, j)),
        out_shape=jax.ShapeDtypeStruct((M, N), out_dtype),
        scratch_shapes=[pltpu.VMEM((tm, tn), F32)] if nk > 1 else [],
        compiler_params=_params(("parallel", "parallel", "arbitrary")),
    )(*operands)


BAND = (2 * BLOCK, BLOCK)


def _band_constants():
    qq = np.arange(BLOCK)[None, :]
    kk = np.arange(2 * BLOCK)[:, None]
    dist = qq + BLOCK - kk
    in_window = (dist >= 0) & (dist < WINDOW)
    max_exact = REL_BUCKETS // 2
    d = np.maximum(dist, 0)
    d_f = np.maximum(d, 1).astype(np.float32)
    large = max_exact + (
        np.log(d_f / np.float32(max_exact)) / np.float32(math.log(REL_MAX_DIST / max_exact)) * np.float32(REL_BUCKETS - max_exact)
    ).astype(np.int32)
    large = np.minimum(large, REL_BUCKETS - 1)
    bucket = np.where(d < max_exact, d, large).astype(np.int32)
    bucket = np.where(in_window, bucket, 0).astype(np.int32)
    first = in_window & (kk >= BLOCK)
    masks = np.stack([first, in_window]).astype(np.float32)
    return bucket, masks, in_window.astype(np.float32)


def _bias_build(rel_bias, bucket):
    def body(rb_ref, bkt_ref, o_ref):
        h = pl.program_id(0)
        bkt = bkt_ref[...]

        def step(b, acc):
            return jnp.where(bkt == b, rb_ref[b, h], acc)

        o_ref[0] = lax.fori_loop(0, REL_BUCKETS, step, jnp.zeros(BAND, F32))

    return pl.pallas_call(
        body,
        name="bias_build",
        grid=(N_HEADS,),
        in_specs=[pl.BlockSpec(memory_space=pltpu.SMEM), pl.BlockSpec(BAND, lambda h: (0, 0))],
        out_specs=pl.BlockSpec((1,) + BAND, lambda h: (h, 0, 0)),
        out_shape=jax.ShapeDtypeStruct((N_HEADS,) + BAND, F32),
        compiler_params=_params(("arbitrary",)),
    )(rel_bias, bucket)


def _bias_bwd(dbias, bucket, window):
    def body(db_ref, bkt_ref, win_ref, o_ref):
        h = pl.program_id(0)

        @pl.when(h == 0)
        def _():
            o_ref[...] = jnp.zeros_like(o_ref)

        bkt = bkt_ref[...]
        x = jnp.where(win_ref[...] > 0.5, db_ref[0], 0.0)
        row = lax.broadcasted_iota(jnp.int32, (REL_BUCKETS, LANES), 0)
        col = lax.broadcasted_iota(jnp.int32, (REL_BUCKETS, LANES), 1)

        def step(b, acc):
            s = jnp.sum(jnp.where(bkt == b, x, 0.0), axis=1, keepdims=True)
            s = jnp.sum(s, axis=0, keepdims=True)
            return acc + jnp.where((row == b) & (col == h), s, 0.0)

        o_ref[...] += lax.fori_loop(0, REL_BUCKETS, step, jnp.zeros((REL_BUCKETS, LANES), F32))

    return pl.pallas_call(
        body,
        name="bias_bwd",
        grid=(N_HEADS,),
        in_specs=[
            pl.BlockSpec((1,) + BAND, lambda h: (h, 0, 0)),
            pl.BlockSpec(BAND, lambda h: (0, 0)),
            pl.BlockSpec(BAND, lambda h: (0, 0)),
        ],
        out_specs=pl.BlockSpec((REL_BUCKETS, LANES), lambda h: (0, 0)),
        out_shape=jax.ShapeDtypeStruct((REL_BUCKETS, LANES), F32),
        compiler_params=_params(("arbitrary",)),
    )(dbias, bucket, window)


def _lane_lo(shape):
    return lax.broadcasted_iota(jnp.int32, shape, 1) < HEAD_DIM


def _row_lo(shape):
    return lax.broadcasted_iota(jnp.int32, shape, 0) < HEAD_DIM


def _dup_heads(x):
    r = pltpu.roll(x, HEAD_DIM, axis=1)
    lo = _lane_lo(x.shape)
    return jnp.where(lo, x, r), jnp.where(lo, r, x)


def _kv_operands(kvp_ref, kvc_ref):
    kvp, kvc = kvp_ref[...], kvc_ref[...]
    k2 = jnp.concatenate([kvp[:, :KV_COLS], kvc[:, :KV_COLS]], axis=0)
    v2 = jnp.concatenate([kvp[:, KV_COLS:], kvc[:, KV_COLS:]], axis=0)
    return _dup_heads(k2), _dup_heads(v2)


def _head_probs(k_r, qs_t, bias, mask, sink):
    s = jnp.dot(k_r, qs_t, preferred_element_type=F32) * ATTN_SCALE + bias
    s = jnp.where(mask, s, NEG_BIG)
    m = jnp.maximum(jnp.max(s, axis=0, keepdims=True), sink)
    e = jnp.exp(s - m)
    e_sink = jnp.exp(sink - m)
    inv = 1.0 / (jnp.sum(e, axis=0, keepdims=True) + e_sink)
    return e * inv, e_sink * inv


def _gate_cols(ga_refs, pair):
    off = LANES * (pair % 2)
    return ga_refs[pair // 2][:, off:off + LANES]


def _attn_specs(order):
    return [
        pl.BlockSpec((BLOCK, ATTN_WIDTH), lambda t: (order(t), Q_OFF // ATTN_WIDTH)),
        pl.BlockSpec((BLOCK, 2 * KV_COLS), lambda t: (order(t), KV_OFF // (2 * KV_COLS))),
        pl.BlockSpec((BLOCK, 2 * KV_COLS), lambda t: (jnp.maximum(order(t) - 1, 0), KV_OFF // (2 * KV_COLS))),
    ] + [
        pl.BlockSpec((BLOCK, 256), functools.partial(lambda t, c: (order(t), GA_OFF // 256 + c), c=c)) for c in range(4)
    ] + [
        pl.BlockSpec((N_KV_HEADS, KV_GROUP * 2 * BLOCK, BLOCK), lambda t: (0, 0, 0)),
        pl.BlockSpec((None,) + BAND, lambda t: (jnp.minimum(order(t), 1), 0, 0)),
        pl.BlockSpec(memory_space=pltpu.SMEM),
    ]


def _attn_fwd(h, bias, masks, sinks, name):
    S = h.shape[0]
    nb = S // BLOCK

    def body(q_ref, kvc_ref, kvp_ref, ga0, ga1, ga2, ga3, bias_ref, mask_ref, sink_ref, o_ref):
        kd, vd = _kv_operands(kvp_ref, kvc_ref)
        mask = mask_ref[...] > 0.5
        lo = _row_lo((LANES, BLOCK))
        for g in range(N_KV_HEADS):
            k_r = kd[g].astype(BF16)
            v_t = vd[g].T.astype(BF16)
            for pr in range(KV_GROUP // 2):
                pair = (KV_GROUP // 2) * g + pr
                qp_t = q_ref[:, LANES * pair:LANES * (pair + 1)].T
                outs = []
                for hh in range(2):
                    j = 2 * pr + hh
                    qs_t = jnp.where(lo if hh == 0 else ~lo, qp_t, 0.0).astype(BF16)
                    p, _ = _head_probs(k_r, qs_t, bias_ref[g, 2 * BLOCK * j:2 * BLOCK * (j + 1), :], mask,
                                       sink_ref[KV_GROUP * g + j])
                    outs.append(jnp.dot(v_t, p.astype(BF16), preferred_element_type=F32))
                ga = _gate_cols((ga0, ga1, ga2, ga3), pair)
                o_ref[:, LANES * pair:LANES * (pair + 1)] = (
                    jnp.where(lo, outs[0], outs[1]).T * (ga * _sigmoid(ga))).astype(BF16)

    return pl.pallas_call(
        body,
        name=name,
        grid=(nb,),
        in_specs=_attn_specs(lambda t: t),
        out_specs=pl.BlockSpec((BLOCK, ATTN_WIDTH), lambda t: (t, 0)),
        out_shape=jax.ShapeDtypeStruct((S, ATTN_WIDTH + POOL_WIDTH), BF16),
        compiler_params=_params(("arbitrary",)),
    )(h, h, h, h, h, h, h, bias, masks, sinks)


DH_ATTN_COLS = U_OFF
BIAS_ROWS = KV_GROUP * 2 * BLOCK


def _attn_bwd(h, dab, dh, bias, masks, sinks, dbias_in, name):
    S = h.shape[0]
    nb = S // BLOCK

    def order(t):
        return nb - 1 - t

    def body(q_ref, kvc_ref, kvp_ref, ga0, ga1, ga2, ga3, bias_ref, mask_ref, sink_ref, da_ref, dbin_ref, dh_in_ref,
             dh_ref, dbias_ref, dsink_ref, carry_scr):
        del dh_in_ref
        t = pl.program_id(0)

        @pl.when(t == 0)
        def _():
            dbias_ref[...] = dbin_ref[...]
            dsink_ref[...] = jnp.zeros_like(dsink_ref)
            carry_scr[...] = jnp.zeros_like(carry_scr)

        kd, vd = _kv_operands(kvp_ref, kvc_ref)
        mask = mask_ref[...] > 0.5
        lo = _lane_lo((BLOCK, LANES))
        lo_t = _row_lo((LANES, BLOCK))
        dk_tot, dv_tot = [], []
        for g in range(N_KV_HEADS):
            k_t, k_r = kd[g].T.astype(BF16), kd[g].astype(BF16)
            v_t, v_r = vd[g].T.astype(BF16), vd[g].astype(BF16)
            dk = jnp.zeros((2 * BLOCK, LANES), F32)
            dv = jnp.zeros((2 * BLOCK, LANES), F32)
            for pr in range(KV_GROUP // 2):
                pair = (KV_GROUP // 2) * g + pr
                cols = slice(LANES * pair, LANES * (pair + 1))
                qp = q_ref[:, cols]
                qp_t = qp.T
                ga = _gate_cols((ga0, ga1, ga2, ga3), pair)
                sg = _sigmoid(ga)
                da = da_ref[:, cols]
                do_p = da * (ga * sg)
                do_t = do_p.T
                outs, dqs = [], []
                for hh in range(2):
                    j = 2 * pr + hh
                    rows = slice(2 * BLOCK * j, 2 * BLOCK * (j + 1))
                    half, half_t = (lo, lo_t) if hh == 0 else (~lo, ~lo_t)
                    qs_t = jnp.where(half_t, qp_t, 0.0).astype(BF16)
                    p, p_sink = _head_probs(k_r, qs_t, bias_ref[g, rows, :], mask, sink_ref[KV_GROUP * g + j])
                    pb = p.astype(BF16)
                    outs.append(jnp.dot(v_t, pb, preferred_element_type=F32))
                    dos_t = jnp.where(half_t, do_t, 0.0).astype(BF16)
                    dp = jnp.dot(v_r, dos_t, preferred_element_type=F32)
                    dsum = jnp.sum(p * dp, axis=0, keepdims=True)
                    ds = p * (dp - dsum)
                    dbias_ref[g, rows, :] += ds
                    tot = jnp.sum(-(p_sink * dsum), axis=1, keepdims=True)
                    dsink_ref[g, j:j + 1, :] += jnp.broadcast_to(tot, (1, LANES))
                    dsb = ds.astype(BF16)
                    dqs.append(jnp.dot(k_t, dsb, preferred_element_type=F32))
                    dk = dk + jnp.dot(dsb, jnp.where(half, qp, 0.0).astype(BF16), preferred_element_type=F32)
                    dv = dv + jnp.dot(pb, jnp.where(half, do_p, 0.0).astype(BF16), preferred_element_type=F32)
                attn = jnp.where(lo_t, outs[0], outs[1]).T
                dh_ref[:, cols] = (jnp.where(lo_t, dqs[0], dqs[1]).T * ATTN_SCALE).astype(BF16)
                dh_ref[:, GA_OFF + LANES * pair:GA_OFF + LANES * (pair + 1)] = (
                    da * attn * (sg * (1.0 + ga * (1.0 - sg)))).astype(BF16)
            dk = dk * ATTN_SCALE
            dk_tot.append(dk + pltpu.roll(dk, HEAD_DIM, axis=1))
            dv_tot.append(dv + pltpu.roll(dv, HEAD_DIM, axis=1))
        lo2 = _lane_lo((2 * BLOCK, LANES))
        dkv = jnp.concatenate([jnp.where(lo2, dk_tot[0], dk_tot[1]), jnp.where(lo2, dv_tot[0], dv_tot[1])], axis=1)
        dh_ref[:, KV_OFF:KV_OFF + 2 * KV_COLS] = (dkv[BLOCK:, :] + carry_scr[...]).astype(BF16)
        carry_scr[...] = dkv[:BLOCK, :]

    n_in = 12
    return pl.pallas_call(
        body,
        name=name,
        grid=(nb,),
        in_specs=_attn_specs(order) + [
            pl.BlockSpec((BLOCK, ATTN_WIDTH), lambda t: (order(t), 0)),
            pl.BlockSpec((N_KV_HEADS, BIAS_ROWS, BLOCK), lambda t: (0, 0, 0)),
            pl.BlockSpec(memory_space=pl.ANY),
        ],
        out_specs=[
            pl.BlockSpec((BLOCK, DH_ATTN_COLS), lambda t: (order(t), 0)),
            pl.BlockSpec((N_KV_HEADS, BIAS_ROWS, BLOCK), lambda t: (0, 0, 0)),
            pl.BlockSpec((N_KV_HEADS, KV_GROUP, LANES), lambda t: (0, 0, 0)),
        ],
        out_shape=[
            jax.ShapeDtypeStruct((S, IN_COLS), BF16),
            jax.ShapeDtypeStruct((N_KV_HEADS, BIAS_ROWS, BLOCK), F32),
            jax.ShapeDtypeStruct((N_KV_HEADS, KV_GROUP, LANES), F32),
        ],
        scratch_shapes=[pltpu.VMEM((BLOCK, 2 * KV_COLS), F32)],
        input_output_aliases={n_in: 0},
        compiler_params=_params(("arbitrary",)),
    )(h, h, h, h, h, h, h, bias, masks, sinks, dab, dbias_in, dh)


def _window_sum(x, w, back):
    n = x.shape[0]
    s, sh = x, 1
    while sh < w:
        s = s + pltpu.roll(s, sh if back else n - sh, axis=0)
        sh *= 2
    return s


def _pool_counts(first_row, n, w):
    t = first_row + lax.broadcasted_iota(jnp.int32, (n, 1), 0)
    return jnp.minimum(t + 1, w).astype(F32)


def _pool_diff(u_ref, uh_ref, i, T, g):
    u = u_ref[...]
    halo = jnp.where(i > 0, uh_ref[...], 0.0)
    ext = jnp.concatenate([halo, u], axis=0)
    w = POOL_WINDOWS[g]
    s = _window_sum(ext, w, back=True)[POOL_HALO:, :]
    return s / _pool_counts(i * T, T, w) - u


def _pool_in_specs(T):
    hb = T // POOL_HALO
    specs = []
    for g in range(4):
        specs.append(pl.BlockSpec((T, 256), functools.partial(lambda i, g: (i, U_OFF // 256 + g), g=g)))
        specs.append(pl.BlockSpec((POOL_HALO, 256), functools.partial(
            lambda i, g: (jnp.maximum(i * hb - 1, 0), U_OFF // 256 + g), g=g)))
    return specs


def _pool_weight_specs():
    return [pl.BlockSpec((4, 256, 256), lambda i: (0, 0, 0)), pl.BlockSpec((1, POOL_WIDTH), lambda i: (0, 0))]


def _pool_fwd(h, ab, w_pool, pool_scale, name):
    S = h.shape[0]
    T = _tile(S, 512)

    def body(*refs):
        u_refs = refs[0:8]
        gb_refs = refs[8:12]
        wp_ref, sc_ref, o_ref = refs[12], refs[13], refs[15]
        i = pl.program_id(0)
        for g in range(4):
            diff = _pool_diff(u_refs[2 * g], u_refs[2 * g + 1], i, T, g).astype(BF16)
            mixed = jnp.dot(diff, wp_ref[g], preferred_element_type=F32) * sc_ref[:, 256 * g:256 * (g + 1)]
            gb = gb_refs[g][...]
            o_ref[:, 256 * g:256 * (g + 1)] = (mixed * (gb * _sigmoid(gb))).astype(BF16)

    in_specs = _pool_in_specs(T) + [
        pl.BlockSpec((T, 256), functools.partial(lambda i, g: (i, GB_OFF // 256 + g), g=g)) for g in range(4)
    ] + _pool_weight_specs() + [pl.BlockSpec(memory_space=pl.ANY)]
    return pl.pallas_call(
        body,
        name=name,
        grid=(S // T,),
        in_specs=in_specs,
        out_specs=pl.BlockSpec((T, POOL_WIDTH), lambda i: (i, 1)),
        out_shape=jax.ShapeDtypeStruct(ab.shape, BF16),
        input_output_aliases={14: 0},
        compiler_params=_params(("arbitrary",)),
    )(*([h] * 12), w_pool, pool_scale, ab)


DH_POOL_COLS = IN_COLS // 2


def _pool_bwd(h, dab, w_pool, pool_scale, after, name):
    S = h.shape[0]
    T = _tile(S, 512)
    nt = S // T
    hb = T // POOL_HALO
    E = T + POOL_HALO
    lead = U_OFF - DH_POOL_COLS

    def body(*refs):
        u_refs = refs[0:8]
        gb_refs = refs[8:16]
        db_refs = refs[16:24]
        wp_ref, sc_ref = refs[24], refs[25]
        dh_ref, dwp_ref, dsc_ref = refs[27:30]
        i = pl.program_id(0)

        @pl.when(i == 0)
        def _():
            dwp_ref[...] = jnp.zeros_like(dwp_ref)
            dsc_ref[...] = jnp.zeros_like(dsc_ref)

        dh_ref[:, 0:lead] = jnp.zeros((T, lead), BF16)
        for g in range(4):
            w = POOL_WINDOWS[g]
            cols = slice(256 * g, 256 * (g + 1))
            scale = sc_ref[:, cols]
            wp = wp_ref[g]
            diff = _pool_diff(u_refs[2 * g], u_refs[2 * g + 1], i, T, g).astype(BF16)
            mixed = jnp.dot(diff, wp, preferred_element_type=F32)
            keep = i < nt - 1
            gb = jnp.concatenate([gb_refs[2 * g][...], jnp.where(keep, gb_refs[2 * g + 1][...], 0.0)], axis=0)
            db = jnp.concatenate([db_refs[2 * g][...], jnp.where(keep, db_refs[2 * g + 1][...], 0.0)], axis=0)
            sg = _sigmoid(gb)
            dms = db * (gb * sg)
            dmixed = (dms * scale).astype(BF16)
            ddiff = lax.dot_general(dmixed, wp, (((1,), (1,)), ((), ())), preferred_element_type=F32)
            r = ddiff / _pool_counts(i * T, E, w)
            du = _window_sum(r, w, back=False)[:T, :] - ddiff[:T, :]
            dh_ref[:, lead + 256 * g:lead + 256 * (g + 1)] = du.astype(BF16)
            dh_ref[:, lead + POOL_WIDTH + 256 * g:lead + POOL_WIDTH + 256 * (g + 1)] = (
                db[:T, :] * (mixed * scale) * (sg[:T, :] * (1.0 + gb[:T, :] * (1.0 - sg[:T, :])))).astype(BF16)
            dsc_ref[:, cols] += jnp.sum(dms[:T, :] * mixed, axis=0, keepdims=True)
            dwp_ref[g] += lax.dot_general(diff, dmixed[:T, :], (((0,), (0,)), ((), ())), preferred_element_type=F32)

    def rows_after(i):
        return jnp.minimum((i + 1) * hb, S // POOL_HALO - 1)

    in_specs = _pool_in_specs(T)
    for off in (GB_OFF // 256, ATTN_WIDTH // 256):
        for g in range(4):
            in_specs.append(pl.BlockSpec((T, 256), functools.partial(lambda i, c: (i, c), c=off + g)))
            in_specs.append(pl.BlockSpec((POOL_HALO, 256), functools.partial(lambda i, c: (rows_after(i), c), c=off + g)))
    in_specs += _pool_weight_specs() + [pl.BlockSpec(memory_space=pl.ANY)]
    return pl.pallas_call(
        body,
        name=name,
        grid=(nt,),
        in_specs=in_specs,
        out_specs=[
            pl.BlockSpec((T, DH_POOL_COLS), lambda i: (i, 1)),
            pl.BlockSpec((4, 256, 256), lambda i: (0, 0, 0)),
            pl.BlockSpec((1, POOL_WIDTH), lambda i: (0, 0)),
        ],
        out_shape=[
            jax.ShapeDtypeStruct((S, IN_COLS), BF16),
            jax.ShapeDtypeStruct((4, 256, 256), F32),
            jax.ShapeDtypeStruct((1, POOL_WIDTH), F32),
        ],
        compiler_params=_params(("arbitrary",)),
    )(*([h] * 16), *([dab] * 8), w_pool, pool_scale, after)


def _ln_fwd(x, mix, gp, pe, gain, bias, alpha, name):
    S = x.shape[0]
    T = _tile(S, 256)

    def body(x_ref, mix_ref, gp_ref, pe_ref, g_ref, b_ref, y_ref, yb_ref, xh_ref, rs_ref):
        z = alpha * x_ref[...] + mix_ref[...] + _sigmoid(gp_ref[...]) * pe_ref[...]
        mu = jnp.mean(z, axis=-1, keepdims=True)
        zc = z - mu
        var = jnp.mean(zc * zc, axis=-1, keepdims=True)
        rstd = lax.rsqrt(var + LN_EPS)
        xhat = zc * rstd
        y = xhat * g_ref[...] + b_ref[...]
        y_ref[...] = y
        yb_ref[...] = y.astype(BF16)
        xh_ref[...] = xhat
        rs_ref[...] = rstd

    row = pl.BlockSpec((T, D_MODEL), lambda i: (i, 0))
    vec = pl.BlockSpec((1, D_MODEL), lambda i: (0, 0))
    return pl.pallas_call(
        body,
        name=name,
        grid=(S // T,),
        in_specs=[row, row, row, row, vec, vec],
        out_specs=[row, row, row, pl.BlockSpec((T, 1), lambda i: (i, 0))],
        out_shape=[
            jax.ShapeDtypeStruct((S, D_MODEL), F32),
            jax.ShapeDtypeStruct((S, D_MODEL), BF16),
            jax.ShapeDtypeStruct((S, D_MODEL), F32),
            jax.ShapeDtypeStruct((S, 1), F32),
        ],
        compiler_params=_params(("parallel",)),
    )(x, mix, gp, pe, gain, bias)


def _ln_bwd(dy, xhat, rstd, gain, gp, pe, after, name):
    S = dy.shape[0]
    T = _tile(S, 256)

    def body(dy_ref, xh_ref, rs_ref, g_ref, gp_ref, pe_ref, after_ref, dz_ref, dzb_ref, dpe_ref, dgp_ref, dgain_ref, dbias_ref):
        del after_ref
        @pl.when(pl.program_id(0) == 0)
        def _():
            dgain_ref[...] = jnp.zeros_like(dgain_ref)
            dbias_ref[...] = jnp.zeros_like(dbias_ref)

        dy = dy_ref[...]
        xhat = xh_ref[...]
        dyg = dy * g_ref[...]
        c1 = jnp.mean(dyg, axis=-1, keepdims=True)
        c2 = jnp.mean(dyg * xhat, axis=-1, keepdims=True)
        dz = rs_ref[...] * (dyg - c1 - xhat * c2)
        dgain_ref[...] += jnp.sum(dy * xhat, axis=0, keepdims=True)
        dbias_ref[...] += jnp.sum(dy, axis=0, keepdims=True)
        sg = _sigmoid(gp_ref[...])
        dz_ref[...] = dz
        dzb_ref[...] = dz.astype(BF16)
        dpe_ref[...] = (dz * sg).astype(BF16)
        dgp_ref[...] = (dz * pe_ref[...] * (sg * (1.0 - sg))).astype(BF16)

    row = pl.BlockSpec((T, D_MODEL), lambda i: (i, 0))
    vec = pl.BlockSpec((1, D_MODEL), lambda i: (0, 0))
    return pl.pallas_call(
        body,
        name=name,
        grid=(S // T,),
        in_specs=[row, row, pl.BlockSpec((T, 1), lambda i: (i, 0)), vec, row, row, pl.BlockSpec(memory_space=pl.ANY)],
        out_specs=[row, row, row, row, vec, vec],
        out_shape=[
            jax.ShapeDtypeStruct((S, D_MODEL), F32),
            jax.ShapeDtypeStruct((S, D_MODEL), BF16),
            jax.ShapeDtypeStruct((S, D_MODEL), BF16),
            jax.ShapeDtypeStruct((S, D_MODEL), BF16),
            jax.ShapeDtypeStruct((1, D_MODEL), F32),
            jax.ShapeDtypeStruct((1, D_MODEL), F32),
        ],
        compiler_params=_params(("arbitrary",)),
    )(dy, xhat, rstd, gain, gp, pe, after)


def _loss_head(y, target):
    S = y.shape[0]
    T = _tile(S, 256)

    def body(y_ref, t_ref, dy_ref, l_ref):
        @pl.when(pl.program_id(0) == 0)
        def _():
            l_ref[...] = jnp.zeros_like(l_ref)

        err = y_ref[...] - t_ref[...]
        dy_ref[...] = err * (1.0 / D_MODEL)
        per_token = jnp.mean(err * err, axis=-1, keepdims=True)
        l_ref[...] += 0.5 * jnp.sum(per_token, axis=0, keepdims=True)

    row = pl.BlockSpec((T, D_MODEL), lambda i: (i, 0))
    return pl.pallas_call(
        body,
        name="loss_head",
        grid=(S // T,),
        in_specs=[row, row],
        out_specs=[row, pl.BlockSpec((8, LANES), lambda i: (0, 0))],
        out_shape=[jax.ShapeDtypeStruct((S, D_MODEL), F32), jax.ShapeDtypeStruct((8, LANES), F32)],
        compiler_params=_params(("arbitrary",)),
    )(y, target)


def _col_sum(a, name):
    S, C = a.shape
    T = _tile(S, 512)

    def body(a_ref, o_ref):
        @pl.when(pl.program_id(0) == 0)
        def _():
            o_ref[...] = jnp.zeros_like(o_ref)

        o_ref[...] += jnp.sum(a_ref[...].astype(F32), axis=0, keepdims=True)

    return pl.pallas_call(
        body,
        name=name,
        grid=(S // T,),
        in_specs=[pl.BlockSpec((T, C), lambda i: (i, 0))],
        out_specs=pl.BlockSpec((1, C), lambda i: (0, 0)),
        out_shape=jax.ShapeDtypeStruct((1, C), F32),
        compiler_params=_params(("arbitrary",)),
    )(a)


def _sum_slabs(r, name):
    _, R, C = r.shape
    T = _tile(R, 256)

    def body(r_ref, o_ref):
        acc = r_ref[0].astype(F32)
        for s in range(1, N_DEV):
            acc = acc + r_ref[s].astype(F32)
        o_ref[...] = acc

    return pl.pallas_call(
        body,
        name=name,
        grid=(R // T,),
        in_specs=[pl.BlockSpec((N_DEV, T, C), lambda i: (0, i, 0))],
        out_specs=pl.BlockSpec((T, C), lambda i: (i, 0)),
        out_shape=jax.ShapeDtypeStruct((R, C), F32),
        compiler_params=_params(("parallel",)),
    )(r)


def _adamw(w, g, m, v, name):
    R, C = w.shape
    T = _tile(R, 256)

    def body(w_ref, g_ref, m_ref, v_ref, d_ref, nm_ref, nv_ref):
        g = g_ref[...]
        m = ADAM_B1 * m_ref[...] + (1.0 - ADAM_B1) * g
        v = ADAM_B2 * v_ref[...] + (1.0 - ADAM_B2) * jnp.square(g)
        m_hat = m / (1.0 - ADAM_B1 ** ADAM_STEP)
        v_hat = v / (1.0 - ADAM_B2 ** ADAM_STEP)
        d_ref[...] = -ADAM_LR * (m_hat / (jnp.sqrt(v_hat) + ADAM_EPS) + ADAM_WD * w_ref[...])
        nm_ref[...] = m
        nv_ref[...] = v

    blk = pl.BlockSpec((T, C), lambda i: (i, 0))
    shp = jax.ShapeDtypeStruct((R, C), F32)
    return pl.pallas_call(
        body,
        name=name,
        grid=(R // T,),
        in_specs=[blk] * 4,
        out_specs=[blk] * 3,
        out_shape=[shp] * 3,
        compiler_params=_params(("parallel",)),
    )(w, g, m, v)


def _mesh_pos():
    return lax.axis_index("x"), lax.axis_index("y"), lax.axis_index("c")


def _flip(pos, k):
    x, y, c = pos
    return (1 - x if k & 4 else x, 1 - y if k & 2 else y, 1 - c if k & 1 else c)


def _index(pos):
    return 4 * pos[0] + 2 * pos[1] + pos[2]


HBM_SPEC = pl.BlockSpec(memory_space=pltpu.HBM)
SEM_SPEC = pl.BlockSpec(memory_space=pltpu.SEMAPHORE)
ANY_SPEC = pl.BlockSpec(memory_space=pl.ANY)
SPLIT_EFFECT = pltpu.SideEffectType.DATAFLOW_SIDE_EFFECTING
GATHER_FLIPS = (1, 4, 2, 6)
CHIP_FLIPS = (4, 2, 6)
TOKEN = jax.ShapeDtypeStruct((8, LANES), F32)


def _hbm(a):
    return pltpu.with_memory_space_constraint(a, pltpu.HBM)


def _hbm_like(a):
    return pltpu.HBM(a.shape, a.dtype)


def _block_rows(ref, pos, r):
    return ref.at[:, pl.ds(_index(pos) * r, r), :]


def _gather_start(shards, after, name):
    n = len(shards)
    lands = [lax.empty((s.shape[0], N_DEV * s.shape[1], s.shape[2]), s.dtype) for s in shards]

    def body(*refs):
        ins, bufs = refs[:n], refs[n:2 * n]
        send_sems, recv_sems = refs[2 * n + 1], refs[2 * n + 2]
        token = refs[4 * n + 3]
        me = _mesh_pos()
        for a in range(n):
            for j, k in enumerate(GATHER_FLIPS):
                pltpu.make_async_remote_copy(
                    src_ref=ins[a], dst_ref=_block_rows(bufs[a], me, shards[a].shape[1]),
                    send_sem=send_sems.at[4 * a + j], recv_sem=recv_sems.at[4 * a + j],
                    device_id=_flip(me, k), device_id_type=MESH_ID).start()
        token[...] = jnp.zeros_like(token)

    outs = pl.pallas_call(
        body,
        name=name,
        in_specs=[HBM_SPEC] * (2 * n) + [ANY_SPEC],
        out_specs=[SEM_SPEC, SEM_SPEC] + [HBM_SPEC] * (2 * n) + [pl.BlockSpec(memory_space=pltpu.VMEM)],
        out_shape=[pltpu.SemaphoreType.DMA((4 * n,)), pltpu.SemaphoreType.DMA((4 * n,))]
        + [_hbm_like(s) for s in shards] + [_hbm_like(b) for b in lands] + [TOKEN],
        input_output_aliases={i: 2 + i for i in range(2 * n)},
        compiler_params=pltpu.CompilerParams(has_side_effects=SPLIT_EFFECT),
    )(*[_hbm(s) for s in shards], *[_hbm(b) for b in lands], after)
    return outs[0], outs[1], outs[2:2 + n], outs[2 + n:2 + 2 * n], outs[2 + 2 * n]


def _gather_wait(started, after, name):
    send_sems, recv_sems, shards, lands, _ = started
    n = len(shards)

    def body(*refs):
        ins, bufs = refs[:n], refs[n:2 * n]
        send_sems, recv_sems = refs[2 * n], refs[2 * n + 1]
        me = _mesh_pos()
        for a in range(n):
            for j, k in enumerate(GATHER_FLIPS):
                cp = pltpu.make_async_remote_copy(
                    src_ref=ins[a], dst_ref=_block_rows(bufs[a], _flip(me, k), shards[a].shape[1]),
                    send_sem=send_sems.at[4 * a + j], recv_sem=recv_sems.at[4 * a + j],
                    device_id=_flip(me, k), device_id_type=MESH_ID)
                cp.wait_send()
                cp.wait_recv()

    outs = pl.pallas_call(
        body,
        name=name,
        in_specs=[HBM_SPEC] * (2 * n) + [SEM_SPEC, SEM_SPEC, ANY_SPEC],
        out_specs=[HBM_SPEC] * (2 * n),
        out_shape=[_hbm_like(s) for s in shards] + [_hbm_like(b) for b in lands],
        input_output_aliases={i: i for i in range(2 * n)},
        compiler_params=pltpu.CompilerParams(has_side_effects=SPLIT_EFFECT),
    )(*shards, *lands, send_sems, recv_sems, after)
    return outs[:n], outs[n:]


def _gather_pass(shards, lands, name):
    n = len(shards)

    def body(*refs):
        ins, bufs = refs[:n], refs[n:2 * n]
        token = refs[3 * n]
        send_sems, recv_sems, local_sems = refs[3 * n + 1:]
        me = _mesh_pos()
        sibling = _flip(me, 1)

        def copy(a, j, block):
            rows = _block_rows(bufs[a], block, shards[a].shape[1])
            return pltpu.make_async_remote_copy(
                src_ref=rows, dst_ref=rows, send_sem=send_sems.at[3 * a + j], recv_sem=recv_sems.at[3 * a + j],
                device_id=sibling, device_id_type=MESH_ID)

        mine = [pltpu.make_async_copy(ins[a], _block_rows(bufs[a], me, shards[a].shape[1]), local_sems.at[a])
                for a in range(n)]
        sends = [copy(a, j, _flip(me, k)) for a in range(n) for j, k in enumerate(CHIP_FLIPS)]
        for cp in mine + sends:
            cp.start()
        for a in range(n):
            for j, k in enumerate(CHIP_FLIPS):
                copy(a, j, _flip(sibling, k)).wait_recv()
        for cp in sends:
            cp.wait_send()
        for cp in mine:
            cp.wait()
        token[...] = jnp.zeros_like(token)

    outs = pl.pallas_call(
        body,
        name=name,
        in_specs=[pl.BlockSpec(memory_space=pltpu.VMEM)] * n + [ANY_SPEC] * n,
        out_specs=[ANY_SPEC] * n + [pl.BlockSpec(memory_space=pltpu.VMEM)],
        out_shape=[jax.ShapeDtypeStruct(b.shape, b.dtype) for b in lands] + [TOKEN],
        scratch_shapes=[pltpu.SemaphoreType.DMA((3 * n,)), pltpu.SemaphoreType.DMA((3 * n,)), pltpu.SemaphoreType.DMA((n,))],
        input_output_aliases={n + i: i for i in range(n)},
        compiler_params=pltpu.CompilerParams(has_side_effects=True, vmem_limit_bytes=VMEM_LIMIT_BYTES),
    )(*shards, *lands)
    return outs[:n], outs[n]


def _place_own(grads, lands, layer, name):
    n = len(grads)
    blocks = [(g.shape[0], g.shape[1] // N_DEV, g.shape[2]) for g in grads]

    def body(*refs):
        ins, bufs = refs[:n], refs[n:2 * n]
        stage, in_sems, out_sems = refs[3 * n:4 * n], refs[4 * n], refs[4 * n + 1]
        me = _mesh_pos()
        loads = [pltpu.make_async_copy(_block_rows(ins[a], me, blocks[a][1]), stage[a], in_sems.at[a]) for a in range(n)]
        stores = [pltpu.make_async_copy(stage[a], bufs[a].at[_index(me), layer], out_sems.at[a]) for a in range(n)]
        for cp in loads:
            cp.start()
        for a in range(n):
            loads[a].wait()
            stores[a].start()
        for cp in stores:
            cp.wait()

    return pl.pallas_call(
        body,
        name=name,
        in_specs=[ANY_SPEC] * (2 * n),
        out_specs=[ANY_SPEC] * n,
        out_shape=[jax.ShapeDtypeStruct(b.shape, b.dtype) for b in lands],
        scratch_shapes=[pltpu.VMEM(blk, g.dtype) for blk, g in zip(blocks, grads)]
        + [pltpu.SemaphoreType.DMA((n,)), pltpu.SemaphoreType.DMA((n,))],
        input_output_aliases={n + i: i for i in range(n)},
        compiler_params=pltpu.CompilerParams(has_side_effects=True, vmem_limit_bytes=VMEM_LIMIT_BYTES),
    )(*grads, *lands)


def _scatter_copy(ins, bufs, send_sems, recv_sems, a, k, r, layer, me, slab):
    peer = _flip(me, k)
    return pltpu.make_async_remote_copy(
        src_ref=_block_rows(ins[a], peer, r), dst_ref=bufs[a].at[_index(slab), layer],
        send_sem=send_sems.at[7 * a + k - 1], recv_sem=recv_sems.at[7 * a + k - 1],
        device_id=peer, device_id_type=MESH_ID)


def _scatter_start(grads, lands, layer, name):
    n = len(grads)

    def body(*refs):
        ins, bufs = refs[:n], refs[n:2 * n]
        send_sems, recv_sems = refs[2 * n], refs[2 * n + 1]
        token = refs[4 * n + 2]
        me = _mesh_pos()
        for a in range(n):
            for k in range(1, N_DEV):
                _scatter_copy(ins, bufs, send_sems, recv_sems, a, k, grads[a].shape[1] // N_DEV, layer, me, me).start()
        token[...] = jnp.zeros_like(token)

    outs = pl.pallas_call(
        body,
        name=name,
        in_specs=[HBM_SPEC] * (2 * n),
        out_specs=[SEM_SPEC, SEM_SPEC] + [HBM_SPEC] * (2 * n) + [pl.BlockSpec(memory_space=pltpu.VMEM)],
        out_shape=[pltpu.SemaphoreType.DMA((7 * n,)), pltpu.SemaphoreType.DMA((7 * n,))]
        + [_hbm_like(g) for g in grads] + [_hbm_like(b) for b in lands] + [TOKEN],
        input_output_aliases={i: 2 + i for i in range(2 * n)},
        compiler_params=pltpu.CompilerParams(has_side_effects=SPLIT_EFFECT),
    )(*[_hbm(g) for g in grads], *[_hbm(b) for b in lands])
    return outs[0], outs[1], outs[2:2 + n], outs[2 + n:2 + 2 * n], outs[2 + 2 * n]


def _scatter_wait(started, layer, after, name):
    send_sems, recv_sems, grads, lands, _ = started
    n = len(grads)

    def body(*refs):
        ins, bufs = refs[:n], refs[n:2 * n]
        send_sems, recv_sems = refs[2 * n], refs[2 * n + 1]
        me = _mesh_pos()
        for a in range(n):
            for k in range(1, N_DEV):
                cp = _scatter_copy(ins, bufs, send_sems, recv_sems, a, k, grads[a].shape[1] // N_DEV, layer, me, _flip(me, k))
                cp.wait_send()
                cp.wait_recv()

    outs = pl.pallas_call(
        body,
        name=name,
        in_specs=[HBM_SPEC] * (2 * n) + [SEM_SPEC, SEM_SPEC, ANY_SPEC],
        out_specs=[HBM_SPEC] * (2 * n),
        out_shape=[_hbm_like(g) for g in grads] + [_hbm_like(b) for b in lands],
        input_output_aliases={i: i for i in range(2 * n)},
        compiler_params=pltpu.CompilerParams(has_side_effects=SPLIT_EFFECT),
    )(*grads, *lands, send_sems, recv_sems, after)
    return outs[n:]


def _allreduce_small(vec):
    R, C = vec.shape

    def body(v_ref, o_ref, buf, send_sems, recv_sems):
        me = _mesh_pos()
        buf[_index(me)] = v_ref[...]
        sends = []
        for k in range(1, N_DEV):
            sends.append(pltpu.make_async_remote_copy(
                src_ref=buf.at[_index(me)], dst_ref=buf.at[_index(me)],
                send_sem=send_sems.at[k - 1], recv_sem=recv_sems.at[k - 1],
                device_id=_flip(me, k), device_id_type=MESH_ID))
        for cp in sends:
            cp.start()
        for cp in sends:
            cp.wait_recv()
        for cp in sends:
            cp.wait_send()
        acc = buf[0]
        for s in range(1, N_DEV):
            acc = acc + buf[s]
        o_ref[...] = acc

    return pl.pallas_call(
        body,
        name="allreduce_small",
        in_specs=[pl.BlockSpec(memory_space=pltpu.VMEM)],
        out_specs=pl.BlockSpec(memory_space=pltpu.VMEM),
        out_shape=jax.ShapeDtypeStruct((R, C), F32),
        scratch_shapes=[pltpu.VMEM((N_DEV, R, C), F32), pltpu.SemaphoreType.DMA((7,)), pltpu.SemaphoreType.DMA((7,))],
        compiler_params=pltpu.CompilerParams(has_side_effects=True, vmem_limit_bytes=VMEM_LIMIT_BYTES),
    )(vec)


def _pack_small(parts):
    flat = jnp.concatenate([p.reshape(-1) for p in parts])
    n = flat.shape[0]
    rows = -(-n // SMALL_COLS)
    rows = -(-rows // 8) * 8
    return jnp.pad(flat, (0, rows * SMALL_COLS - n)).reshape(rows, SMALL_COLS)


def _unpack_small(packed, like):
    flat = packed.reshape(-1)
    out, pos = [], 0
    for p in like:
        out.append(flat[pos:pos + p.size].reshape(p.shape))
        pos += p.size
    return out


def kernel(x, p, w_in, b_in, w_out, attn_sinks, rel_bias, w_pool, pool_scale, w_ple, w_gate_ple, ln_gain, ln_bias, loss_target, m_w_in, m_b_in, m_w_out, m_attn_sinks, m_rel_bias, m_w_pool, m_pool_scale, m_w_ple, m_w_gate_ple, m_ln_gain, m_ln_bias, v_w_in, v_b_in, v_w_out, v_attn_sinks, v_rel_bias, v_w_pool, v_pool_scale, v_w_ple, v_w_gate_ple, v_ln_gain, v_ln_bias):
    L = w_in.shape[0]
    S = x.shape[1]
    alpha = (2.0 * L) ** 0.25
    bucket_np, masks_np, window_np = _band_constants()
    bucket, masks, window = jnp.asarray(bucket_np), jnp.asarray(masks_np), jnp.asarray(window_np)

    w_in_s = jnp.swapaxes(w_in, 1, 2).astype(BF16)
    w_out_s = w_out.astype(BF16)
    w_gate_s = w_gate_ple.astype(BF16)
    w_ple_s = jnp.swapaxes(w_ple, 1, 2).astype(BF16)
    w_pool_s = w_pool.astype(BF16)

    def shards_of(l):
        return [w_in_s[l][None], w_out_s[l][None], w_gate_s[l][None], w_ple_s[l][None], w_pool_s[l]]

    def gathered(started, after, tag):
        shards, lands = _gather_wait(started, after, name=f"gather_wait_{tag}")
        return _gather_pass(shards, lands, name=f"gather_pass_{tag}")

    bias = _bias_build(rel_bias, bucket).reshape(N_KV_HEADS, BIAS_ROWS, BLOCK)

    xs = x[0]
    xb = xs.astype(BF16)
    first_groups = ((2, 3), (0,), (1, 4))
    token, first_started = rel_bias, []
    for tag, idxs in zip("abc", first_groups):
        first_started.append(_gather_start([shards_of(0)[i] for i in idxs], token, name=f"gather_start_0{tag}"))
        token = first_started[-1][4]
    started = _gather_start(shards_of(1), token, name="gather_start_1") if L > 1 else None
    saved = []
    for l in range(L):
        pb = p[l, 0].astype(BF16)
        sinks_l = attn_sinks[l]
        scale_l = pool_scale[l].reshape(1, POOL_WIDTH)
        bias_l = b_in[l].reshape(1, IN_COLS)
        if l == 0:
            (w_gate_f, w_ple_f), _ = gathered(first_started[0], xb, "0a")
            w_gate_g, w_ple_t = w_gate_f[0], w_ple_f[0]
            gp = _matmul(xb, w_gate_g, tm=512, tn=2048, tk=2048, out_dtype=F32, name=f"gate_proj_{l}")
            pe = _matmul(pb, w_ple_t, tb=True, tm=1024, tn=2048, tk=256, out_dtype=F32, name=f"ple_proj_{l}")
            (w_in_f,), _ = gathered(first_started[1], pe, "0b")
            w_in_t = w_in_f[0]
            h = _matmul(xb, w_in_t, tb=True, tm=512, tn=2176, tk=2048, out_dtype=F32, bias=bias_l, name=f"in_proj_{l}")
            (w_out_f, w_pool_g), pass_token = gathered(first_started[2], h, "0c")
            w_out_g = w_out_f[0]
        else:
            w_in_t, w_out_g, w_gate_g, w_ple_t, w_pool_g = weights
            if l + 1 < L:
                started = _gather_start(shards_of(l + 1), pass_token, name=f"gather_start_{l + 1}")
            h = _matmul(xb, w_in_t, tb=True, tm=512, tn=2176, tk=2048, out_dtype=F32, bias=bias_l,
                        after=started[4] if l + 1 < L else None, name=f"in_proj_{l}")
            gp = _matmul(xb, w_gate_g, tm=512, tn=2048, tk=2048, out_dtype=F32, name=f"gate_proj_{l}")
            pe = _matmul(pb, w_ple_t, tb=True, tm=1024, tn=2048, tk=256, out_dtype=F32, name=f"ple_proj_{l}")
        weights = (w_in_t, w_out_g, w_gate_g, w_ple_t, w_pool_g)
        ab = _attn_fwd(h, bias, masks, sinks_l, name=f"attn_fwd_{l}")
        ab = _pool_fwd(h, ab, w_pool_g, scale_l, name=f"pool_fwd_{l}")
        mix = _matmul(ab, w_out_g, tm=512, tn=2048, tk=2048, out_dtype=F32, name=f"out_proj_{l}")
        y, yb, xhat, rstd = _ln_fwd(xs, mix, gp, pe, ln_gain[l].reshape(1, D_MODEL), ln_bias[l].reshape(1, D_MODEL),
                                    alpha, name=f"ln_fwd_{l}")
        saved.append((xb, pb, h, gp, pe, ab, xhat, rstd, sinks_l, scale_l, weights))
        xs, xb = y, yb
        if l + 1 < L:
            full, pass_token = gathered(started, yb, l + 1)
            weights = (full[0][0], full[1][0], full[2][0], full[3][0], full[4])

    dy, loss_tile = _loss_head(xs, loss_target[0])

    dbias = jnp.zeros((N_KV_HEADS, BIAS_ROWS, BLOCK), F32)
    sh0 = shards_of(0)
    lands_a = [lax.empty((N_DEV, L) + sh0[i].shape, BF16) for i in (1, 2, 3)]
    lands_b = [lax.empty((N_DEV, L) + sh0[i].shape, BF16) for i in (0, 4)]
    g_b_in, g_sinks, g_scale, g_gain, g_beta = [], [], [], [], []
    pend_a = pend_b = None

    def scatter(grads, lands, pending, l, tag):
        if pending:
            lands = _scatter_wait(pending[0], pending[1], grads[0], name=f"scatter_wait_{pending[1]}{tag}")
        lands = _place_own(grads, lands, l, name=f"place_own_{l}{tag}")
        return _scatter_start(grads, lands, l, name=f"scatter_start_{l}{tag}"), l

    for l in reversed(range(L)):
        xb, pb, h, gp, pe, ab, xhat, rstd, sinks_l, scale_l, weights = saved[l]
        w_in_t, w_out_g, w_gate_g, w_ple_t, w_pool_g = weights
        dz, dzb, dpe, dgp, dgain, dbeta = _ln_bwd(dy, xhat, rstd, ln_gain[l].reshape(1, D_MODEL), gp, pe,
                                                  pend_b[0][4] if pend_b else rel_bias, name=f"ln_bwd_{l}")
        dab = _matmul(dzb, w_out_g, tb=True, tm=512, tn=2048, tk=2048, out_dtype=F32, name=f"dmix_{l}")
        g_w_out = _matmul(ab, dzb, ta=True, tm=1024, tn=2048, tk=1024, out_dtype=BF16, name=f"dw_out_{l}")
        g_w_gate = _matmul(xb, dgp, ta=True, tm=1024, tn=2048, tk=1024, out_dtype=BF16, name=f"dw_gate_{l}")
        g_w_ple_t = _matmul(dpe, pb, ta=True, tm=1024, tn=256, tk=1024, out_dtype=BF16, name=f"dw_ple_{l}")
        pend_a = scatter([g_w_out[None], g_w_gate[None], g_w_ple_t[None]], lands_a, pend_a, l, "a")
        dh, dwp, dsc = _pool_bwd(h, dab, w_pool_g, scale_l, pend_a[0][4], name=f"pool_bwd_{l}")
        dh, dbias, dsink = _attn_bwd(h, dab, dh, bias, masks, sinks_l, dbias, name=f"attn_bwd_{l}")
        g_w_in_t = _matmul(dh, xb, ta=True, tm=2176, tn=1024, tk=1024, out_dtype=BF16, name=f"dw_in_{l}")
        pend_b = scatter([g_w_in_t[None], dwp.astype(BF16)], lands_b, pend_b, l, "b")
        g_b_in.append(_col_sum(dh, name=f"db_in_{l}"))
        dx = _matmul(dgp, w_gate_g, tb=True, tm=512, tn=2048, tk=2048, out_dtype=F32, add=dz, add_scale=alpha,
                     after=pend_b[0][4], name=f"dx_gate_{l}")
        dy = _matmul(dh, w_in_t, tm=512, tn=1024, tk=4352, out_dtype=F32, add=dx, name=f"dx_in_{l}")
        g_sinks.append(dsink[:, :, 0].reshape(N_HEADS))
        g_scale.append(dsc.reshape(POOL_WIDTH))
        g_gain.append(dgain.reshape(D_MODEL))
        g_beta.append(dbeta.reshape(D_MODEL))
    grad_x = dy[None]
    for lst in (g_b_in, g_sinks, g_scale, g_gain, g_beta):
        lst.reverse()
    g_rel = _bias_bwd(dbias.reshape((N_HEADS,) + BAND), bucket, window)[:, :N_HEADS]

    def big(w, g, m, v, name):
        shape = w.shape
        two_d = (shape[0] * shape[1], shape[2]) if len(shape) == 3 else (shape[0] * shape[1] * shape[2], shape[3])
        d, nm, nv = _adamw(w.reshape(two_d), g.reshape(two_d), m.reshape(two_d), v.reshape(two_d), name=name)
        return d.reshape(shape), nm.reshape(shape), nv.reshape(shape)

    r_out, r_gate, r_ple = _scatter_wait(pend_a[0], pend_a[1], dy, name="scatter_wait_0a")
    small_like = [b_in, attn_sinks, rel_bias, pool_scale, ln_gain, ln_bias]
    small_g = _allreduce_small(_pack_small([
        jnp.stack(g_b_in).reshape(L, IN_COLS), jnp.stack(g_sinks), g_rel, jnp.stack(g_scale), jnp.stack(g_gain),
        jnp.stack(g_beta), loss_tile[0, :1]]))
    grad_w_out = _sum_slabs(r_out.reshape(N_DEV, L * 256, D_MODEL), name="sum_w_out").reshape(L, 256, D_MODEL)
    grad_w_gate = _sum_slabs(r_gate.reshape(N_DEV, L * 256, D_MODEL), name="sum_w_gate").reshape(L, 256, D_MODEL)
    gt_ple = _sum_slabs(r_ple.reshape(N_DEV, L * 256, PLE_DIM), name="sum_w_ple")
    grad_w_ple = jnp.swapaxes(gt_ple.reshape(L, 256, PLE_DIM), 1, 2)
    upd_out = big(w_out, grad_w_out, m_w_out, v_w_out, "adamw_w_out")
    upd_ple = big(w_ple, grad_w_ple, m_w_ple, v_w_ple, "adamw_w_ple")
    upd_gate = big(w_gate_ple, grad_w_gate, m_w_gate_ple, v_w_gate_ple, "adamw_w_gate")

    r_in, r_pool = _scatter_wait(pend_b[0], pend_b[1], upd_gate[0], name="scatter_wait_0b")
    gt_in = _sum_slabs(r_in.reshape(N_DEV, L * 544, D_MODEL), name="sum_w_in")
    grad_w_in = jnp.swapaxes(gt_in.reshape(L, 544, D_MODEL), 1, 2)
    grad_w_pool = _sum_slabs(r_pool.reshape(N_DEV, L * 4 * 32, 256), name="sum_w_pool").reshape(L, 4, 32, 256)
    upd_in = big(w_in, grad_w_in, m_w_in, v_w_in, "adamw_w_in")
    upd_pool = big(w_pool, grad_w_pool, m_w_pool, v_w_pool, "adamw_w_pool")

    zero1 = jnp.zeros((1,), F32)
    sw = _pack_small(small_like + [zero1])
    sm = _pack_small([m_b_in, m_attn_sinks, m_rel_bias, m_pool_scale, m_ln_gain, m_ln_bias, zero1])
    sv = _pack_small([v_b_in, v_attn_sinks, v_rel_bias, v_pool_scale, v_ln_gain, v_ln_bias, zero1])
    sd, snm, snv = _adamw(sw, small_g, sm, sv, name="adamw_small")
    like = small_like + [zero1]
    sg_parts = _unpack_small(small_g, like)
    sd_parts, snm_parts, snv_parts = _unpack_small(sd, like), _unpack_small(snm, like), _unpack_small(snv, like)
    loss = sg_parts[6][0]

    def assemble(big_parts, small_parts):
        w_in_, w_out_, w_pool_, w_ple_, w_gate_ = big_parts
        b_in_, sinks_, rel_, scale_, gain_, beta_ = small_parts[:6]
        return [w_in_, b_in_, w_out_, sinks_, rel_, w_pool_, scale_, w_ple_, w_gate_, gain_, beta_]

    grads = assemble([grad_w_in, grad_w_out, grad_w_pool, grad_w_ple, grad_w_gate], sg_parts)
    ups = [upd_in, upd_out, upd_pool, upd_ple, upd_gate]
    deltas = assemble([u[0] for u in ups], sd_parts)
    new_m = assemble([u[1] for u in ups], snm_parts)
    new_v = assemble([u[2] for u in ups], snv_parts)
    return (loss, grad_x, *grads, *deltas, *new_m, *new_v)
```

```python
import functools
import math

import numpy as np
import jax
import jax.numpy as jnp
from jax import lax
from jax.experimental import pallas as pl
from jax.experimental.pallas import tpu as pltpu

F32 = jnp.float32
BF16 = jnp.bfloat16

D_MODEL = 2048
PLE_DIM = 256
ATTN_WIDTH = 1024
POOL_WIDTH = 1024
HEAD_DIM = 64
N_HEADS = 16
N_KV_HEADS = 2
KV_GROUP = 8
WINDOW = 128
BLOCK = 128
POOL_WINDOWS = (2, 4, 8, 16)
POOL_GROUP_DIM = 256
POOL_HALO = 16
REL_BUCKETS = 32
REL_MAX_DIST = 128
LN_EPS = 1e-5
KV_COLS = N_KV_HEADS * HEAD_DIM
IN_COLS = 4352
Q_OFF, KV_OFF, GA_OFF, U_OFF, GB_OFF = 0, 1024, 1280, 2304, 3328
ATTN_SCALE = 1.0 / math.sqrt(HEAD_DIM)
NEG_BIG = -1e30
LANES = 128

ADAM_LR = 0.001
ADAM_B1 = 0.9
ADAM_B2 = 0.999
ADAM_EPS = 1e-08
ADAM_WD = 0.01
ADAM_STEP = 10

N_DEV = 8
MESH_ID = pl.DeviceIdType.MESH
VMEM_LIMIT_BYTES = 52 * 1024 * 1024
SMALL_COLS = 1024


def _params(sem=None):
    return pltpu.CompilerParams(dimension_semantics=sem, vmem_limit_bytes=VMEM_LIMIT_BYTES)


def _sigmoid(x):
    return 1.0 / (1.0 + jnp.exp(-x))


def _tile(n, pref, unit=16):
    if n <= pref:
        return n
    t = pref - pref % unit
    while n % t:
        t -= unit
    assert t > 0, (n, pref)
    return t


def _matmul(a, b, *, name, ta=False, tb=False, tm, tn, tk, out_dtype, bias=None, add=None, add_scale=1.0, after=None):
    M, K = (a.shape[1], a.shape[0]) if ta else a.shape
    N = b.shape[0] if tb else b.shape[1]
    assert (b.shape[1] if tb else b.shape[0]) == K
    tm, tn, tk = _tile(M, tm), _tile(N, tn), _tile(K, tk)
    nm, nn, nk = M // tm, N // tn, K // tk
    a_spec = pl.BlockSpec((tk, tm), lambda j, i, k: (k, i)) if ta else pl.BlockSpec((tm, tk), lambda j, i, k: (i, k))
    b_spec = pl.BlockSpec((tn, tk), lambda j, i, k: (j, k)) if tb else pl.BlockSpec((tk, tn), lambda j, i, k: (k, j))
    dims = (((0 if ta else 1,), (1 if tb else 0,)), ((), ()))
    operands, in_specs = [a, b], [a_spec, b_spec]
    if bias is not None:
        operands.append(bias)
        in_specs.append(pl.BlockSpec((1, tn), lambda j, i, k: (0, j)))
    if add is not None:
        operands.append(add)
        in_specs.append(pl.BlockSpec((tm, tn), lambda j, i, k: (i, j)))
    if after is not None:
        operands.append(after)
        in_specs.append(pl.BlockSpec(memory_space=pl.ANY))

    def body(*refs):
        a_ref, b_ref = refs[0], refs[1]
        pos = 2
        bias_ref = add_ref = None
        if bias is not None:
            bias_ref = refs[pos]
            pos += 1
        if add is not None:
            add_ref = refs[pos]
            pos += 1
        if after is not None:
            pos += 1
        o_ref = refs[pos]
        part = lax.dot_general(a_ref[...].astype(BF16), b_ref[...].astype(BF16), dims, preferred_element_type=F32)

        def finish(acc):
            if bias_ref is not None:
                acc = acc + bias_ref[...]
            if add_ref is not None:
                acc = acc + add_scale * add_ref[...].astype(F32)
            o_ref[...] = acc.astype(out_dtype)

        if nk == 1:
            finish(part)
        else:
            acc_ref = refs[pos + 1]
            k = pl.program_id(2)

            @pl.when(k == 0)
            def _():
                acc_ref[...] = part

            @pl.when(k > 0)
            def _():
                acc_ref[...] += part

            @pl.when(k == nk - 1)
            def _():
                finish(acc_ref[...])

    return pl.pallas_call(
        body,
        name=name,
        grid=(nn, nm, nk),
        in_specs=in_specs,
        out_specs=pl.BlockSpec((tm, tn), lambda j, i, k: (i, j)),
        out_shape=jax.ShapeDtypeStruct((M, N), out_dtype),
        scratch_shapes=[pltpu.VMEM((tm, tn), F32)] if nk > 1 else [],
        compiler_params=_params(("parallel", "parallel", "arbitrary")),
    )(*operands)


BAND = (2 * BLOCK, BLOCK)


def _band_constants():
    qq = np.arange(BLOCK)[None, :]
    kk = np.arange(2 * BLOCK)[:, None]
    dist = qq + BLOCK - kk
    in_window = (dist >= 0) & (dist < WINDOW)
    max_exact = REL_BUCKETS // 2
    d = np.maximum(dist, 0)
    d_f = np.maximum(d, 1).astype(np.float32)
    large = max_exact + (
        np.log(d_f / np.float32(max_exact)) / np.float32(math.log(REL_MAX_DIST / max_exact)) * np.float32(REL_BUCKETS - max_exact)
    ).astype(np.int32)
    large = np.minimum(large, REL_BUCKETS - 1)
    bucket = np.where(d < max_exact, d, large).astype(np.int32)
    bucket = np.where(in_window, bucket, 0).astype(np.int32)
    first = in_window & (kk >= BLOCK)
    masks = np.stack([first, in_window]).astype(np.float32)
    return bucket, masks, in_window.astype(np.float32)


def _bias_build(rel_bias, bucket):
    def body(rb_ref, bkt_ref, o_ref):
        h = pl.program_id(0)
        bkt = bkt_ref[...]

        def step(b, acc):
            return jnp.where(bkt == b, rb_ref[b, h], acc)

        o_ref[0] = lax.fori_loop(0, REL_BUCKETS, step, jnp.zeros(BAND, F32))

    return pl.pallas_call(
        body,
        name="bias_build",
        grid=(N_HEADS,),
        in_specs=[pl.BlockSpec(memory_space=pltpu.SMEM), pl.BlockSpec(BAND, lambda h: (0, 0))],
        out_specs=pl.BlockSpec((1,) + BAND, lambda h: (h, 0, 0)),
        out_shape=jax.ShapeDtypeStruct((N_HEADS,) + BAND, F32),
        compiler_params=_params(("arbitrary",)),
    )(rel_bias, bucket)


def _bias_bwd(dbias, bucket, window):
    def body(db_ref, bkt_ref, win_ref, o_ref):
        h = pl.program_id(0)

        @pl.when(h == 0)
        def _():
            o_ref[...] = jnp.zeros_like(o_ref)

        bkt = bkt_ref[...]
        x = jnp.where(win_ref[...] > 0.5, db_ref[0], 0.0)
        row = lax.broadcasted_iota(jnp.int32, (REL_BUCKETS, LANES), 0)
        col = lax.broadcasted_iota(jnp.int32, (REL_BUCKETS, LANES), 1)

        def step(b, acc):
            s = jnp.sum(jnp.where(bkt == b, x, 0.0), axis=1, keepdims=True)
            s = jnp.sum(s, axis=0, keepdims=True)
            return acc + jnp.where((row == b) & (col == h), s, 0.0)

        o_ref[...] += lax.fori_loop(0, REL_BUCKETS, step, jnp.zeros((REL_BUCKETS, LANES), F32))

    return pl.pallas_call(
        body,
        name="bias_bwd",
        grid=(N_HEADS,),
        in_specs=[
            pl.BlockSpec((1,) + BAND, lambda h: (h, 0, 0)),
            pl.BlockSpec(BAND, lambda h: (0, 0)),
            pl.BlockSpec(BAND, lambda h: (0, 0)),
        ],
        out_specs=pl.BlockSpec((REL_BUCKETS, LANES), lambda h: (0, 0)),
        out_shape=jax.ShapeDtypeStruct((REL_BUCKETS, LANES), F32),
        compiler_params=_params(("arbitrary",)),
    )(dbias, bucket, window)


def _lane_lo(shape):
    return lax.broadcasted_iota(jnp.int32, shape, 1) < HEAD_DIM


def _row_lo(shape):
    return lax.broadcasted_iota(jnp.int32, shape, 0) < HEAD_DIM


def _dup_heads(x):
    r = pltpu.roll(x, HEAD_DIM, axis=1)
    lo = _lane_lo(x.shape)
    return jnp.where(lo, x, r), jnp.where(lo, r, x)


def _kv_operands(kvp_ref, kvc_ref):
    kvp, kvc = kvp_ref[...], kvc_ref[...]
    k2 = jnp.concatenate([kvp[:, :KV_COLS], kvc[:, :KV_COLS]], axis=0)
    v2 = jnp.concatenate([kvp[:, KV_COLS:], kvc[:, KV_COLS:]], axis=0)
    return _dup_heads(k2), _dup_heads(v2)


def _head_probs(k_r, qs_t, bias, mask, sink):
    s = jnp.dot(k_r, qs_t, preferred_element_type=F32) * ATTN_SCALE + bias
    s = jnp.where(mask, s, NEG_BIG)
    m = jnp.maximum(jnp.max(s, axis=0, keepdims=True), sink)
    e = jnp.exp(s - m)
    e_sink = jnp.exp(sink - m)
    inv = 1.0 / (jnp.sum(e, axis=0, keepdims=True) + e_sink)
    return e * inv, e_sink * inv


def _gate_cols(ga_refs, pair):
    off = LANES * (pair % 2)
    return ga_refs[pair // 2][:, off:off + LANES]


def _attn_specs(order):
    return [
        pl.BlockSpec((BLOCK, ATTN_WIDTH), lambda t: (order(t), Q_OFF // ATTN_WIDTH)),
        pl.BlockSpec((BLOCK, 2 * KV_COLS), lambda t: (order(t), KV_OFF // (2 * KV_COLS))),
        pl.BlockSpec((BLOCK, 2 * KV_COLS), lambda t: (jnp.maximum(order(t) - 1, 0), KV_OFF // (2 * KV_COLS))),
    ] + [
        pl.BlockSpec((BLOCK, 256), functools.partial(lambda t, c: (order(t), GA_OFF // 256 + c), c=c)) for c in range(4)
    ] + [
        pl.BlockSpec((N_KV_HEADS, KV_GROUP * 2 * BLOCK, BLOCK), lambda t: (0, 0, 0)),
        pl.BlockSpec((None,) + BAND, lambda t: (jnp.minimum(order(t), 1), 0, 0)),
        pl.BlockSpec(memory_space=pltpu.SMEM),
    ]


def _attn_fwd(h, bias, masks, sinks, name):
    S = h.shape[0]
    nb = S // BLOCK

    def body(q_ref, kvc_ref, kvp_ref, ga0, ga1, ga2, ga3, bias_ref, mask_ref, sink_ref, o_ref):
        kd, vd = _kv_operands(kvp_ref, kvc_ref)
        mask = mask_ref[...] > 0.5
        lo = _row_lo((LANES, BLOCK))
        for g in range(N_KV_HEADS):
            k_r = kd[g].astype(BF16)
            v_t = vd[g].T.astype(BF16)
            for pr in range(KV_GROUP // 2):
                pair = (KV_GROUP // 2) * g + pr
                qp_t = q_ref[:, LANES * pair:LANES * (pair + 1)].T
                outs = []
                for hh in range(2):
                    j = 2 * pr + hh
                    qs_t = jnp.where(lo if hh == 0 else ~lo, qp_t, 0.0).astype(BF16)
                    p, _ = _head_probs(k_r, qs_t, bias_ref[g, 2 * BLOCK * j:2 * BLOCK * (j + 1), :], mask,
                                       sink_ref[KV_GROUP * g + j])
                    outs.append(jnp.dot(v_t, p.astype(BF16), preferred_element_type=F32))
                ga = _gate_cols((ga0, ga1, ga2, ga3), pair)
                o_ref[:, LANES * pair:LANES * (pair + 1)] = (
                    jnp.where(lo, outs[0], outs[1]).T * (ga * _sigmoid(ga))).astype(BF16)

    return pl.pallas_call(
        body,
        name=name,
        grid=(nb,),
        in_specs=_attn_specs(lambda t: t),
        out_specs=pl.BlockSpec((BLOCK, ATTN_WIDTH), lambda t: (t, 0)),
        out_shape=jax.ShapeDtypeStruct((S, ATTN_WIDTH + POOL_WIDTH), BF16),
        compiler_params=_params(("arbitrary",)),
    )(h, h, h, h, h, h, h, bias, masks, sinks)


DH_ATTN_COLS = U_OFF
BIAS_ROWS = KV_GROUP * 2 * BLOCK


def _attn_bwd(h, dab, dh, bias, masks, sinks, dbias_in, name):
    S = h.shape[0]
    nb = S // BLOCK

    def order(t):
        return nb - 1 - t

    def body(q_ref, kvc_ref, kvp_ref, ga0, ga1, ga2, ga3, bias_ref, mask_ref, sink_ref, da_ref, dbin_ref, dh_in_ref,
             dh_ref, dbias_ref, dsink_ref, db_ref, carry_scr):
        del dh_in_ref
        t = pl.program_id(0)

        @pl.when(t == 0)
        def _():
            dbias_ref[...] = dbin_ref[...]
            dsink_ref[...] = jnp.zeros_like(dsink_ref)
            db_ref[...] = jnp.zeros_like(db_ref)
            carry_scr[...] = jnp.zeros_like(carry_scr)

        kd, vd = _kv_operands(kvp_ref, kvc_ref)
        mask = mask_ref[...] > 0.5
        lo = _lane_lo((BLOCK, LANES))
        lo_t = _row_lo((LANES, BLOCK))
        dk_tot, dv_tot = [], []
        for g in range(N_KV_HEADS):
            k_t, k_r = kd[g].T.astype(BF16), kd[g].astype(BF16)
            v_t, v_r = vd[g].T.astype(BF16), vd[g].astype(BF16)
            dk = jnp.zeros((2 * BLOCK, LANES), F32)
            dv = jnp.zeros((2 * BLOCK, LANES), F32)
            for pr in range(KV_GROUP // 2):
                pair = (KV_GROUP // 2) * g + pr
                cols = slice(LANES * pair, LANES * (pair + 1))
                qp = q_ref[:, cols]
                qp_t = qp.T
                ga = _gate_cols((ga0, ga1, ga2, ga3), pair)
                sg = _sigmoid(ga)
                da = da_ref[:, cols]
                do_p = da * (ga * sg)
                do_t = do_p.T
                outs, dqs = [], []
                for hh in range(2):
                    j = 2 * pr + hh
                    rows = slice(2 * BLOCK * j, 2 * BLOCK * (j + 1))
                    half, half_t = (lo, lo_t) if hh == 0 else (~lo, ~lo_t)
                    qs_t = jnp.where(half_t, qp_t, 0.0).astype(BF16)
                    p, p_sink = _head_probs(k_r, qs_t, bias_ref[g, rows, :], mask, sink_ref[KV_GROUP * g + j])
                    pb = p.astype(BF16)
                    outs.append(jnp.dot(v_t, pb, preferred_element_type=F32))
                    dos_t = jnp.where(half_t, do_t, 0.0).astype(BF16)
                    dp = jnp.dot(v_r, dos_t, preferred_element_type=F32)
                    dsum = jnp.sum(p * dp, axis=0, keepdims=True)
                    ds = p * (dp - dsum)
                    dbias_ref[g, rows, :] += ds
                    tot = jnp.sum(-(p_sink * dsum), axis=1, keepdims=True)
                    dsink_ref[g, j:j + 1, :] += jnp.broadcast_to(tot, (1, LANES))
                    dsb = ds.astype(BF16)
                    dqs.append(jnp.dot(k_t, dsb, preferred_element_type=F32))
                    dk = dk + jnp.dot(dsb, jnp.where(half, qp, 0.0).astype(BF16), preferred_element_type=F32)
                    dv = dv + jnp.dot(pb, jnp.where(half, do_p, 0.0).astype(BF16), preferred_element_type=F32)
                attn = jnp.where(lo_t, outs[0], outs[1]).T
                dq = jnp.where(lo_t, dqs[0], dqs[1]).T * ATTN_SCALE
                dga = da * attn * (sg * (1.0 + ga * (1.0 - sg)))
                ga_cols = slice(GA_OFF + LANES * pair, GA_OFF + LANES * (pair + 1))
                dh_ref[:, cols] = dq.astype(BF16)
                dh_ref[:, ga_cols] = dga.astype(BF16)
                db_ref[:, cols] += jnp.sum(dq, axis=0, keepdims=True)
                db_ref[:, ga_cols] += jnp.sum(dga, axis=0, keepdims=True)
            dk = dk * ATTN_SCALE
            dk_tot.append(dk + pltpu.roll(dk, HEAD_DIM, axis=1))
            dv_tot.append(dv + pltpu.roll(dv, HEAD_DIM, axis=1))
        lo2 = _lane_lo((2 * BLOCK, LANES))
        dkv = jnp.concatenate([jnp.where(lo2, dk_tot[0], dk_tot[1]), jnp.where(lo2, dv_tot[0], dv_tot[1])], axis=1)
        dkv_done = dkv[BLOCK:, :] + carry_scr[...]
        dh_ref[:, KV_OFF:KV_OFF + 2 * KV_COLS] = dkv_done.astype(BF16)
        db_ref[:, KV_OFF:KV_OFF + 2 * KV_COLS] += jnp.sum(dkv_done, axis=0, keepdims=True)
        carry_scr[...] = dkv[:BLOCK, :]

    n_in = 12
    return pl.pallas_call(
        body,
        name=name,
        grid=(nb,),
        in_specs=_attn_specs(order) + [
            pl.BlockSpec((BLOCK, ATTN_WIDTH), lambda t: (order(t), 0)),
            pl.BlockSpec((N_KV_HEADS, BIAS_ROWS, BLOCK), lambda t: (0, 0, 0)),
            pl.BlockSpec(memory_space=pl.ANY),
        ],
        out_specs=[
            pl.BlockSpec((BLOCK, DH_ATTN_COLS), lambda t: (order(t), 0)),
            pl.BlockSpec((N_KV_HEADS, BIAS_ROWS, BLOCK), lambda t: (0, 0, 0)),
            pl.BlockSpec((N_KV_HEADS, KV_GROUP, LANES), lambda t: (0, 0, 0)),
            pl.BlockSpec((1, DH_ATTN_COLS), lambda t: (0, 0)),
        ],
        out_shape=[
            jax.ShapeDtypeStruct((S, IN_COLS), BF16),
            jax.ShapeDtypeStruct((N_KV_HEADS, BIAS_ROWS, BLOCK), F32),
            jax.ShapeDtypeStruct((N_KV_HEADS, KV_GROUP, LANES), F32),
            jax.ShapeDtypeStruct((1, DH_ATTN_COLS), F32),
        ],
        scratch_shapes=[pltpu.VMEM((BLOCK, 2 * KV_COLS), F32)],
        input_output_aliases={n_in: 0},
        compiler_params=_params(("arbitrary",)),
    )(h, h, h, h, h, h, h, bias, masks, sinks, dab, dbias_in, dh)


def _window_sum(x, w, back):
    n = x.shape[0]
    s, sh = x, 1
    while sh < w:
        s = s + pltpu.roll(s, sh if back else n - sh, axis=0)
        sh *= 2
    return s


def _pool_counts(first_row, n, w):
    t = first_row + lax.broadcasted_iota(jnp.int32, (n, 1), 0)
    return jnp.minimum(t + 1, w).astype(F32)


def _pool_diff(u_ref, uh_ref, i, T, g):
    u = u_ref[...]
    halo = jnp.where(i > 0, uh_ref[...], 0.0)
    ext = jnp.concatenate([halo, u], axis=0)
    w = POOL_WINDOWS[g]
    s = _window_sum(ext, w, back=True)[POOL_HALO:, :]
    return s / _pool_counts(i * T, T, w) - u


def _pool_in_specs(T):
    hb = T // POOL_HALO
    specs = []
    for g in range(4):
        specs.append(pl.BlockSpec((T, 256), functools.partial(lambda i, g: (i, U_OFF // 256 + g), g=g)))
        specs.append(pl.BlockSpec((POOL_HALO, 256), functools.partial(
            lambda i, g: (jnp.maximum(i * hb - 1, 0), U_OFF // 256 + g), g=g)))
    return specs


def _pool_weight_specs():
    return [pl.BlockSpec((4, 256, 256), lambda i: (0, 0, 0)), pl.BlockSpec((1, POOL_WIDTH), lambda i: (0, 0))]


def _pool_fwd(h, ab, w_pool, pool_scale, name):
    S = h.shape[0]
    T = _tile(S, 512)

    def body(*refs):
        u_refs = refs[0:8]
        gb_refs = refs[8:12]
        wp_ref, sc_ref, o_ref = refs[12], refs[13], refs[15]
        i = pl.program_id(0)
        for g in range(4):
            diff = _pool_diff(u_refs[2 * g], u_refs[2 * g + 1], i, T, g).astype(BF16)
            mixed = jnp.dot(diff, wp_ref[g], preferred_element_type=F32) * sc_ref[:, 256 * g:256 * (g + 1)]
            gb = gb_refs[g][...]
            o_ref[:, 256 * g:256 * (g + 1)] = (mixed * (gb * _sigmoid(gb))).astype(BF16)

    in_specs = _pool_in_specs(T) + [
        pl.BlockSpec((T, 256), functools.partial(lambda i, g: (i, GB_OFF // 256 + g), g=g)) for g in range(4)
    ] + _pool_weight_specs() + [pl.BlockSpec(memory_space=pl.ANY)]
    return pl.pallas_call(
        body,
        name=name,
        grid=(S // T,),
        in_specs=in_specs,
        out_specs=pl.BlockSpec((T, POOL_WIDTH), lambda i: (i, 1)),
        out_shape=jax.ShapeDtypeStruct(ab.shape, BF16),
        input_output_aliases={14: 0},
        compiler_params=_params(("arbitrary",)),
    )(*([h] * 12), w_pool, pool_scale, ab)


DH_POOL_COLS = IN_COLS // 2


def _pool_bwd(h, dab, w_pool, pool_scale, after, name):
    S = h.shape[0]
    T = _tile(S, 512)
    nt = S // T
    hb = T // POOL_HALO
    E = T + POOL_HALO
    lead = U_OFF - DH_POOL_COLS

    def body(*refs):
        u_refs = refs[0:8]
        gb_refs = refs[8:16]
        db_refs = refs[16:24]
        wp_ref, sc_ref = refs[24], refs[25]
        dh_ref, dwp_ref, dsc_ref, dbi_ref = refs[27:31]
        i = pl.program_id(0)

        @pl.when(i == 0)
        def _():
            dwp_ref[...] = jnp.zeros_like(dwp_ref)
            dsc_ref[...] = jnp.zeros_like(dsc_ref)
            dbi_ref[...] = jnp.zeros_like(dbi_ref)

        dh_ref[:, 0:lead] = jnp.zeros((T, lead), BF16)
        for g in range(4):
            w = POOL_WINDOWS[g]
            cols = slice(256 * g, 256 * (g + 1))
            scale = sc_ref[:, cols]
            wp = wp_ref[g]
            diff = _pool_diff(u_refs[2 * g], u_refs[2 * g + 1], i, T, g).astype(BF16)
            mixed = jnp.dot(diff, wp, preferred_element_type=F32)
            keep = i < nt - 1
            gb = jnp.concatenate([gb_refs[2 * g][...], jnp.where(keep, gb_refs[2 * g + 1][...], 0.0)], axis=0)
            db = jnp.concatenate([db_refs[2 * g][...], jnp.where(keep, db_refs[2 * g + 1][...], 0.0)], axis=0)
            sg = _sigmoid(gb)
            dms = db * (gb * sg)
            dmixed = (dms * scale).astype(BF16)
            ddiff = lax.dot_general(dmixed, wp, (((1,), (1,)), ((), ())), preferred_element_type=F32)
            r = ddiff / _pool_counts(i * T, E, w)
            du = _window_sum(r, w, back=False)[:T, :] - ddiff[:T, :]
            dgb = db[:T, :] * (mixed * scale) * (sg[:T, :] * (1.0 + gb[:T, :] * (1.0 - sg[:T, :])))
            u_cols = slice(lead + 256 * g, lead + 256 * (g + 1))
            gb_cols = slice(lead + POOL_WIDTH + 256 * g, lead + POOL_WIDTH + 256 * (g + 1))
            dh_ref[:, u_cols] = du.astype(BF16)
            dh_ref[:, gb_cols] = dgb.astype(BF16)
            dbi_ref[:, u_cols] += jnp.sum(du, axis=0, keepdims=True)
            dbi_ref[:, gb_cols] += jnp.sum(dgb, axis=0, keepdims=True)
            dsc_ref[:, cols] += jnp.sum(dms[:T, :] * mixed, axis=0, keepdims=True)
            dwp_ref[g] += lax.dot_general(diff, dmixed[:T, :], (((0,), (0,)), ((), ())), preferred_element_type=F32)

    def rows_after(i):
        return jnp.minimum((i + 1) * hb, S // POOL_HALO - 1)

    in_specs = _pool_in_specs(T)
    for off in (GB_OFF // 256, ATTN_WIDTH // 256):
        for g in range(4):
            in_specs.append(pl.BlockSpec((T, 256), functools.partial(lambda i, c: (i, c), c=off + g)))
            in_specs.append(pl.BlockSpec((POOL_HALO, 256), functools.partial(lambda i, c: (rows_after(i), c), c=off + g)))
    in_specs += _pool_weight_specs() + [pl.BlockSpec(memory_space=pl.ANY)]
    return pl.pallas_call(
        body,
        name=name,
        grid=(nt,),
        in_specs=in_specs,
        out_specs=[
            pl.BlockSpec((T, DH_POOL_COLS), lambda i: (i, 1)),
            pl.BlockSpec((4, 256, 256), lambda i: (0, 0, 0)),
            pl.BlockSpec((1, POOL_WIDTH), lambda i: (0, 0)),
            pl.BlockSpec((1, DH_POOL_COLS), lambda i: (0, 0)),
        ],
        out_shape=[
            jax.ShapeDtypeStruct((S, IN_COLS), BF16),
            jax.ShapeDtypeStruct((4, 256, 256), F32),
            jax.ShapeDtypeStruct((1, POOL_WIDTH), F32),
            jax.ShapeDtypeStruct((1, DH_POOL_COLS), F32),
        ],
        compiler_params=_params(("arbitrary",)),
    )(*([h] * 16), *([dab] * 8), w_pool, pool_scale, after)


def _load_resident(pairs, sems):
    @pl.when(pl.program_id(0) == 0)
    def _():
        cps = [pltpu.make_async_copy(src, dst, sems.at[n]) for n, (src, dst) in enumerate(pairs)]
        for cp in cps:
            cp.start()
        for cp in cps:
            cp.wait()


def _mix_ln_fwd(ab, x, pb, w_out, w_gate, w_ple_t, gain, bias, alpha, name):
    S = x.shape[0]
    T = _tile(S, 256)

    def body(ab_ref, x_ref, p_ref, wo_hbm, wg_hbm, wp_hbm, g_ref, b_ref,
             y_ref, yb_ref, xh_ref, rs_ref, gp_ref, pe_ref, wo, wg, wp, sems):
        _load_resident(((wo_hbm, wo), (wg_hbm, wg), (wp_hbm, wp)), sems)
        x = x_ref[...]
        mix = jnp.dot(ab_ref[...], wo[...], preferred_element_type=F32)
        gp = jnp.dot(x.astype(BF16), wg[...], preferred_element_type=F32)
        pe = lax.dot_general(p_ref[...], wp[...], (((1,), (1,)), ((), ())), preferred_element_type=F32)
        z = alpha * x + mix + _sigmoid(gp) * pe
        mu = jnp.mean(z, axis=-1, keepdims=True)
        zc = z - mu
        var = jnp.mean(zc * zc, axis=-1, keepdims=True)
        rstd = lax.rsqrt(var + LN_EPS)
        xhat = zc * rstd
        y = xhat * g_ref[...] + b_ref[...]
        y_ref[...] = y
        yb_ref[...] = y.astype(BF16)
        xh_ref[...] = xhat
        rs_ref[...] = rstd
        gp_ref[...] = gp
        pe_ref[...] = pe

    row = pl.BlockSpec((T, D_MODEL), lambda i: (i, 0))
    vec = pl.BlockSpec((1, D_MODEL), lambda i: (0, 0))
    any_spec = pl.BlockSpec(memory_space=pl.ANY)
    f32_rows = jax.ShapeDtypeStruct((S, D_MODEL), F32)
    return pl.pallas_call(
        body,
        name=name,
        grid=(S // T,),
        in_specs=[row, row, pl.BlockSpec((T, PLE_DIM), lambda i: (i, 0)), any_spec, any_spec, any_spec, vec, vec],
        out_specs=[row, row, row, pl.BlockSpec((T, 1), lambda i: (i, 0)), row, row],
        out_shape=[f32_rows, jax.ShapeDtypeStruct((S, D_MODEL), BF16), f32_rows, jax.ShapeDtypeStruct((S, 1), F32),
                   f32_rows, f32_rows],
        scratch_shapes=[pltpu.VMEM(w_out.shape, BF16), pltpu.VMEM(w_gate.shape, BF16), pltpu.VMEM(w_ple_t.shape, BF16),
                        pltpu.SemaphoreType.DMA((3,))],
        compiler_params=_params(("arbitrary",)),
    )(ab, x, pb, w_out, w_gate, w_ple_t, gain, bias)


def _ln_dmix_bwd(dy, xhat, rstd, gain, gp, pe, w_out, after, name):
    S = dy.shape[0]
    T = _tile(S, 256)

    def body(dy_ref, xh_ref, rs_ref, g_ref, gp_ref, pe_ref, wo_hbm, after_ref,
             dz_ref, dzb_ref, dpe_ref, dgp_ref, dab_ref, dgain_ref, dbias_ref, wo, sems):
        del after_ref
        _load_resident(((wo_hbm, wo),), sems)

        @pl.when(pl.program_id(0) == 0)
        def _():
            dgain_ref[...] = jnp.zeros_like(dgain_ref)
            dbias_ref[...] = jnp.zeros_like(dbias_ref)

        dy = dy_ref[...]
        xhat = xh_ref[...]
        dyg = dy * g_ref[...]
        c1 = jnp.mean(dyg, axis=-1, keepdims=True)
        c2 = jnp.mean(dyg * xhat, axis=-1, keepdims=True)
        dz = rs_ref[...] * (dyg - c1 - xhat * c2)
        dgain_ref[...] += jnp.sum(dy * xhat, axis=0, keepdims=True)
        dbias_ref[...] += jnp.sum(dy, axis=0, keepdims=True)
        sg = _sigmoid(gp_ref[...])
        dzb = dz.astype(BF16)
        dz_ref[...] = dz
        dzb_ref[...] = dzb
        dpe_ref[...] = (dz * sg).astype(BF16)
        dgp_ref[...] = (dz * pe_ref[...] * (sg * (1.0 - sg))).astype(BF16)
        dab_ref[...] = lax.dot_general(dzb, wo[...], (((1,), (1,)), ((), ())), preferred_element_type=F32)

    row = pl.BlockSpec((T, D_MODEL), lambda i: (i, 0))
    vec = pl.BlockSpec((1, D_MODEL), lambda i: (0, 0))
    any_spec = pl.BlockSpec(memory_space=pl.ANY)
    bf16_rows = jax.ShapeDtypeStruct((S, D_MODEL), BF16)
    return pl.pallas_call(
        body,
        name=name,
        grid=(S // T,),
        in_specs=[row, row, pl.BlockSpec((T, 1), lambda i: (i, 0)), vec, row, row, any_spec, any_spec],
        out_specs=[row, row, row, row, row, vec, vec],
        out_shape=[
            jax.ShapeDtypeStruct((S, D_MODEL), F32), bf16_rows, bf16_rows, bf16_rows,
            jax.ShapeDtypeStruct((S, D_MODEL), F32),
            jax.ShapeDtypeStruct((1, D_MODEL), F32),
            jax.ShapeDtypeStruct((1, D_MODEL), F32),
        ],
        scratch_shapes=[pltpu.VMEM(w_out.shape, BF16), pltpu.SemaphoreType.DMA((1,))],
        compiler_params=_params(("arbitrary",)),
    )(dy, xhat, rstd, gain, gp, pe, w_out, after)


def _loss_head(y, target):
    S = y.shape[0]
    T = _tile(S, 256)

    def body(y_ref, t_ref, dy_ref, l_ref):
        @pl.when(pl.program_id(0) == 0)
        def _():
            l_ref[...] = jnp.zeros_like(l_ref)

        err = y_ref[...] - t_ref[...]
        dy_ref[...] = err * (1.0 / D_MODEL)
        per_token = jnp.mean(err * err, axis=-1, keepdims=True)
        l_ref[...] += 0.5 * jnp.sum(per_token, axis=0, keepdims=True)

    row = pl.BlockSpec((T, D_MODEL), lambda i: (i, 0))
    return pl.pallas_call(
        body,
        name="loss_head",
        grid=(S // T,),
        in_specs=[row, row],
        out_specs=[row, pl.BlockSpec((8, LANES), lambda i: (0, 0))],
        out_shape=[jax.ShapeDtypeStruct((S, D_MODEL), F32), jax.ShapeDtypeStruct((8, LANES), F32)],
        compiler_params=_params(("arbitrary",)),
    )(y, target)


def _sum_slabs(r, name):
    _, R, C = r.shape
    T = _tile(R, 256)

    def body(r_ref, o_ref):
        acc = r_ref[0].astype(F32)
        for s in range(1, N_DEV):
            acc = acc + r_ref[s].astype(F32)
        o_ref[...] = acc

    return pl.pallas_call(
        body,
        name=name,
        grid=(R // T,),
        in_specs=[pl.BlockSpec((N_DEV, T, C), lambda i: (0, i, 0))],
        out_specs=pl.BlockSpec((T, C), lambda i: (i, 0)),
        out_shape=jax.ShapeDtypeStruct((R, C), F32),
        compiler_params=_params(("parallel",)),
    )(r)


def _adamw(w, g, m, v, name):
    R, C = w.shape
    T = _tile(R, 256)

    def body(w_ref, g_ref, m_ref, v_ref, d_ref, nm_ref, nv_ref):
        g = g_ref[...]
        m = ADAM_B1 * m_ref[...] + (1.0 - ADAM_B1) * g
        v = ADAM_B2 * v_ref[...] + (1.0 - ADAM_B2) * jnp.square(g)
        m_hat = m / (1.0 - ADAM_B1 ** ADAM_STEP)
        v_hat = v / (1.0 - ADAM_B2 ** ADAM_STEP)
        d_ref[...] = -ADAM_LR * (m_hat / (jnp.sqrt(v_hat) + ADAM_EPS) + ADAM_WD * w_ref[...])
        nm_ref[...] = m
        nv_ref[...] = v

    blk = pl.BlockSpec((T, C), lambda i: (i, 0))
    shp = jax.ShapeDtypeStruct((R, C), F32)
    return pl.pallas_call(
        body,
        name=name,
        grid=(R // T,),
        in_specs=[blk] * 4,
        out_specs=[blk] * 3,
        out_shape=[shp] * 3,
        compiler_params=_params(("parallel",)),
    )(w, g, m, v)


def _mesh_pos():
    return lax.axis_index("x"), lax.axis_index("y"), lax.axis_index("c")


def _flip(pos, k):
    x, y, c = pos
    return (1 - x if k & 4 else x, 1 - y if k & 2 else y, 1 - c if k & 1 else c)


def _index(pos):
    return 4 * pos[0] + 2 * pos[1] + pos[2]


HBM_SPEC = pl.BlockSpec(memory_space=pltpu.HBM)
SEM_SPEC = pl.BlockSpec(memory_space=pltpu.SEMAPHORE)
ANY_SPEC = pl.BlockSpec(memory_space=pl.ANY)
SPLIT_EFFECT = pltpu.SideEffectType.DATAFLOW_SIDE_EFFECTING
GATHER_FLIPS = (1, 4, 2, 6)
CHIP_FLIPS = (4, 2, 6)
TOKEN = jax.ShapeDtypeStruct((8, LANES), F32)


def _hbm(a):
    return pltpu.with_memory_space_constraint(a, pltpu.HBM)


def _hbm_like(a):
    return pltpu.HBM(a.shape, a.dtype)


def _block_rows(ref, pos, r):
    return ref.at[:, pl.ds(_index(pos) * r, r), :]


def _gather_start(shards, after, name):
    n = len(shards)
    lands = [lax.empty((s.shape[0], N_DEV * s.shape[1], s.shape[2]), s.dtype) for s in shards]

    def body(*refs):
        ins, bufs = refs[:n], refs[n:2 * n]
        send_sems, recv_sems = refs[2 * n + 1], refs[2 * n + 2]
        token = refs[4 * n + 3]
        me = _mesh_pos()
        for a in range(n):
            for j, k in enumerate(GATHER_FLIPS):
                pltpu.make_async_remote_copy(
                    src_ref=ins[a], dst_ref=_block_rows(bufs[a], me, shards[a].shape[1]),
                    send_sem=send_sems.at[4 * a + j], recv_sem=recv_sems.at[4 * a + j],
                    device_id=_flip(me, k), device_id_type=MESH_ID).start()
        token[...] = jnp.zeros_like(token)

    outs = pl.pallas_call(
        body,
        name=name,
        in_specs=[HBM_SPEC] * (2 * n) + [ANY_SPEC],
        out_specs=[SEM_SPEC, SEM_SPEC] + [HBM_SPEC] * (2 * n) + [pl.BlockSpec(memory_space=pltpu.VMEM)],
        out_shape=[pltpu.SemaphoreType.DMA((4 * n,)), pltpu.SemaphoreType.DMA((4 * n,))]
        + [_hbm_like(s) for s in shards] + [_hbm_like(b) for b in lands] + [TOKEN],
        input_output_aliases={i: 2 + i for i in range(2 * n)},
        compiler_params=pltpu.CompilerParams(has_side_effects=SPLIT_EFFECT),
    )(*[_hbm(s) for s in shards], *[_hbm(b) for b in lands], after)
    return outs[0], outs[1], outs[2:2 + n], outs[2 + n:2 + 2 * n], outs[2 + 2 * n]


def _gather_wait(started, after, name):
    send_sems, recv_sems, shards, lands, _ = started
    n = len(shards)

    def body(*refs):
        ins, bufs = refs[:n], refs[n:2 * n]
        send_sems, recv_sems = refs[2 * n], refs[2 * n + 1]
        me = _mesh_pos()
        for a in range(n):
            for j, k in enumerate(GATHER_FLIPS):
                cp = pltpu.make_async_remote_copy(
                    src_ref=ins[a], dst_ref=_block_rows(bufs[a], _flip(me, k), shards[a].shape[1]),
                    send_sem=send_sems.at[4 * a + j], recv_sem=recv_sems.at[4 * a + j],
                    device_id=_flip(me, k), device_id_type=MESH_ID)
                cp.wait_send()
                cp.wait_recv()

    outs = pl.pallas_call(
        body,
        name=name,
        in_specs=[HBM_SPEC] * (2 * n) + [SEM_SPEC, SEM_SPEC, ANY_SPEC],
        out_specs=[HBM_SPEC] * (2 * n),
        out_shape=[_hbm_like(s) for s in shards] + [_hbm_like(b) for b in lands],
        input_output_aliases={i: i for i in range(2 * n)},
        compiler_params=pltpu.CompilerParams(has_side_effects=SPLIT_EFFECT),
    )(*shards, *lands, send_sems, recv_sems, after)
    return outs[:n], outs[n:]


def _gather_pass(shards, lands, name):
    n = len(shards)

    def body(*refs):
        ins, bufs = refs[:n], refs[n:2 * n]
        token = refs[3 * n]
        send_sems, recv_sems, local_sems = refs[3 * n + 1:]
        me = _mesh_pos()
        sibling = _flip(me, 1)

        def copy(a, j, block):
            rows = _block_rows(bufs[a], block, shards[a].shape[1])
            return pltpu.make_async_remote_copy(
                src_ref=rows, dst_ref=rows, send_sem=send_sems.at[3 * a + j], recv_sem=recv_sems.at[3 * a + j],
                device_id=sibling, device_id_type=MESH_ID)

        mine = [pltpu.make_async_copy(ins[a], _block_rows(bufs[a], me, shards[a].shape[1]), local_sems.at[a])
                for a in range(n)]
        sends = [copy(a, j, _flip(me, k)) for a in range(n) for j, k in enumerate(CHIP_FLIPS)]
        for cp in mine + sends:
            cp.start()
        for a in range(n):
            for j, k in enumerate(CHIP_FLIPS):
                copy(a, j, _flip(sibling, k)).wait_recv()
        for cp in sends:
            cp.wait_send()
        for cp in mine:
            cp.wait()
        token[...] = jnp.zeros_like(token)

    outs = pl.pallas_call(
        body,
        name=name,
        in_specs=[pl.BlockSpec(memory_space=pltpu.VMEM)] * n + [ANY_SPEC] * n,
        out_specs=[ANY_SPEC] * n + [pl.BlockSpec(memory_space=pltpu.VMEM)],
        out_shape=[jax.ShapeDtypeStruct(b.shape, b.dtype) for b in lands] + [TOKEN],
        scratch_shapes=[pltpu.SemaphoreType.DMA((3 * n,)), pltpu.SemaphoreType.DMA((3 * n,)), pltpu.SemaphoreType.DMA((n,))],
        input_output_aliases={n + i: i for i in range(n)},
        compiler_params=pltpu.CompilerParams(has_side_effects=True, vmem_limit_bytes=VMEM_LIMIT_BYTES),
    )(*shards, *lands)
    return outs[:n], outs[n]


def _place_own(grads, lands, layer, name):
    n = len(grads)
    blocks = [(g.shape[0], g.shape[1] // N_DEV, g.shape[2]) for g in grads]

    def body(*refs):
        ins, bufs = refs[:n], refs[n:2 * n]
        stage, in_sems, out_sems = refs[3 * n:4 * n], refs[4 * n], refs[4 * n + 1]
        me = _mesh_pos()
        loads = [pltpu.make_async_copy(_block_rows(ins[a], me, blocks[a][1]), stage[a], in_sems.at[a]) for a in range(n)]
        stores = [pltpu.make_async_copy(stage[a], bufs[a].at[_index(me), layer], out_sems.at[a]) for a in range(n)]
        for cp in loads:
            cp.start()
        for a in range(n):
            loads[a].wait()
            stores[a].start()
        for cp in stores:
            cp.wait()

    return pl.pallas_call(
        body,
        name=name,
        in_specs=[ANY_SPEC] * (2 * n),
        out_specs=[ANY_SPEC] * n,
        out_shape=[jax.ShapeDtypeStruct(b.shape, b.dtype) for b in lands],
        scratch_shapes=[pltpu.VMEM(blk, g.dtype) for blk, g in zip(blocks, grads)]
        + [pltpu.SemaphoreType.DMA((n,)), pltpu.SemaphoreType.DMA((n,))],
        input_output_aliases={n + i: i for i in range(n)},
        compiler_params=pltpu.CompilerParams(has_side_effects=True, vmem_limit_bytes=VMEM_LIMIT_BYTES),
    )(*grads, *lands)


def _scatter_copy(ins, bufs, send_sems, recv_sems, a, k, r, layer, me, slab):
    peer = _flip(me, k)
    return pltpu.make_async_remote_copy(
        src_ref=_block_rows(ins[a], peer, r), dst_ref=bufs[a].at[_index(slab), layer],
        send_sem=send_sems.at[7 * a + k - 1], recv_sem=recv_sems.at[7 * a + k - 1],
        device_id=peer, device_id_type=MESH_ID)


def _scatter_start(grads, lands, layer, name):
    n = len(grads)

    def body(*refs):
        ins, bufs = refs[:n], refs[n:2 * n]
        send_sems, recv_sems = refs[2 * n], refs[2 * n + 1]
        token = refs[4 * n + 2]
        me = _mesh_pos()
        for a in range(n):
            for k in range(1, N_DEV):
                _scatter_copy(ins, bufs, send_sems, recv_sems, a, k, grads[a].shape[1] // N_DEV, layer, me, me).start()
        token[...] = jnp.zeros_like(token)

    outs = pl.pallas_call(
        body,
        name=name,
        in_specs=[HBM_SPEC] * (2 * n),
        out_specs=[SEM_SPEC, SEM_SPEC] + [HBM_SPEC] * (2 * n) + [pl.BlockSpec(memory_space=pltpu.VMEM)],
        out_shape=[pltpu.SemaphoreType.DMA((7 * n,)), pltpu.SemaphoreType.DMA((7 * n,))]
        + [_hbm_like(g) for g in grads] + [_hbm_like(b) for b in lands] + [TOKEN],
        input_output_aliases={i: 2 + i for i in range(2 * n)},
        compiler_params=pltpu.CompilerParams(has_side_effects=SPLIT_EFFECT),
    )(*[_hbm(g) for g in grads], *[_hbm(b) for b in lands])
    return outs[0], outs[1], outs[2:2 + n], outs[2 + n:2 + 2 * n], outs[2 + 2 * n]


def _scatter_wait(started, layer, after, name):
    send_sems, recv_sems, grads, lands, _ = started
    n = len(grads)

    def body(*refs):
        ins, bufs = refs[:n], refs[n:2 * n]
        send_sems, recv_sems = refs[2 * n], refs[2 * n + 1]
        me = _mesh_pos()
        for a in range(n):
            for k in range(1, N_DEV):
                cp = _scatter_copy(ins, bufs, send_sems, recv_sems, a, k, grads[a].shape[1] // N_DEV, layer, me, _flip(me, k))
                cp.wait_send()
                cp.wait_recv()

    outs = pl.pallas_call(
        body,
        name=name,
        in_specs=[HBM_SPEC] * (2 * n) + [SEM_SPEC, SEM_SPEC, ANY_SPEC],
        out_specs=[HBM_SPEC] * (2 * n),
        out_shape=[_hbm_like(g) for g in grads] + [_hbm_like(b) for b in lands],
        input_output_aliases={i: i for i in range(2 * n)},
        compiler_params=pltpu.CompilerParams(has_side_effects=SPLIT_EFFECT),
    )(*grads, *lands, send_sems, recv_sems, after)
    return outs[n:]


def _allreduce_small(vec):
    R, C = vec.shape

    def body(v_ref, o_ref, buf, send_sems, recv_sems):
        me = _mesh_pos()
        buf[_index(me)] = v_ref[...]
        sends = []
        for k in range(1, N_DEV):
            sends.append(pltpu.make_async_remote_copy(
                src_ref=buf.at[_index(me)], dst_ref=buf.at[_index(me)],
                send_sem=send_sems.at[k - 1], recv_sem=recv_sems.at[k - 1],
                device_id=_flip(me, k), device_id_type=MESH_ID))
        for cp in sends:
            cp.start()
        for cp in sends:
            cp.wait_recv()
        for cp in sends:
            cp.wait_send()
        acc = buf[0]
        for s in range(1, N_DEV):
            acc = acc + buf[s]
        o_ref[...] = acc

    return pl.pallas_call(
        body,
        name="allreduce_small",
        in_specs=[pl.BlockSpec(memory_space=pltpu.VMEM)],
        out_specs=pl.BlockSpec(memory_space=pltpu.VMEM),
        out_shape=jax.ShapeDtypeStruct((R, C), F32),
        scratch_shapes=[pltpu.VMEM((N_DEV, R, C), F32), pltpu.SemaphoreType.DMA((7,)), pltpu.SemaphoreType.DMA((7,))],
        compiler_params=pltpu.CompilerParams(has_side_effects=True, vmem_limit_bytes=VMEM_LIMIT_BYTES),
    )(vec)


def _pack_small(parts):
    flat = jnp.concatenate([p.reshape(-1) for p in parts])
    n = flat.shape[0]
    rows = -(-n // SMALL_COLS)
    rows = -(-rows // 8) * 8
    return jnp.pad(flat, (0, rows * SMALL_COLS - n)).reshape(rows, SMALL_COLS)


def _unpack_small(packed, like):
    flat = packed.reshape(-1)
    out, pos = [], 0
    for p in like:
        out.append(flat[pos:pos + p.size].reshape(p.shape))
        pos += p.size
    return out


def kernel(x, p, w_in, b_in, w_out, attn_sinks, rel_bias, w_pool, pool_scale, w_ple, w_gate_ple, ln_gain, ln_bias, loss_target, m_w_in, m_b_in, m_w_out, m_attn_sinks, m_rel_bias, m_w_pool, m_pool_scale, m_w_ple, m_w_gate_ple, m_ln_gain, m_ln_bias, v_w_in, v_b_in, v_w_out, v_attn_sinks, v_rel_bias, v_w_pool, v_pool_scale, v_w_ple, v_w_gate_ple, v_ln_gain, v_ln_bias):
    L = w_in.shape[0]
    S = x.shape[1]
    alpha = (2.0 * L) ** 0.25
    bucket_np, masks_np, window_np = _band_constants()
    bucket, masks, window = jnp.asarray(bucket_np), jnp.asarray(masks_np), jnp.asarray(window_np)

    w_in_s = jnp.swapaxes(w_in, 1, 2).astype(BF16)
    w_out_s = w_out.astype(BF16)
    w_gate_s = w_gate_ple.astype(BF16)
    w_ple_s = jnp.swapaxes(w_ple, 1, 2).astype(BF16)
    w_pool_s = w_pool.astype(BF16)

    def shards_of(l):
        return [w_in_s[l][None], w_out_s[l][None], w_gate_s[l][None], w_ple_s[l][None], w_pool_s[l]]

    def gathered(started, after, tag):
        shards, lands = _gather_wait(started, after, name=f"gather_wait_{tag}")
        return _gather_pass(shards, lands, name=f"gather_pass_{tag}")

    bias = _bias_build(rel_bias, bucket).reshape(N_KV_HEADS, BIAS_ROWS, BLOCK)

    xs = x[0]
    xb = xs.astype(BF16)
    first_groups = ((0, 4), (1, 2, 3))
    token, first_started = rel_bias, []
    for tag, idxs in zip("ab", first_groups):
        first_started.append(_gather_start([shards_of(0)[i] for i in idxs], token, name=f"gather_start_0{tag}"))
        token = first_started[-1][4]
    started = _gather_start(shards_of(1), token, name="gather_start_1") if L > 1 else None
    saved = []
    for l in range(L):
        pb = p[l, 0].astype(BF16)
        sinks_l = attn_sinks[l]
        scale_l = pool_scale[l].reshape(1, POOL_WIDTH)
        bias_l = b_in[l].reshape(1, IN_COLS)
        if l == 0:
            (w_in_f, w_pool_g), _ = gathered(first_started[0], started[4] if L > 1 else xb, "0a")
            w_in_t = w_in_f[0]
            h = _matmul(xb, w_in_t, tb=True, tm=512, tn=2176, tk=2048, out_dtype=F32, bias=bias_l, name=f"in_proj_{l}")
            (w_out_f, w_gate_f, w_ple_f), pass_token = gathered(first_started[1], h, "0b")
            w_out_g, w_gate_g, w_ple_t = w_out_f[0], w_gate_f[0], w_ple_f[0]
        else:
            w_in_t, w_out_g, w_gate_g, w_ple_t, w_pool_g = weights
            if l + 1 < L:
                started = _gather_start(shards_of(l + 1), pass_token, name=f"gather_start_{l + 1}")
            h = _matmul(xb, w_in_t, tb=True, tm=512, tn=2176, tk=2048, out_dtype=F32, bias=bias_l,
                        after=started[4] if l + 1 < L else None, name=f"in_proj_{l}")
        weights = (w_in_t, w_out_g, w_gate_g, w_ple_t, w_pool_g)
        ab = _attn_fwd(h, bias, masks, sinks_l, name=f"attn_fwd_{l}")
        ab = _pool_fwd(h, ab, w_pool_g, scale_l, name=f"pool_fwd_{l}")
        y, yb, xhat, rstd, gp, pe = _mix_ln_fwd(ab, xs, pb, w_out_g, w_gate_g, w_ple_t, ln_gain[l].reshape(1, D_MODEL),
                                                ln_bias[l].reshape(1, D_MODEL), alpha, name=f"mix_ln_fwd_{l}")
        saved.append((xb, pb, h, gp, pe, ab, xhat, rstd, sinks_l, scale_l, weights))
        xs, xb = y, yb
        if l + 1 < L:
            full, pass_token = gathered(started, yb, l + 1)
            weights = (full[0][0], full[1][0], full[2][0], full[3][0], full[4])

    dy, loss_tile = _loss_head(xs, loss_target[0])

    dbias = jnp.zeros((N_KV_HEADS, BIAS_ROWS, BLOCK), F32)
    sh0 = shards_of(0)
    lands_a = [lax.empty((N_DEV, L) + sh0[i].shape, BF16) for i in (1, 2, 3)]
    lands_b = [lax.empty((N_DEV, L) + sh0[i].shape, BF16) for i in (0, 4)]
    g_b_in, g_sinks, g_scale, g_gain, g_beta = [], [], [], [], []
    pend_a = pend_b = None

    def scatter(grads, lands, pending, l, tag):
        if pending:
            lands = _scatter_wait(pending[0], pending[1], grads[0], name=f"scatter_wait_{pending[1]}{tag}")
        lands = _place_own(grads, lands, l, name=f"place_own_{l}{tag}")
        return _scatter_start(grads, lands, l, name=f"scatter_start_{l}{tag}"), l

    for l in reversed(range(L)):
        xb, pb, h, gp, pe, ab, xhat, rstd, sinks_l, scale_l, weights = saved[l]
        w_in_t, w_out_g, w_gate_g, w_ple_t, w_pool_g = weights
        dz, dzb, dpe, dgp, dab, dgain, dbeta = _ln_dmix_bwd(
            dy, xhat, rstd, ln_gain[l].reshape(1, D_MODEL), gp, pe, w_out_g, pend_b[0][4] if pend_b else rel_bias,
            name=f"ln_dmix_bwd_{l}")
        g_w_out = _matmul(ab, dzb, ta=True, tm=1024, tn=2048, tk=1024, out_dtype=BF16, name=f"dw_out_{l}")
        g_w_gate = _matmul(xb, dgp, ta=True, tm=1024, tn=2048, tk=1024, out_dtype=BF16, name=f"dw_gate_{l}")
        g_w_ple_t = _matmul(dpe, pb, ta=True, tm=1024, tn=256, tk=1024, out_dtype=BF16, name=f"dw_ple_{l}")
        pend_a = scatter([g_w_out[None], g_w_gate[None], g_w_ple_t[None]], lands_a, pend_a, l, "a")
        dh, dwp, dsc, db_pool = _pool_bwd(h, dab, w_pool_g, scale_l, pend_a[0][4], name=f"pool_bwd_{l}")
        dh, dbias, dsink, db_attn = _attn_bwd(h, dab, dh, bias, masks, sinks_l, dbias, name=f"attn_bwd_{l}")
        g_w_in_t = _matmul(dh, xb, ta=True, tm=2176, tn=1024, tk=1024, out_dtype=BF16, name=f"dw_in_{l}")
        pend_b = scatter([g_w_in_t[None], dwp.astype(BF16)], lands_b, pend_b, l, "b")
        g_b_in.append(jnp.concatenate([db_attn[0], db_pool[0, U_OFF - DH_POOL_COLS:]]))
        dx = _matmul(dgp, w_gate_g, tb=True, tm=512, tn=2048, tk=2048, out_dtype=F32, add=dz, add_scale=alpha,
                     after=pend_b[0][4], name=f"dx_gate_{l}")
        dy = _matmul(dh, w_in_t, tm=512, tn=1024, tk=4352, out_dtype=F32, add=dx, name=f"dx_in_{l}")
        g_sinks.append(dsink[:, :, 0].reshape(N_HEADS))
        g_scale.append(dsc.reshape(POOL_WIDTH))
        g_gain.append(dgain.reshape(D_MODEL))
        g_beta.append(dbeta.reshape(D_MODEL))
    grad_x = dy[None]
    for lst in (g_b_in, g_sinks, g_scale, g_gain, g_beta):
        lst.reverse()
    g_rel = _bias_bwd(dbias.reshape((N_HEADS,) + BAND), bucket, window)[:, :N_HEADS]

    def big(w, g, m, v, name):
        shape = w.shape
        two_d = (shape[0] * shape[1], shape[2]) if len(shape) == 3 else (shape[0] * shape[1] * shape[2], shape[3])
        d, nm, nv = _adamw(w.reshape(two_d), g.reshape(two_d), m.reshape(two_d), v.reshape(two_d), name=name)
        return d.reshape(shape), nm.reshape(shape), nv.reshape(shape)

    r_out, r_gate, r_ple = _scatter_wait(pend_a[0], pend_a[1], dy, name="scatter_wait_0a")
    small_like = [b_in, attn_sinks, rel_bias, pool_scale, ln_gain, ln_bias]
    small_g = _allreduce_small(_pack_small([
        jnp.stack(g_b_in).reshape(L, IN_COLS), jnp.stack(g_sinks), g_rel, jnp.stack(g_scale), jnp.stack(g_gain),
        jnp.stack(g_beta), loss_tile[0, :1]]))
    grad_w_out = _sum_slabs(r_out.reshape(N_DEV, L * 256, D_MODEL), name="sum_w_out").reshape(L, 256, D_MODEL)
    grad_w_gate = _sum_slabs(r_gate.reshape(N_DEV, L * 256, D_MODEL), name="sum_w_gate").reshape(L, 256, D_MODEL)
    gt_ple = _sum_slabs(r_ple.reshape(N_DEV, L * 256, PLE_DIM), name="sum_w_ple")
    grad_w_ple = jnp.swapaxes(gt_ple.reshape(L, 256, PLE_DIM), 1, 2)
    upd_out = big(w_out, grad_w_out, m_w_out, v_w_out, "adamw_w_out")
    upd_ple = big(w_ple, grad_w_ple, m_w_ple, v_w_ple, "adamw_w_ple")
    upd_gate = big(w_gate_ple, grad_w_gate, m_w_gate_ple, v_w_gate_ple, "adamw_w_gate")

    r_in, r_pool = _scatter_wait(pend_b[0], pend_b[1], upd_gate[0], name="scatter_wait_0b")
    gt_in = _sum_slabs(r_in.reshape(N_DEV, L * 544, D_MODEL), name="sum_w_in")
    grad_w_in = jnp.swapaxes(gt_in.reshape(L, 544, D_MODEL), 1, 2)
    grad_w_pool = _sum_slabs(r_pool.reshape(N_DEV, L * 4 * 32, 256), name="sum_w_pool").reshape(L, 4, 32, 256)
    upd_in = big(w_in, grad_w_in, m_w_in, v_w_in, "adamw_w_in")
    upd_pool = big(w_pool, grad_w_pool, m_w_pool, v_w_pool, "adamw_w_pool")

    zero1 = jnp.zeros((1,), F32)
    sw = _pack_small(small_like + [zero1])
    sm = _pack_small([m_b_in, m_attn_sinks, m_rel_bias, m_pool_scale, m_ln_gain, m_ln_bias, zero1])
    sv = _pack_small([v_b_in, v_attn_sinks, v_rel_bias, v_pool_scale, v_ln_gain, v_ln_bias, zero1])
    sd, snm, snv = _adamw(sw, small_g, sm, sv, name="adamw_small")
    like = small_like + [zero1]
    sg_parts = _unpack_small(small_g, like)
    sd_parts, snm_parts, snv_parts = _unpack_small(sd, like), _unpack_small(snm, like), _unpack_small(snv, like)
    loss = sg_parts[6][0]

    def assemble(big_parts, small_parts):
        w_in_, w_out_, w_pool_, w_ple_, w_gate_ = big_parts
        b_in_, sinks_, rel_, scale_, gain_, beta_ = small_parts[:6]
        return [w_in_, b_in_, w_out_, sinks_, rel_, w_pool_, scale_, w_ple_, w_gate_, gain_, beta_]

    grads = assemble([grad_w_in, grad_w_out, grad_w_pool, grad_w_ple, grad_w_gate], sg_parts)
    ups = [upd_in, upd_out, upd_pool, upd_ple, upd_gate]
    deltas = assemble([u[0] for u in ups], sd_parts)
    new_m = assemble([u[1] for u in ups], snm_parts)
    new_v = assemble([u[2] for u in ups], snv_parts)
    return (loss, grad_x, *grads, *deltas, *new_m, *new_v)
```

```python
import functools
import math

import numpy as np
import jax
import jax.numpy as jnp
from jax import lax
from jax.experimental import pallas as pl
from jax.experimental.pallas import tpu as pltpu

F32 = jnp.float32
BF16 = jnp.bfloat16

D_MODEL = 2048
PLE_DIM = 256
ATTN_WIDTH = 1024
POOL_WIDTH = 1024
HEAD_DIM = 64
N_HEADS = 16
N_KV_HEADS = 2
KV_GROUP = 8
WINDOW = 128
BLOCK = 128
POOL_WINDOWS = (2, 4, 8, 16)
POOL_GROUP_DIM = 256
POOL_HALO = 16
REL_BUCKETS = 32
REL_MAX_DIST = 128
LN_EPS = 1e-5
KV_COLS = N_KV_HEADS * HEAD_DIM
IN_COLS = 4352
Q_OFF, KV_OFF, GA_OFF, U_OFF, GB_OFF = 0, 1024, 1280, 2304, 3328
ATTN_SCALE = 1.0 / math.sqrt(HEAD_DIM)
NEG_BIG = -1e30
LANES = 128

ADAM_LR = 0.001
ADAM_B1 = 0.9
ADAM_B2 = 0.999
ADAM_EPS = 1e-08
ADAM_WD = 0.01
ADAM_STEP = 10

N_DEV = 8
MESH_ID = pl.DeviceIdType.MESH
VMEM_LIMIT_BYTES = 52 * 1024 * 1024
SMALL_COLS = 1024


def _params(sem=None):
    return pltpu.CompilerParams(dimension_semantics=sem, vmem_limit_bytes=VMEM_LIMIT_BYTES)


def _sigmoid(x):
    return 1.0 / (1.0 + jnp.exp(-x))


def _tile(n, pref, unit=16):
    if n <= pref:
        return n
    t = pref - pref % unit
    while n % t:
        t -= unit
    assert t > 0, (n, pref)
    return t


def _matmul(a, b, *, name, ta=False, tb=False, tm, tn, tk, out_dtype, bias=None, add=None, add_scale=1.0, after=None):
    M, K = (a.shape[1], a.shape[0]) if ta else a.shape
    N = b.shape[0] if tb else b.shape[1]
    assert (b.shape[1] if tb else b.shape[0]) == K
    tm, tn, tk = _tile(M, tm), _tile(N, tn), _tile(K, tk)
    nm, nn, nk = M // tm, N // tn, K // tk
    a_spec = pl.BlockSpec((tk, tm), lambda j, i, k: (k, i)) if ta else pl.BlockSpec((tm, tk), lambda j, i, k: (i, k))
    b_spec = pl.BlockSpec((tn, tk), lambda j, i, k: (j, k)) if tb else pl.BlockSpec((tk, tn), lambda j, i, k: (k, j))
    dims = (((0 if ta else 1,), (1 if tb else 0,)), ((), ()))
    operands, in_specs = [a, b], [a_spec, b_spec]
    if bias is not None:
        operands.append(bias)
        in_specs.append(pl.BlockSpec((1, tn), lambda j, i, k: (0, j)))
    if add is not None:
        operands.append(add)
        in_specs.append(pl.BlockSpec((tm, tn), lambda j, i, k: (i, j)))
    if after is not None:
        operands.append(after)
        in_specs.append(pl.BlockSpec(memory_space=pl.ANY))

    def body(*refs):
        a_ref, b_ref = refs[0], refs[1]
        pos = 2
        bias_ref = add_ref = None
        if bias is not None:
            bias_ref = refs[pos]
            pos += 1
        if add is not None:
            add_ref = refs[pos]
            pos += 1
        if after is not None:
            pos += 1
        o_ref = refs[pos]
        part = lax.dot_general(a_ref[...].astype(BF16), b_ref[...].astype(BF16), dims, preferred_element_type=F32)

        def finish(acc):
            if bias_ref is not None:
                acc = acc + bias_ref[...]
            if add_ref is not None:
                acc = acc + add_scale * add_ref[...].astype(F32)
            o_ref[...] = acc.astype(out_dtype)

        if nk == 1:
            finish(part)
        else:
            acc_ref = refs[pos + 1]
            k = pl.program_id(2)

            @pl.when(k == 0)
            def _():
                acc_ref[...] = part

            @pl.when(k > 0)
            def _():
                acc_ref[...] += part

            @pl.when(k == nk - 1)
            def _():
                finish(acc_ref[...])

    return pl.pallas_call(
        body,
        name=name,
        grid=(nn, nm, nk),
        in_specs=in_specs,
        out_specs=pl.BlockSpec((tm, tn), lambda j, i, k: (i, j)),
        out_shape=jax.ShapeDtypeStruct((M, N), out_dtype),
        scratch_shapes=[pltpu.VMEM((tm, tn), F32)] if nk > 1 else [],
        compiler_params=_params(("parallel", "parallel", "arbitrary")),
    )(*operands)


BAND = (2 * BLOCK, BLOCK)
BIAS_ROWS = KV_GROUP * 2 * BLOCK


def _band_constants():
    qq = np.arange(BLOCK)[None, :]
    kk = np.arange(2 * BLOCK)[:, None]
    dist = qq + BLOCK - kk
    in_window = (dist >= 0) & (dist < WINDOW)
    max_exact = REL_BUCKETS // 2
    d = np.maximum(dist, 0)
    d_f = np.maximum(d, 1).astype(np.float32)
    large = max_exact + (
        np.log(d_f / np.float32(max_exact)) / np.float32(math.log(REL_MAX_DIST / max_exact)) * np.float32(REL_BUCKETS - max_exact)
    ).astype(np.int32)
    large = np.minimum(large, REL_BUCKETS - 1)
    bucket = np.where(d < max_exact, d, large).astype(np.int32)
    bucket = np.where(in_window, bucket, 0).astype(np.int32)
    first = in_window & (kk >= BLOCK)
    masks = np.stack([first, in_window]).astype(np.float32)
    return bucket, masks, in_window.astype(np.float32)


def _bias_build(rel_bias, bucket):
    def body(rb_ref, bkt_ref, o_ref):
        h = pl.program_id(0)
        bkt = bkt_ref[...]

        def step(b, acc):
            return jnp.where(bkt == b, rb_ref[b, h], acc)

        o_ref[0] = lax.fori_loop(0, REL_BUCKETS, step, jnp.zeros(BAND, F32))

    return pl.pallas_call(
        body,
        name="bias_build",
        grid=(N_HEADS,),
        in_specs=[pl.BlockSpec(memory_space=pltpu.SMEM), pl.BlockSpec(BAND, lambda h: (0, 0))],
        out_specs=pl.BlockSpec((1,) + BAND, lambda h: (h, 0, 0)),
        out_shape=jax.ShapeDtypeStruct((N_HEADS,) + BAND, F32),
        compiler_params=_params(("arbitrary",)),
    )(rel_bias, bucket)


def _bias_bwd(dbias, bucket, window):
    def body(db_ref, bkt_ref, win_ref, o_ref):
        h = pl.program_id(0)

        @pl.when(h == 0)
        def _():
            o_ref[...] = jnp.zeros_like(o_ref)

        bkt = bkt_ref[...]
        x = jnp.where(win_ref[...] > 0.5, db_ref[0], 0.0)
        row = lax.broadcasted_iota(jnp.int32, (REL_BUCKETS, LANES), 0)
        col = lax.broadcasted_iota(jnp.int32, (REL_BUCKETS, LANES), 1)

        def step(b, acc):
            s = jnp.sum(jnp.where(bkt == b, x, 0.0), axis=0, keepdims=True)
            return acc + jnp.where(row == b, s, 0.0)

        per_lane = lax.fori_loop(0, REL_BUCKETS, step, jnp.zeros((REL_BUCKETS, LANES), F32))
        o_ref[...] += jnp.where(col == h, jnp.sum(per_lane, axis=1, keepdims=True), 0.0)

    return pl.pallas_call(
        body,
        name="bias_bwd",
        grid=(N_HEADS,),
        in_specs=[
            pl.BlockSpec((1,) + BAND, lambda h: (h, 0, 0)),
            pl.BlockSpec(BAND, lambda h: (0, 0)),
            pl.BlockSpec(BAND, lambda h: (0, 0)),
        ],
        out_specs=pl.BlockSpec((REL_BUCKETS, LANES), lambda h: (0, 0)),
        out_shape=jax.ShapeDtypeStruct((REL_BUCKETS, LANES), F32),
        compiler_params=_params(("arbitrary",)),
    )(dbias, bucket, window)


def _lane_lo(shape):
    return lax.broadcasted_iota(jnp.int32, shape, 1) < HEAD_DIM


def _row_lo(shape):
    return lax.broadcasted_iota(jnp.int32, shape, 0) < HEAD_DIM


def _dup_heads(x):
    r = pltpu.roll(x, HEAD_DIM, axis=1)
    lo = _lane_lo(x.shape)
    return jnp.where(lo, x, r), jnp.where(lo, r, x)


def _kv_operands(kvp_ref, kvc_ref):
    kvp, kvc = kvp_ref[...], kvc_ref[...]
    k2 = jnp.concatenate([kvp[:, :KV_COLS], kvc[:, :KV_COLS]], axis=0)
    v2 = jnp.concatenate([kvp[:, KV_COLS:], kvc[:, KV_COLS:]], axis=0)
    return _dup_heads(k2), _dup_heads(v2)


def _head_probs(k_r, qs_t, bias, mask, sink):
    s = jnp.dot(k_r, qs_t, preferred_element_type=F32) * ATTN_SCALE + bias
    s = jnp.where(mask, s, NEG_BIG)
    m = jnp.maximum(jnp.max(s, axis=0, keepdims=True), sink)
    e = jnp.exp(s - m)
    e_sink = jnp.exp(sink - m)
    inv = 1.0 / (jnp.sum(e, axis=0, keepdims=True) + e_sink)
    return e * inv, e_sink * inv


def _gate_cols(ga_refs, pair):
    off = LANES * (pair % 2)
    return ga_refs[pair // 2][:, off:off + LANES]


def _attn_specs(order):
    return [
        pl.BlockSpec((BLOCK, ATTN_WIDTH), lambda t: (order(t), Q_OFF // ATTN_WIDTH)),
        pl.BlockSpec((BLOCK, 2 * KV_COLS), lambda t: (order(t), KV_OFF // (2 * KV_COLS))),
        pl.BlockSpec((BLOCK, 2 * KV_COLS), lambda t: (jnp.maximum(order(t) - 1, 0), KV_OFF // (2 * KV_COLS))),
    ] + [
        pl.BlockSpec((BLOCK, 256), functools.partial(lambda t, c: (order(t), GA_OFF // 256 + c), c=c)) for c in range(4)
    ] + [
        pl.BlockSpec((N_KV_HEADS, BIAS_ROWS, BLOCK), lambda t: (0, 0, 0)),
        pl.BlockSpec((None,) + BAND, lambda t: (jnp.minimum(order(t), 1), 0, 0)),
        pl.BlockSpec(memory_space=pltpu.SMEM),
    ]


def _attn_fwd(h, bias, masks, sinks, name):
    S = h.shape[0]
    nb = S // BLOCK

    def body(q_ref, kvc_ref, kvp_ref, ga0, ga1, ga2, ga3, bias_ref, mask_ref, sink_ref, o_ref):
        kd, vd = _kv_operands(kvp_ref, kvc_ref)
        mask = mask_ref[...] > 0.5
        lo = _row_lo((LANES, BLOCK))
        for g in range(N_KV_HEADS):
            k_r = kd[g].astype(BF16)
            v_t = vd[g].T.astype(BF16)
            for pr in range(KV_GROUP // 2):
                pair = (KV_GROUP // 2) * g + pr
                qp_t = q_ref[:, LANES * pair:LANES * (pair + 1)].T
                outs = []
                for hh in range(2):
                    j = 2 * pr + hh
                    qs_t = jnp.where(lo if hh == 0 else ~lo, qp_t, 0.0).astype(BF16)
                    p, _ = _head_probs(k_r, qs_t, bias_ref[g, 2 * BLOCK * j:2 * BLOCK * (j + 1), :], mask,
                                       sink_ref[KV_GROUP * g + j])
                    outs.append(jnp.dot(v_t, p.astype(BF16), preferred_element_type=F32))
                ga = _gate_cols((ga0, ga1, ga2, ga3), pair)
                o_ref[:, LANES * pair:LANES * (pair + 1)] = (
                    jnp.where(lo, outs[0], outs[1]).T * (ga * _sigmoid(ga))).astype(BF16)

    return pl.pallas_call(
        body,
        name=name,
        grid=(nb,),
        in_specs=_attn_specs(lambda t: t),
        out_specs=pl.BlockSpec((BLOCK, ATTN_WIDTH), lambda t: (t, 0)),
        out_shape=jax.ShapeDtypeStruct((S, ATTN_WIDTH + POOL_WIDTH), BF16),
        compiler_params=_params(("arbitrary",)),
    )(h, h, h, h, h, h, h, bias, masks, sinks)


DH_ATTN_COLS = U_OFF


def _attn_bwd(h, dab, dh, bias, masks, sinks, dbias_in, name):
    S = h.shape[0]
    nb = S // BLOCK

    def order(t):
        return nb - 1 - t

    def body(q_ref, kvc_ref, kvp_ref, ga0, ga1, ga2, ga3, bias_ref, mask_ref, sink_ref, da_ref, dbin_ref, dh_in_ref,
             dh_ref, dbias_ref, dsink_ref, db_ref, carry_scr):
        del dh_in_ref
        t = pl.program_id(0)

        @pl.when(t == 0)
        def _():
            dbias_ref[...] = dbin_ref[...]
            dsink_ref[...] = jnp.zeros_like(dsink_ref)
            db_ref[...] = jnp.zeros_like(db_ref)
            carry_scr[...] = jnp.zeros_like(carry_scr)

        kd, vd = _kv_operands(kvp_ref, kvc_ref)
        mask = mask_ref[...] > 0.5
        lo = _lane_lo((BLOCK, LANES))
        lo_t = _row_lo((LANES, BLOCK))
        dk_tot, dv_tot = [], []
        for g in range(N_KV_HEADS):
            k_t, k_r = kd[g].T.astype(BF16), kd[g].astype(BF16)
            v_t, v_r = vd[g].T.astype(BF16), vd[g].astype(BF16)
            dk = jnp.zeros((2 * BLOCK, LANES), F32)
            dv = jnp.zeros((2 * BLOCK, LANES), F32)
            for pr in range(KV_GROUP // 2):
                pair = (KV_GROUP // 2) * g + pr
                cols = slice(LANES * pair, LANES * (pair + 1))
                qp = q_ref[:, cols]
                qp_t = qp.T
                ga = _gate_cols((ga0, ga1, ga2, ga3), pair)
                sg = _sigmoid(ga)
                da = da_ref[:, cols]
                do_p = da * (ga * sg)
                do_t = do_p.T
                outs, dqs = [], []
                for hh in range(2):
                    j = 2 * pr + hh
                    rows = slice(2 * BLOCK * j, 2 * BLOCK * (j + 1))
                    half, half_t = (lo, lo_t) if hh == 0 else (~lo, ~lo_t)
                    qs_t = jnp.where(half_t, qp_t, 0.0).astype(BF16)
                    p, p_sink = _head_probs(k_r, qs_t, bias_ref[g, rows, :], mask, sink_ref[KV_GROUP * g + j])
                    pb = p.astype(BF16)
                    outs.append(jnp.dot(v_t, pb, preferred_element_type=F32))
                    dos_t = jnp.where(half_t, do_t, 0.0).astype(BF16)
                    dp = jnp.dot(v_r, dos_t, preferred_element_type=F32)
                    dsum = jnp.sum(p * dp, axis=0, keepdims=True)
                    ds = p * (dp - dsum)
                    dbias_ref[g, rows, :] += ds
                    tot = jnp.sum(-(p_sink * dsum), axis=1, keepdims=True)
                    dsink_ref[g, j:j + 1, :] += jnp.broadcast_to(tot, (1, LANES))
                    dsb = ds.astype(BF16)
                    dqs.append(jnp.dot(k_t, dsb, preferred_element_type=F32))
                    dk = dk + jnp.dot(dsb, jnp.where(half, qp, 0.0).astype(BF16), preferred_element_type=F32)
                    dv = dv + jnp.dot(pb, jnp.where(half, do_p, 0.0).astype(BF16), preferred_element_type=F32)
                attn = jnp.where(lo_t, outs[0], outs[1]).T
                dq = jnp.where(lo_t, dqs[0], dqs[1]).T * ATTN_SCALE
                dga = da * attn * (sg * (1.0 + ga * (1.0 - sg)))
                ga_cols = slice(GA_OFF + LANES * pair, GA_OFF + LANES * (pair + 1))
                dh_ref[:, cols] = dq.astype(BF16)
                dh_ref[:, ga_cols] = dga.astype(BF16)
                db_ref[:, cols] += jnp.sum(dq, axis=0, keepdims=True)
                db_ref[:, ga_cols] += jnp.sum(dga, axis=0, keepdims=True)
            dk = dk * ATTN_SCALE
            dk_tot.append(dk + pltpu.roll(dk, HEAD_DIM, axis=1))
            dv_tot.append(dv + pltpu.roll(dv, HEAD_DIM, axis=1))
        lo2 = _lane_lo((2 * BLOCK, LANES))
        dkv = jnp.concatenate([jnp.where(lo2, dk_tot[0], dk_tot[1]), jnp.where(lo2, dv_tot[0], dv_tot[1])], axis=1)
        dkv_done = dkv[BLOCK:, :] + carry_scr[...]
        dh_ref[:, KV_OFF:KV_OFF + 2 * KV_COLS] = dkv_done.astype(BF16)
        db_ref[:, KV_OFF:KV_OFF + 2 * KV_COLS] += jnp.sum(dkv_done, axis=0, keepdims=True)
        carry_scr[...] = dkv[:BLOCK, :]

    n_in = 12
    return pl.pallas_call(
        body,
        name=name,
        grid=(nb,),
        in_specs=_attn_specs(order) + [
            pl.BlockSpec((BLOCK, ATTN_WIDTH), lambda t: (order(t), 0)),
            pl.BlockSpec((N_KV_HEADS, BIAS_ROWS, BLOCK), lambda t: (0, 0, 0)),
            pl.BlockSpec(memory_space=pl.ANY),
        ],
        out_specs=[
            pl.BlockSpec((BLOCK, DH_ATTN_COLS), lambda t: (order(t), 0)),
            pl.BlockSpec((N_KV_HEADS, BIAS_ROWS, BLOCK), lambda t: (0, 0, 0)),
            pl.BlockSpec((N_KV_HEADS, KV_GROUP, LANES), lambda t: (0, 0, 0)),
            pl.BlockSpec((1, DH_ATTN_COLS), lambda t: (0, 0)),
        ],
        out_shape=[
            jax.ShapeDtypeStruct((S, IN_COLS), BF16),
            jax.ShapeDtypeStruct((N_KV_HEADS, BIAS_ROWS, BLOCK), F32),
            jax.ShapeDtypeStruct((N_KV_HEADS, KV_GROUP, LANES), F32),
            jax.ShapeDtypeStruct((1, DH_ATTN_COLS), F32),
        ],
        scratch_shapes=[pltpu.VMEM((BLOCK, 2 * KV_COLS), F32)],
        input_output_aliases={n_in: 0},
        compiler_params=_params(("arbitrary",)),
    )(h, h, h, h, h, h, h, bias, masks, sinks, dab, dbias_in, dh)


def _window_sum(x, w, back):
    n = x.shape[0]
    s, sh = x, 1
    while sh < w:
        s = s + pltpu.roll(s, sh if back else n - sh, axis=0)
        sh *= 2
    return s


def _pool_counts(first_row, n, w):
    t = first_row + lax.broadcasted_iota(jnp.int32, (n, 1), 0)
    return jnp.minimum(t + 1, w).astype(F32)


def _pool_diff(u_ref, uh_ref, i, T, g):
    u = u_ref[...]
    halo = jnp.where(i > 0, uh_ref[...], 0.0)
    ext = jnp.concatenate([halo, u], axis=0)
    w = POOL_WINDOWS[g]
    s = _window_sum(ext, w, back=True)[POOL_HALO:, :]
    return s / _pool_counts(i * T, T, w) - u


def _pool_in_specs(T):
    hb = T // POOL_HALO
    specs = []
    for g in range(4):
        specs.append(pl.BlockSpec((T, 256), functools.partial(lambda i, g: (i, U_OFF // 256 + g), g=g)))
        specs.append(pl.BlockSpec((POOL_HALO, 256), functools.partial(
            lambda i, g: (jnp.maximum(i * hb - 1, 0), U_OFF // 256 + g), g=g)))
    return specs


def _pool_weight_specs():
    return [pl.BlockSpec((4, 256, 256), lambda i: (0, 0, 0)), pl.BlockSpec((1, POOL_WIDTH), lambda i: (0, 0))]


def _pool_fwd(h, ab, w_pool, pool_scale, name):
    S = h.shape[0]
    T = _tile(S, 512)

    def body(*refs):
        u_refs = refs[0:8]
        gb_refs = refs[8:12]
        wp_ref, sc_ref, o_ref = refs[12], refs[13], refs[15]
        i = pl.program_id(0)
        for g in range(4):
            diff = _pool_diff(u_refs[2 * g], u_refs[2 * g + 1], i, T, g).astype(BF16)
            mixed = jnp.dot(diff, wp_ref[g], preferred_element_type=F32) * sc_ref[:, 256 * g:256 * (g + 1)]
            gb = gb_refs[g][...]
            o_ref[:, 256 * g:256 * (g + 1)] = (mixed * (gb * _sigmoid(gb))).astype(BF16)

    in_specs = _pool_in_specs(T) + [
        pl.BlockSpec((T, 256), functools.partial(lambda i, g: (i, GB_OFF // 256 + g), g=g)) for g in range(4)
    ] + _pool_weight_specs() + [pl.BlockSpec(memory_space=pl.ANY)]
    return pl.pallas_call(
        body,
        name=name,
        grid=(S // T,),
        in_specs=in_specs,
        out_specs=pl.BlockSpec((T, POOL_WIDTH), lambda i: (i, 1)),
        out_shape=jax.ShapeDtypeStruct(ab.shape, BF16),
        input_output_aliases={14: 0},
        compiler_params=_params(("arbitrary",)),
    )(*([h] * 12), w_pool, pool_scale, ab)


DH_POOL_COLS = IN_COLS // 2


def _pool_bwd(h, dab, w_pool, pool_scale, after, name):
    S = h.shape[0]
    T = _tile(S, 512)
    nt = S // T
    hb = T // POOL_HALO
    E = T + POOL_HALO
    lead = U_OFF - DH_POOL_COLS

    def body(*refs):
        u_refs = refs[0:8]
        gb_refs = refs[8:16]
        db_refs = refs[16:24]
        wp_ref, sc_ref = refs[24], refs[25]
        dh_ref, dwp_ref, dsc_ref, dbi_ref = refs[27:31]
        i = pl.program_id(0)

        @pl.when(i == 0)
        def _():
            dwp_ref[...] = jnp.zeros_like(dwp_ref)
            dsc_ref[...] = jnp.zeros_like(dsc_ref)
            dbi_ref[...] = jnp.zeros_like(dbi_ref)

        dh_ref[:, 0:lead] = jnp.zeros((T, lead), BF16)
        for g in range(4):
            w = POOL_WINDOWS[g]
            cols = slice(256 * g, 256 * (g + 1))
            scale = sc_ref[:, cols]
            wp = wp_ref[g]
            diff = _pool_diff(u_refs[2 * g], u_refs[2 * g + 1], i, T, g).astype(BF16)
            mixed = jnp.dot(diff, wp, preferred_element_type=F32)
            keep = i < nt - 1
            gb = jnp.concatenate([gb_refs[2 * g][...], jnp.where(keep, gb_refs[2 * g + 1][...], 0.0)], axis=0)
            db = jnp.concatenate([db_refs[2 * g][...], jnp.where(keep, db_refs[2 * g + 1][...], 0.0)], axis=0)
            sg = _sigmoid(gb)
            dms = db * (gb * sg)
            dmixed = (dms * scale).astype(BF16)
            ddiff = lax.dot_general(dmixed, wp, (((1,), (1,)), ((), ())), preferred_element_type=F32)
            r = ddiff / _pool_counts(i * T, E, w)
            du = _window_sum(r, w, back=False)[:T, :] - ddiff[:T, :]
            dgb = db[:T, :] * (mixed * scale) * (sg[:T, :] * (1.0 + gb[:T, :] * (1.0 - sg[:T, :])))
            u_cols = slice(lead + 256 * g, lead + 256 * (g + 1))
            gb_cols = slice(lead + POOL_WIDTH + 256 * g, lead + POOL_WIDTH + 256 * (g + 1))
            dh_ref[:, u_cols] = du.astype(BF16)
            dh_ref[:, gb_cols] = dgb.astype(BF16)
            dbi_ref[:, u_cols] += jnp.sum(du, axis=0, keepdims=True)
            dbi_ref[:, gb_cols] += jnp.sum(dgb, axis=0, keepdims=True)
            dsc_ref[:, cols] += jnp.sum(dms[:T, :] * mixed, axis=0, keepdims=True)
            dwp_ref[g] += lax.dot_general(diff, dmixed[:T, :], (((0,), (0,)), ((), ())), preferred_element_type=F32)

    def rows_after(i):
        return jnp.minimum((i + 1) * hb, S // POOL_HALO - 1)

    in_specs = _pool_in_specs(T)
    for off in (GB_OFF // 256, ATTN_WIDTH // 256):
        for g in range(4):
            in_specs.append(pl.BlockSpec((T, 256), functools.partial(lambda i, c: (i, c), c=off + g)))
            in_specs.append(pl.BlockSpec((POOL_HALO, 256), functools.partial(lambda i, c: (rows_after(i), c), c=off + g)))
    in_specs += _pool_weight_specs() + [pl.BlockSpec(memory_space=pl.ANY)]
    return pl.pallas_call(
        body,
        name=name,
        grid=(nt,),
        in_specs=in_specs,
        out_specs=[
            pl.BlockSpec((T, DH_POOL_COLS), lambda i: (i, 1)),
            pl.BlockSpec((4, 256, 256), lambda i: (0, 0, 0)),
            pl.BlockSpec((1, POOL_WIDTH), lambda i: (0, 0)),
            pl.BlockSpec((1, DH_POOL_COLS), lambda i: (0, 0)),
        ],
        out_shape=[
            jax.ShapeDtypeStruct((S, IN_COLS), BF16),
            jax.ShapeDtypeStruct((4, 256, 256), F32),
            jax.ShapeDtypeStruct((1, POOL_WIDTH), F32),
            jax.ShapeDtypeStruct((1, DH_POOL_COLS), F32),
        ],
        compiler_params=_params(("arbitrary",)),
    )(*([h] * 16), *([dab] * 8), w_pool, pool_scale, after)


def _load_resident(pairs, sems):
    @pl.when(pl.program_id(0) == 0)
    def _():
        cps = [pltpu.make_async_copy(src, dst, sems.at[n]) for n, (src, dst) in enumerate(pairs)]
        for cp in cps:
            cp.start()
        for cp in cps:
            cp.wait()


def _mix_ln_fwd(ab, x, pb, w_out, w_gate, w_ple_t, gain, bias, alpha, name):
    S = x.shape[0]
    T = _tile(S, 256)

    def body(ab_ref, x_ref, p_ref, wo_hbm, wg_hbm, wp_hbm, g_ref, b_ref,
             y_ref, yb_ref, xh_ref, rs_ref, gp_ref, pe_ref, wo, wg, wp, sems):
        _load_resident(((wo_hbm, wo), (wg_hbm, wg), (wp_hbm, wp)), sems)
        x = x_ref[...]
        mix = jnp.dot(ab_ref[...], wo[...], preferred_element_type=F32)
        gp = jnp.dot(x.astype(BF16), wg[...], preferred_element_type=F32)
        pe = lax.dot_general(p_ref[...], wp[...], (((1,), (1,)), ((), ())), preferred_element_type=F32)
        z = alpha * x + mix + _sigmoid(gp) * pe
        mu = jnp.mean(z, axis=-1, keepdims=True)
        zc = z - mu
        var = jnp.mean(zc * zc, axis=-1, keepdims=True)
        rstd = lax.rsqrt(var + LN_EPS)
        xhat = zc * rstd
        y = xhat * g_ref[...] + b_ref[...]
        y_ref[...] = y
        yb_ref[...] = y.astype(BF16)
        xh_ref[...] = xhat
        rs_ref[...] = rstd
        gp_ref[...] = gp
        pe_ref[...] = pe

    row = pl.BlockSpec((T, D_MODEL), lambda i: (i, 0))
    vec = pl.BlockSpec((1, D_MODEL), lambda i: (0, 0))
    any_spec = pl.BlockSpec(memory_space=pl.ANY)
    f32_rows = jax.ShapeDtypeStruct((S, D_MODEL), F32)
    return pl.pallas_call(
        body,
        name=name,
        grid=(S // T,),
        in_specs=[row, row, pl.BlockSpec((T, PLE_DIM), lambda i: (i, 0)), any_spec, any_spec, any_spec, vec, vec],
        out_specs=[row, row, row, pl.BlockSpec((T, 1), lambda i: (i, 0)), row, row],
        out_shape=[f32_rows, jax.ShapeDtypeStruct((S, D_MODEL), BF16), f32_rows, jax.ShapeDtypeStruct((S, 1), F32),
                   f32_rows, f32_rows],
        scratch_shapes=[pltpu.VMEM(w_out.shape, BF16), pltpu.VMEM(w_gate.shape, BF16), pltpu.VMEM(w_ple_t.shape, BF16),
                        pltpu.SemaphoreType.DMA((3,))],
        compiler_params=_params(("arbitrary",)),
    )(ab, x, pb, w_out, w_gate, w_ple_t, gain, bias)


def _ln_dmix_bwd(dy, xhat, rstd, gain, gp, pe, w_out, after, name):
    S = dy.shape[0]
    T = _tile(S, 256)

    def body(dy_ref, xh_ref, rs_ref, g_ref, gp_ref, pe_ref, wo_hbm, after_ref,
             dz_ref, dzb_ref, dpe_ref, dgp_ref, dab_ref, dgain_ref, dbias_ref, wo, sems):
        del after_ref
        _load_resident(((wo_hbm, wo),), sems)

        @pl.when(pl.program_id(0) == 0)
        def _():
            dgain_ref[...] = jnp.zeros_like(dgain_ref)
            dbias_ref[...] = jnp.zeros_like(dbias_ref)

        dy = dy_ref[...]
        xhat = xh_ref[...]
        dyg = dy * g_ref[...]
        c1 = jnp.mean(dyg, axis=-1, keepdims=True)
        c2 = jnp.mean(dyg * xhat, axis=-1, keepdims=True)
        dz = rs_ref[...] * (dyg - c1 - xhat * c2)
        dgain_ref[...] += jnp.sum(dy * xhat, axis=0, keepdims=True)
        dbias_ref[...] += jnp.sum(dy, axis=0, keepdims=True)
        sg = _sigmoid(gp_ref[...])
        dzb = dz.astype(BF16)
        dz_ref[...] = dz
        dzb_ref[...] = dzb
        dpe_ref[...] = (dz * sg).astype(BF16)
        dgp_ref[...] = (dz * pe_ref[...] * (sg * (1.0 - sg))).astype(BF16)
        dab_ref[...] = lax.dot_general(dzb, wo[...], (((1,), (1,)), ((), ())), preferred_element_type=F32)

    row = pl.BlockSpec((T, D_MODEL), lambda i: (i, 0))
    vec = pl.BlockSpec((1, D_MODEL), lambda i: (0, 0))
    any_spec = pl.BlockSpec(memory_space=pl.ANY)
    bf16_rows = jax.ShapeDtypeStruct((S, D_MODEL), BF16)
    return pl.pallas_call(
        body,
        name=name,
        grid=(S // T,),
        in_specs=[row, row, pl.BlockSpec((T, 1), lambda i: (i, 0)), vec, row, row, any_spec, any_spec],
        out_specs=[row, row, row, row, row, vec, vec],
        out_shape=[
            jax.ShapeDtypeStruct((S, D_MODEL), F32), bf16_rows, bf16_rows, bf16_rows,
            jax.ShapeDtypeStruct((S, D_MODEL), F32),
            jax.ShapeDtypeStruct((1, D_MODEL), F32),
            jax.ShapeDtypeStruct((1, D_MODEL), F32),
        ],
        scratch_shapes=[pltpu.VMEM(w_out.shape, BF16), pltpu.SemaphoreType.DMA((1,))],
        compiler_params=_params(("arbitrary",)),
    )(dy, xhat, rstd, gain, gp, pe, w_out, after)


def _loss_head(y, target):
    S = y.shape[0]
    T = _tile(S, 256)

    def body(y_ref, t_ref, dy_ref, l_ref):
        @pl.when(pl.program_id(0) == 0)
        def _():
            l_ref[...] = jnp.zeros_like(l_ref)

        err = y_ref[...] - t_ref[...]
        dy_ref[...] = err * (1.0 / D_MODEL)
        per_token = jnp.mean(err * err, axis=-1, keepdims=True)
        l_ref[...] += 0.5 * jnp.sum(per_token, axis=0, keepdims=True)

    row = pl.BlockSpec((T, D_MODEL), lambda i: (i, 0))
    return pl.pallas_call(
        body,
        name="loss_head",
        grid=(S // T,),
        in_specs=[row, row],
        out_specs=[row, pl.BlockSpec((8, LANES), lambda i: (0, 0))],
        out_shape=[jax.ShapeDtypeStruct((S, D_MODEL), F32), jax.ShapeDtypeStruct((8, LANES), F32)],
        compiler_params=_params(("arbitrary",)),
    )(y, target)


def _sum_slabs(r, name):
    _, R, C = r.shape
    T = _tile(R, 256)

    def body(r_ref, o_ref):
        acc = r_ref[0].astype(F32)
        for s in range(1, N_DEV):
            acc = acc + r_ref[s].astype(F32)
        o_ref[...] = acc

    return pl.pallas_call(
        body,
        name=name,
        grid=(R // T,),
        in_specs=[pl.BlockSpec((N_DEV, T, C), lambda i: (0, i, 0))],
        out_specs=pl.BlockSpec((T, C), lambda i: (i, 0)),
        out_shape=jax.ShapeDtypeStruct((R, C), F32),
        compiler_params=_params(("parallel",)),
    )(r)


def _adamw(w, g, m, v, name):
    R, C = w.shape
    T = _tile(R, 256)

    def body(w_ref, g_ref, m_ref, v_ref, d_ref, nm_ref, nv_ref):
        g = g_ref[...]
        m = ADAM_B1 * m_ref[...] + (1.0 - ADAM_B1) * g
        v = ADAM_B2 * v_ref[...] + (1.0 - ADAM_B2) * jnp.square(g)
        m_hat = m / (1.0 - ADAM_B1 ** ADAM_STEP)
        v_hat = v / (1.0 - ADAM_B2 ** ADAM_STEP)
        d_ref[...] = -ADAM_LR * (m_hat / (jnp.sqrt(v_hat) + ADAM_EPS) + ADAM_WD * w_ref[...])
        nm_ref[...] = m
        nv_ref[...] = v

    blk = pl.BlockSpec((T, C), lambda i: (i, 0))
    shp = jax.ShapeDtypeStruct((R, C), F32)
    return pl.pallas_call(
        body,
        name=name,
        grid=(R // T,),
        in_specs=[blk] * 4,
        out_specs=[blk] * 3,
        out_shape=[shp] * 3,
        compiler_params=_params(("parallel",)),
    )(w, g, m, v)


def _mesh_pos():
    return lax.axis_index("x"), lax.axis_index("y"), lax.axis_index("c")


def _flip(pos, k):
    x, y, c = pos
    return (1 - x if k & 4 else x, 1 - y if k & 2 else y, 1 - c if k & 1 else c)


def _index(pos):
    return 4 * pos[0] + 2 * pos[1] + pos[2]


HBM_SPEC = pl.BlockSpec(memory_space=pltpu.HBM)
SEM_SPEC = pl.BlockSpec(memory_space=pltpu.SEMAPHORE)
ANY_SPEC = pl.BlockSpec(memory_space=pl.ANY)
SPLIT_EFFECT = pltpu.SideEffectType.DATAFLOW_SIDE_EFFECTING
GATHER_FLIPS = (1, 4, 2, 6)
CHIP_FLIPS = (4, 2, 6)
TOKEN = jax.ShapeDtypeStruct((8, LANES), F32)


def _hbm(a):
    return pltpu.with_memory_space_constraint(a, pltpu.HBM)


def _hbm_like(a):
    return pltpu.HBM(a.shape, a.dtype)


def _block_rows(ref, pos, r):
    return ref.at[:, pl.ds(_index(pos) * r, r), :]


def _gather_start(shards, after, name):
    n = len(shards)
    lands = [lax.empty((s.shape[0], N_DEV * s.shape[1], s.shape[2]), s.dtype) for s in shards]

    def body(*refs):
        ins, bufs = refs[:n], refs[n:2 * n]
        send_sems, recv_sems = refs[2 * n + 1], refs[2 * n + 2]
        token = refs[4 * n + 3]
        me = _mesh_pos()
        for a in range(n):
            for j, k in enumerate(GATHER_FLIPS):
                pltpu.make_async_remote_copy(
                    src_ref=ins[a], dst_ref=_block_rows(bufs[a], me, shards[a].shape[1]),
                    send_sem=send_sems.at[4 * a + j], recv_sem=recv_sems.at[4 * a + j],
                    device_id=_flip(me, k), device_id_type=MESH_ID).start()
        token[...] = jnp.zeros_like(token)

    outs = pl.pallas_call(
        body,
        name=name,
        in_specs=[HBM_SPEC] * (2 * n) + [ANY_SPEC],
        out_specs=[SEM_SPEC, SEM_SPEC] + [HBM_SPEC] * (2 * n) + [pl.BlockSpec(memory_space=pltpu.VMEM)],
        out_shape=[pltpu.SemaphoreType.DMA((4 * n,)), pltpu.SemaphoreType.DMA((4 * n,))]
        + [_hbm_like(s) for s in shards] + [_hbm_like(b) for b in lands] + [TOKEN],
        input_output_aliases={i: 2 + i for i in range(2 * n)},
        compiler_params=pltpu.CompilerParams(has_side_effects=SPLIT_EFFECT),
    )(*[_hbm(s) for s in shards], *[_hbm(b) for b in lands], after)
    return outs[0], outs[1], outs[2:2 + n], outs[2 + n:2 + 2 * n], outs[2 + 2 * n]


def _gather_wait(started, after, name):
    send_sems, recv_sems, shards, lands, _ = started
    n = len(shards)

    def body(*refs):
        ins, bufs = refs[:n], refs[n:2 * n]
        send_sems, recv_sems = refs[2 * n], refs[2 * n + 1]
        me = _mesh_pos()
        for a in range(n):
            for j, k in enumerate(GATHER_FLIPS):
                cp = pltpu.make_async_remote_copy(
                    src_ref=ins[a], dst_ref=_block_rows(bufs[a], _flip(me, k), shards[a].shape[1]),
                    send_sem=send_sems.at[4 * a + j], recv_sem=recv_sems.at[4 * a + j],
                    device_id=_flip(me, k), device_id_type=MESH_ID)
                cp.wait_send()
                cp.wait_recv()

    outs = pl.pallas_call(
        body,
        name=name,
        in_specs=[HBM_SPEC] * (2 * n) + [SEM_SPEC, SEM_SPEC, ANY_SPEC],
        out_specs=[HBM_SPEC] * (2 * n),
        out_shape=[_hbm_like(s) for s in shards] + [_hbm_like(b) for b in lands],
        input_output_aliases={i: i for i in range(2 * n)},
        compiler_params=pltpu.CompilerParams(has_side_effects=SPLIT_EFFECT),
    )(*shards, *lands, send_sems, recv_sems, after)
    return outs[:n], outs[n:]


def _gather_pass(shards, lands, name):
    n = len(shards)

    def body(*refs):
        ins, bufs = refs[:n], refs[n:2 * n]
        token = refs[3 * n]
        send_sems, recv_sems, local_sems = refs[3 * n + 1:]
        me = _mesh_pos()
        sibling = _flip(me, 1)

        def copy(a, j, block):
            rows = _block_rows(bufs[a], block, shards[a].shape[1])
            return pltpu.make_async_remote_copy(
                src_ref=rows, dst_ref=rows, send_sem=send_sems.at[3 * a + j], recv_sem=recv_sems.at[3 * a + j],
                device_id=sibling, device_id_type=MESH_ID)

        mine = [pltpu.make_async_copy(ins[a], _block_rows(bufs[a], me, shards[a].shape[1]), local_sems.at[a])
                for a in range(n)]
        sends = [copy(a, j, _flip(me, k)) for a in range(n) for j, k in enumerate(CHIP_FLIPS)]
        for cp in mine + sends:
            cp.start()
        for a in range(n):
            for j, k in enumerate(CHIP_FLIPS):
                copy(a, j, _flip(sibling, k)).wait_recv()
        for cp in sends:
            cp.wait_send()
        for cp in mine:
            cp.wait()
        token[...] = jnp.zeros_like(token)

    outs = pl.pallas_call(
        body,
        name=name,
        in_specs=[pl.BlockSpec(memory_space=pltpu.VMEM)] * n + [ANY_SPEC] * n,
        out_specs=[ANY_SPEC] * n + [pl.BlockSpec(memory_space=pltpu.VMEM)],
        out_shape=[jax.ShapeDtypeStruct(b.shape, b.dtype) for b in lands] + [TOKEN],
        scratch_shapes=[pltpu.SemaphoreType.DMA((3 * n,)), pltpu.SemaphoreType.DMA((3 * n,)), pltpu.SemaphoreType.DMA((n,))],
        input_output_aliases={n + i: i for i in range(n)},
        compiler_params=pltpu.CompilerParams(has_side_effects=True, vmem_limit_bytes=VMEM_LIMIT_BYTES),
    )(*shards, *lands)
    return outs[:n], outs[n]


def _place_own(grads, lands, layer, name):
    n = len(grads)
    blocks = [(g.shape[0], g.shape[1] // N_DEV, g.shape[2]) for g in grads]

    def body(*refs):
        ins, bufs = refs[:n], refs[n:2 * n]
        stage, in_sems, out_sems = refs[3 * n:4 * n], refs[4 * n], refs[4 * n + 1]
        me = _mesh_pos()
        loads = [pltpu.make_async_copy(_block_rows(ins[a], me, blocks[a][1]), stage[a], in_sems.at[a]) for a in range(n)]
        stores = [pltpu.make_async_copy(stage[a], bufs[a].at[_index(me), layer], out_sems.at[a]) for a in range(n)]
        for cp in loads:
            cp.start()
        for a in range(n):
            loads[a].wait()
            stores[a].start()
        for cp in stores:
            cp.wait()

    return pl.pallas_call(
        body,
        name=name,
        in_specs=[ANY_SPEC] * (2 * n),
        out_specs=[ANY_SPEC] * n,
        out_shape=[jax.ShapeDtypeStruct(b.shape, b.dtype) for b in lands],
        scratch_shapes=[pltpu.VMEM(blk, g.dtype) for blk, g in zip(blocks, grads)]
        + [pltpu.SemaphoreType.DMA((n,)), pltpu.SemaphoreType.DMA((n,))],
        input_output_aliases={n + i: i for i in range(n)},
        compiler_params=pltpu.CompilerParams(has_side_effects=True, vmem_limit_bytes=VMEM_LIMIT_BYTES),
    )(*grads, *lands)


def _scatter_copy(ins, bufs, send_sems, recv_sems, a, k, r, layer, me, slab):
    peer = _flip(me, k)
    return pltpu.make_async_remote_copy(
        src_ref=_block_rows(ins[a], peer, r), dst_ref=bufs[a].at[_index(slab), layer],
        send_sem=send_sems.at[7 * a + k - 1], recv_sem=recv_sems.at[7 * a + k - 1],
        device_id=peer, device_id_type=MESH_ID)


def _scatter_start(grads, lands, layer, name):
    n = len(grads)

    def body(*refs):
        ins, bufs = refs[:n], refs[n:2 * n]
        send_sems, recv_sems = refs[2 * n], refs[2 * n + 1]
        token = refs[4 * n + 2]
        me = _mesh_pos()
        for a in range(n):
            for k in range(1, N_DEV):
                _scatter_copy(ins, bufs, send_sems, recv_sems, a, k, grads[a].shape[1] // N_DEV, layer, me, me).start()
        token[...] = jnp.zeros_like(token)

    outs = pl.pallas_call(
        body,
        name=name,
        in_specs=[HBM_SPEC] * (2 * n),
        out_specs=[SEM_SPEC, SEM_SPEC] + [HBM_SPEC] * (2 * n) + [pl.BlockSpec(memory_space=pltpu.VMEM)],
        out_shape=[pltpu.SemaphoreType.DMA((7 * n,)), pltpu.SemaphoreType.DMA((7 * n,))]
        + [_hbm_like(g) for g in grads] + [_hbm_like(b) for b in lands] + [TOKEN],
        input_output_aliases={i: 2 + i for i in range(2 * n)},
        compiler_params=pltpu.CompilerParams(has_side_effects=SPLIT_EFFECT),
    )(*[_hbm(g) for g in grads], *[_hbm(b) for b in lands])
    return outs[0], outs[1], outs[2:2 + n], outs[2 + n:2 + 2 * n], outs[2 + 2 * n]


def _scatter_wait(started, layer, after, name):
    send_sems, recv_sems, grads, lands, _ = started
    n = len(grads)

    def body(*refs):
        ins, bufs = refs[:n], refs[n:2 * n]
        send_sems, recv_sems = refs[2 * n], refs[2 * n + 1]
        me = _mesh_pos()
        for a in range(n):
            for k in range(1, N_DEV):
                cp = _scatter_copy(ins, bufs, send_sems, recv_sems, a, k, grads[a].shape[1] // N_DEV, layer, me, _flip(me, k))
                cp.wait_send()
                cp.wait_recv()

    outs = pl.pallas_call(
        body,
        name=name,
        in_specs=[HBM_SPEC] * (2 * n) + [SEM_SPEC, SEM_SPEC, ANY_SPEC],
        out_specs=[HBM_SPEC] * (2 * n),
        out_shape=[_hbm_like(g) for g in grads] + [_hbm_like(b) for b in lands],
        input_output_aliases={i: i for i in range(2 * n)},
        compiler_params=pltpu.CompilerParams(has_side_effects=SPLIT_EFFECT),
    )(*grads, *lands, send_sems, recv_sems, after)
    return outs[n:]


def _allreduce_small(vec):
    R, C = vec.shape

    def body(v_ref, o_ref, buf, send_sems, recv_sems):
        me = _mesh_pos()
        buf[_index(me)] = v_ref[...]
        sends = []
        for k in range(1, N_DEV):
            sends.append(pltpu.make_async_remote_copy(
                src_ref=buf.at[_index(me)], dst_ref=buf.at[_index(me)],
                send_sem=send_sems.at[k - 1], recv_sem=recv_sems.at[k - 1],
                device_id=_flip(me, k), device_id_type=MESH_ID))
        for cp in sends:
            cp.start()
        for cp in sends:
            cp.wait_recv()
        for cp in sends:
            cp.wait_send()
        acc = buf[0]
        for s in range(1, N_DEV):
            acc = acc + buf[s]
        o_ref[...] = acc

    return pl.pallas_call(
        body,
        name="allreduce_small",
        in_specs=[pl.BlockSpec(memory_space=pltpu.VMEM)],
        out_specs=pl.BlockSpec(memory_space=pltpu.VMEM),
        out_shape=jax.ShapeDtypeStruct((R, C), F32),
        scratch_shapes=[pltpu.VMEM((N_DEV, R, C), F32), pltpu.SemaphoreType.DMA((7,)), pltpu.SemaphoreType.DMA((7,))],
        compiler_params=pltpu.CompilerParams(has_side_effects=True, vmem_limit_bytes=VMEM_LIMIT_BYTES),
    )(vec)


def _pack_small(parts):
    flat = jnp.concatenate([p.reshape(-1) for p in parts])
    n = flat.shape[0]
    rows = -(-n // SMALL_COLS)
    rows = -(-rows // 8) * 8
    return jnp.pad(flat, (0, rows * SMALL_COLS - n)).reshape(rows, SMALL_COLS)


def _unpack_small(packed, like):
    flat = packed.reshape(-1)
    out, pos = [], 0
    for p in like:
        out.append(flat[pos:pos + p.size].reshape(p.shape))
        pos += p.size
    return out


def kernel(x, p, w_in, b_in, w_out, attn_sinks, rel_bias, w_pool, pool_scale, w_ple, w_gate_ple, ln_gain, ln_bias, loss_target, m_w_in, m_b_in, m_w_out, m_attn_sinks, m_rel_bias, m_w_pool, m_pool_scale, m_w_ple, m_w_gate_ple, m_ln_gain, m_ln_bias, v_w_in, v_b_in, v_w_out, v_attn_sinks, v_rel_bias, v_w_pool, v_pool_scale, v_w_ple, v_w_gate_ple, v_ln_gain, v_ln_bias):
    L = w_in.shape[0]
    S = x.shape[1]
    alpha = (2.0 * L) ** 0.25
    bucket_np, masks_np, window_np = _band_constants()
    bucket, masks, window = jnp.asarray(bucket_np), jnp.asarray(masks_np), jnp.asarray(window_np)

    @functools.lru_cache(maxsize=None)
    def shards_of(l):
        return (jnp.swapaxes(w_in[l], 0, 1).astype(BF16)[None], w_out[l].astype(BF16)[None],
                w_gate_ple[l].astype(BF16)[None], jnp.swapaxes(w_ple[l], 0, 1).astype(BF16)[None], w_pool[l].astype(BF16))

    def gathered(started, after, tag):
        shards, lands = _gather_wait(started, after, name=f"gather_wait_{tag}")
        return _gather_pass(shards, lands, name=f"gather_pass_{tag}")

    bias = _bias_build(rel_bias, bucket).reshape(N_KV_HEADS, BIAS_ROWS, BLOCK)

    xs = x[0]
    xb = xs.astype(BF16)
    first_groups = ((0, 4), (1, 2, 3))
    token, first_started = rel_bias, []
    for tag, idxs in zip("ab", first_groups):
        first_started.append(_gather_start([shards_of(0)[i] for i in idxs], token, name=f"gather_start_0{tag}"))
        token = first_started[-1][4]
    started = _gather_start(shards_of(1), token, name="gather_start_1") if L > 1 else None
    saved = []
    for l in range(L):
        pb = p[l, 0].astype(BF16)
        sinks_l = attn_sinks[l]
        scale_l = pool_scale[l].reshape(1, POOL_WIDTH)
        bias_l = b_in[l].reshape(1, IN_COLS)
        if l == 0:
            (w_in_f, w_pool_g), _ = gathered(first_started[0], started[4] if L > 1 else xb, "0a")
            w_in_t = w_in_f[0]
            h = _matmul(xb, w_in_t, tb=True, tm=512, tn=2176, tk=2048, out_dtype=F32, bias=bias_l, name=f"in_proj_{l}")
            (w_out_f, w_gate_f, w_ple_f), pass_token = gathered(first_started[1], h, "0b")
            w_out_g, w_gate_g, w_ple_t = w_out_f[0], w_gate_f[0], w_ple_f[0]
        else:
            w_in_t, w_out_g, w_gate_g, w_ple_t, w_pool_g = weights
            if l + 1 < L:
                started = _gather_start(shards_of(l + 1), pass_token, name=f"gather_start_{l + 1}")
            h = _matmul(xb, w_in_t, tb=True, tm=512, tn=2176, tk=2048, out_dtype=F32, bias=bias_l,
                        after=started[4] if l + 1 < L else None, name=f"in_proj_{l}")
        weights = (w_in_t, w_out_g, w_gate_g, w_ple_t, w_pool_g)
        ab = _attn_fwd(h, bias, masks, sinks_l, name=f"attn_fwd_{l}")
        ab = _pool_fwd(h, ab, w_pool_g, scale_l, name=f"pool_fwd_{l}")
        y, yb, xhat, rstd, gp, pe = _mix_ln_fwd(ab, xs, pb, w_out_g, w_gate_g, w_ple_t, ln_gain[l].reshape(1, D_MODEL),
                                                ln_bias[l].reshape(1, D_MODEL), alpha, name=f"mix_ln_fwd_{l}")
        saved.append((xb, pb, h, gp, pe, ab, xhat, rstd, sinks_l, scale_l, weights))
        xs, xb = y, yb
        if l + 1 < L:
            full, pass_token = gathered(started, yb, l + 1)
            weights = (full[0][0], full[1][0], full[2][0], full[3][0], full[4])

    dy, loss_tile = _loss_head(xs, loss_target[0])

    dbias = jnp.zeros((N_KV_HEADS, BIAS_ROWS, BLOCK), F32)
    sh0 = shards_of(0)
    lands_a = [lax.empty((N_DEV, L) + sh0[i].shape, BF16) for i in (1, 2, 3)]
    lands_b = [lax.empty((N_DEV, L) + sh0[i].shape, BF16) for i in (0, 4)]
    g_b_in, g_sinks, g_scale, g_gain, g_beta = [], [], [], [], []
    pend_a = pend_b = None

    def scatter(grads, lands, pending, l, tag):
        if pending:
            lands = _scatter_wait(pending[0], pending[1], grads[0], name=f"scatter_wait_{pending[1]}{tag}")
        lands = _place_own(grads, lands, l, name=f"place_own_{l}{tag}")
        return _scatter_start(grads, lands, l, name=f"scatter_start_{l}{tag}"), l

    for l in reversed(range(L)):
        xb, pb, h, gp, pe, ab, xhat, rstd, sinks_l, scale_l, weights = saved[l]
        w_in_t, w_out_g, w_gate_g, w_ple_t, w_pool_g = weights
        dz, dzb, dpe, dgp, dab, dgain, dbeta = _ln_dmix_bwd(
            dy, xhat, rstd, ln_gain[l].reshape(1, D_MODEL), gp, pe, w_out_g, pend_b[0][4] if pend_b else rel_bias,
            name=f"ln_dmix_bwd_{l}")
        g_w_out = _matmul(ab, dzb, ta=True, tm=512, tn=1024, tk=4096, out_dtype=BF16, name=f"dw_out_{l}")
        g_w_gate = _matmul(xb, dgp, ta=True, tm=512, tn=1024, tk=4096, out_dtype=BF16, name=f"dw_gate_{l}")
        g_w_ple_t = _matmul(dpe, pb, ta=True, tm=1024, tn=256, tk=1024, out_dtype=BF16, name=f"dw_ple_{l}")
        pend_a = scatter([g_w_out[None], g_w_gate[None], g_w_ple_t[None]], lands_a, pend_a, l, "a")
        dh, dwp, dsc, db_pool = _pool_bwd(h, dab, w_pool_g, scale_l, pend_a[0][4], name=f"pool_bwd_{l}")
        dh, dbias, dsink, db_attn = _attn_bwd(h, dab, dh, bias, masks, sinks_l, dbias, name=f"attn_bwd_{l}")
        g_w_in_t = _matmul(dh, xb, ta=True, tm=256, tn=1024, tk=4096, out_dtype=BF16, name=f"dw_in_{l}")
        pend_b = scatter([g_w_in_t[None], dwp.astype(BF16)], lands_b, pend_b, l, "b")
        g_b_in.append(jnp.concatenate([db_attn[0], db_pool[0, U_OFF - DH_POOL_COLS:]]))
        dx = _matmul(dgp, w_gate_g, tb=True, tm=512, tn=2048, tk=2048, out_dtype=F32, add=dz, add_scale=alpha,
                     after=pend_b[0][4], name=f"dx_gate_{l}")
        dy = _matmul(dh, w_in_t, tm=512, tn=1024, tk=4352, out_dtype=F32, add=dx, name=f"dx_in_{l}")
        g_sinks.append(dsink[:, :, 0].reshape(N_HEADS))
        g_scale.append(dsc.reshape(POOL_WIDTH))
        g_gain.append(dgain.reshape(D_MODEL))
        g_beta.append(dbeta.reshape(D_MODEL))
    grad_x = dy[None]
    for lst in (g_b_in, g_sinks, g_scale, g_gain, g_beta):
        lst.reverse()
    g_rel = _bias_bwd(dbias.reshape((N_HEADS,) + BAND), bucket, window)[:, :N_HEADS]

    def big(w, g, m, v, name):
        shape = w.shape
        two_d = (shape[0] * shape[1], shape[2]) if len(shape) == 3 else (shape[0] * shape[1] * shape[2], shape[3])
        d, nm, nv = _adamw(w.reshape(two_d), g.reshape(two_d), m.reshape(two_d), v.reshape(two_d), name=name)
        return d.reshape(shape), nm.reshape(shape), nv.reshape(shape)

    r_out, r_gate, r_ple = _scatter_wait(pend_a[0], pend_a[1], dy, name="scatter_wait_0a")
    small_like = [b_in, attn_sinks, rel_bias, pool_scale, ln_gain, ln_bias]
    small_g = _allreduce_small(_pack_small([
        jnp.stack(g_b_in).reshape(L, IN_COLS), jnp.stack(g_sinks), g_rel, jnp.stack(g_scale), jnp.stack(g_gain),
        jnp.stack(g_beta), loss_tile[0, :1]]))
    grad_w_out = _sum_slabs(r_out.reshape(N_DEV, L * 256, D_MODEL), name="sum_w_out").reshape(L, 256, D_MODEL)
    grad_w_gate = _sum_slabs(r_gate.reshape(N_DEV, L * 256, D_MODEL), name="sum_w_gate").reshape(L, 256, D_MODEL)
    gt_ple = _sum_slabs(r_ple.reshape(N_DEV, L * 256, PLE_DIM), name="sum_w_ple")
    grad_w_ple = jnp.swapaxes(gt_ple.reshape(L, 256, PLE_DIM), 1, 2)
    upd_out = big(w_out, grad_w_out, m_w_out, v_w_out, "adamw_w_out")
    upd_ple = big(w_ple, grad_w_ple, m_w_ple, v_w_ple, "adamw_w_ple")
    upd_gate = big(w_gate_ple, grad_w_gate, m_w_gate_ple, v_w_gate_ple, "adamw_w_gate")

    r_in, r_pool = _scatter_wait(pend_b[0], pend_b[1], upd_gate[0], name="scatter_wait_0b")
    gt_in = _sum_slabs(r_in.reshape(N_DEV, L * 544, D_MODEL), name="sum_w_in")
    grad_w_in = jnp.swapaxes(gt_in.reshape(L, 544, D_MODEL), 1, 2)
    grad_w_pool = _sum_slabs(r_pool.reshape(N_DEV, L * 4 * 32, 256), name="sum_w_pool").reshape(L, 4, 32, 256)
    upd_in = big(w_in, grad_w_in, m_w_in, v_w_in, "adamw_w_in")
    upd_pool = big(w_pool, grad_w_pool, m_w_pool, v_w_pool, "adamw_w_pool")

    zero1 = jnp.zeros((1,), F32)
    sw = _pack_small(small_like + [zero1])
    sm = _pack_small([m_b_in, m_attn_sinks, m_rel_bias, m_pool_scale, m_ln_gain, m_ln_bias, zero1])
    sv = _pack_small([v_b_in, v_attn_sinks, v_rel_bias, v_pool_scale, v_ln_gain, v_ln_bias, zero1])
    sd, snm, snv = _adamw(sw, small_g, sm, sv, name="adamw_small")
    like = small_like + [zero1]
    sg_parts = _unpack_small(small_g, like)
    sd_parts, snm_parts, snv_parts = _unpack_small(sd, like), _unpack_small(snm, like), _unpack_small(snv, like)
    loss = sg_parts[6][0]

    def assemble(big_parts, small_parts):
        w_in_, w_out_, w_pool_, w_ple_, w_gate_ = big_parts
        b_in_, sinks_, rel_, scale_, gain_, beta_ = small_parts[:6]
        return [w_in_, b_in_, w_out_, sinks_, rel_, w_pool_, scale_, w_ple_, w_gate_, gain_, beta_]

    grads = assemble([grad_w_in, grad_w_out, grad_w_pool, grad_w_ple, grad_w_gate], sg_parts)
    ups = [upd_in, upd_out, upd_pool, upd_ple, upd_gate]
    deltas = assemble([u[0] for u in ups], sd_parts)
    new_m = assemble([u[1] for u in ups], snm_parts)
    new_v = assemble([u[2] for u in ups], snv_parts)
    return (loss, grad_x, *grads, *deltas, *new_m, *new_v)
```

```python
import functools
import math

import numpy as np
import jax
import jax.numpy as jnp
from jax import lax
from jax.experimental import pallas as pl
from jax.experimental.pallas import tpu as pltpu

F32 = jnp.float32
BF16 = jnp.bfloat16

D_MODEL = 2048
PLE_DIM = 256
ATTN_WIDTH = 1024
POOL_WIDTH = 1024
HEAD_DIM = 64
N_HEADS = 16
N_KV_HEADS = 2
KV_GROUP = 8
WINDOW = 128
BLOCK = 128
POOL_WINDOWS = (2, 4, 8, 16)
POOL_GROUP_DIM = 256
POOL_HALO = 16
REL_BUCKETS = 32
REL_MAX_DIST = 128
LN_EPS = 1e-5
KV_COLS = N_KV_HEADS * HEAD_DIM
IN_COLS = 4352
Q_OFF, KV_OFF, GA_OFF, U_OFF, GB_OFF = 0, 1024, 1280, 2304, 3328
ATTN_SCALE = 1.0 / math.sqrt(HEAD_DIM)
NEG_BIG = -1e30
LANES = 128

ADAM_LR = 0.001
ADAM_B1 = 0.9
ADAM_B2 = 0.999
ADAM_EPS = 1e-08
ADAM_WD = 0.01
ADAM_STEP = 10

N_DEV = 8
MESH_ID = pl.DeviceIdType.MESH
VMEM_LIMIT_BYTES = 52 * 1024 * 1024
SMALL_COLS = 1024


def _params(sem=None):
    return pltpu.CompilerParams(dimension_semantics=sem, vmem_limit_bytes=VMEM_LIMIT_BYTES)


def _sigmoid(x):
    return 1.0 / (1.0 + jnp.exp(-x))


def _tile(n, pref, unit=16):
    if n <= pref:
        return n
    t = pref - pref % unit
    while n % t:
        t -= unit
    assert t > 0, (n, pref)
    return t


def _matmul(a, b, *, name, ta=False, tb=False, tm, tn, tk, out_dtype, bias=None, add=None, add_scale=1.0, after=None):
    M, K = (a.shape[1], a.shape[0]) if ta else a.shape
    N = b.shape[0] if tb else b.shape[1]
    assert (b.shape[1] if tb else b.shape[0]) == K
    tm, tn, tk = _tile(M, tm), _tile(N, tn), _tile(K, tk)
    nm, nn, nk = M // tm, N // tn, K // tk
    a_spec = pl.BlockSpec((tk, tm), lambda j, i, k: (k, i)) if ta else pl.BlockSpec((tm, tk), lambda j, i, k: (i, k))
    b_spec = pl.BlockSpec((tn, tk), lambda j, i, k: (j, k)) if tb else pl.BlockSpec((tk, tn), lambda j, i, k: (k, j))
    dims = (((0 if ta else 1,), (1 if tb else 0,)), ((), ()))
    operands, in_specs = [a, b], [a_spec, b_spec]
    if bias is not None:
        operands.append(bias)
        in_specs.append(pl.BlockSpec((1, tn), lambda j, i, k: (0, j)))
    if add is not None:
        operands.append(add)
        in_specs.append(pl.BlockSpec((tm, tn), lambda j, i, k: (i, j)))
    if after is not None:
        operands.append(after)
        in_specs.append(pl.BlockSpec(memory_space=pl.ANY))

    def body(*refs):
        a_ref, b_ref = refs[0], refs[1]
        pos = 2
        bias_ref = add_ref = None
        if bias is not None:
            bias_ref = refs[pos]
            pos += 1
        if add is not None:
            add_ref = refs[pos]
            pos += 1
        if after is not None:
            pos += 1
        o_ref = refs[pos]
        part = lax.dot_general(a_ref[...].astype(BF16), b_ref[...].astype(BF16), dims, preferred_element_type=F32)

        def finish(acc):
            if bias_ref is not None:
                acc = acc + bias_ref[...]
            if add_ref is not None:
                acc = acc + add_scale * add_ref[...].astype(F32)
            o_ref[...] = acc.astype(out_dtype)

        if nk == 1:
            finish(part)
        else:
            acc_ref = refs[pos + 1]
            k = pl.program_id(2)

            @pl.when(k == 0)
            def _():
                acc_ref[...] = part

            @pl.when(k > 0)
            def _():
                acc_ref[...] += part

            @pl.when(k == nk - 1)
            def _():
                finish(acc_ref[...])

    return pl.pallas_call(
        body,
        name=name,
        grid=(nn, nm, nk),
        in_specs=in_specs,
        out_specs=pl.BlockSpec((tm, tn), lambda j, i, k: (i, j)),
        out_shape=jax.ShapeDtypeStruct((M, N), out_dtype),
        scratch_shapes=[pltpu.VMEM((tm, tn), F32)] if nk > 1 else [],
        compiler_params=_params(("parallel", "parallel", "arbitrary")),
    )(*operands)


BAND = (2 * BLOCK, BLOCK)
BIAS_ROWS = KV_GROUP * 2 * BLOCK


def _band_constants():
    qq = np.arange(BLOCK)[None, :]
    kk = np.arange(2 * BLOCK)[:, None]
    dist = qq + BLOCK - kk
    in_window = (dist >= 0) & (dist < WINDOW)
    max_exact = REL_BUCKETS // 2
    d = np.maximum(dist, 0)
    d_f = np.maximum(d, 1).astype(np.float32)
    large = max_exact + (
        np.log(d_f / np.float32(max_exact)) / np.float32(math.log(REL_MAX_DIST / max_exact)) * np.float32(REL_BUCKETS - max_exact)
    ).astype(np.int32)
    large = np.minimum(large, REL_BUCKETS - 1)
    bucket = np.where(d < max_exact, d, large).astype(np.int32)
    bucket = np.where(in_window, bucket, 0).astype(np.int32)
    first = in_window & (kk >= BLOCK)
    masks = np.stack([first, in_window]).astype(np.float32)
    return bucket, masks, in_window.astype(np.float32)


def _bias_build(rel_bias, bucket):
    def body(rb_ref, bkt_ref, o_ref):
        h = pl.program_id(0)
        bkt = bkt_ref[...]

        def step(b, acc):
            return jnp.where(bkt == b, rb_ref[b, h], acc)

        o_ref[0] = lax.fori_loop(0, REL_BUCKETS, step, jnp.zeros(BAND, F32))

    return pl.pallas_call(
        body,
        name="bias_build",
        grid=(N_HEADS,),
        in_specs=[pl.BlockSpec(memory_space=pltpu.SMEM), pl.BlockSpec(BAND, lambda h: (0, 0))],
        out_specs=pl.BlockSpec((1,) + BAND, lambda h: (h, 0, 0)),
        out_shape=jax.ShapeDtypeStruct((N_HEADS,) + BAND, F32),
        compiler_params=_params(("arbitrary",)),
    )(rel_bias, bucket)


def _bias_bwd(dbias, bucket, window):
    def body(db_ref, bkt_ref, win_ref, o_ref):
        h = pl.program_id(0)

        @pl.when(h == 0)
        def _():
            o_ref[...] = jnp.zeros_like(o_ref)

        bkt = bkt_ref[...]
        x = jnp.where(win_ref[...] > 0.5, db_ref[0], 0.0)
        row = lax.broadcasted_iota(jnp.int32, (REL_BUCKETS, LANES), 0)
        col = lax.broadcasted_iota(jnp.int32, (REL_BUCKETS, LANES), 1)

        def step(b, acc):
            s = jnp.sum(jnp.where(bkt == b, x, 0.0), axis=0, keepdims=True)
            return acc + jnp.where(row == b, s, 0.0)

        per_lane = lax.fori_loop(0, REL_BUCKETS, step, jnp.zeros((REL_BUCKETS, LANES), F32))
        o_ref[...] += jnp.where(col == h, jnp.sum(per_lane, axis=1, keepdims=True), 0.0)

    return pl.pallas_call(
        body,
        name="bias_bwd",
        grid=(N_HEADS,),
        in_specs=[
            pl.BlockSpec((1,) + BAND, lambda h: (h, 0, 0)),
            pl.BlockSpec(BAND, lambda h: (0, 0)),
            pl.BlockSpec(BAND, lambda h: (0, 0)),
        ],
        out_specs=pl.BlockSpec((REL_BUCKETS, LANES), lambda h: (0, 0)),
        out_shape=jax.ShapeDtypeStruct((REL_BUCKETS, LANES), F32),
        compiler_params=_params(("arbitrary",)),
    )(dbias, bucket, window)


def _lane_lo(shape):
    return lax.broadcasted_iota(jnp.int32, shape, 1) < HEAD_DIM


def _row_lo(shape):
    return lax.broadcasted_iota(jnp.int32, shape, 0) < HEAD_DIM


def _dup_heads(x):
    r = pltpu.roll(x, HEAD_DIM, axis=1)
    lo = _lane_lo(x.shape)
    return jnp.where(lo, x, r), jnp.where(lo, r, x)


def _kv_operands(kvp_ref, kvc_ref):
    kvp, kvc = kvp_ref[...], kvc_ref[...]
    k2 = jnp.concatenate([kvp[:, :KV_COLS], kvc[:, :KV_COLS]], axis=0)
    v2 = jnp.concatenate([kvp[:, KV_COLS:], kvc[:, KV_COLS:]], axis=0)
    return _dup_heads(k2), _dup_heads(v2)


def _head_probs(k_r, qs_t, bias, mask, sink):
    s = jnp.dot(k_r, qs_t, preferred_element_type=F32) * ATTN_SCALE + bias
    s = jnp.where(mask, s, NEG_BIG)
    m = jnp.maximum(jnp.max(s, axis=0, keepdims=True), sink)
    e = jnp.exp(s - m)
    e_sink = jnp.exp(sink - m)
    inv = 1.0 / (jnp.sum(e, axis=0, keepdims=True) + e_sink)
    return e * inv, e_sink * inv


def _gate_cols(ga_refs, pair):
    off = LANES * (pair % 2)
    return ga_refs[pair // 2][:, off:off + LANES]


def _attn_specs(order):
    return [
        pl.BlockSpec((BLOCK, ATTN_WIDTH), lambda t: (order(t), Q_OFF // ATTN_WIDTH)),
        pl.BlockSpec((BLOCK, 2 * KV_COLS), lambda t: (order(t), KV_OFF // (2 * KV_COLS))),
        pl.BlockSpec((BLOCK, 2 * KV_COLS), lambda t: (jnp.maximum(order(t) - 1, 0), KV_OFF // (2 * KV_COLS))),
    ] + [
        pl.BlockSpec((BLOCK, 256), functools.partial(lambda t, c: (order(t), GA_OFF // 256 + c), c=c)) for c in range(4)
    ] + [
        pl.BlockSpec((N_KV_HEADS, BIAS_ROWS, BLOCK), lambda t: (0, 0, 0)),
        pl.BlockSpec((None,) + BAND, lambda t: (jnp.minimum(order(t), 1), 0, 0)),
        pl.BlockSpec(memory_space=pltpu.SMEM),
    ]


def _attn_fwd(h, bias, masks, sinks, name):
    S = h.shape[0]
    nb = S // BLOCK

    def body(q_ref, kvc_ref, kvp_ref, ga0, ga1, ga2, ga3, bias_ref, mask_ref, sink_ref, o_ref):
        kd, vd = _kv_operands(kvp_ref, kvc_ref)
        mask = mask_ref[...] > 0.5
        lo = _row_lo((LANES, BLOCK))
        for g in range(N_KV_HEADS):
            k_r = kd[g].astype(BF16)
            v_t = vd[g].T.astype(BF16)
            for pr in range(KV_GROUP // 2):
                pair = (KV_GROUP // 2) * g + pr
                qp_t = q_ref[:, LANES * pair:LANES * (pair + 1)].T
                outs = []
                for hh in range(2):
                    j = 2 * pr + hh
                    qs_t = jnp.where(lo if hh == 0 else ~lo, qp_t, 0.0).astype(BF16)
                    p, _ = _head_probs(k_r, qs_t, bias_ref[g, 2 * BLOCK * j:2 * BLOCK * (j + 1), :], mask,
                                       sink_ref[KV_GROUP * g + j])
                    outs.append(jnp.dot(v_t, p.astype(BF16), preferred_element_type=F32))
                ga = _gate_cols((ga0, ga1, ga2, ga3), pair)
                o_ref[:, LANES * pair:LANES * (pair + 1)] = (
                    jnp.where(lo, outs[0], outs[1]).T * (ga * _sigmoid(ga))).astype(BF16)

    return pl.pallas_call(
        body,
        name=name,
        grid=(nb,),
        in_specs=_attn_specs(lambda t: t),
        out_specs=pl.BlockSpec((BLOCK, ATTN_WIDTH), lambda t: (t, 0)),
        out_shape=jax.ShapeDtypeStruct((S, ATTN_WIDTH + POOL_WIDTH), BF16),
        compiler_params=_params(("arbitrary",)),
    )(h, h, h, h, h, h, h, bias, masks, sinks)


DH_ATTN_COLS = U_OFF


def _attn_bwd(h, dab, dh, bias, masks, sinks, dbias_in, name):
    S = h.shape[0]
    nb = S // BLOCK

    def order(t):
        return nb - 1 - t

    def body(q_ref, kvc_ref, kvp_ref, ga0, ga1, ga2, ga3, bias_ref, mask_ref, sink_ref, da_ref, dbin_ref, dh_in_ref,
             dh_ref, dbias_ref, dsink_ref, db_ref, carry_scr):
        del dh_in_ref
        t = pl.program_id(0)

        @pl.when(t == 0)
        def _():
            dbias_ref[...] = dbin_ref[...]
            dsink_ref[...] = jnp.zeros_like(dsink_ref)
            db_ref[...] = jnp.zeros_like(db_ref)
            carry_scr[...] = jnp.zeros_like(carry_scr)

        kd, vd = _kv_operands(kvp_ref, kvc_ref)
        mask = mask_ref[...] > 0.5
        lo = _lane_lo((BLOCK, LANES))
        lo_t = _row_lo((LANES, BLOCK))
        dk_tot, dv_tot = [], []
        for g in range(N_KV_HEADS):
            k_t, k_r = kd[g].T.astype(BF16), kd[g].astype(BF16)
            v_t, v_r = vd[g].T.astype(BF16), vd[g].astype(BF16)
            dk = jnp.zeros((2 * BLOCK, LANES), F32)
            dv = jnp.zeros((2 * BLOCK, LANES), F32)
            for pr in range(KV_GROUP // 2):
                pair = (KV_GROUP // 2) * g + pr
                cols = slice(LANES * pair, LANES * (pair + 1))
                qp = q_ref[:, cols]
                qp_t = qp.T
                ga = _gate_cols((ga0, ga1, ga2, ga3), pair)
                sg = _sigmoid(ga)
                da = da_ref[:, cols]
                do_p = da * (ga * sg)
                do_t = do_p.T
                outs, dqs = [], []
                for hh in range(2):
                    j = 2 * pr + hh
                    rows = slice(2 * BLOCK * j, 2 * BLOCK * (j + 1))
                    half, half_t = (lo, lo_t) if hh == 0 else (~lo, ~lo_t)
                    qs_t = jnp.where(half_t, qp_t, 0.0).astype(BF16)
                    p, p_sink = _head_probs(k_r, qs_t, bias_ref[g, rows, :], mask, sink_ref[KV_GROUP * g + j])
                    pb = p.astype(BF16)
                    outs.append(jnp.dot(v_t, pb, preferred_element_type=F32))
                    dos_t = jnp.where(half_t, do_t, 0.0).astype(BF16)
                    dp = jnp.dot(v_r, dos_t, preferred_element_type=F32)
                    dsum = jnp.sum(p * dp, axis=0, keepdims=True)
                    ds = p * (dp - dsum)
                    dbias_ref[g, rows, :] += ds
                    tot = jnp.sum(-(p_sink * dsum), axis=1, keepdims=True)
                    dsink_ref[g, j:j + 1, :] += jnp.broadcast_to(tot, (1, LANES))
                    dsb = ds.astype(BF16)
                    dqs.append(jnp.dot(k_t, dsb, preferred_element_type=F32))
                    dk = dk + jnp.dot(dsb, jnp.where(half, qp, 0.0).astype(BF16), preferred_element_type=F32)
                    dv = dv + jnp.dot(pb, jnp.where(half, do_p, 0.0).astype(BF16), preferred_element_type=F32)
                attn = jnp.where(lo_t, outs[0], outs[1]).T
                dq = jnp.where(lo_t, dqs[0], dqs[1]).T * ATTN_SCALE
                dga = da * attn * (sg * (1.0 + ga * (1.0 - sg)))
                ga_cols = slice(GA_OFF + LANES * pair, GA_OFF + LANES * (pair + 1))
                dh_ref[:, cols] = dq.astype(BF16)
                dh_ref[:, ga_cols] = dga.astype(BF16)
                db_ref[:, cols] += jnp.sum(dq, axis=0, keepdims=True)
                db_ref[:, ga_cols] += jnp.sum(dga, axis=0, keepdims=True)
            dk = dk * ATTN_SCALE
            dk_tot.append(dk + pltpu.roll(dk, HEAD_DIM, axis=1))
            dv_tot.append(dv + pltpu.roll(dv, HEAD_DIM, axis=1))
        lo2 = _lane_lo((2 * BLOCK, LANES))
        dkv = jnp.concatenate([jnp.where(lo2, dk_tot[0], dk_tot[1]), jnp.where(lo2, dv_tot[0], dv_tot[1])], axis=1)
        dkv_done = dkv[BLOCK:, :] + carry_scr[...]
        dh_ref[:, KV_OFF:KV_OFF + 2 * KV_COLS] = dkv_done.astype(BF16)
        db_ref[:, KV_OFF:KV_OFF + 2 * KV_COLS] += jnp.sum(dkv_done, axis=0, keepdims=True)
        carry_scr[...] = dkv[:BLOCK, :]

    n_in = 12
    return pl.pallas_call(
        body,
        name=name,
        grid=(nb,),
        in_specs=_attn_specs(order) + [
            pl.BlockSpec((BLOCK, ATTN_WIDTH), lambda t: (order(t), 0)),
            pl.BlockSpec((N_KV_HEADS, BIAS_ROWS, BLOCK), lambda t: (0, 0, 0)),
            pl.BlockSpec(memory_space=pl.ANY),
        ],
        out_specs=[
            pl.BlockSpec((BLOCK, DH_ATTN_COLS), lambda t: (order(t), 0)),
            pl.BlockSpec((N_KV_HEADS, BIAS_ROWS, BLOCK), lambda t: (0, 0, 0)),
            pl.BlockSpec((N_KV_HEADS, KV_GROUP, LANES), lambda t: (0, 0, 0)),
            pl.BlockSpec((1, DH_ATTN_COLS), lambda t: (0, 0)),
        ],
        out_shape=[
            jax.ShapeDtypeStruct((S, IN_COLS), BF16),
            jax.ShapeDtypeStruct((N_KV_HEADS, BIAS_ROWS, BLOCK), F32),
            jax.ShapeDtypeStruct((N_KV_HEADS, KV_GROUP, LANES), F32),
            jax.ShapeDtypeStruct((1, DH_ATTN_COLS), F32),
        ],
        scratch_shapes=[pltpu.VMEM((BLOCK, 2 * KV_COLS), F32)],
        input_output_aliases={n_in: 0},
        compiler_params=_params(("arbitrary",)),
    )(h, h, h, h, h, h, h, bias, masks, sinks, dab, dbias_in, dh)


def _window_sum(x, w, back):
    n = x.shape[0]
    s, sh = x, 1
    while sh < w:
        s = s + pltpu.roll(s, sh if back else n - sh, axis=0)
        sh *= 2
    return s


def _pool_counts(first_row, n, w):
    t = first_row + lax.broadcasted_iota(jnp.int32, (n, 1), 0)
    return jnp.minimum(t + 1, w).astype(F32)


def _pool_diff(u_ref, uh_ref, i, T, g):
    u = u_ref[...]
    halo = jnp.where(i > 0, uh_ref[...], 0.0)
    ext = jnp.concatenate([halo, u], axis=0)
    w = POOL_WINDOWS[g]
    s = _window_sum(ext, w, back=True)[POOL_HALO:, :]
    return s / _pool_counts(i * T, T, w) - u


def _pool_in_specs(T):
    hb = T // POOL_HALO
    specs = []
    for g in range(4):
        specs.append(pl.BlockSpec((T, 256), functools.partial(lambda i, g: (i, U_OFF // 256 + g), g=g)))
        specs.append(pl.BlockSpec((POOL_HALO, 256), functools.partial(
            lambda i, g: (jnp.maximum(i * hb - 1, 0), U_OFF // 256 + g), g=g)))
    return specs


def _pool_weight_specs():
    return [pl.BlockSpec((4, 256, 256), lambda i: (0, 0, 0)), pl.BlockSpec((1, POOL_WIDTH), lambda i: (0, 0))]


def _pool_fwd(h, ab, w_pool, pool_scale, name):
    S = h.shape[0]
    T = _tile(S, 512)

    def body(*refs):
        u_refs = refs[0:8]
        gb_refs = refs[8:12]
        wp_ref, sc_ref, o_ref = refs[12], refs[13], refs[15]
        i = pl.program_id(0)
        for g in range(4):
            diff = _pool_diff(u_refs[2 * g], u_refs[2 * g + 1], i, T, g).astype(BF16)
            mixed = jnp.dot(diff, wp_ref[g], preferred_element_type=F32) * sc_ref[:, 256 * g:256 * (g + 1)]
            gb = gb_refs[g][...]
            o_ref[:, 256 * g:256 * (g + 1)] = (mixed * (gb * _sigmoid(gb))).astype(BF16)

    in_specs = _pool_in_specs(T) + [
        pl.BlockSpec((T, 256), functools.partial(lambda i, g: (i, GB_OFF // 256 + g), g=g)) for g in range(4)
    ] + _pool_weight_specs() + [pl.BlockSpec(memory_space=pl.ANY)]
    return pl.pallas_call(
        body,
        name=name,
        grid=(S // T,),
        in_specs=in_specs,
        out_specs=pl.BlockSpec((T, POOL_WIDTH), lambda i: (i, 1)),
        out_shape=jax.ShapeDtypeStruct(ab.shape, BF16),
        input_output_aliases={14: 0},
        compiler_params=_params(("arbitrary",)),
    )(*([h] * 12), w_pool, pool_scale, ab)


DH_POOL_COLS = IN_COLS // 2


def _pool_bwd(h, dab, w_pool, pool_scale, after, name):
    S = h.shape[0]
    T = _tile(S, 512)
    nt = S // T
    hb = T // POOL_HALO
    E = T + POOL_HALO
    lead = U_OFF - DH_POOL_COLS

    def body(*refs):
        u_refs = refs[0:8]
        gb_refs = refs[8:16]
        db_refs = refs[16:24]
        wp_ref, sc_ref = refs[24], refs[25]
        dh_ref, dwp_ref, dsc_ref, dbi_ref = refs[27:31]
        i = pl.program_id(0)

        @pl.when(i == 0)
        def _():
            dwp_ref[...] = jnp.zeros_like(dwp_ref)
            dsc_ref[...] = jnp.zeros_like(dsc_ref)
            dbi_ref[...] = jnp.zeros_like(dbi_ref)

        dh_ref[:, 0:lead] = jnp.zeros((T, lead), BF16)
        for g in range(4):
            w = POOL_WINDOWS[g]
            cols = slice(256 * g, 256 * (g + 1))
            scale = sc_ref[:, cols]
            wp = wp_ref[g]
            diff = _pool_diff(u_refs[2 * g], u_refs[2 * g + 1], i, T, g).astype(BF16)
            mixed = jnp.dot(diff, wp, preferred_element_type=F32)
            keep = i < nt - 1
            gb = jnp.concatenate([gb_refs[2 * g][...], jnp.where(keep, gb_refs[2 * g + 1][...], 0.0)], axis=0)
            db = jnp.concatenate([db_refs[2 * g][...], jnp.where(keep, db_refs[2 * g + 1][...], 0.0)], axis=0)
            sg = _sigmoid(gb)
            dms = db * (gb * sg)
            dmixed = (dms * scale).astype(BF16)
            ddiff = lax.dot_general(dmixed, wp, (((1,), (1,)), ((), ())), preferred_element_type=F32)
            r = ddiff / _pool_counts(i * T, E, w)
            du = _window_sum(r, w, back=False)[:T, :] - ddiff[:T, :]
            dgb = db[:T, :] * (mixed * scale) * (sg[:T, :] * (1.0 + gb[:T, :] * (1.0 - sg[:T, :])))
            u_cols = slice(lead + 256 * g, lead + 256 * (g + 1))
            gb_cols = slice(lead + POOL_WIDTH + 256 * g, lead + POOL_WIDTH + 256 * (g + 1))
            dh_ref[:, u_cols] = du.astype(BF16)
            dh_ref[:, gb_cols] = dgb.astype(BF16)
            dbi_ref[:, u_cols] += jnp.sum(du, axis=0, keepdims=True)
            dbi_ref[:, gb_cols] += jnp.sum(dgb, axis=0, keepdims=True)
            dsc_ref[:, cols] += jnp.sum(dms[:T, :] * mixed, axis=0, keepdims=True)
            dwp_ref[g] += lax.dot_general(diff, dmixed[:T, :], (((0,), (0,)), ((), ())), preferred_element_type=F32)

    def rows_after(i):
        return jnp.minimum((i + 1) * hb, S // POOL_HALO - 1)

    in_specs = _pool_in_specs(T)
    for off in (GB_OFF // 256, ATTN_WIDTH // 256):
        for g in range(4):
            in_specs.append(pl.BlockSpec((T, 256), functools.partial(lambda i, c: (i, c), c=off + g)))
            in_specs.append(pl.BlockSpec((POOL_HALO, 256), functools.partial(lambda i, c: (rows_after(i), c), c=off + g)))
    in_specs += _pool_weight_specs() + [pl.BlockSpec(memory_space=pl.ANY)]
    return pl.pallas_call(
        body,
        name=name,
        grid=(nt,),
        in_specs=in_specs,
        out_specs=[
            pl.BlockSpec((T, DH_POOL_COLS), lambda i: (i, 1)),
            pl.BlockSpec((4, 256, 256), lambda i: (0, 0, 0)),
            pl.BlockSpec((1, POOL_WIDTH), lambda i: (0, 0)),
            pl.BlockSpec((1, DH_POOL_COLS), lambda i: (0, 0)),
        ],
        out_shape=[
            jax.ShapeDtypeStruct((S, IN_COLS), BF16),
            jax.ShapeDtypeStruct((4, 256, 256), F32),
            jax.ShapeDtypeStruct((1, POOL_WIDTH), F32),
            jax.ShapeDtypeStruct((1, DH_POOL_COLS), F32),
        ],
        compiler_params=_params(("arbitrary",)),
    )(*([h] * 16), *([dab] * 8), w_pool, pool_scale, after)


def _load_resident(pairs, sems):
    @pl.when(pl.program_id(0) == 0)
    def _():
        cps = [pltpu.make_async_copy(src, dst, sems.at[n]) for n, (src, dst) in enumerate(pairs)]
        for cp in cps:
            cp.start()
        for cp in cps:
            cp.wait()


def _mix_ln_fwd(ab, x, pb, w_out, w_gate, w_ple_t, gain, bias, alpha, after, name):
    S = x.shape[0]
    T = _tile(S, 256)

    def body(ab_ref, x_ref, p_ref, wo_hbm, wg_hbm, wp_hbm, g_ref, b_ref, after_ref,
             y_ref, yb_ref, xh_ref, rs_ref, gp_ref, pe_ref, wo, wg, wp, sems):
        del after_ref
        _load_resident(((wo_hbm, wo), (wg_hbm, wg), (wp_hbm, wp)), sems)
        x = x_ref[...]
        mix = jnp.dot(ab_ref[...], wo[...], preferred_element_type=F32)
        gp = jnp.dot(x.astype(BF16), wg[...], preferred_element_type=F32)
        pe = lax.dot_general(p_ref[...], wp[...], (((1,), (1,)), ((), ())), preferred_element_type=F32)
        z = alpha * x + mix + _sigmoid(gp) * pe
        mu = jnp.mean(z, axis=-1, keepdims=True)
        zc = z - mu
        var = jnp.mean(zc * zc, axis=-1, keepdims=True)
        rstd = lax.rsqrt(var + LN_EPS)
        xhat = zc * rstd
        y = xhat * g_ref[...] + b_ref[...]
        y_ref[...] = y
        yb_ref[...] = y.astype(BF16)
        xh_ref[...] = xhat
        rs_ref[...] = rstd
        gp_ref[...] = gp
        pe_ref[...] = pe

    row = pl.BlockSpec((T, D_MODEL), lambda i: (i, 0))
    vec = pl.BlockSpec((1, D_MODEL), lambda i: (0, 0))
    any_spec = pl.BlockSpec(memory_space=pl.ANY)
    f32_rows = jax.ShapeDtypeStruct((S, D_MODEL), F32)
    return pl.pallas_call(
        body,
        name=name,
        grid=(S // T,),
        in_specs=[row, row, pl.BlockSpec((T, PLE_DIM), lambda i: (i, 0)), any_spec, any_spec, any_spec, vec, vec, any_spec],
        out_specs=[row, row, row, pl.BlockSpec((T, 1), lambda i: (i, 0)), row, row],
        out_shape=[f32_rows, jax.ShapeDtypeStruct((S, D_MODEL), BF16), f32_rows, jax.ShapeDtypeStruct((S, 1), F32),
                   f32_rows, f32_rows],
        scratch_shapes=[pltpu.VMEM(w_out.shape, BF16), pltpu.VMEM(w_gate.shape, BF16), pltpu.VMEM(w_ple_t.shape, BF16),
                        pltpu.SemaphoreType.DMA((3,))],
        compiler_params=_params(("arbitrary",)),
    )(ab, x, pb, w_out, w_gate, w_ple_t, gain, bias, after)


def _ln_dmix_bwd(dy, xhat, rstd, gain, gp, pe, w_out, after, name):
    S = dy.shape[0]
    T = _tile(S, 256)

    def body(dy_ref, xh_ref, rs_ref, g_ref, gp_ref, pe_ref, wo_hbm, after_ref,
             dz_ref, dzb_ref, dpe_ref, dgp_ref, dab_ref, dgain_ref, dbias_ref, wo, sems):
        del after_ref
        _load_resident(((wo_hbm, wo),), sems)

        @pl.when(pl.program_id(0) == 0)
        def _():
            dgain_ref[...] = jnp.zeros_like(dgain_ref)
            dbias_ref[...] = jnp.zeros_like(dbias_ref)

        dy = dy_ref[...]
        xhat = xh_ref[...]
        dyg = dy * g_ref[...]
        c1 = jnp.mean(dyg, axis=-1, keepdims=True)
        c2 = jnp.mean(dyg * xhat, axis=-1, keepdims=True)
        dz = rs_ref[...] * (dyg - c1 - xhat * c2)
        dgain_ref[...] += jnp.sum(dy * xhat, axis=0, keepdims=True)
        dbias_ref[...] += jnp.sum(dy, axis=0, keepdims=True)
        sg = _sigmoid(gp_ref[...])
        dzb = dz.astype(BF16)
        dz_ref[...] = dz
        dzb_ref[...] = dzb
        dpe_ref[...] = (dz * sg).astype(BF16)
        dgp_ref[...] = (dz * pe_ref[...] * (sg * (1.0 - sg))).astype(BF16)
        dab_ref[...] = lax.dot_general(dzb, wo[...], (((1,), (1,)), ((), ())), preferred_element_type=F32)

    row = pl.BlockSpec((T, D_MODEL), lambda i: (i, 0))
    vec = pl.BlockSpec((1, D_MODEL), lambda i: (0, 0))
    any_spec = pl.BlockSpec(memory_space=pl.ANY)
    bf16_rows = jax.ShapeDtypeStruct((S, D_MODEL), BF16)
    return pl.pallas_call(
        body,
        name=name,
        grid=(S // T,),
        in_specs=[row, row, pl.BlockSpec((T, 1), lambda i: (i, 0)), vec, row, row, any_spec, any_spec],
        out_specs=[row, row, row, row, row, vec, vec],
        out_shape=[
            jax.ShapeDtypeStruct((S, D_MODEL), F32), bf16_rows, bf16_rows, bf16_rows,
            jax.ShapeDtypeStruct((S, D_MODEL), F32),
            jax.ShapeDtypeStruct((1, D_MODEL), F32),
            jax.ShapeDtypeStruct((1, D_MODEL), F32),
        ],
        scratch_shapes=[pltpu.VMEM(w_out.shape, BF16), pltpu.SemaphoreType.DMA((1,))],
        compiler_params=_params(("arbitrary",)),
    )(dy, xhat, rstd, gain, gp, pe, w_out, after)


def _loss_head(y, target):
    S = y.shape[0]
    T = _tile(S, 256)

    def body(y_ref, t_ref, dy_ref, l_ref):
        @pl.when(pl.program_id(0) == 0)
        def _():
            l_ref[...] = jnp.zeros_like(l_ref)

        err = y_ref[...] - t_ref[...]
        dy_ref[...] = err * (1.0 / D_MODEL)
        per_token = jnp.mean(err * err, axis=-1, keepdims=True)
        l_ref[...] += 0.5 * jnp.sum(per_token, axis=0, keepdims=True)

    row = pl.BlockSpec((T, D_MODEL), lambda i: (i, 0))
    return pl.pallas_call(
        body,
        name="loss_head",
        grid=(S // T,),
        in_specs=[row, row],
        out_specs=[row, pl.BlockSpec((8, LANES), lambda i: (0, 0))],
        out_shape=[jax.ShapeDtypeStruct((S, D_MODEL), F32), jax.ShapeDtypeStruct((8, LANES), F32)],
        compiler_params=_params(("arbitrary",)),
    )(y, target)


def _sum_slabs(r, name):
    _, R, C = r.shape
    T = _tile(R, 256)

    def body(r_ref, o_ref):
        acc = r_ref[0].astype(F32)
        for s in range(1, N_DEV):
            acc = acc + r_ref[s].astype(F32)
        o_ref[...] = acc

    return pl.pallas_call(
        body,
        name=name,
        grid=(R // T,),
        in_specs=[pl.BlockSpec((N_DEV, T, C), lambda i: (0, i, 0))],
        out_specs=pl.BlockSpec((T, C), lambda i: (i, 0)),
        out_shape=jax.ShapeDtypeStruct((R, C), F32),
        compiler_params=_params(("parallel",)),
    )(r)


def _adamw(w, g, m, v, name):
    if w.ndim == 3:
        L, R, C = w.shape
        T = _tile(R, 256)
        grid = (L, R // T)
        blk = pl.BlockSpec((None, T, C), lambda l, i: (l, i, 0))
    else:
        R, C = w.shape
        T = _tile(R, 256)
        grid = (R // T,)
        blk = pl.BlockSpec((T, C), lambda i: (i, 0))

    def body(w_ref, g_ref, m_ref, v_ref, d_ref, nm_ref, nv_ref):
        g = g_ref[...]
        m = ADAM_B1 * m_ref[...] + (1.0 - ADAM_B1) * g
        v = ADAM_B2 * v_ref[...] + (1.0 - ADAM_B2) * jnp.square(g)
        m_hat = m / (1.0 - ADAM_B1 ** ADAM_STEP)
        v_hat = v / (1.0 - ADAM_B2 ** ADAM_STEP)
        d_ref[...] = -ADAM_LR * (m_hat / (jnp.sqrt(v_hat) + ADAM_EPS) + ADAM_WD * w_ref[...])
        nm_ref[...] = m
        nv_ref[...] = v

    shp = jax.ShapeDtypeStruct(w.shape, F32)
    return pl.pallas_call(
        body,
        name=name,
        grid=grid,
        in_specs=[blk] * 4,
        out_specs=[blk] * 3,
        out_shape=[shp] * 3,
        compiler_params=_params(("parallel",) * len(grid)),
    )(w, g, m, v)


def _mesh_pos():
    return lax.axis_index("x"), lax.axis_index("y"), lax.axis_index("c")


def _flip(pos, k):
    x, y, c = pos
    return (1 - x if k & 4 else x, 1 - y if k & 2 else y, 1 - c if k & 1 else c)


def _index(pos):
    return 4 * pos[0] + 2 * pos[1] + pos[2]


HBM_SPEC = pl.BlockSpec(memory_space=pltpu.HBM)
SEM_SPEC = pl.BlockSpec(memory_space=pltpu.SEMAPHORE)
ANY_SPEC = pl.BlockSpec(memory_space=pl.ANY)
SPLIT_EFFECT = pltpu.SideEffectType.DATAFLOW_SIDE_EFFECTING
GATHER_FLIPS = (1, 4, 2, 6)
CHIP_FLIPS = (4, 2, 6)
TOKEN = jax.ShapeDtypeStruct((8, LANES), F32)


def _hbm(a):
    return pltpu.with_memory_space_constraint(a, pltpu.HBM)


def _hbm_like(a):
    return pltpu.HBM(a.shape, a.dtype)


def _block_rows(ref, pos, r):
    return ref.at[:, pl.ds(_index(pos) * r, r), :]


def _gather_start(shards, after, name):
    n = len(shards)
    lands = [lax.empty((s.shape[0], N_DEV * s.shape[1], s.shape[2]), s.dtype) for s in shards]

    def body(*refs):
        ins, bufs = refs[:n], refs[n:2 * n]
        send_sems, recv_sems = refs[2 * n + 1], refs[2 * n + 2]
        token = refs[4 * n + 3]
        me = _mesh_pos()
        for a in range(n):
            for j, k in enumerate(GATHER_FLIPS):
                pltpu.make_async_remote_copy(
                    src_ref=ins[a], dst_ref=_block_rows(bufs[a], me, shards[a].shape[1]),
                    send_sem=send_sems.at[4 * a + j], recv_sem=recv_sems.at[4 * a + j],
                    device_id=_flip(me, k), device_id_type=MESH_ID).start()
        token[...] = jnp.zeros_like(token)

    outs = pl.pallas_call(
        body,
        name=name,
        in_specs=[HBM_SPEC] * (2 * n) + [ANY_SPEC],
        out_specs=[SEM_SPEC, SEM_SPEC] + [HBM_SPEC] * (2 * n) + [pl.BlockSpec(memory_space=pltpu.VMEM)],
        out_shape=[pltpu.SemaphoreType.DMA((4 * n,)), pltpu.SemaphoreType.DMA((4 * n,))]
        + [_hbm_like(s) for s in shards] + [_hbm_like(b) for b in lands] + [TOKEN],
        input_output_aliases={i: 2 + i for i in range(2 * n)},
        compiler_params=pltpu.CompilerParams(has_side_effects=SPLIT_EFFECT),
    )(*[_hbm(s) for s in shards], *[_hbm(b) for b in lands], after)
    return outs[0], outs[1], outs[2:2 + n], outs[2 + n:2 + 2 * n], outs[2 + 2 * n]


def _gather_wait(started, after, name):
    send_sems, recv_sems, shards, lands, _ = started
    n = len(shards)

    def body(*refs):
        ins, bufs = refs[:n], refs[n:2 * n]
        send_sems, recv_sems = refs[2 * n], refs[2 * n + 1]
        me = _mesh_pos()
        for a in range(n):
            for j, k in enumerate(GATHER_FLIPS):
                cp = pltpu.make_async_remote_copy(
                    src_ref=ins[a], dst_ref=_block_rows(bufs[a], _flip(me, k), shards[a].shape[1]),
                    send_sem=send_sems.at[4 * a + j], recv_sem=recv_sems.at[4 * a + j],
                    device_id=_flip(me, k), device_id_type=MESH_ID)
                cp.wait_send()
                cp.wait_recv()

    outs = pl.pallas_call(
        body,
        name=name,
        in_specs=[HBM_SPEC] * (2 * n) + [SEM_SPEC, SEM_SPEC, ANY_SPEC],
        out_specs=[HBM_SPEC] * (2 * n),
        out_shape=[_hbm_like(s) for s in shards] + [_hbm_like(b) for b in lands],
        input_output_aliases={i: i for i in range(2 * n)},
        compiler_params=pltpu.CompilerParams(has_side_effects=SPLIT_EFFECT),
    )(*shards, *lands, send_sems, recv_sems, after)
    return outs[:n], outs[n:]


def _gather_pass(shards, lands, name):
    n = len(shards)

    def body(*refs):
        ins, bufs = refs[:n], refs[n:2 * n]
        token = refs[3 * n]
        send_sems, recv_sems, local_sems = refs[3 * n + 1:]
        me = _mesh_pos()
        sibling = _flip(me, 1)

        def copy(a, j, block):
            rows = _block_rows(bufs[a], block, shards[a].shape[1])
            return pltpu.make_async_remote_copy(
                src_ref=rows, dst_ref=rows, send_sem=send_sems.at[3 * a + j], recv_sem=recv_sems.at[3 * a + j],
                device_id=sibling, device_id_type=MESH_ID)

        mine = [pltpu.make_async_copy(ins[a], _block_rows(bufs[a], me, shards[a].shape[1]), local_sems.at[a])
                for a in range(n)]
        sends = [copy(a, j, _flip(me, k)) for a in range(n) for j, k in enumerate(CHIP_FLIPS)]
        for cp in mine + sends:
            cp.start()
        for a in range(n):
            for j, k in enumerate(CHIP_FLIPS):
                copy(a, j, _flip(sibling, k)).wait_recv()
        for cp in sends:
            cp.wait_send()
        for cp in mine:
            cp.wait()
        token[...] = jnp.zeros_like(token)

    outs = pl.pallas_call(
        body,
        name=name,
        in_specs=[pl.BlockSpec(memory_space=pltpu.VMEM)] * n + [ANY_SPEC] * n,
        out_specs=[ANY_SPEC] * n + [pl.BlockSpec(memory_space=pltpu.VMEM)],
        out_shape=[jax.ShapeDtypeStruct(b.shape, b.dtype) for b in lands] + [TOKEN],
        scratch_shapes=[pltpu.SemaphoreType.DMA((3 * n,)), pltpu.SemaphoreType.DMA((3 * n,)), pltpu.SemaphoreType.DMA((n,))],
        input_output_aliases={n + i: i for i in range(n)},
        compiler_params=pltpu.CompilerParams(has_side_effects=True, vmem_limit_bytes=VMEM_LIMIT_BYTES),
    )(*shards, *lands)
    return outs[:n], outs[n]


def _gather_own(shards, lands, name):
    n = len(shards)

    def body(*refs):
        ins, bufs, local_sems = refs[:n], refs[n:2 * n], refs[3 * n]
        me = _mesh_pos()
        cps = [pltpu.make_async_copy(ins[a], _block_rows(bufs[a], me, shards[a].shape[1]), local_sems.at[a])
               for a in range(n)]
        for cp in cps:
            cp.start()
        for cp in cps:
            cp.wait()

    return pl.pallas_call(
        body,
        name=name,
        in_specs=[pl.BlockSpec(memory_space=pltpu.VMEM)] * n + [ANY_SPEC] * n,
        out_specs=[ANY_SPEC] * n,
        out_shape=[jax.ShapeDtypeStruct(b.shape, b.dtype) for b in lands],
        scratch_shapes=[pltpu.SemaphoreType.DMA((n,))],
        input_output_aliases={n + i: i for i in range(n)},
        compiler_params=pltpu.CompilerParams(has_side_effects=True, vmem_limit_bytes=VMEM_LIMIT_BYTES),
    )(*shards, *lands)


def _pass_copy(bufs, send_sems, recv_sems, a, j, block, sibling):
    rows = _block_rows(bufs[a], block, bufs[a].shape[1] // N_DEV)
    return pltpu.make_async_remote_copy(
        src_ref=rows, dst_ref=rows, send_sem=send_sems.at[3 * a + j], recv_sem=recv_sems.at[3 * a + j],
        device_id=sibling, device_id_type=MESH_ID)


def _pass_start(lands, after, name):
    n = len(lands)

    def body(*refs):
        bufs = refs[:n]
        send_sems, recv_sems = refs[n + 1], refs[n + 2]
        token = refs[2 * n + 3]
        me = _mesh_pos()
        for a in range(n):
            for j, k in enumerate(CHIP_FLIPS):
                _pass_copy(bufs, send_sems, recv_sems, a, j, _flip(me, k), _flip(me, 1)).start()
        token[...] = jnp.zeros_like(token)

    outs = pl.pallas_call(
        body,
        name=name,
        in_specs=[HBM_SPEC] * n + [ANY_SPEC],
        out_specs=[SEM_SPEC, SEM_SPEC] + [HBM_SPEC] * n + [pl.BlockSpec(memory_space=pltpu.VMEM)],
        out_shape=[pltpu.SemaphoreType.DMA((3 * n,)), pltpu.SemaphoreType.DMA((3 * n,))]
        + [_hbm_like(b) for b in lands] + [TOKEN],
        input_output_aliases={i: 2 + i for i in range(n)},
        compiler_params=pltpu.CompilerParams(has_side_effects=SPLIT_EFFECT),
    )(*[_hbm(b) for b in lands], after)
    return outs[0], outs[1], outs[2:2 + n], outs[2 + n]


def _pass_wait(started, after, name):
    send_sems, recv_sems, lands, _ = started
    n = len(lands)

    def body(*refs):
        bufs = refs[:n]
        send_sems, recv_sems = refs[n], refs[n + 1]
        me = _mesh_pos()
        sibling = _flip(me, 1)
        for a in range(n):
            for j, k in enumerate(CHIP_FLIPS):
                _pass_copy(bufs, send_sems, recv_sems, a, j, _flip(me, k), sibling).wait_send()
                _pass_copy(bufs, send_sems, recv_sems, a, j, _flip(sibling, k), sibling).wait_recv()

    return pl.pallas_call(
        body,
        name=name,
        in_specs=[HBM_SPEC] * n + [SEM_SPEC, SEM_SPEC, ANY_SPEC],
        out_specs=[HBM_SPEC] * n,
        out_shape=[_hbm_like(b) for b in lands],
        input_output_aliases={i: i for i in range(n)},
        compiler_params=pltpu.CompilerParams(has_side_effects=SPLIT_EFFECT),
    )(*lands, send_sems, recv_sems, after)


def _place_own(grads, lands, layer, name):
    n = len(grads)
    blocks = [(g.shape[0], g.shape[1] // N_DEV, g.shape[2]) for g in grads]

    def body(*refs):
        ins, bufs = refs[:n], refs[n:2 * n]
        stage, in_sems, out_sems = refs[3 * n:4 * n], refs[4 * n], refs[4 * n + 1]
        me = _mesh_pos()
        loads = [pltpu.make_async_copy(_block_rows(ins[a], me, blocks[a][1]), stage[a], in_sems.at[a]) for a in range(n)]
        stores = [pltpu.make_async_copy(stage[a], bufs[a].at[_index(me), layer], out_sems.at[a]) for a in range(n)]
        for cp in loads:
            cp.start()
        for a in range(n):
            loads[a].wait()
            stores[a].start()
        for cp in stores:
            cp.wait()

    return pl.pallas_call(
        body,
        name=name,
        in_specs=[ANY_SPEC] * (2 * n),
        out_specs=[ANY_SPEC] * n,
        out_shape=[jax.ShapeDtypeStruct(b.shape, b.dtype) for b in lands],
        scratch_shapes=[pltpu.VMEM(blk, g.dtype) for blk, g in zip(blocks, grads)]
        + [pltpu.SemaphoreType.DMA((n,)), pltpu.SemaphoreType.DMA((n,))],
        input_output_aliases={n + i: i for i in range(n)},
        compiler_params=pltpu.CompilerParams(has_side_effects=True, vmem_limit_bytes=VMEM_LIMIT_BYTES),
    )(*grads, *lands)


def _scatter_copy(ins, bufs, send_sems, recv_sems, a, k, r, layer, me, slab):
    peer = _flip(me, k)
    return pltpu.make_async_remote_copy(
        src_ref=_block_rows(ins[a], peer, r), dst_ref=bufs[a].at[_index(slab), layer],
        send_sem=send_sems.at[7 * a + k - 1], recv_sem=recv_sems.at[7 * a + k - 1],
        device_id=peer, device_id_type=MESH_ID)


def _scatter_start(grads, lands, layer, name):
    n = len(grads)

    def body(*refs):
        ins, bufs = refs[:n], refs[n:2 * n]
        send_sems, recv_sems = refs[2 * n], refs[2 * n + 1]
        token = refs[4 * n + 2]
        me = _mesh_pos()
        for a in range(n):
            for k in range(1, N_DEV):
                _scatter_copy(ins, bufs, send_sems, recv_sems, a, k, grads[a].shape[1] // N_DEV, layer, me, me).start()
        token[...] = jnp.zeros_like(token)

    outs = pl.pallas_call(
        body,
        name=name,
        in_specs=[HBM_SPEC] * (2 * n),
        out_specs=[SEM_SPEC, SEM_SPEC] + [HBM_SPEC] * (2 * n) + [pl.BlockSpec(memory_space=pltpu.VMEM)],
        out_shape=[pltpu.SemaphoreType.DMA((7 * n,)), pltpu.SemaphoreType.DMA((7 * n,))]
        + [_hbm_like(g) for g in grads] + [_hbm_like(b) for b in lands] + [TOKEN],
        input_output_aliases={i: 2 + i for i in range(2 * n)},
        compiler_params=pltpu.CompilerParams(has_side_effects=SPLIT_EFFECT),
    )(*[_hbm(g) for g in grads], *[_hbm(b) for b in lands])
    return outs[0], outs[1], outs[2:2 + n], outs[2 + n:2 + 2 * n], outs[2 + 2 * n]


def _scatter_wait(started, layer, after, name):
    send_sems, recv_sems, grads, lands, _ = started
    n = len(grads)

    def body(*refs):
        ins, bufs = refs[:n], refs[n:2 * n]
        send_sems, recv_sems = refs[2 * n], refs[2 * n + 1]
        me = _mesh_pos()
        for a in range(n):
            for k in range(1, N_DEV):
                cp = _scatter_copy(ins, bufs, send_sems, recv_sems, a, k, grads[a].shape[1] // N_DEV, layer, me, _flip(me, k))
                cp.wait_send()
                cp.wait_recv()

    outs = pl.pallas_call(
        body,
        name=name,
        in_specs=[HBM_SPEC] * (2 * n) + [SEM_SPEC, SEM_SPEC, ANY_SPEC],
        out_specs=[HBM_SPEC] * (2 * n),
        out_shape=[_hbm_like(g) for g in grads] + [_hbm_like(b) for b in lands],
        input_output_aliases={i: i for i in range(2 * n)},
        compiler_params=pltpu.CompilerParams(has_side_effects=SPLIT_EFFECT),
    )(*grads, *lands, send_sems, recv_sems, after)
    return outs[n:]


def _allreduce_small(vec):
    R, C = vec.shape

    def body(v_ref, o_ref, buf, send_sems, recv_sems):
        me = _mesh_pos()
        buf[_index(me)] = v_ref[...]
        sends = []
        for k in range(1, N_DEV):
            sends.append(pltpu.make_async_remote_copy(
                src_ref=buf.at[_index(me)], dst_ref=buf.at[_index(me)],
                send_sem=send_sems.at[k - 1], recv_sem=recv_sems.at[k - 1],
                device_id=_flip(me, k), device_id_type=MESH_ID))
        for cp in sends:
            cp.start()
        for cp in sends:
            cp.wait_recv()
        for cp in sends:
            cp.wait_send()
        acc = buf[0]
        for s in range(1, N_DEV):
            acc = acc + buf[s]
        o_ref[...] = acc

    return pl.pallas_call(
        body,
        name="allreduce_small",
        in_specs=[pl.BlockSpec(memory_space=pltpu.VMEM)],
        out_specs=pl.BlockSpec(memory_space=pltpu.VMEM),
        out_shape=jax.ShapeDtypeStruct((R, C), F32),
        scratch_shapes=[pltpu.VMEM((N_DEV, R, C), F32), pltpu.SemaphoreType.DMA((7,)), pltpu.SemaphoreType.DMA((7,))],
        compiler_params=pltpu.CompilerParams(has_side_effects=True, vmem_limit_bytes=VMEM_LIMIT_BYTES),
    )(vec)


def _pack_small(parts):
    flat = jnp.concatenate([p.reshape(-1) for p in parts])
    n = flat.shape[0]
    rows = -(-n // SMALL_COLS)
    rows = -(-rows // 8) * 8
    return jnp.pad(flat, (0, rows * SMALL_COLS - n)).reshape(rows, SMALL_COLS)


def _unpack_small(packed, like):
    flat = packed.reshape(-1)
    out, pos = [], 0
    for p in like:
        out.append(flat[pos:pos + p.size].reshape(p.shape))
        pos += p.size
    return out


def kernel(x, p, w_in, b_in, w_out, attn_sinks, rel_bias, w_pool, pool_scale, w_ple, w_gate_ple, ln_gain, ln_bias, loss_target, m_w_in, m_b_in, m_w_out, m_attn_sinks, m_rel_bias, m_w_pool, m_pool_scale, m_w_ple, m_w_gate_ple, m_ln_gain, m_ln_bias, v_w_in, v_b_in, v_w_out, v_attn_sinks, v_rel_bias, v_w_pool, v_pool_scale, v_w_ple, v_w_gate_ple, v_ln_gain, v_ln_bias):
    L = w_in.shape[0]
    S = x.shape[1]
    alpha = (2.0 * L) ** 0.25
    bucket_np, masks_np, window_np = _band_constants()
    bucket, masks, window = jnp.asarray(bucket_np), jnp.asarray(masks_np), jnp.asarray(window_np)

    @functools.lru_cache(maxsize=None)
    def shards_of(l):
        return (jnp.swapaxes(w_in[l], 0, 1).astype(BF16)[None], w_out[l].astype(BF16)[None],
                w_gate_ple[l].astype(BF16)[None], jnp.swapaxes(w_ple[l], 0, 1).astype(BF16)[None], w_pool[l].astype(BF16))

    def gathered(started, after, tag):
        shards, lands = _gather_wait(started, after, name=f"gather_wait_{tag}")
        return _gather_pass(shards, lands, name=f"gather_pass_{tag}")

    bias = _bias_build(rel_bias, bucket).reshape(N_KV_HEADS, BIAS_ROWS, BLOCK)

    xs = x[0]
    xb = xs.astype(BF16)
    first_groups = ((0, 4), (1, 2, 3))
    token, first_started = rel_bias, []
    for tag, idxs in zip("ab", first_groups):
        first_started.append(_gather_start([shards_of(0)[i] for i in idxs], token, name=f"gather_start_0{tag}"))
        token = first_started[-1][4]
    started = {1: _gather_start(shards_of(1), token, name="gather_start_1")} if L > 1 else {}
    saved = []
    for l in range(L):
        pb = p[l, 0].astype(BF16)
        sinks_l = attn_sinks[l]
        scale_l = pool_scale[l].reshape(1, POOL_WIDTH)
        bias_l = b_in[l].reshape(1, IN_COLS)
        if l == 0:
            (w_in_f, w_pool_g), _ = gathered(first_started[0], started[1][4] if L > 1 else xb, "0a")
            w_in_t = w_in_f[0]
            h = _matmul(xb, w_in_t, tb=True, tm=512, tn=2176, tk=2048, out_dtype=F32, bias=bias_l, name=f"in_proj_{l}")
            (w_out_f, w_gate_f, w_ple_f), _ = gathered(first_started[1], h, "0b")
            w_out_g, w_gate_g, w_ple_t = w_out_f[0], w_gate_f[0], w_ple_f[0]
        else:
            w_in_t, w_out_g, w_gate_g, w_ple_t, w_pool_g = weights
            h = _matmul(xb, w_in_t, tb=True, tm=512, tn=2176, tk=2048, out_dtype=F32, bias=bias_l, name=f"in_proj_{l}")
        weights = (w_in_t, w_out_g, w_gate_g, w_ple_t, w_pool_g)
        ab = _attn_fwd(h, bias, masks, sinks_l, name=f"attn_fwd_{l}")
        ab = _pool_fwd(h, ab, w_pool_g, scale_l, name=f"pool_fwd_{l}")
        pin, passing = ab, None
        if 1 <= l and l + 1 < L:
            shards, lands = _gather_wait(started[l + 1], ab, name=f"gather_wait_{l + 1}")
            passing = _pass_start(_gather_own(shards, lands, name=f"gather_own_{l + 1}"), ab, name=f"pass_start_{l + 1}")
            pin = passing[3]
        if l + 2 < L:
            started[l + 2] = _gather_start(shards_of(l + 2), pin, name=f"gather_start_{l + 2}")
            pin = started[l + 2][4]
        y, yb, xhat, rstd, gp, pe = _mix_ln_fwd(ab, xs, pb, w_out_g, w_gate_g, w_ple_t, ln_gain[l].reshape(1, D_MODEL),
                                                ln_bias[l].reshape(1, D_MODEL), alpha, pin, name=f"mix_ln_fwd_{l}")
        saved.append((xb, pb, h, gp, pe, ab, xhat, rstd, sinks_l, scale_l, weights))
        xs, xb = y, yb
        if l + 1 < L:
            if passing is None:
                full, _ = gathered(started[l + 1], yb, l + 1)
            else:
                full = _pass_wait(passing, yb, name=f"pass_wait_{l + 1}")
            weights = (full[0][0], full[1][0], full[2][0], full[3][0], full[4])

    dy, loss_tile = _loss_head(xs, loss_target[0])

    dbias = jnp.zeros((N_KV_HEADS, BIAS_ROWS, BLOCK), F32)
    sh0 = shards_of(0)
    lands_a = [lax.empty((N_DEV, L) + sh0[i].shape, BF16) for i in (1, 2, 3)]
    lands_b = [lax.empty((N_DEV, L) + sh0[i].shape, BF16) for i in (0, 4)]
    g_b_in, g_sinks, g_scale, g_gain, g_beta = [], [], [], [], []
    pend_a = pend_b = None

    def scatter(grads, lands, pending, l, tag):
        if pending:
            lands = _scatter_wait(pending[0], pending[1], grads[0], name=f"scatter_wait_{pending[1]}{tag}")
        lands = _place_own(grads, lands, l, name=f"place_own_{l}{tag}")
        return _scatter_start(grads, lands, l, name=f"scatter_start_{l}{tag}"), l

    for l in reversed(range(L)):
        xb, pb, h, gp, pe, ab, xhat, rstd, sinks_l, scale_l, weights = saved[l]
        w_in_t, w_out_g, w_gate_g, w_ple_t, w_pool_g = weights
        dz, dzb, dpe, dgp, dab, dgain, dbeta = _ln_dmix_bwd(
            dy, xhat, rstd, ln_gain[l].reshape(1, D_MODEL), gp, pe, w_out_g, pend_b[0][4] if pend_b else rel_bias,
            name=f"ln_dmix_bwd_{l}")
        g_w_out = _matmul(ab, dzb, ta=True, tm=512, tn=1024, tk=4096, out_dtype=BF16, name=f"dw_out_{l}")
        g_w_gate = _matmul(xb, dgp, ta=True, tm=512, tn=1024, tk=4096, out_dtype=BF16, name=f"dw_gate_{l}")
        g_w_ple_t = _matmul(dpe, pb, ta=True, tm=1024, tn=256, tk=1024, out_dtype=BF16, name=f"dw_ple_{l}")
        pend_a = scatter([g_w_out[None], g_w_gate[None], g_w_ple_t[None]], lands_a, pend_a, l, "a")
        dh, dwp, dsc, db_pool = _pool_bwd(h, dab, w_pool_g, scale_l, pend_a[0][4], name=f"pool_bwd_{l}")
        dh, dbias, dsink, db_attn = _attn_bwd(h, dab, dh, bias, masks, sinks_l, dbias, name=f"attn_bwd_{l}")
        g_w_in_t = _matmul(dh, xb, ta=True, tm=256, tn=1024, tk=4096, out_dtype=BF16, name=f"dw_in_{l}")
        pend_b = scatter([g_w_in_t[None], dwp.astype(BF16)], lands_b, pend_b, l, "b")
        g_b_in.append(jnp.concatenate([db_attn[0], db_pool[0, U_OFF - DH_POOL_COLS:]]))
        dx = _matmul(dgp, w_gate_g, tb=True, tm=512, tn=2048, tk=2048, out_dtype=F32, add=dz, add_scale=alpha,
                     after=pend_b[0][4], name=f"dx_gate_{l}")
        dy = _matmul(dh, w_in_t, tm=512, tn=1024, tk=4352, out_dtype=F32, add=dx, name=f"dx_in_{l}")
        g_sinks.append(dsink[:, :, 0].reshape(N_HEADS))
        g_scale.append(dsc.reshape(POOL_WIDTH))
        g_gain.append(dgain.reshape(D_MODEL))
        g_beta.append(dbeta.reshape(D_MODEL))
    grad_x = dy[None]
    for lst in (g_b_in, g_sinks, g_scale, g_gain, g_beta):
        lst.reverse()
    g_rel = _bias_bwd(dbias.reshape((N_HEADS,) + BAND), bucket, window)[:, :N_HEADS]

    def big(w, g, m, v, name):
        shape = w.shape
        three_d = shape if len(shape) == 3 else (shape[0] * shape[1], shape[2], shape[3])
        d, nm, nv = _adamw(w.reshape(three_d), g.reshape(three_d), m.reshape(three_d), v.reshape(three_d), name=name)
        return d.reshape(shape), nm.reshape(shape), nv.reshape(shape)

    r_out, r_gate, r_ple = _scatter_wait(pend_a[0], pend_a[1], dy, name="scatter_wait_0a")
    small_like = [b_in, attn_sinks, rel_bias, pool_scale, ln_gain, ln_bias]
    small_g = _allreduce_small(_pack_small([
        jnp.stack(g_b_in).reshape(L, IN_COLS), jnp.stack(g_sinks), g_rel, jnp.stack(g_scale), jnp.stack(g_gain),
        jnp.stack(g_beta), loss_tile[0, :1]]))
    grad_w_out = _sum_slabs(r_out.reshape(N_DEV, L * 256, D_MODEL), name="sum_w_out").reshape(L, 256, D_MODEL)
    grad_w_gate = _sum_slabs(r_gate.reshape(N_DEV, L * 256, D_MODEL), name="sum_w_gate").reshape(L, 256, D_MODEL)
    gt_ple = _sum_slabs(r_ple.reshape(N_DEV, L * 256, PLE_DIM), name="sum_w_ple")
    grad_w_ple = jnp.swapaxes(gt_ple.reshape(L, 256, PLE_DIM), 1, 2)
    upd_out = big(w_out, grad_w_out, m_w_out, v_w_out, "adamw_w_out")
    upd_ple = big(w_ple, grad_w_ple, m_w_ple, v_w_ple, "adamw_w_ple")
    upd_gate = big(w_gate_ple, grad_w_gate, m_w_gate_ple, v_w_gate_ple, "adamw_w_gate")

    r_in, r_pool = _scatter_wait(pend_b[0], pend_b[1], upd_gate[0], name="scatter_wait_0b")
    gt_in = _sum_slabs(r_in.reshape(N_DEV, L * 544, D_MODEL), name="sum_w_in")
    grad_w_in = jnp.swapaxes(gt_in.reshape(L, 544, D_MODEL), 1, 2)
    grad_w_pool = _sum_slabs(r_pool.reshape(N_DEV, L * 4 * 32, 256), name="sum_w_pool").reshape(L, 4, 32, 256)
    upd_in = big(w_in, grad_w_in, m_w_in, v_w_in, "adamw_w_in")
    upd_pool = big(w_pool, grad_w_pool, m_w_pool, v_w_pool, "adamw_w_pool")

    zero1 = jnp.zeros((1,), F32)
    sw = _pack_small(small_like + [zero1])
    sm = _pack_small([m_b_in, m_attn_sinks, m_rel_bias, m_pool_scale, m_ln_gain, m_ln_bias, zero1])
    sv = _pack_small([v_b_in, v_attn_sinks, v_rel_bias, v_pool_scale, v_ln_gain, v_ln_bias, zero1])
    sd, snm, snv = _adamw(sw, small_g, sm, sv, name="adamw_small")
    like = small_like + [zero1]
    sg_parts = _unpack_small(small_g, like)
    sd_parts, snm_parts, snv_parts = _unpack_small(sd, like), _unpack_small(snm, like), _unpack_small(snv, like)
    loss = sg_parts[6][0]

    def assemble(big_parts, small_parts):
        w_in_, w_out_, w_pool_, w_ple_, w_gate_ = big_parts
        b_in_, sinks_, rel_, scale_, gain_, beta_ = small_parts[:6]
        return [w_in_, b_in_, w_out_, sinks_, rel_, w_pool_, scale_, w_ple_, w_gate_, gain_, beta_]

    grads = assemble([grad_w_in, grad_w_out, grad_w_pool, grad_w_ple, grad_w_gate], sg_parts)
    ups = [upd_in, upd_out, upd_pool, upd_ple, upd_gate]
    deltas = assemble([u[0] for u in ups], sd_parts)
    new_m = assemble([u[1] for u in ups], snm_parts)
    new_v = assemble([u[2] for u in ups], snv_parts)
    return (loss, grad_x, *grads, *deltas, *new_m, *new_v)
```

```python
import functools
import math

import numpy as np
import jax
import jax.numpy as jnp
from jax import lax
from jax.experimental import pallas as pl
from jax.experimental.pallas import tpu as pltpu

F32 = jnp.float32
BF16 = jnp.bfloat16

D_MODEL = 2048
PLE_DIM = 256
ATTN_WIDTH = 1024
POOL_WIDTH = 1024
HEAD_DIM = 64
N_HEADS = 16
N_KV_HEADS = 2
KV_GROUP = 8
WINDOW = 128
BLOCK = 128
POOL_WINDOWS = (2, 4, 8, 16)
POOL_GROUP_DIM = 256
POOL_HALO = 16
REL_BUCKETS = 32
REL_MAX_DIST = 128
LN_EPS = 1e-5
KV_COLS = N_KV_HEADS * HEAD_DIM
IN_COLS = 4352
Q_OFF, KV_OFF, GA_OFF, U_OFF, GB_OFF = 0, 1024, 1280, 2304, 3328
ATTN_SCALE = 1.0 / math.sqrt(HEAD_DIM)
NEG_BIG = -1e30
LANES = 128

ADAM_LR = 0.001
ADAM_B1 = 0.9
ADAM_B2 = 0.999
ADAM_EPS = 1e-08
ADAM_WD = 0.01
ADAM_STEP = 10

N_DEV = 8
MESH_ID = pl.DeviceIdType.MESH
VMEM_LIMIT_BYTES = 52 * 1024 * 1024
SMALL_COLS = 1024


def _params(sem=None):
    return pltpu.CompilerParams(dimension_semantics=sem, vmem_limit_bytes=VMEM_LIMIT_BYTES)


def _sigmoid(x):
    return 1.0 / (1.0 + jnp.exp(-x))


def _tile(n, pref, unit=16):
    if n <= pref:
        return n
    t = pref - pref % unit
    while n % t:
        t -= unit
    assert t > 0, (n, pref)
    return t


def _matmul(a, b, *, name, ta=False, tb=False, tm, tn, tk, out_dtype, bias=None, add=None, add_scale=1.0, after=None):
    M, K = (a.shape[1], a.shape[0]) if ta else a.shape
    N = b.shape[0] if tb else b.shape[1]
    assert (b.shape[1] if tb else b.shape[0]) == K
    tm, tn, tk = _tile(M, tm), _tile(N, tn), _tile(K, tk)
    nm, nn, nk = M // tm, N // tn, K // tk
    a_spec = pl.BlockSpec((tk, tm), lambda j, i, k: (k, i)) if ta else pl.BlockSpec((tm, tk), lambda j, i, k: (i, k))
    b_spec = pl.BlockSpec((tn, tk), lambda j, i, k: (j, k)) if tb else pl.BlockSpec((tk, tn), lambda j, i, k: (k, j))
    dims = (((0 if ta else 1,), (1 if tb else 0,)), ((), ()))
    operands, in_specs = [a, b], [a_spec, b_spec]
    if bias is not None:
        operands.append(bias)
        in_specs.append(pl.BlockSpec((1, tn), lambda j, i, k: (0, j)))
    if add is not None:
        operands.append(add)
        in_specs.append(pl.BlockSpec((tm, tn), lambda j, i, k: (i, j)))
    if after is not None:
        operands.append(after)
        in_specs.append(pl.BlockSpec(memory_space=pl.ANY))

    def body(*refs):
        a_ref, b_ref = refs[0], refs[1]
        pos = 2
        bias_ref = add_ref = None
        if bias is not None:
            bias_ref = refs[pos]
            pos += 1
        if add is not None:
            add_ref = refs[pos]
            pos += 1
        if after is not None:
            pos += 1
        o_ref = refs[pos]
        part = lax.dot_general(a_ref[...].astype(BF16), b_ref[...].astype(BF16), dims, preferred_element_type=F32)

        def finish(acc):
            if bias_ref is not None:
                acc = acc + bias_ref[...]
            if add_ref is not None:
                acc = acc + add_scale * add_ref[...].astype(F32)
            o_ref[...] = acc.astype(out_dtype)

        if nk == 1:
            finish(part)
        else:
            acc_ref = refs[pos + 1]
            k = pl.program_id(2)

            @pl.when(k == 0)
            def _():
                acc_ref[...] = part

            @pl.when(k > 0)
            def _():
                acc_ref[...] += part

            @pl.when(k == nk - 1)
            def _():
                finish(acc_ref[...])

    return pl.pallas_call(
        body,
        name=name,
        grid=(nn, nm, nk),
        in_specs=in_specs,
        out_specs=pl.BlockSpec((tm, tn), lambda j, i, k: (i, j)),
        out_shape=jax.ShapeDtypeStruct((M, N), out_dtype),
        scratch_shapes=[pltpu.VMEM((tm, tn), F32)] if nk > 1 else [],
        compiler_params=_params(("parallel", "parallel", "arbitrary")),
    )(*operands)


BAND = (2 * BLOCK, BLOCK)
BIAS_ROWS = KV_GROUP * 2 * BLOCK


def _band_constants():
    qq = np.arange(BLOCK)[None, :]
    kk = np.arange(2 * BLOCK)[:, None]
    dist = qq + BLOCK - kk
    in_window = (dist >= 0) & (dist < WINDOW)
    max_exact = REL_BUCKETS // 2
    d = np.maximum(dist, 0)
    d_f = np.maximum(d, 1).astype(np.float32)
    large = max_exact + (
        np.log(d_f / np.float32(max_exact)) / np.float32(math.log(REL_MAX_DIST / max_exact)) * np.float32(REL_BUCKETS - max_exact)
    ).astype(np.int32)
    large = np.minimum(large, REL_BUCKETS - 1)
    bucket = np.where(d < max_exact, d, large).astype(np.int32)
    bucket = np.where(in_window, bucket, 0).astype(np.int32)
    first = in_window & (kk >= BLOCK)
    masks = np.stack([first, in_window]).astype(np.float32)
    return bucket, masks, in_window.astype(np.float32)


def _bias_build(rel_bias, bucket):
    def body(rb_ref, bkt_ref, o_ref):
        h = pl.program_id(0)
        bkt = bkt_ref[...]

        def step(b, acc):
            return jnp.where(bkt == b, rb_ref[b, h], acc)

        o_ref[0] = lax.fori_loop(0, REL_BUCKETS, step, jnp.zeros(BAND, F32))

    return pl.pallas_call(
        body,
        name="bias_build",
        grid=(N_HEADS,),
        in_specs=[pl.BlockSpec(memory_space=pltpu.SMEM), pl.BlockSpec(BAND, lambda h: (0, 0))],
        out_specs=pl.BlockSpec((1,) + BAND, lambda h: (h, 0, 0)),
        out_shape=jax.ShapeDtypeStruct((N_HEADS,) + BAND, F32),
        compiler_params=_params(("arbitrary",)),
    )(rel_bias, bucket)


def _bias_bwd(dbias, bucket, window):
    def body(db_ref, bkt_ref, win_ref, o_ref):
        h = pl.program_id(0)

        @pl.when(h == 0)
        def _():
            o_ref[...] = jnp.zeros_like(o_ref)

        bkt = bkt_ref[...]
        x = jnp.where(win_ref[...] > 0.5, db_ref[0], 0.0)
        row = lax.broadcasted_iota(jnp.int32, (REL_BUCKETS, LANES), 0)
        col = lax.broadcasted_iota(jnp.int32, (REL_BUCKETS, LANES), 1)

        def step(b, acc):
            s = jnp.sum(jnp.where(bkt == b, x, 0.0), axis=0, keepdims=True)
            return acc + jnp.where(row == b, s, 0.0)

        per_lane = lax.fori_loop(0, REL_BUCKETS, step, jnp.zeros((REL_BUCKETS, LANES), F32))
        o_ref[...] += jnp.where(col == h, jnp.sum(per_lane, axis=1, keepdims=True), 0.0)

    return pl.pallas_call(
        body,
        name="bias_bwd",
        grid=(N_HEADS,),
        in_specs=[
            pl.BlockSpec((1,) + BAND, lambda h: (h, 0, 0)),
            pl.BlockSpec(BAND, lambda h: (0, 0)),
            pl.BlockSpec(BAND, lambda h: (0, 0)),
        ],
        out_specs=pl.BlockSpec((REL_BUCKETS, LANES), lambda h: (0, 0)),
        out_shape=jax.ShapeDtypeStruct((REL_BUCKETS, LANES), F32),
        compiler_params=_params(("arbitrary",)),
    )(dbias, bucket, window)


def _lane_lo(shape):
    return lax.broadcasted_iota(jnp.int32, shape, 1) < HEAD_DIM


def _row_lo(shape):
    return lax.broadcasted_iota(jnp.int32, shape, 0) < HEAD_DIM


def _dup_heads(x):
    r = pltpu.roll(x, HEAD_DIM, axis=1)
    lo = _lane_lo(x.shape)
    return jnp.where(lo, x, r), jnp.where(lo, r, x)


def _kv_operands(kvp_ref, kvc_ref):
    kvp, kvc = kvp_ref[...], kvc_ref[...]
    k2 = jnp.concatenate([kvp[:, :KV_COLS], kvc[:, :KV_COLS]], axis=0)
    v2 = jnp.concatenate([kvp[:, KV_COLS:], kvc[:, KV_COLS:]], axis=0)
    return _dup_heads(k2), _dup_heads(v2)


def _head_probs(k_r, qs_t, bias, mask, sink):
    s = jnp.dot(k_r, qs_t, preferred_element_type=F32) * ATTN_SCALE + bias
    s = jnp.where(mask, s, NEG_BIG)
    m = jnp.maximum(jnp.max(s, axis=0, keepdims=True), sink)
    e = jnp.exp(s - m)
    e_sink = jnp.exp(sink - m)
    inv = 1.0 / (jnp.sum(e, axis=0, keepdims=True) + e_sink)
    return e * inv, e_sink * inv


def _gate_cols(ga_refs, pair):
    off = LANES * (pair % 2)
    return ga_refs[pair // 2][:, off:off + LANES]


def _attn_specs(order):
    return [
        pl.BlockSpec((BLOCK, ATTN_WIDTH), lambda t: (order(t), Q_OFF // ATTN_WIDTH)),
        pl.BlockSpec((BLOCK, 2 * KV_COLS), lambda t: (order(t), KV_OFF // (2 * KV_COLS))),
        pl.BlockSpec((BLOCK, 2 * KV_COLS), lambda t: (jnp.maximum(order(t) - 1, 0), KV_OFF // (2 * KV_COLS))),
    ] + [
        pl.BlockSpec((BLOCK, 256), functools.partial(lambda t, c: (order(t), GA_OFF // 256 + c), c=c)) for c in range(4)
    ] + [
        pl.BlockSpec((N_KV_HEADS, BIAS_ROWS, BLOCK), lambda t: (0, 0, 0)),
        pl.BlockSpec((None,) + BAND, lambda t: (jnp.minimum(order(t), 1), 0, 0)),
        pl.BlockSpec(memory_space=pltpu.SMEM),
    ]


def _attn_fwd(h, bias, masks, sinks, name):
    S = h.shape[0]
    nb = S // BLOCK

    def body(q_ref, kvc_ref, kvp_ref, ga0, ga1, ga2, ga3, bias_ref, mask_ref, sink_ref, o_ref):
        kd, vd = _kv_operands(kvp_ref, kvc_ref)
        mask = mask_ref[...] > 0.5
        lo = _row_lo((LANES, BLOCK))
        for g in range(N_KV_HEADS):
            k_r = kd[g].astype(BF16)
            v_t = vd[g].T.astype(BF16)
            for pr in range(KV_GROUP // 2):
                pair = (KV_GROUP // 2) * g + pr
                qp_t = q_ref[:, LANES * pair:LANES * (pair + 1)].T
                outs = []
                for hh in range(2):
                    j = 2 * pr + hh
                    qs_t = jnp.where(lo if hh == 0 else ~lo, qp_t, 0.0).astype(BF16)
                    p, _ = _head_probs(k_r, qs_t, bias_ref[g, 2 * BLOCK * j:2 * BLOCK * (j + 1), :], mask,
                                       sink_ref[KV_GROUP * g + j])
                    outs.append(jnp.dot(v_t, p.astype(BF16), preferred_element_type=F32))
                ga = _gate_cols((ga0, ga1, ga2, ga3), pair)
                o_ref[:, LANES * pair:LANES * (pair + 1)] = (
                    jnp.where(lo, outs[0], outs[1]).T * (ga * _sigmoid(ga))).astype(BF16)

    return pl.pallas_call(
        body,
        name=name,
        grid=(nb,),
        in_specs=_attn_specs(lambda t: t),
        out_specs=pl.BlockSpec((BLOCK, ATTN_WIDTH), lambda t: (t, 0)),
        out_shape=jax.ShapeDtypeStruct((S, ATTN_WIDTH + POOL_WIDTH), BF16),
        compiler_params=_params(("arbitrary",)),
    )(h, h, h, h, h, h, h, bias, masks, sinks)


DH_ATTN_COLS = U_OFF


def _attn_bwd(h, dab, dh, bias, masks, sinks, dbias_in, name):
    S = h.shape[0]
    nb = S // BLOCK

    def order(t):
        return nb - 1 - t

    def body(q_ref, kvc_ref, kvp_ref, ga0, ga1, ga2, ga3, bias_ref, mask_ref, sink_ref, da_ref, dbin_ref, dh_in_ref,
             dh_ref, dbias_ref, dsink_ref, db_ref, carry_scr):
        del dh_in_ref
        t = pl.program_id(0)

        @pl.when(t == 0)
        def _():
            dbias_ref[...] = dbin_ref[...]
            dsink_ref[...] = jnp.zeros_like(dsink_ref)
            db_ref[...] = jnp.zeros_like(db_ref)
            carry_scr[...] = jnp.zeros_like(carry_scr)

        kd, vd = _kv_operands(kvp_ref, kvc_ref)
        mask = mask_ref[...] > 0.5
        lo = _lane_lo((BLOCK, LANES))
        lo_t = _row_lo((LANES, BLOCK))
        dk_tot, dv_tot = [], []
        for g in range(N_KV_HEADS):
            k_t, k_r = kd[g].T.astype(BF16), kd[g].astype(BF16)
            v_t, v_r = vd[g].T.astype(BF16), vd[g].astype(BF16)
            dk = jnp.zeros((2 * BLOCK, LANES), F32)
            dv = jnp.zeros((2 * BLOCK, LANES), F32)
            for pr in range(KV_GROUP // 2):
                pair = (KV_GROUP // 2) * g + pr
                cols = slice(LANES * pair, LANES * (pair + 1))
                qp = q_ref[:, cols]
                qp_t = qp.T
                ga = _gate_cols((ga0, ga1, ga2, ga3), pair)
                sg = _sigmoid(ga)
                da = da_ref[:, cols]
                do_p = da * (ga * sg)
                do_t = do_p.T
                outs, dqs = [], []
                for hh in range(2):
                    j = 2 * pr + hh
                    rows = slice(2 * BLOCK * j, 2 * BLOCK * (j + 1))
                    half, half_t = (lo, lo_t) if hh == 0 else (~lo, ~lo_t)
                    qs_t = jnp.where(half_t, qp_t, 0.0).astype(BF16)
                    p, p_sink = _head_probs(k_r, qs_t, bias_ref[g, rows, :], mask, sink_ref[KV_GROUP * g + j])
                    pb = p.astype(BF16)
                    outs.append(jnp.dot(v_t, pb, preferred_element_type=F32))
                    dos_t = jnp.where(half_t, do_t, 0.0).astype(BF16)
                    dp = jnp.dot(v_r, dos_t, preferred_element_type=F32)
                    dsum = jnp.sum(p * dp, axis=0, keepdims=True)
                    ds = p * (dp - dsum)
                    dbias_ref[g, rows, :] += ds
                    tot = jnp.sum(-(p_sink * dsum), axis=1, keepdims=True)
                    dsink_ref[g, j:j + 1, :] += jnp.broadcast_to(tot, (1, LANES))
                    dsb = ds.astype(BF16)
                    dqs.append(jnp.dot(k_t, dsb, preferred_element_type=F32))
                    dk = dk + jnp.dot(dsb, jnp.where(half, qp, 0.0).astype(BF16), preferred_element_type=F32)
                    dv = dv + jnp.dot(pb, jnp.where(half, do_p, 0.0).astype(BF16), preferred_element_type=F32)
                attn = jnp.where(lo_t, outs[0], outs[1]).T
                dq = jnp.where(lo_t, dqs[0], dqs[1]).T * ATTN_SCALE
                dga = da * attn * (sg * (1.0 + ga * (1.0 - sg)))
                ga_cols = slice(GA_OFF + LANES * pair, GA_OFF + LANES * (pair + 1))
                dh_ref[:, cols] = dq.astype(BF16)
                dh_ref[:, ga_cols] = dga.astype(BF16)
                db_ref[:, cols] += jnp.sum(dq, axis=0, keepdims=True)
                db_ref[:, ga_cols] += jnp.sum(dga, axis=0, keepdims=True)
            dk = dk * ATTN_SCALE
            dk_tot.append(dk + pltpu.roll(dk, HEAD_DIM, axis=1))
            dv_tot.append(dv + pltpu.roll(dv, HEAD_DIM, axis=1))
        lo2 = _lane_lo((2 * BLOCK, LANES))
        dkv = jnp.concatenate([jnp.where(lo2, dk_tot[0], dk_tot[1]), jnp.where(lo2, dv_tot[0], dv_tot[1])], axis=1)
        dkv_done = dkv[BLOCK:, :] + carry_scr[...]
        dh_ref[:, KV_OFF:KV_OFF + 2 * KV_COLS] = dkv_done.astype(BF16)
        db_ref[:, KV_OFF:KV_OFF + 2 * KV_COLS] += jnp.sum(dkv_done, axis=0, keepdims=True)
        carry_scr[...] = dkv[:BLOCK, :]

    n_in = 12
    return pl.pallas_call(
        body,
        name=name,
        grid=(nb,),
        in_specs=_attn_specs(order) + [
            pl.BlockSpec((BLOCK, ATTN_WIDTH), lambda t: (order(t), 0)),
            pl.BlockSpec((N_KV_HEADS, BIAS_ROWS, BLOCK), lambda t: (0, 0, 0)),
            pl.BlockSpec(memory_space=pl.ANY),
        ],
        out_specs=[
            pl.BlockSpec((BLOCK, DH_ATTN_COLS), lambda t: (order(t), 0)),
            pl.BlockSpec((N_KV_HEADS, BIAS_ROWS, BLOCK), lambda t: (0, 0, 0)),
            pl.BlockSpec((N_KV_HEADS, KV_GROUP, LANES), lambda t: (0, 0, 0)),
            pl.BlockSpec((1, DH_ATTN_COLS), lambda t: (0, 0)),
        ],
        out_shape=[
            jax.ShapeDtypeStruct((S, IN_COLS), BF16),
            jax.ShapeDtypeStruct((N_KV_HEADS, BIAS_ROWS, BLOCK), F32),
            jax.ShapeDtypeStruct((N_KV_HEADS, KV_GROUP, LANES), F32),
            jax.ShapeDtypeStruct((1, DH_ATTN_COLS), F32),
        ],
        scratch_shapes=[pltpu.VMEM((BLOCK, 2 * KV_COLS), F32)],
        input_output_aliases={n_in: 0},
        compiler_params=_params(("arbitrary",)),
    )(h, h, h, h, h, h, h, bias, masks, sinks, dab, dbias_in, dh)


def _window_sum(x, w, back):
    n = x.shape[0]
    s, sh = x, 1
    while sh < w:
        s = s + pltpu.roll(s, sh if back else n - sh, axis=0)
        sh *= 2
    return s


def _pool_counts(first_row, n, w):
    t = first_row + lax.broadcasted_iota(jnp.int32, (n, 1), 0)
    return jnp.minimum(t + 1, w).astype(F32)


def _pool_diff(u_ref, uh_ref, i, T, g):
    u = u_ref[...]
    halo = jnp.where(i > 0, uh_ref[...], 0.0)
    ext = jnp.concatenate([halo, u], axis=0)
    w = POOL_WINDOWS[g]
    s = _window_sum(ext, w, back=True)[POOL_HALO:, :]
    return s / _pool_counts(i * T, T, w) - u


def _pool_in_specs(T):
    hb = T // POOL_HALO
    specs = []
    for g in range(4):
        specs.append(pl.BlockSpec((T, 256), functools.partial(lambda i, g: (i, U_OFF // 256 + g), g=g)))
        specs.append(pl.BlockSpec((POOL_HALO, 256), functools.partial(
            lambda i, g: (jnp.maximum(i * hb - 1, 0), U_OFF // 256 + g), g=g)))
    return specs


def _pool_weight_specs():
    return [pl.BlockSpec((4, 256, 256), lambda i: (0, 0, 0)), pl.BlockSpec((1, POOL_WIDTH), lambda i: (0, 0))]


def _pool_fwd(h, ab, w_pool, pool_scale, name):
    S = h.shape[0]
    T = _tile(S, 512)

    def body(*refs):
        u_refs = refs[0:8]
        gb_refs = refs[8:12]
        wp_ref, sc_ref, o_ref = refs[12], refs[13], refs[15]
        i = pl.program_id(0)
        for g in range(4):
            diff = _pool_diff(u_refs[2 * g], u_refs[2 * g + 1], i, T, g).astype(BF16)
            mixed = jnp.dot(diff, wp_ref[g], preferred_element_type=F32) * sc_ref[:, 256 * g:256 * (g + 1)]
            gb = gb_refs[g][...]
            o_ref[:, 256 * g:256 * (g + 1)] = (mixed * (gb * _sigmoid(gb))).astype(BF16)

    in_specs = _pool_in_specs(T) + [
        pl.BlockSpec((T, 256), functools.partial(lambda i, g: (i, GB_OFF // 256 + g), g=g)) for g in range(4)
    ] + _pool_weight_specs() + [pl.BlockSpec(memory_space=pl.ANY)]
    return pl.pallas_call(
        body,
        name=name,
        grid=(S // T,),
        in_specs=in_specs,
        out_specs=pl.BlockSpec((T, POOL_WIDTH), lambda i: (i, 1)),
        out_shape=jax.ShapeDtypeStruct(ab.shape, BF16),
        input_output_aliases={14: 0},
        compiler_params=_params(("arbitrary",)),
    )(*([h] * 12), w_pool, pool_scale, ab)


DH_POOL_COLS = IN_COLS // 2


def _pool_bwd(h, dab, w_pool, pool_scale, after, name):
    S = h.shape[0]
    T = _tile(S, 512)
    nt = S // T
    hb = T // POOL_HALO
    E = T + POOL_HALO
    lead = U_OFF - DH_POOL_COLS

    def body(*refs):
        u_refs = refs[0:8]
        gb_refs = refs[8:16]
        db_refs = refs[16:24]
        wp_ref, sc_ref = refs[24], refs[25]
        dh_ref, dwp_ref, dsc_ref, dbi_ref = refs[27:31]
        i = pl.program_id(0)

        @pl.when(i == 0)
        def _():
            dwp_ref[...] = jnp.zeros_like(dwp_ref)
            dsc_ref[...] = jnp.zeros_like(dsc_ref)
            dbi_ref[...] = jnp.zeros_like(dbi_ref)

        dh_ref[:, 0:lead] = jnp.zeros((T, lead), BF16)
        for g in range(4):
            w = POOL_WINDOWS[g]
            cols = slice(256 * g, 256 * (g + 1))
            scale = sc_ref[:, cols]
            wp = wp_ref[g]
            diff = _pool_diff(u_refs[2 * g], u_refs[2 * g + 1], i, T, g).astype(BF16)
            mixed = jnp.dot(diff, wp, preferred_element_type=F32)
            keep = i < nt - 1
            gb = jnp.concatenate([gb_refs[2 * g][...], jnp.where(keep, gb_refs[2 * g + 1][...], 0.0)], axis=0)
            db = jnp.concatenate([db_refs[2 * g][...], jnp.where(keep, db_refs[2 * g + 1][...], 0.0)], axis=0)
            sg = _sigmoid(gb)
            dms = db * (gb * sg)
            dmixed = (dms * scale).astype(BF16)
            ddiff = lax.dot_general(dmixed, wp, (((1,), (1,)), ((), ())), preferred_element_type=F32)
            r = ddiff / _pool_counts(i * T, E, w)
            du = _window_sum(r, w, back=False)[:T, :] - ddiff[:T, :]
            dgb = db[:T, :] * (mixed * scale) * (sg[:T, :] * (1.0 + gb[:T, :] * (1.0 - sg[:T, :])))
            u_cols = slice(lead + 256 * g, lead + 256 * (g + 1))
            gb_cols = slice(lead + POOL_WIDTH + 256 * g, lead + POOL_WIDTH + 256 * (g + 1))
            dh_ref[:, u_cols] = du.astype(BF16)
            dh_ref[:, gb_cols] = dgb.astype(BF16)
            dbi_ref[:, u_cols] += jnp.sum(du, axis=0, keepdims=True)
            dbi_ref[:, gb_cols] += jnp.sum(dgb, axis=0, keepdims=True)
            dsc_ref[:, cols] += jnp.sum(dms[:T, :] * mixed, axis=0, keepdims=True)
            dwp_ref[g] += lax.dot_general(diff, dmixed[:T, :], (((0,), (0,)), ((), ())), preferred_element_type=F32)

    def rows_after(i):
        return jnp.minimum((i + 1) * hb, S // POOL_HALO - 1)

    in_specs = _pool_in_specs(T)
    for off in (GB_OFF // 256, ATTN_WIDTH // 256):
        for g in range(4):
            in_specs.append(pl.BlockSpec((T, 256), functools.partial(lambda i, c: (i, c), c=off + g)))
            in_specs.append(pl.BlockSpec((POOL_HALO, 256), functools.partial(lambda i, c: (rows_after(i), c), c=off + g)))
    in_specs += _pool_weight_specs() + [pl.BlockSpec(memory_space=pl.ANY)]
    return pl.pallas_call(
        body,
        name=name,
        grid=(nt,),
        in_specs=in_specs,
        out_specs=[
            pl.BlockSpec((T, DH_POOL_COLS), lambda i: (i, 1)),
            pl.BlockSpec((4, 256, 256), lambda i: (0, 0, 0)),
            pl.BlockSpec((1, POOL_WIDTH), lambda i: (0, 0)),
            pl.BlockSpec((1, DH_POOL_COLS), lambda i: (0, 0)),
        ],
        out_shape=[
            jax.ShapeDtypeStruct((S, IN_COLS), BF16),
            jax.ShapeDtypeStruct((4, 256, 256), F32),
            jax.ShapeDtypeStruct((1, POOL_WIDTH), F32),
            jax.ShapeDtypeStruct((1, DH_POOL_COLS), F32),
        ],
        compiler_params=_params(("arbitrary",)),
    )(*([h] * 16), *([dab] * 8), w_pool, pool_scale, after)


def _load_resident(pairs, sems):
    @pl.when(pl.program_id(0) == 0)
    def _():
        cps = [pltpu.make_async_copy(src, dst, sems.at[n]) for n, (src, dst) in enumerate(pairs)]
        for cp in cps:
            cp.start()
        for cp in cps:
            cp.wait()


def _mix_ln_fwd(ab, x, pb, w_out, w_gate, w_ple_t, gain, bias, alpha, after, name):
    S = x.shape[0]
    T = _tile(S, 256)

    def body(ab_ref, x_ref, p_ref, wo_hbm, wg_hbm, wp_hbm, g_ref, b_ref, after_ref,
             y_ref, yb_ref, xh_ref, rs_ref, gp_ref, pe_ref, wo, wg, wp, sems):
        del after_ref
        _load_resident(((wo_hbm, wo), (wg_hbm, wg), (wp_hbm, wp)), sems)
        x = x_ref[...]
        mix = jnp.dot(ab_ref[...], wo[...], preferred_element_type=F32)
        gp = jnp.dot(x.astype(BF16), wg[...], preferred_element_type=F32)
        pe = lax.dot_general(p_ref[...], wp[...], (((1,), (1,)), ((), ())), preferred_element_type=F32)
        z = alpha * x + mix + _sigmoid(gp) * pe
        mu = jnp.mean(z, axis=-1, keepdims=True)
        zc = z - mu
        var = jnp.mean(zc * zc, axis=-1, keepdims=True)
        rstd = lax.rsqrt(var + LN_EPS)
        xhat = zc * rstd
        y = xhat * g_ref[...] + b_ref[...]
        y_ref[...] = y
        yb_ref[...] = y.astype(BF16)
        xh_ref[...] = xhat
        rs_ref[...] = rstd
        gp_ref[...] = gp
        pe_ref[...] = pe

    row = pl.BlockSpec((T, D_MODEL), lambda i: (i, 0))
    vec = pl.BlockSpec((1, D_MODEL), lambda i: (0, 0))
    any_spec = pl.BlockSpec(memory_space=pl.ANY)
    f32_rows = jax.ShapeDtypeStruct((S, D_MODEL), F32)
    return pl.pallas_call(
        body,
        name=name,
        grid=(S // T,),
        in_specs=[row, row, pl.BlockSpec((T, PLE_DIM), lambda i: (i, 0)), any_spec, any_spec, any_spec, vec, vec, any_spec],
        out_specs=[row, row, row, pl.BlockSpec((T, 1), lambda i: (i, 0)), row, row],
        out_shape=[f32_rows, jax.ShapeDtypeStruct((S, D_MODEL), BF16), f32_rows, jax.ShapeDtypeStruct((S, 1), F32),
                   f32_rows, f32_rows],
        scratch_shapes=[pltpu.VMEM(w_out.shape, BF16), pltpu.VMEM(w_gate.shape, BF16), pltpu.VMEM(w_ple_t.shape, BF16),
                        pltpu.SemaphoreType.DMA((3,))],
        compiler_params=_params(("arbitrary",)),
    )(ab, x, pb, w_out, w_gate, w_ple_t, gain, bias, after)


def _ln_dmix_bwd(dy, xhat, rstd, gain, gp, pe, w_out, w_gate, alpha, after, name):
    S = dy.shape[0]
    T = _tile(S, 256)

    def body(dy_ref, xh_ref, rs_ref, g_ref, gp_ref, pe_ref, wo_hbm, wg_hbm, after_ref,
             dzb_ref, dpe_ref, dgp_ref, dab_ref, dx_ref, dgain_ref, dbias_ref, wo, wg, sems):
        del after_ref
        _load_resident(((wo_hbm, wo), (wg_hbm, wg)), sems)

        @pl.when(pl.program_id(0) == 0)
        def _():
            dgain_ref[...] = jnp.zeros_like(dgain_ref)
            dbias_ref[...] = jnp.zeros_like(dbias_ref)

        dy = dy_ref[...]
        xhat = xh_ref[...]
        dyg = dy * g_ref[...]
        c1 = jnp.mean(dyg, axis=-1, keepdims=True)
        c2 = jnp.mean(dyg * xhat, axis=-1, keepdims=True)
        dz = rs_ref[...] * (dyg - c1 - xhat * c2)
        dgain_ref[...] += jnp.sum(dy * xhat, axis=0, keepdims=True)
        dbias_ref[...] += jnp.sum(dy, axis=0, keepdims=True)
        sg = _sigmoid(gp_ref[...])
        dzb = dz.astype(BF16)
        dgp = (dz * pe_ref[...] * (sg * (1.0 - sg))).astype(BF16)
        nt = (((1,), (1,)), ((), ()))
        dzb_ref[...] = dzb
        dpe_ref[...] = (dz * sg).astype(BF16)
        dgp_ref[...] = dgp
        dab_ref[...] = lax.dot_general(dzb, wo[...], nt, preferred_element_type=F32)
        dx_ref[...] = lax.dot_general(dgp, wg[...], nt, preferred_element_type=F32) + alpha * dz

    row = pl.BlockSpec((T, D_MODEL), lambda i: (i, 0))
    vec = pl.BlockSpec((1, D_MODEL), lambda i: (0, 0))
    any_spec = pl.BlockSpec(memory_space=pl.ANY)
    bf16_rows = jax.ShapeDtypeStruct((S, D_MODEL), BF16)
    f32_rows = jax.ShapeDtypeStruct((S, D_MODEL), F32)
    return pl.pallas_call(
        body,
        name=name,
        grid=(S // T,),
        in_specs=[row, row, pl.BlockSpec((T, 1), lambda i: (i, 0)), vec, row, row, any_spec, any_spec, any_spec],
        out_specs=[row, row, row, row, row, vec, vec],
        out_shape=[bf16_rows, bf16_rows, bf16_rows, f32_rows, f32_rows,
                   jax.ShapeDtypeStruct((1, D_MODEL), F32), jax.ShapeDtypeStruct((1, D_MODEL), F32)],
        scratch_shapes=[pltpu.VMEM(w_out.shape, BF16), pltpu.VMEM(w_gate.shape, BF16), pltpu.SemaphoreType.DMA((2,))],
        compiler_params=_params(("arbitrary",)),
    )(dy, xhat, rstd, gain, gp, pe, w_out, w_gate, after)


def _loss_head(y, target):
    S = y.shape[0]
    T = _tile(S, 256)

    def body(y_ref, t_ref, dy_ref, l_ref):
        @pl.when(pl.program_id(0) == 0)
        def _():
            l_ref[...] = jnp.zeros_like(l_ref)

        err = y_ref[...] - t_ref[...]
        dy_ref[...] = err * (1.0 / D_MODEL)
        per_token = jnp.mean(err * err, axis=-1, keepdims=True)
        l_ref[...] += 0.5 * jnp.sum(per_token, axis=0, keepdims=True)

    row = pl.BlockSpec((T, D_MODEL), lambda i: (i, 0))
    return pl.pallas_call(
        body,
        name="loss_head",
        grid=(S // T,),
        in_specs=[row, row],
        out_specs=[row, pl.BlockSpec((8, LANES), lambda i: (0, 0))],
        out_shape=[jax.ShapeDtypeStruct((S, D_MODEL), F32), jax.ShapeDtypeStruct((8, LANES), F32)],
        compiler_params=_params(("arbitrary",)),
    )(y, target)


def _sum_slabs(r, name):
    _, R, C = r.shape
    T = _tile(R, 256)

    def body(r_ref, o_ref):
        acc = r_ref[0].astype(F32)
        for s in range(1, N_DEV):
            acc = acc + r_ref[s].astype(F32)
        o_ref[...] = acc

    return pl.pallas_call(
        body,
        name=name,
        grid=(R // T,),
        in_specs=[pl.BlockSpec((N_DEV, T, C), lambda i: (0, i, 0))],
        out_specs=pl.BlockSpec((T, C), lambda i: (i, 0)),
        out_shape=jax.ShapeDtypeStruct((R, C), F32),
        compiler_params=_params(("parallel",)),
    )(r)


def _adamw(w, g, m, v, name):
    R, C = w.shape
    T = _tile(R, 256)
    grid = (R // T,)
    blk = pl.BlockSpec((T, C), lambda i: (i, 0))

    def body(w_ref, g_ref, m_ref, v_ref, d_ref, nm_ref, nv_ref):
        g = g_ref[...]
        m = ADAM_B1 * m_ref[...] + (1.0 - ADAM_B1) * g
        v = ADAM_B2 * v_ref[...] + (1.0 - ADAM_B2) * jnp.square(g)
        m_hat = m / (1.0 - ADAM_B1 ** ADAM_STEP)
        v_hat = v / (1.0 - ADAM_B2 ** ADAM_STEP)
        d_ref[...] = -ADAM_LR * (m_hat / (jnp.sqrt(v_hat) + ADAM_EPS) + ADAM_WD * w_ref[...])
        nm_ref[...] = m
        nv_ref[...] = v

    shp = jax.ShapeDtypeStruct(w.shape, F32)
    return pl.pallas_call(
        body,
        name=name,
        grid=grid,
        in_specs=[blk] * 4,
        out_specs=[blk] * 3,
        out_shape=[shp] * 3,
        compiler_params=_params(("parallel",) * len(grid)),
    )(w, g, m, v)


def _mesh_pos():
    return lax.axis_index("x"), lax.axis_index("y"), lax.axis_index("c")


def _flip(pos, k):
    x, y, c = pos
    return (1 - x if k & 4 else x, 1 - y if k & 2 else y, 1 - c if k & 1 else c)


def _index(pos):
    return 4 * pos[0] + 2 * pos[1] + pos[2]


HBM_SPEC = pl.BlockSpec(memory_space=pltpu.HBM)
SEM_SPEC = pl.BlockSpec(memory_space=pltpu.SEMAPHORE)
ANY_SPEC = pl.BlockSpec(memory_space=pl.ANY)
SPLIT_EFFECT = pltpu.SideEffectType.DATAFLOW_SIDE_EFFECTING
GATHER_FLIPS = (1, 4, 2, 6)
CHIP_FLIPS = (4, 2, 6)
TOKEN = jax.ShapeDtypeStruct((8, LANES), F32)


def _hbm(a):
    return pltpu.with_memory_space_constraint(a, pltpu.HBM)


def _hbm_like(a):
    return pltpu.HBM(a.shape, a.dtype)


def _block_rows(ref, pos, r):
    return ref.at[:, pl.ds(_index(pos) * r, r), :]


def _gather_start(shards, after, name):
    n = len(shards)
    lands = [lax.empty((s.shape[0], N_DEV * s.shape[1], s.shape[2]), s.dtype) for s in shards]

    def body(*refs):
        ins, bufs = refs[:n], refs[n:2 * n]
        send_sems, recv_sems = refs[2 * n + 1], refs[2 * n + 2]
        token = refs[4 * n + 3]
        me = _mesh_pos()
        for a in range(n):
            for j, k in enumerate(GATHER_FLIPS):
                pltpu.make_async_remote_copy(
                    src_ref=ins[a], dst_ref=_block_rows(bufs[a], me, shards[a].shape[1]),
                    send_sem=send_sems.at[4 * a + j], recv_sem=recv_sems.at[4 * a + j],
                    device_id=_flip(me, k), device_id_type=MESH_ID).start()
        token[...] = jnp.zeros_like(token)

    outs = pl.pallas_call(
        body,
        name=name,
        in_specs=[HBM_SPEC] * (2 * n) + [ANY_SPEC],
        out_specs=[SEM_SPEC, SEM_SPEC] + [HBM_SPEC] * (2 * n) + [pl.BlockSpec(memory_space=pltpu.VMEM)],
        out_shape=[pltpu.SemaphoreType.DMA((4 * n,)), pltpu.SemaphoreType.DMA((4 * n,))]
        + [_hbm_like(s) for s in shards] + [_hbm_like(b) for b in lands] + [TOKEN],
        input_output_aliases={i: 2 + i for i in range(2 * n)},
        compiler_params=pltpu.CompilerParams(has_side_effects=SPLIT_EFFECT),
    )(*[_hbm(s) for s in shards], *[_hbm(b) for b in lands], after)
    return outs[0], outs[1], outs[2:2 + n], outs[2 + n:2 + 2 * n], outs[2 + 2 * n]


def _gather_wait(started, after, name):
    send_sems, recv_sems, shards, lands, _ = started
    n = len(shards)

    def body(*refs):
        ins, bufs = refs[:n], refs[n:2 * n]
        send_sems, recv_sems = refs[2 * n], refs[2 * n + 1]
        me = _mesh_pos()
        for a in range(n):
            for j, k in enumerate(GATHER_FLIPS):
                cp = pltpu.make_async_remote_copy(
                    src_ref=ins[a], dst_ref=_block_rows(bufs[a], _flip(me, k), shards[a].shape[1]),
                    send_sem=send_sems.at[4 * a + j], recv_sem=recv_sems.at[4 * a + j],
                    device_id=_flip(me, k), device_id_type=MESH_ID)
                cp.wait_send()
                cp.wait_recv()

    outs = pl.pallas_call(
        body,
        name=name,
        in_specs=[HBM_SPEC] * (2 * n) + [SEM_SPEC, SEM_SPEC, ANY_SPEC],
        out_specs=[HBM_SPEC] * (2 * n),
        out_shape=[_hbm_like(s) for s in shards] + [_hbm_like(b) for b in lands],
        input_output_aliases={i: i for i in range(2 * n)},
        compiler_params=pltpu.CompilerParams(has_side_effects=SPLIT_EFFECT),
    )(*shards, *lands, send_sems, recv_sems, after)
    return outs[:n], outs[n:]


def _gather_pass(shards, lands, name):
    n = len(shards)

    def body(*refs):
        ins, bufs = refs[:n], refs[n:2 * n]
        token = refs[3 * n]
        send_sems, recv_sems, local_sems = refs[3 * n + 1:]
        me = _mesh_pos()
        sibling = _flip(me, 1)

        def copy(a, j, block):
            rows = _block_rows(bufs[a], block, shards[a].shape[1])
            return pltpu.make_async_remote_copy(
                src_ref=rows, dst_ref=rows, send_sem=send_sems.at[3 * a + j], recv_sem=recv_sems.at[3 * a + j],
                device_id=sibling, device_id_type=MESH_ID)

        mine = [pltpu.make_async_copy(ins[a], _block_rows(bufs[a], me, shards[a].shape[1]), local_sems.at[a])
                for a in range(n)]
        sends = [copy(a, j, _flip(me, k)) for a in range(n) for j, k in enumerate(CHIP_FLIPS)]
        for cp in mine + sends:
            cp.start()
        for a in range(n):
            for j, k in enumerate(CHIP_FLIPS):
                copy(a, j, _flip(sibling, k)).wait_recv()
        for cp in sends:
            cp.wait_send()
        for cp in mine:
            cp.wait()
        token[...] = jnp.zeros_like(token)

    outs = pl.pallas_call(
        body,
        name=name,
        in_specs=[pl.BlockSpec(memory_space=pltpu.VMEM)] * n + [ANY_SPEC] * n,
        out_specs=[ANY_SPEC] * n + [pl.BlockSpec(memory_space=pltpu.VMEM)],
        out_shape=[jax.ShapeDtypeStruct(b.shape, b.dtype) for b in lands] + [TOKEN],
        scratch_shapes=[pltpu.SemaphoreType.DMA((3 * n,)), pltpu.SemaphoreType.DMA((3 * n,)), pltpu.SemaphoreType.DMA((n,))],
        input_output_aliases={n + i: i for i in range(n)},
        compiler_params=pltpu.CompilerParams(has_side_effects=True, vmem_limit_bytes=VMEM_LIMIT_BYTES),
    )(*shards, *lands)
    return outs[:n], outs[n]


def _gather_own(shards, lands, name):
    n = len(shards)

    def body(*refs):
        ins, bufs, local_sems = refs[:n], refs[n:2 * n], refs[3 * n]
        me = _mesh_pos()
        cps = [pltpu.make_async_copy(ins[a], _block_rows(bufs[a], me, shards[a].shape[1]), local_sems.at[a])
               for a in range(n)]
        for cp in cps:
            cp.start()
        for cp in cps:
            cp.wait()

    return pl.pallas_call(
        body,
        name=name,
        in_specs=[pl.BlockSpec(memory_space=pltpu.VMEM)] * n + [ANY_SPEC] * n,
        out_specs=[ANY_SPEC] * n,
        out_shape=[jax.ShapeDtypeStruct(b.shape, b.dtype) for b in lands],
        scratch_shapes=[pltpu.SemaphoreType.DMA((n,))],
        input_output_aliases={n + i: i for i in range(n)},
        compiler_params=pltpu.CompilerParams(has_side_effects=True, vmem_limit_bytes=VMEM_LIMIT_BYTES),
    )(*shards, *lands)


def _pass_copy(bufs, send_sems, recv_sems, a, j, block, sibling):
    rows = _block_rows(bufs[a], block, bufs[a].shape[1] // N_DEV)
    return pltpu.make_async_remote_copy(
        src_ref=rows, dst_ref=rows, send_sem=send_sems.at[3 * a + j], recv_sem=recv_sems.at[3 * a + j],
        device_id=sibling, device_id_type=MESH_ID)


def _pass_start(lands, after, name):
    n = len(lands)

    def body(*refs):
        bufs = refs[:n]
        send_sems, recv_sems = refs[n + 1], refs[n + 2]
        token = refs[2 * n + 3]
        me = _mesh_pos()
        for a in range(n):
            for j, k in enumerate(CHIP_FLIPS):
                _pass_copy(bufs, send_sems, recv_sems, a, j, _flip(me, k), _flip(me, 1)).start()
        token[...] = jnp.zeros_like(token)

    outs = pl.pallas_call(
        body,
        name=name,
        in_specs=[HBM_SPEC] * n + [ANY_SPEC],
        out_specs=[SEM_SPEC, SEM_SPEC] + [HBM_SPEC] * n + [pl.BlockSpec(memory_space=pltpu.VMEM)],
        out_shape=[pltpu.SemaphoreType.DMA((3 * n,)), pltpu.SemaphoreType.DMA((3 * n,))]
        + [_hbm_like(b) for b in lands] + [TOKEN],
        input_output_aliases={i: 2 + i for i in range(n)},
        compiler_params=pltpu.CompilerParams(has_side_effects=SPLIT_EFFECT),
    )(*[_hbm(b) for b in lands], after)
    return outs[0], outs[1], outs[2:2 + n], outs[2 + n]


def _pass_wait(started, after, name):
    send_sems, recv_sems, lands, _ = started
    n = len(lands)

    def body(*refs):
        bufs = refs[:n]
        send_sems, recv_sems = refs[n], refs[n + 1]
        me = _mesh_pos()
        sibling = _flip(me, 1)
        for a in range(n):
            for j, k in enumerate(CHIP_FLIPS):
                _pass_copy(bufs, send_sems, recv_sems, a, j, _flip(me, k), sibling).wait_send()
                _pass_copy(bufs, send_sems, recv_sems, a, j, _flip(sibling, k), sibling).wait_recv()

    return pl.pallas_call(
        body,
        name=name,
        in_specs=[HBM_SPEC] * n + [SEM_SPEC, SEM_SPEC, ANY_SPEC],
        out_specs=[HBM_SPEC] * n,
        out_shape=[_hbm_like(b) for b in lands],
        input_output_aliases={i: i for i in range(n)},
        compiler_params=pltpu.CompilerParams(has_side_effects=SPLIT_EFFECT),
    )(*lands, send_sems, recv_sems, after)


def _place_own(grads, lands, layer, name):
    n = len(grads)
    blocks = [(g.shape[0], g.shape[1] // N_DEV, g.shape[2]) for g in grads]

    def body(*refs):
        ins, bufs = refs[:n], refs[n:2 * n]
        stage, in_sems, out_sems = refs[3 * n:4 * n], refs[4 * n], refs[4 * n + 1]
        me = _mesh_pos()
        loads = [pltpu.make_async_copy(_block_rows(ins[a], me, blocks[a][1]), stage[a], in_sems.at[a]) for a in range(n)]
        stores = [pltpu.make_async_copy(stage[a], bufs[a].at[_index(me), layer], out_sems.at[a]) for a in range(n)]
        for cp in loads:
            cp.start()
        for a in range(n):
            loads[a].wait()
            stores[a].start()
        for cp in stores:
            cp.wait()

    return pl.pallas_call(
        body,
        name=name,
        in_specs=[ANY_SPEC] * (2 * n),
        out_specs=[ANY_SPEC] * n,
        out_shape=[jax.ShapeDtypeStruct(b.shape, b.dtype) for b in lands],
        scratch_shapes=[pltpu.VMEM(blk, g.dtype) for blk, g in zip(blocks, grads)]
        + [pltpu.SemaphoreType.DMA((n,)), pltpu.SemaphoreType.DMA((n,))],
        input_output_aliases={n + i: i for i in range(n)},
        compiler_params=pltpu.CompilerParams(has_side_effects=True, vmem_limit_bytes=VMEM_LIMIT_BYTES),
    )(*grads, *lands)


def _scatter_copy(ins, bufs, send_sems, recv_sems, a, k, r, layer, me, slab):
    peer = _flip(me, k)
    return pltpu.make_async_remote_copy(
        src_ref=_block_rows(ins[a], peer, r), dst_ref=bufs[a].at[_index(slab), layer],
        send_sem=send_sems.at[7 * a + k - 1], recv_sem=recv_sems.at[7 * a + k - 1],
        device_id=peer, device_id_type=MESH_ID)


def _scatter_start(grads, lands, layer, name):
    n = len(grads)

    def body(*refs):
        ins, bufs = refs[:n], refs[n:2 * n]
        send_sems, recv_sems = refs[2 * n], refs[2 * n + 1]
        token = refs[4 * n + 2]
        me = _mesh_pos()
        for a in range(n):
            for k in range(1, N_DEV):
                _scatter_copy(ins, bufs, send_sems, recv_sems, a, k, grads[a].shape[1] // N_DEV, layer, me, me).start()
        token[...] = jnp.zeros_like(token)

    outs = pl.pallas_call(
        body,
        name=name,
        in_specs=[HBM_SPEC] * (2 * n),
        out_specs=[SEM_SPEC, SEM_SPEC] + [HBM_SPEC] * (2 * n) + [pl.BlockSpec(memory_space=pltpu.VMEM)],
        out_shape=[pltpu.SemaphoreType.DMA((7 * n,)), pltpu.SemaphoreType.DMA((7 * n,))]
        + [_hbm_like(g) for g in grads] + [_hbm_like(b) for b in lands] + [TOKEN],
        input_output_aliases={i: 2 + i for i in range(2 * n)},
        compiler_params=pltpu.CompilerParams(has_side_effects=SPLIT_EFFECT),
    )(*[_hbm(g) for g in grads], *[_hbm(b) for b in lands])
    return outs[0], outs[1], outs[2:2 + n], outs[2 + n:2 + 2 * n], outs[2 + 2 * n]


def _scatter_wait(started, layer, after, name):
    send_sems, recv_sems, grads, lands, _ = started
    n = len(grads)

    def body(*refs):
        ins, bufs = refs[:n], refs[n:2 * n]
        send_sems, recv_sems = refs[2 * n], refs[2 * n + 1]
        me = _mesh_pos()
        for a in range(n):
            for k in range(1, N_DEV):
                cp = _scatter_copy(ins, bufs, send_sems, recv_sems, a, k, grads[a].shape[1] // N_DEV, layer, me, _flip(me, k))
                cp.wait_send()
                cp.wait_recv()

    outs = pl.pallas_call(
        body,
        name=name,
        in_specs=[HBM_SPEC] * (2 * n) + [SEM_SPEC, SEM_SPEC, ANY_SPEC],
        out_specs=[HBM_SPEC] * (2 * n),
        out_shape=[_hbm_like(g) for g in grads] + [_hbm_like(b) for b in lands],
        input_output_aliases={i: i for i in range(2 * n)},
        compiler_params=pltpu.CompilerParams(has_side_effects=SPLIT_EFFECT),
    )(*grads, *lands, send_sems, recv_sems, after)
    return outs[n:]


def _allreduce_small(vec):
    R, C = vec.shape

    def body(v_ref, o_ref, buf, send_sems, recv_sems):
        me = _mesh_pos()
        buf[_index(me)] = v_ref[...]
        sends = []
        for k in range(1, N_DEV):
            sends.append(pltpu.make_async_remote_copy(
                src_ref=buf.at[_index(me)], dst_ref=buf.at[_index(me)],
                send_sem=send_sems.at[k - 1], recv_sem=recv_sems.at[k - 1],
                device_id=_flip(me, k), device_id_type=MESH_ID))
        for cp in sends:
            cp.start()
        for cp in sends:
            cp.wait_recv()
        for cp in sends:
            cp.wait_send()
        acc = buf[0]
        for s in range(1, N_DEV):
            acc = acc + buf[s]
        o_ref[...] = acc

    return pl.pallas_call(
        body,
        name="allreduce_small",
        in_specs=[pl.BlockSpec(memory_space=pltpu.VMEM)],
        out_specs=pl.BlockSpec(memory_space=pltpu.VMEM),
        out_shape=jax.ShapeDtypeStruct((R, C), F32),
        scratch_shapes=[pltpu.VMEM((N_DEV, R, C), F32), pltpu.SemaphoreType.DMA((7,)), pltpu.SemaphoreType.DMA((7,))],
        compiler_params=pltpu.CompilerParams(has_side_effects=True, vmem_limit_bytes=VMEM_LIMIT_BYTES),
    )(vec)


def _pack_small(parts):
    flat = jnp.concatenate([p.reshape(-1) for p in parts])
    n = flat.shape[0]
    rows = -(-n // SMALL_COLS)
    rows = -(-rows // 8) * 8
    return jnp.pad(flat, (0, rows * SMALL_COLS - n)).reshape(rows, SMALL_COLS)


def _unpack_small(packed, like):
    flat = packed.reshape(-1)
    out, pos = [], 0
    for p in like:
        out.append(flat[pos:pos + p.size].reshape(p.shape))
        pos += p.size
    return out


def kernel(x, p, w_in, b_in, w_out, attn_sinks, rel_bias, w_pool, pool_scale, w_ple, w_gate_ple, ln_gain, ln_bias, loss_target, m_w_in, m_b_in, m_w_out, m_attn_sinks, m_rel_bias, m_w_pool, m_pool_scale, m_w_ple, m_w_gate_ple, m_ln_gain, m_ln_bias, v_w_in, v_b_in, v_w_out, v_attn_sinks, v_rel_bias, v_w_pool, v_pool_scale, v_w_ple, v_w_gate_ple, v_ln_gain, v_ln_bias):
    L = w_in.shape[0]
    S = x.shape[1]
    alpha = (2.0 * L) ** 0.25
    bucket_np, masks_np, window_np = _band_constants()
    bucket, masks, window = jnp.asarray(bucket_np), jnp.asarray(masks_np), jnp.asarray(window_np)

    @functools.lru_cache(maxsize=None)
    def shards_of(l):
        return (jnp.swapaxes(w_in[l], 0, 1).astype(BF16)[None], w_out[l].astype(BF16)[None],
                w_gate_ple[l].astype(BF16)[None], jnp.swapaxes(w_ple[l], 0, 1).astype(BF16)[None], w_pool[l].astype(BF16))

    def gathered(started, after, tag):
        shards, lands = _gather_wait(started, after, name=f"gather_wait_{tag}")
        return _gather_pass(shards, lands, name=f"gather_pass_{tag}")

    bias = _bias_build(rel_bias, bucket).reshape(N_KV_HEADS, BIAS_ROWS, BLOCK)

    xs = x[0]
    xb = xs.astype(BF16)
    first_groups = ((0, 4), (1, 2, 3))
    token, first_started = rel_bias, []
    for tag, idxs in zip("ab", first_groups):
        first_started.append(_gather_start([shards_of(0)[i] for i in idxs], token, name=f"gather_start_0{tag}"))
        token = first_started[-1][4]
    started = {1: _gather_start(shards_of(1), token, name="gather_start_1")} if L > 1 else {}
    saved = []
    for l in range(L):
        pb = p[l, 0].astype(BF16)
        sinks_l = attn_sinks[l]
        scale_l = pool_scale[l].reshape(1, POOL_WIDTH)
        bias_l = b_in[l].reshape(1, IN_COLS)
        if l == 0:
            (w_in_f, w_pool_g), _ = gathered(first_started[0], started[1][4] if L > 1 else xb, "0a")
            w_in_t = w_in_f[0]
            h = _matmul(xb, w_in_t, tb=True, tm=512, tn=2176, tk=2048, out_dtype=F32, bias=bias_l, name=f"in_proj_{l}")
            (w_out_f, w_gate_f, w_ple_f), _ = gathered(first_started[1], h, "0b")
            w_out_g, w_gate_g, w_ple_t = w_out_f[0], w_gate_f[0], w_ple_f[0]
        else:
            w_in_t, w_out_g, w_gate_g, w_ple_t, w_pool_g = weights
            h = _matmul(xb, w_in_t, tb=True, tm=512, tn=2176, tk=2048, out_dtype=F32, bias=bias_l, name=f"in_proj_{l}")
        weights = (w_in_t, w_out_g, w_gate_g, w_ple_t, w_pool_g)
        ab = _attn_fwd(h, bias, masks, sinks_l, name=f"attn_fwd_{l}")
        ab = _pool_fwd(h, ab, w_pool_g, scale_l, name=f"pool_fwd_{l}")
        pin, passing = ab, None
        if 1 <= l and l + 1 < L:
            shards, lands = _gather_wait(started[l + 1], ab, name=f"gather_wait_{l + 1}")
            passing = _pass_start(_gather_own(shards, lands, name=f"gather_own_{l + 1}"), ab, name=f"pass_start_{l + 1}")
            pin = passing[3]
        if l + 2 < L:
            started[l + 2] = _gather_start(shards_of(l + 2), pin, name=f"gather_start_{l + 2}")
            pin = started[l + 2][4]
        y, yb, xhat, rstd, gp, pe = _mix_ln_fwd(ab, xs, pb, w_out_g, w_gate_g, w_ple_t, ln_gain[l].reshape(1, D_MODEL),
                                                ln_bias[l].reshape(1, D_MODEL), alpha, pin, name=f"mix_ln_fwd_{l}")
        saved.append((xb, pb, h, gp, pe, ab, xhat, rstd, sinks_l, scale_l, weights))
        xs, xb = y, yb
        if l + 1 < L:
            if passing is None:
                full, _ = gathered(started[l + 1], yb, l + 1)
            else:
                full = _pass_wait(passing, yb, name=f"pass_wait_{l + 1}")
            weights = (full[0][0], full[1][0], full[2][0], full[3][0], full[4])

    dy, loss_tile = _loss_head(xs, loss_target[0])

    dbias = jnp.zeros((N_KV_HEADS, BIAS_ROWS, BLOCK), F32)
    sh0 = shards_of(0)
    lands_a = [lax.empty((N_DEV, L) + sh0[i].shape, BF16) for i in (1, 2, 3)]
    lands_b = [lax.empty((N_DEV, L) + sh0[i].shape, BF16) for i in (0, 4)]
    g_b_in, g_sinks, g_scale, g_gain, g_beta = [], [], [], [], []
    pend_a = pend_b = None

    def scatter(grads, lands, pending, l, tag):
        if pending:
            lands = _scatter_wait(pending[0], pending[1], grads[0], name=f"scatter_wait_{pending[1]}{tag}")
        lands = _place_own(grads, lands, l, name=f"place_own_{l}{tag}")
        return _scatter_start(grads, lands, l, name=f"scatter_start_{l}{tag}"), l

    for l in reversed(range(L)):
        xb, pb, h, gp, pe, ab, xhat, rstd, sinks_l, scale_l, weights = saved[l]
        w_in_t, w_out_g, w_gate_g, w_ple_t, w_pool_g = weights
        dzb, dpe, dgp, dab, dx, dgain, dbeta = _ln_dmix_bwd(
            dy, xhat, rstd, ln_gain[l].reshape(1, D_MODEL), gp, pe, w_out_g, w_gate_g, alpha,
            pend_b[0][4] if pend_b else rel_bias, name=f"ln_dmix_bwd_{l}")
        g_w_out = _matmul(ab, dzb, ta=True, tm=512, tn=1024, tk=4096, out_dtype=BF16, name=f"dw_out_{l}")
        g_w_gate = _matmul(xb, dgp, ta=True, tm=512, tn=1024, tk=4096, out_dtype=BF16, name=f"dw_gate_{l}")
        g_w_ple_t = _matmul(dpe, pb, ta=True, tm=1024, tn=256, tk=1024, out_dtype=BF16, name=f"dw_ple_{l}")
        pend_a = scatter([g_w_out[None], g_w_gate[None], g_w_ple_t[None]], lands_a, pend_a, l, "a")
        dh, dwp, dsc, db_pool = _pool_bwd(h, dab, w_pool_g, scale_l, pend_a[0][4], name=f"pool_bwd_{l}")
        dh, dbias, dsink, db_attn = _attn_bwd(h, dab, dh, bias, masks, sinks_l, dbias, name=f"attn_bwd_{l}")
        g_w_in_t = _matmul(dh, xb, ta=True, tm=256, tn=1024, tk=4096, out_dtype=BF16, name=f"dw_in_{l}")
        pend_b = scatter([g_w_in_t[None], dwp.astype(BF16)], lands_b, pend_b, l, "b")
        g_b_in.append(jnp.concatenate([db_attn[0], db_pool[0, U_OFF - DH_POOL_COLS:]]))
        dy = _matmul(dh, w_in_t, tm=512, tn=1024, tk=4352, out_dtype=F32, add=dx, after=pend_b[0][4], name=f"dx_in_{l}")
        g_sinks.append(dsink[:, :, 0].reshape(N_HEADS))
        g_scale.append(dsc.reshape(POOL_WIDTH))
        g_gain.append(dgain.reshape(D_MODEL))
        g_beta.append(dbeta.reshape(D_MODEL))
    grad_x = dy[None]
    for lst in (g_b_in, g_sinks, g_scale, g_gain, g_beta):
        lst.reverse()
    g_rel = _bias_bwd(dbias.reshape((N_HEADS,) + BAND), bucket, window)[:, :N_HEADS]

    def big(w, g, m, v, name):
        shape = w.shape
        two_d = (shape[0] * shape[1], shape[2]) if len(shape) == 3 else (shape[0] * shape[1] * shape[2], shape[3])
        d, nm, nv = _adamw(w.reshape(two_d), g.reshape(two_d), m.reshape(two_d), v.reshape(two_d), name=name)
        return d.reshape(shape), nm.reshape(shape), nv.reshape(shape)

    r_out, r_gate, r_ple = _scatter_wait(pend_a[0], pend_a[1], dy, name="scatter_wait_0a")
    small_like = [b_in, attn_sinks, rel_bias, pool_scale, ln_gain, ln_bias]
    small_g = _allreduce_small(_pack_small([
        jnp.stack(g_b_in).reshape(L, IN_COLS), jnp.stack(g_sinks), g_rel, jnp.stack(g_scale), jnp.stack(g_gain),
        jnp.stack(g_beta), loss_tile[0, :1]]))
    grad_w_out = _sum_slabs(r_out.reshape(N_DEV, L * 256, D_MODEL), name="sum_w_out").reshape(L, 256, D_MODEL)
    grad_w_gate = _sum_slabs(r_gate.reshape(N_DEV, L * 256, D_MODEL), name="sum_w_gate").reshape(L, 256, D_MODEL)
    gt_ple = _sum_slabs(r_ple.reshape(N_DEV, L * 256, PLE_DIM), name="sum_w_ple")
    grad_w_ple = jnp.swapaxes(gt_ple.reshape(L, 256, PLE_DIM), 1, 2)
    upd_out = big(w_out, grad_w_out, m_w_out, v_w_out, "adamw_w_out")
    upd_ple = big(w_ple, grad_w_ple, m_w_ple, v_w_ple, "adamw_w_ple")
    upd_gate = big(w_gate_ple, grad_w_gate, m_w_gate_ple, v_w_gate_ple, "adamw_w_gate")

    r_in, r_pool = _scatter_wait(pend_b[0], pend_b[1], upd_gate[0], name="scatter_wait_0b")
    gt_in = _sum_slabs(r_in.reshape(N_DEV, L * 544, D_MODEL), name="sum_w_in")
    grad_w_in = jnp.swapaxes(gt_in.reshape(L, 544, D_MODEL), 1, 2)
    grad_w_pool = _sum_slabs(r_pool.reshape(N_DEV, L * 4 * 32, 256), name="sum_w_pool").reshape(L, 4, 32, 256)
    upd_in = big(w_in, grad_w_in, m_w_in, v_w_in, "adamw_w_in")
    upd_pool = big(w_pool, grad_w_pool, m_w_pool, v_w_pool, "adamw_w_pool")

    zero1 = jnp.zeros((1,), F32)
    sw = _pack_small(small_like + [zero1])
    sm = _pack_small([m_b_in, m_attn_sinks, m_rel_bias, m_pool_scale, m_ln_gain, m_ln_bias, zero1])
    sv = _pack_small([v_b_in, v_attn_sinks, v_rel_bias, v_pool_scale, v_ln_gain, v_ln_bias, zero1])
    sd, snm, snv = _adamw(sw, small_g, sm, sv, name="adamw_small")
    like = small_like + [zero1]
    sg_parts = _unpack_small(small_g, like)
    sd_parts, snm_parts, snv_parts = _unpack_small(sd, like), _unpack_small(snm, like), _unpack_small(snv, like)
    loss = sg_parts[6][0]

    def assemble(big_parts, small_parts):
        w_in_, w_out_, w_pool_, w_ple_, w_gate_ = big_parts
        b_in_, sinks_, rel_, scale_, gain_, beta_ = small_parts[:6]
        return [w_in_, b_in_, w_out_, sinks_, rel_, w_pool_, scale_, w_ple_, w_gate_, gain_, beta_]

    grads = assemble([grad_w_in, grad_w_out, grad_w_pool, grad_w_ple, grad_w_gate], sg_parts)
    ups = [upd_in, upd_out, upd_pool, upd_ple, upd_gate]
    deltas = assemble([u[0] for u in ups], sd_parts)
    new_m = assemble([u[1] for u in ups], snm_parts)
    new_v = assemble([u[2] for u in ups], snv_parts)
    return (loss, grad_x, *grads, *deltas, *new_m, *new_v)
```

```python
import functools
import math

import numpy as np
import jax
import jax.numpy as jnp
from jax import lax
from jax.experimental import pallas as pl
from jax.experimental.pallas import tpu as pltpu

F32 = jnp.float32
BF16 = jnp.bfloat16

D_MODEL = 2048
PLE_DIM = 256
ATTN_WIDTH = 1024
POOL_WIDTH = 1024
HEAD_DIM = 64
N_HEADS = 16
N_KV_HEADS = 2
KV_GROUP = 8
WINDOW = 128
BLOCK = 128
POOL_WINDOWS = (2, 4, 8, 16)
POOL_GROUP_DIM = 256
POOL_HALO = 16
REL_BUCKETS = 32
REL_MAX_DIST = 128
LN_EPS = 1e-5
KV_COLS = N_KV_HEADS * HEAD_DIM
IN_COLS = 4352
Q_OFF, KV_OFF, GA_OFF, U_OFF, GB_OFF = 0, 1024, 1280, 2304, 3328
ATTN_SCALE = 1.0 / math.sqrt(HEAD_DIM)
NEG_BIG = -1e30
LANES = 128

ADAM_LR = 0.001
ADAM_B1 = 0.9
ADAM_B2 = 0.999
ADAM_EPS = 1e-08
ADAM_WD = 0.01
ADAM_STEP = 10

N_DEV = 8
MESH_ID = pl.DeviceIdType.MESH
VMEM_LIMIT_BYTES = 52 * 1024 * 1024
SMALL_COLS = 1024


def _params(sem=None):
    return pltpu.CompilerParams(dimension_semantics=sem, vmem_limit_bytes=VMEM_LIMIT_BYTES)


def _sigmoid(x):
    return 1.0 / (1.0 + jnp.exp(-x))


def _tile(n, pref, unit=16):
    if n <= pref:
        return n
    t = pref - pref % unit
    while n % t:
        t -= unit
    assert t > 0, (n, pref)
    return t


def _matmul(a, b, *, name, ta=False, tb=False, tm, tn, tk, out_dtype, bias=None, add=None, add_scale=1.0, after=None):
    M, K = (a.shape[1], a.shape[0]) if ta else a.shape
    N = b.shape[0] if tb else b.shape[1]
    assert (b.shape[1] if tb else b.shape[0]) == K
    tm, tn, tk = _tile(M, tm), _tile(N, tn), _tile(K, tk)
    nm, nn, nk = M // tm, N // tn, K // tk
    a_spec = pl.BlockSpec((tk, tm), lambda j, i, k: (k, i)) if ta else pl.BlockSpec((tm, tk), lambda j, i, k: (i, k))
    b_spec = pl.BlockSpec((tn, tk), lambda j, i, k: (j, k)) if tb else pl.BlockSpec((tk, tn), lambda j, i, k: (k, j))
    dims = (((0 if ta else 1,), (1 if tb else 0,)), ((), ()))
    operands, in_specs = [a, b], [a_spec, b_spec]
    if bias is not None:
        operands.append(bias)
        in_specs.append(pl.BlockSpec((1, tn), lambda j, i, k: (0, j)))
    if add is not None:
        operands.append(add)
        in_specs.append(pl.BlockSpec((tm, tn), lambda j, i, k: (i, j)))
    if after is not None:
        operands.append(after)
        in_specs.append(pl.BlockSpec(memory_space=pl.ANY))

    def body(*refs):
        a_ref, b_ref = refs[0], refs[1]
        pos = 2
        bias_ref = add_ref = None
        if bias is not None:
            bias_ref = refs[pos]
            pos += 1
        if add is not None:
            add_ref = refs[pos]
            pos += 1
        if after is not None:
            pos += 1
        o_ref = refs[pos]
        part = lax.dot_general(a_ref[...].astype(BF16), b_ref[...].astype(BF16), dims, preferred_element_type=F32)

        def finish(acc):
            if bias_ref is not None:
                acc = acc + bias_ref[...]
            if add_ref is not None:
                acc = acc + add_scale * add_ref[...].astype(F32)
            o_ref[...] = acc.astype(out_dtype)

        if nk == 1:
            finish(part)
        else:
            acc_ref = refs[pos + 1]
            k = pl.program_id(2)

            @pl.when(k == 0)
            def _():
                acc_ref[...] = part

            @pl.when(k > 0)
            def _():
                acc_ref[...] += part

            @pl.when(k == nk - 1)
            def _():
                finish(acc_ref[...])

    return pl.pallas_call(
        body,
        name=name,
        grid=(nn, nm, nk),
        in_specs=in_specs,
        out_specs=pl.BlockSpec((tm, tn), lambda j, i, k: (i, j)),
        out_shape=jax.ShapeDtypeStruct((M, N), out_dtype),
        scratch_shapes=[pltpu.VMEM((tm, tn), F32)] if nk > 1 else [],
        compiler_params=_params(("parallel", "parallel", "arbitrary")),
    )(*operands)


BAND = (2 * BLOCK, BLOCK)
BIAS_ROWS = KV_GROUP * 2 * BLOCK


def _band_constants():
    qq = np.arange(BLOCK)[None, :]
    kk = np.arange(2 * BLOCK)[:, None]
    dist = qq + BLOCK - kk
    in_window = (dist >= 0) & (dist < WINDOW)
    max_exact = REL_BUCKETS // 2
    d = np.maximum(dist, 0)
    d_f = np.maximum(d, 1).astype(np.float32)
    large = max_exact + (
        np.log(d_f / np.float32(max_exact)) / np.float32(math.log(REL_MAX_DIST / max_exact)) * np.float32(REL_BUCKETS - max_exact)
    ).astype(np.int32)
    large = np.minimum(large, REL_BUCKETS - 1)
    bucket = np.where(d < max_exact, d, large).astype(np.int32)
    bucket = np.where(in_window, bucket, 0).astype(np.int32)
    first = in_window & (kk >= BLOCK)
    masks = np.stack([first, in_window]).astype(np.float32)
    return bucket, masks, in_window.astype(np.float32)


def _bias_build(rel_bias, bucket):
    def body(rb_ref, bkt_ref, o_ref):
        h = pl.program_id(0)
        bkt = bkt_ref[...]

        def step(b, acc):
            return jnp.where(bkt == b, rb_ref[b, h], acc)

        o_ref[0] = lax.fori_loop(0, REL_BUCKETS, step, jnp.zeros(BAND, F32))

    return pl.pallas_call(
        body,
        name="bias_build",
        grid=(N_HEADS,),
        in_specs=[pl.BlockSpec(memory_space=pltpu.SMEM), pl.BlockSpec(BAND, lambda h: (0, 0))],
        out_specs=pl.BlockSpec((1,) + BAND, lambda h: (h, 0, 0)),
        out_shape=jax.ShapeDtypeStruct((N_HEADS,) + BAND, F32),
        compiler_params=_params(("arbitrary",)),
    )(rel_bias, bucket)


def _bias_bwd(dbias, bucket, window):
    def body(db_ref, bkt_ref, win_ref, o_ref):
        h = pl.program_id(0)

        @pl.when(h == 0)
        def _():
            o_ref[...] = jnp.zeros_like(o_ref)

        bkt = bkt_ref[...]
        x = jnp.where(win_ref[...] > 0.5, db_ref[0], 0.0)
        row = lax.broadcasted_iota(jnp.int32, (REL_BUCKETS, LANES), 0)
        col = lax.broadcasted_iota(jnp.int32, (REL_BUCKETS, LANES), 1)

        def step(b, acc):
            s = jnp.sum(jnp.where(bkt == b, x, 0.0), axis=0, keepdims=True)
            return acc + jnp.where(row == b, s, 0.0)

        per_lane = lax.fori_loop(0, REL_BUCKETS, step, jnp.zeros((REL_BUCKETS, LANES), F32))
        o_ref[...] += jnp.where(col == h, jnp.sum(per_lane, axis=1, keepdims=True), 0.0)

    return pl.pallas_call(
        body,
        name="bias_bwd",
        grid=(N_HEADS,),
        in_specs=[
            pl.BlockSpec((1,) + BAND, lambda h: (h, 0, 0)),
            pl.BlockSpec(BAND, lambda h: (0, 0)),
            pl.BlockSpec(BAND, lambda h: (0, 0)),
        ],
        out_specs=pl.BlockSpec((REL_BUCKETS, LANES), lambda h: (0, 0)),
        out_shape=jax.ShapeDtypeStruct((REL_BUCKETS, LANES), F32),
        compiler_params=_params(("arbitrary",)),
    )(dbias, bucket, window)


def _lane_lo(shape):
    return lax.broadcasted_iota(jnp.int32, shape, 1) < HEAD_DIM


def _row_lo(shape):
    return lax.broadcasted_iota(jnp.int32, shape, 0) < HEAD_DIM


def _dup_heads(x):
    r = pltpu.roll(x, HEAD_DIM, axis=1)
    lo = _lane_lo(x.shape)
    return jnp.where(lo, x, r), jnp.where(lo, r, x)


def _kv_operands(kvp_ref, kvc_ref):
    kvp, kvc = kvp_ref[...], kvc_ref[...]
    k2 = jnp.concatenate([kvp[:, :KV_COLS], kvc[:, :KV_COLS]], axis=0)
    v2 = jnp.concatenate([kvp[:, KV_COLS:], kvc[:, KV_COLS:]], axis=0)
    return _dup_heads(k2), _dup_heads(v2)


def _head_probs(k_r, qs_t, bias, mask, sink):
    s = jnp.dot(k_r, qs_t, preferred_element_type=F32) * ATTN_SCALE + bias
    s = jnp.where(mask, s, NEG_BIG)
    m = jnp.maximum(jnp.max(s, axis=0, keepdims=True), sink)
    e = jnp.exp(s - m)
    e_sink = jnp.exp(sink - m)
    inv = 1.0 / (jnp.sum(e, axis=0, keepdims=True) + e_sink)
    return e * inv, e_sink * inv


def _gate_cols(ga_refs, pair):
    off = LANES * (pair % 2)
    return ga_refs[pair // 2][:, off:off + LANES]


def _attn_specs(order):
    return [
        pl.BlockSpec((BLOCK, ATTN_WIDTH), lambda t: (order(t), Q_OFF // ATTN_WIDTH)),
        pl.BlockSpec((BLOCK, 2 * KV_COLS), lambda t: (order(t), KV_OFF // (2 * KV_COLS))),
        pl.BlockSpec((BLOCK, 2 * KV_COLS), lambda t: (jnp.maximum(order(t) - 1, 0), KV_OFF // (2 * KV_COLS))),
    ] + [
        pl.BlockSpec((BLOCK, 256), functools.partial(lambda t, c: (order(t), GA_OFF // 256 + c), c=c)) for c in range(4)
    ] + [
        pl.BlockSpec((N_KV_HEADS, BIAS_ROWS, BLOCK), lambda t: (0, 0, 0)),
        pl.BlockSpec((None,) + BAND, lambda t: (jnp.minimum(order(t), 1), 0, 0)),
        pl.BlockSpec(memory_space=pltpu.SMEM),
    ]


P_ROWS = N_HEADS * 2 * BLOCK


def _attn_fwd(h, bias, masks, sinks, name):
    S = h.shape[0]
    nb = S // BLOCK

    def body(q_ref, kvc_ref, kvp_ref, ga0, ga1, ga2, ga3, bias_ref, mask_ref, sink_ref, o_ref, p_ref, ps_ref):
        kd, vd = _kv_operands(kvp_ref, kvc_ref)
        mask = mask_ref[...] > 0.5
        lo = _row_lo((LANES, BLOCK))
        for g in range(N_KV_HEADS):
            k_r = kd[g].astype(BF16)
            v_t = vd[g].T.astype(BF16)
            for pr in range(KV_GROUP // 2):
                pair = (KV_GROUP // 2) * g + pr
                qp_t = q_ref[:, LANES * pair:LANES * (pair + 1)].T
                outs = []
                for hh in range(2):
                    j = 2 * pr + hh
                    head = KV_GROUP * g + j
                    qs_t = jnp.where(lo if hh == 0 else ~lo, qp_t, 0.0).astype(BF16)
                    p, p_sink = _head_probs(k_r, qs_t, bias_ref[g, 2 * BLOCK * j:2 * BLOCK * (j + 1), :], mask,
                                            sink_ref[head])
                    pb = p.astype(BF16)
                    p_ref[2 * BLOCK * head:2 * BLOCK * (head + 1), :] = pb
                    ps_ref[head:head + 1, :] = p_sink
                    outs.append(jnp.dot(v_t, pb, preferred_element_type=F32))
                ga = _gate_cols((ga0, ga1, ga2, ga3), pair)
                o_ref[:, LANES * pair:LANES * (pair + 1)] = (
                    jnp.where(lo, outs[0], outs[1]).T * (ga * _sigmoid(ga))).astype(BF16)

    return pl.pallas_call(
        body,
        name=name,
        grid=(nb,),
        in_specs=_attn_specs(lambda t: t),
        out_specs=[
            pl.BlockSpec((BLOCK, ATTN_WIDTH), lambda t: (t, 0)),
            pl.BlockSpec((None, P_ROWS, BLOCK), lambda t: (t, 0, 0)),
            pl.BlockSpec((None, N_HEADS, LANES), lambda t: (t, 0, 0)),
        ],
        out_shape=[
            jax.ShapeDtypeStruct((S, ATTN_WIDTH + POOL_WIDTH), BF16),
            jax.ShapeDtypeStruct((nb, P_ROWS, BLOCK), BF16),
            jax.ShapeDtypeStruct((nb, N_HEADS, LANES), F32),
        ],
        compiler_params=_params(("arbitrary",)),
    )(h, h, h, h, h, h, h, bias, masks, sinks)


DH_ATTN_COLS = U_OFF


def _attn_bwd(h, dab, dh, probs, p_sinks, dbias_in, name):
    S = h.shape[0]
    nb = S // BLOCK

    def order(t):
        return nb - 1 - t

    def body(q_ref, kvc_ref, kvp_ref, ga0, ga1, ga2, ga3, p_ref, ps_ref, da_ref, dbin_ref, dh_in_ref,
             dh_ref, dbias_ref, dsink_ref, db_ref, carry_scr):
        del dh_in_ref
        t = pl.program_id(0)

        @pl.when(t == 0)
        def _():
            dbias_ref[...] = dbin_ref[...]
            dsink_ref[...] = jnp.zeros_like(dsink_ref)
            db_ref[...] = jnp.zeros_like(db_ref)
            carry_scr[...] = jnp.zeros_like(carry_scr)

        kd, vd = _kv_operands(kvp_ref, kvc_ref)
        lo = _lane_lo((BLOCK, LANES))
        lo_t = _row_lo((LANES, BLOCK))
        dk_tot, dv_tot = [], []
        for g in range(N_KV_HEADS):
            k_t = kd[g].T.astype(BF16)
            v_t, v_r = vd[g].T.astype(BF16), vd[g].astype(BF16)
            dk = jnp.zeros((2 * BLOCK, LANES), F32)
            dv = jnp.zeros((2 * BLOCK, LANES), F32)
            for pr in range(KV_GROUP // 2):
                pair = (KV_GROUP // 2) * g + pr
                cols = slice(LANES * pair, LANES * (pair + 1))
                qp = q_ref[:, cols]
                ga = _gate_cols((ga0, ga1, ga2, ga3), pair)
                sg = _sigmoid(ga)
                da = da_ref[:, cols]
                do_p = da * (ga * sg)
                do_t = do_p.T
                outs, dqs = [], []
                for hh in range(2):
                    j = 2 * pr + hh
                    head = KV_GROUP * g + j
                    rows = slice(2 * BLOCK * j, 2 * BLOCK * (j + 1))
                    half, half_t = (lo, lo_t) if hh == 0 else (~lo, ~lo_t)
                    pb = p_ref[2 * BLOCK * head:2 * BLOCK * (head + 1), :]
                    p = pb.astype(F32)
                    p_sink = ps_ref[head:head + 1, :]
                    outs.append(jnp.dot(v_t, pb, preferred_element_type=F32))
                    dos_t = jnp.where(half_t, do_t, 0.0).astype(BF16)
                    dp = jnp.dot(v_r, dos_t, preferred_element_type=F32)
                    dsum = jnp.sum(p * dp, axis=0, keepdims=True)
                    ds = p * (dp - dsum)
                    dbias_ref[g, rows, :] += ds
                    tot = jnp.sum(-(p_sink * dsum), axis=1, keepdims=True)
                    dsink_ref[g, j:j + 1, :] += jnp.broadcast_to(tot, (1, LANES))
                    dsb = ds.astype(BF16)
                    dqs.append(jnp.dot(k_t, dsb, preferred_element_type=F32))
                    dk = dk + jnp.dot(dsb, jnp.where(half, qp, 0.0).astype(BF16), preferred_element_type=F32)
                    dv = dv + jnp.dot(pb, jnp.where(half, do_p, 0.0).astype(BF16), preferred_element_type=F32)
                attn = jnp.where(lo_t, outs[0], outs[1]).T
                dq = jnp.where(lo_t, dqs[0], dqs[1]).T * ATTN_SCALE
                dga = da * attn * (sg * (1.0 + ga * (1.0 - sg)))
                ga_cols = slice(GA_OFF + LANES * pair, GA_OFF + LANES * (pair + 1))
                dh_ref[:, cols] = dq.astype(BF16)
                dh_ref[:, ga_cols] = dga.astype(BF16)
                db_ref[:, cols] += jnp.sum(dq, axis=0, keepdims=True)
                db_ref[:, ga_cols] += jnp.sum(dga, axis=0, keepdims=True)
            dk = dk * ATTN_SCALE
            dk_tot.append(dk + pltpu.roll(dk, HEAD_DIM, axis=1))
            dv_tot.append(dv + pltpu.roll(dv, HEAD_DIM, axis=1))
        lo2 = _lane_lo((2 * BLOCK, LANES))
        dkv = jnp.concatenate([jnp.where(lo2, dk_tot[0], dk_tot[1]), jnp.where(lo2, dv_tot[0], dv_tot[1])], axis=1)
        dkv_done = dkv[BLOCK:, :] + carry_scr[...]
        dh_ref[:, KV_OFF:KV_OFF + 2 * KV_COLS] = dkv_done.astype(BF16)
        db_ref[:, KV_OFF:KV_OFF + 2 * KV_COLS] += jnp.sum(dkv_done, axis=0, keepdims=True)
        carry_scr[...] = dkv[:BLOCK, :]

    n_in = 11
    return pl.pallas_call(
        body,
        name=name,
        grid=(nb,),
        in_specs=_attn_specs(order)[:7] + [
            pl.BlockSpec((None, P_ROWS, BLOCK), lambda t: (order(t), 0, 0)),
            pl.BlockSpec((None, N_HEADS, LANES), lambda t: (order(t), 0, 0)),
            pl.BlockSpec((BLOCK, ATTN_WIDTH), lambda t: (order(t), 0)),
            pl.BlockSpec((N_KV_HEADS, BIAS_ROWS, BLOCK), lambda t: (0, 0, 0)),
            pl.BlockSpec(memory_space=pl.ANY),
        ],
        out_specs=[
            pl.BlockSpec((BLOCK, DH_ATTN_COLS), lambda t: (order(t), 0)),
            pl.BlockSpec((N_KV_HEADS, BIAS_ROWS, BLOCK), lambda t: (0, 0, 0)),
            pl.BlockSpec((N_KV_HEADS, KV_GROUP, LANES), lambda t: (0, 0, 0)),
            pl.BlockSpec((1, DH_ATTN_COLS), lambda t: (0, 0)),
        ],
        out_shape=[
            jax.ShapeDtypeStruct((S, IN_COLS), BF16),
            jax.ShapeDtypeStruct((N_KV_HEADS, BIAS_ROWS, BLOCK), F32),
            jax.ShapeDtypeStruct((N_KV_HEADS, KV_GROUP, LANES), F32),
            jax.ShapeDtypeStruct((1, DH_ATTN_COLS), F32),
        ],
        scratch_shapes=[pltpu.VMEM((BLOCK, 2 * KV_COLS), F32)],
        input_output_aliases={n_in: 0},
        compiler_params=_params(("arbitrary",)),
    )(h, h, h, h, h, h, h, probs, p_sinks, dab, dbias_in, dh)


def _window_sum(x, w, back):
    n = x.shape[0]
    s, sh = x, 1
    while sh < w:
        s = s + pltpu.roll(s, sh if back else n - sh, axis=0)
        sh *= 2
    return s


def _pool_counts(first_row, n, w):
    t = first_row + lax.broadcasted_iota(jnp.int32, (n, 1), 0)
    return jnp.minimum(t + 1, w).astype(F32)


def _pool_diff(u_ref, uh_ref, i, T, g):
    u = u_ref[...]
    halo = jnp.where(i > 0, uh_ref[...], 0.0)
    ext = jnp.concatenate([halo, u], axis=0)
    w = POOL_WINDOWS[g]
    s = _window_sum(ext, w, back=True)[POOL_HALO:, :]
    return s / _pool_counts(i * T, T, w) - u


def _pool_in_specs(T):
    hb = T // POOL_HALO
    specs = []
    for g in range(4):
        specs.append(pl.BlockSpec((T, 256), functools.partial(lambda i, g: (i, U_OFF // 256 + g), g=g)))
        specs.append(pl.BlockSpec((POOL_HALO, 256), functools.partial(
            lambda i, g: (jnp.maximum(i * hb - 1, 0), U_OFF // 256 + g), g=g)))
    return specs


def _pool_weight_specs():
    return [pl.BlockSpec((4, 256, 256), lambda i: (0, 0, 0)), pl.BlockSpec((1, POOL_WIDTH), lambda i: (0, 0))]


def _pool_fwd(h, ab, w_pool, pool_scale, name):
    S = h.shape[0]
    T = _tile(S, 512)

    def body(*refs):
        u_refs = refs[0:8]
        gb_refs = refs[8:12]
        wp_ref, sc_ref, o_ref = refs[12], refs[13], refs[15]
        i = pl.program_id(0)
        for g in range(4):
            diff = _pool_diff(u_refs[2 * g], u_refs[2 * g + 1], i, T, g).astype(BF16)
            mixed = jnp.dot(diff, wp_ref[g], preferred_element_type=F32) * sc_ref[:, 256 * g:256 * (g + 1)]
            gb = gb_refs[g][...]
            o_ref[:, 256 * g:256 * (g + 1)] = (mixed * (gb * _sigmoid(gb))).astype(BF16)

    in_specs = _pool_in_specs(T) + [
        pl.BlockSpec((T, 256), functools.partial(lambda i, g: (i, GB_OFF // 256 + g), g=g)) for g in range(4)
    ] + _pool_weight_specs() + [pl.BlockSpec(memory_space=pl.ANY)]
    return pl.pallas_call(
        body,
        name=name,
        grid=(S // T,),
        in_specs=in_specs,
        out_specs=pl.BlockSpec((T, POOL_WIDTH), lambda i: (i, 1)),
        out_shape=jax.ShapeDtypeStruct(ab.shape, BF16),
        input_output_aliases={14: 0},
        compiler_params=_params(("arbitrary",)),
    )(*([h] * 12), w_pool, pool_scale, ab)


DH_POOL_COLS = IN_COLS // 2


def _pool_bwd(h, dab, w_pool, pool_scale, after, name):
    S = h.shape[0]
    T = _tile(S, 512)
    nt = S // T
    hb = T // POOL_HALO
    E = T + POOL_HALO
    lead = U_OFF - DH_POOL_COLS

    def body(*refs):
        u_refs = refs[0:8]
        gb_refs = refs[8:16]
        db_refs = refs[16:24]
        wp_ref, sc_ref = refs[24], refs[25]
        dh_ref, dwp_ref, dsc_ref, dbi_ref = refs[27:31]
        i = pl.program_id(0)

        @pl.when(i == 0)
        def _():
            dwp_ref[...] = jnp.zeros_like(dwp_ref)
            dsc_ref[...] = jnp.zeros_like(dsc_ref)
            dbi_ref[...] = jnp.zeros_like(dbi_ref)

        dh_ref[:, 0:lead] = jnp.zeros((T, lead), BF16)
        for g in range(4):
            w = POOL_WINDOWS[g]
            cols = slice(256 * g, 256 * (g + 1))
            scale = sc_ref[:, cols]
            wp = wp_ref[g]
            diff = _pool_diff(u_refs[2 * g], u_refs[2 * g + 1], i, T, g).astype(BF16)
            mixed = jnp.dot(diff, wp, preferred_element_type=F32)
            keep = i < nt - 1
            gb = jnp.concatenate([gb_refs[2 * g][...], jnp.where(keep, gb_refs[2 * g + 1][...], 0.0)], axis=0)
            db = jnp.concatenate([db_refs[2 * g][...], jnp.where(keep, db_refs[2 * g + 1][...], 0.0)], axis=0)
            sg = _sigmoid(gb)
            dms = db * (gb * sg)
            dmixed = (dms * scale).astype(BF16)
            ddiff = lax.dot_general(dmixed, wp, (((1,), (1,)), ((), ())), preferred_element_type=F32)
            r = ddiff / _pool_counts(i * T, E, w)
            du = _window_sum(r, w, back=False)[:T, :] - ddiff[:T, :]
            dgb = db[:T, :] * (mixed * scale) * (sg[:T, :] * (1.0 + gb[:T, :] * (1.0 - sg[:T, :])))
            u_cols = slice(lead + 256 * g, lead + 256 * (g + 1))
            gb_cols = slice(lead + POOL_WIDTH + 256 * g, lead + POOL_WIDTH + 256 * (g + 1))
            dh_ref[:, u_cols] = du.astype(BF16)
            dh_ref[:, gb_cols] = dgb.astype(BF16)
            dbi_ref[:, u_cols] += jnp.sum(du, axis=0, keepdims=True)
            dbi_ref[:, gb_cols] += jnp.sum(dgb, axis=0, keepdims=True)
            dsc_ref[:, cols] += jnp.sum(dms[:T, :] * mixed, axis=0, keepdims=True)
            dwp_ref[g] += lax.dot_general(diff, dmixed[:T, :], (((0,), (0,)), ((), ())), preferred_element_type=F32)

    def rows_after(i):
        return jnp.minimum((i + 1) * hb, S // POOL_HALO - 1)

    in_specs = _pool_in_specs(T)
    for off in (GB_OFF // 256, ATTN_WIDTH // 256):
        for g in range(4):
            in_specs.append(pl.BlockSpec((T, 256), functools.partial(lambda i, c: (i, c), c=off + g)))
            in_specs.append(pl.BlockSpec((POOL_HALO, 256), functools.partial(lambda i, c: (rows_after(i), c), c=off + g)))
    in_specs += _pool_weight_specs() + [pl.BlockSpec(memory_space=pl.ANY)]
    return pl.pallas_call(
        body,
        name=name,
        grid=(nt,),
        in_specs=in_specs,
        out_specs=[
            pl.BlockSpec((T, DH_POOL_COLS), lambda i: (i, 1)),
            pl.BlockSpec((4, 256, 256), lambda i: (0, 0, 0)),
            pl.BlockSpec((1, POOL_WIDTH), lambda i: (0, 0)),
            pl.BlockSpec((1, DH_POOL_COLS), lambda i: (0, 0)),
        ],
        out_shape=[
            jax.ShapeDtypeStruct((S, IN_COLS), BF16),
            jax.ShapeDtypeStruct((4, 256, 256), F32),
            jax.ShapeDtypeStruct((1, POOL_WIDTH), F32),
            jax.ShapeDtypeStruct((1, DH_POOL_COLS), F32),
        ],
        compiler_params=_params(("arbitrary",)),
    )(*([h] * 16), *([dab] * 8), w_pool, pool_scale, after)


def _load_resident(pairs, sems):
    @pl.when(pl.program_id(0) == 0)
    def _():
        cps = [pltpu.make_async_copy(src, dst, sems.at[n]) for n, (src, dst) in enumerate(pairs)]
        for cp in cps:
            cp.start()
        for cp in cps:
            cp.wait()


def _mix_ln_fwd(ab, x, pb, w_out, w_gate, w_ple_t, gain, bias, alpha, after, name):
    S = x.shape[0]
    T = _tile(S, 256)

    def body(ab_ref, x_ref, p_ref, wo_hbm, wg_hbm, wp_hbm, g_ref, b_ref, after_ref,
             y_ref, yb_ref, xh_ref, rs_ref, gp_ref, pe_ref, wo, wg, wp, sems):
        del after_ref
        _load_resident(((wo_hbm, wo), (wg_hbm, wg), (wp_hbm, wp)), sems)
        x = x_ref[...]
        mix = jnp.dot(ab_ref[...], wo[...], preferred_element_type=F32)
        gp = jnp.dot(x.astype(BF16), wg[...], preferred_element_type=F32)
        pe = lax.dot_general(p_ref[...], wp[...], (((1,), (1,)), ((), ())), preferred_element_type=F32)
        z = alpha * x + mix + _sigmoid(gp) * pe
        mu = jnp.mean(z, axis=-1, keepdims=True)
        zc = z - mu
        var = jnp.mean(zc * zc, axis=-1, keepdims=True)
        rstd = lax.rsqrt(var + LN_EPS)
        xhat = zc * rstd
        y = xhat * g_ref[...] + b_ref[...]
        y_ref[...] = y
        yb_ref[...] = y.astype(BF16)
        xh_ref[...] = xhat
        rs_ref[...] = rstd
        gp_ref[...] = gp
        pe_ref[...] = pe

    row = pl.BlockSpec((T, D_MODEL), lambda i: (i, 0))
    vec = pl.BlockSpec((1, D_MODEL), lambda i: (0, 0))
    any_spec = pl.BlockSpec(memory_space=pl.ANY)
    f32_rows = jax.ShapeDtypeStruct((S, D_MODEL), F32)
    return pl.pallas_call(
        body,
        name=name,
        grid=(S // T,),
        in_specs=[row, row, pl.BlockSpec((T, PLE_DIM), lambda i: (i, 0)), any_spec, any_spec, any_spec, vec, vec, any_spec],
        out_specs=[row, row, row, pl.BlockSpec((T, 1), lambda i: (i, 0)), row, row],
        out_shape=[f32_rows, jax.ShapeDtypeStruct((S, D_MODEL), BF16), f32_rows, jax.ShapeDtypeStruct((S, 1), F32),
                   f32_rows, f32_rows],
        scratch_shapes=[pltpu.VMEM(w_out.shape, BF16), pltpu.VMEM(w_gate.shape, BF16), pltpu.VMEM(w_ple_t.shape, BF16),
                        pltpu.SemaphoreType.DMA((3,))],
        compiler_params=_params(("arbitrary",)),
    )(ab, x, pb, w_out, w_gate, w_ple_t, gain, bias, after)


def _ln_dmix_bwd(dy, xhat, rstd, gain, gp, pe, w_out, w_gate, alpha, after, name):
    S = dy.shape[0]
    T = _tile(S, 256)

    def body(dy_ref, xh_ref, rs_ref, g_ref, gp_ref, pe_ref, wo_hbm, wg_hbm, after_ref,
             dzb_ref, dpe_ref, dgp_ref, dab_ref, dx_ref, dgain_ref, dbias_ref, wo, wg, sems):
        del after_ref
        _load_resident(((wo_hbm, wo), (wg_hbm, wg)), sems)

        @pl.when(pl.program_id(0) == 0)
        def _():
            dgain_ref[...] = jnp.zeros_like(dgain_ref)
            dbias_ref[...] = jnp.zeros_like(dbias_ref)

        dy = dy_ref[...]
        xhat = xh_ref[...]
        dyg = dy * g_ref[...]
        c1 = jnp.mean(dyg, axis=-1, keepdims=True)
        c2 = jnp.mean(dyg * xhat, axis=-1, keepdims=True)
        dz = rs_ref[...] * (dyg - c1 - xhat * c2)
        dgain_ref[...] += jnp.sum(dy * xhat, axis=0, keepdims=True)
        dbias_ref[...] += jnp.sum(dy, axis=0, keepdims=True)
        sg = _sigmoid(gp_ref[...])
        dzb = dz.astype(BF16)
        dgp = (dz * pe_ref[...] * (sg * (1.0 - sg))).astype(BF16)
        nt = (((1,), (1,)), ((), ()))
        dzb_ref[...] = dzb
        dpe_ref[...] = (dz * sg).astype(BF16)
        dgp_ref[...] = dgp
        dab_ref[...] = lax.dot_general(dzb, wo[...], nt, preferred_element_type=F32)
        dx_ref[...] = lax.dot_general(dgp, wg[...], nt, preferred_element_type=F32) + alpha * dz

    row = pl.BlockSpec((T, D_MODEL), lambda i: (i, 0))
    vec = pl.BlockSpec((1, D_MODEL), lambda i: (0, 0))
    any_spec = pl.BlockSpec(memory_space=pl.ANY)
    bf16_rows = jax.ShapeDtypeStruct((S, D_MODEL), BF16)
    f32_rows = jax.ShapeDtypeStruct((S, D_MODEL), F32)
    return pl.pallas_call(
        body,
        name=name,
        grid=(S // T,),
        in_specs=[row, row, pl.BlockSpec((T, 1), lambda i: (i, 0)), vec, row, row, any_spec, any_spec, any_spec],
        out_specs=[row, row, row, row, row, vec, vec],
        out_shape=[bf16_rows, bf16_rows, bf16_rows, f32_rows, f32_rows,
                   jax.ShapeDtypeStruct((1, D_MODEL), F32), jax.ShapeDtypeStruct((1, D_MODEL), F32)],
        scratch_shapes=[pltpu.VMEM(w_out.shape, BF16), pltpu.VMEM(w_gate.shape, BF16), pltpu.SemaphoreType.DMA((2,))],
        compiler_params=_params(("arbitrary",)),
    )(dy, xhat, rstd, gain, gp, pe, w_out, w_gate, after)


def _loss_head(y, target):
    S = y.shape[0]
    T = _tile(S, 256)

    def body(y_ref, t_ref, dy_ref, l_ref):
        @pl.when(pl.program_id(0) == 0)
        def _():
            l_ref[...] = jnp.zeros_like(l_ref)

        err = y_ref[...] - t_ref[...]
        dy_ref[...] = err * (1.0 / D_MODEL)
        per_token = jnp.mean(err * err, axis=-1, keepdims=True)
        l_ref[...] += 0.5 * jnp.sum(per_token, axis=0, keepdims=True)

    row = pl.BlockSpec((T, D_MODEL), lambda i: (i, 0))
    return pl.pallas_call(
        body,
        name="loss_head",
        grid=(S // T,),
        in_specs=[row, row],
        out_specs=[row, pl.BlockSpec((8, LANES), lambda i: (0, 0))],
        out_shape=[jax.ShapeDtypeStruct((S, D_MODEL), F32), jax.ShapeDtypeStruct((8, LANES), F32)],
        compiler_params=_params(("arbitrary",)),
    )(y, target)


def _sum_slabs(r, name):
    _, R, C = r.shape
    T = _tile(R, 256)

    def body(r_ref, o_ref):
        acc = r_ref[0].astype(F32)
        for s in range(1, N_DEV):
            acc = acc + r_ref[s].astype(F32)
        o_ref[...] = acc

    return pl.pallas_call(
        body,
        name=name,
        grid=(R // T,),
        in_specs=[pl.BlockSpec((N_DEV, T, C), lambda i: (0, i, 0))],
        out_specs=pl.BlockSpec((T, C), lambda i: (i, 0)),
        out_shape=jax.ShapeDtypeStruct((R, C), F32),
        compiler_params=_params(("parallel",)),
    )(r)


def _adamw(w, g, m, v, name):
    R, C = w.shape
    T = _tile(R, 256)
    grid = (R // T,)
    blk = pl.BlockSpec((T, C), lambda i: (i, 0))

    def body(w_ref, g_ref, m_ref, v_ref, d_ref, nm_ref, nv_ref):
        g = g_ref[...]
        m = ADAM_B1 * m_ref[...] + (1.0 - ADAM_B1) * g
        v = ADAM_B2 * v_ref[...] + (1.0 - ADAM_B2) * jnp.square(g)
        m_hat = m / (1.0 - ADAM_B1 ** ADAM_STEP)
        v_hat = v / (1.0 - ADAM_B2 ** ADAM_STEP)
        d_ref[...] = -ADAM_LR * (m_hat / (jnp.sqrt(v_hat) + ADAM_EPS) + ADAM_WD * w_ref[...])
        nm_ref[...] = m
        nv_ref[...] = v

    shp = jax.ShapeDtypeStruct(w.shape, F32)
    return pl.pallas_call(
        body,
        name=name,
        grid=grid,
        in_specs=[blk] * 4,
        out_specs=[blk] * 3,
        out_shape=[shp] * 3,
        compiler_params=_params(("parallel",) * len(grid)),
    )(w, g, m, v)


def _mesh_pos():
    return lax.axis_index("x"), lax.axis_index("y"), lax.axis_index("c")


def _flip(pos, k):
    x, y, c = pos
    return (1 - x if k & 4 else x, 1 - y if k & 2 else y, 1 - c if k & 1 else c)


def _index(pos):
    return 4 * pos[0] + 2 * pos[1] + pos[2]


HBM_SPEC = pl.BlockSpec(memory_space=pltpu.HBM)
SEM_SPEC = pl.BlockSpec(memory_space=pltpu.SEMAPHORE)
ANY_SPEC = pl.BlockSpec(memory_space=pl.ANY)
SPLIT_EFFECT = pltpu.SideEffectType.DATAFLOW_SIDE_EFFECTING
GATHER_FLIPS = (1, 4, 2, 6)
CHIP_FLIPS = (4, 2, 6)
TOKEN = jax.ShapeDtypeStruct((8, LANES), F32)


def _hbm(a):
    return pltpu.with_memory_space_constraint(a, pltpu.HBM)


def _hbm_like(a):
    return pltpu.HBM(a.shape, a.dtype)


def _block_rows(ref, pos, r):
    return ref.at[:, pl.ds(_index(pos) * r, r), :]


def _gather_start(shards, after, name):
    n = len(shards)
    lands = [lax.empty((s.shape[0], N_DEV * s.shape[1], s.shape[2]), s.dtype) for s in shards]

    def body(*refs):
        ins, bufs = refs[:n], refs[n:2 * n]
        send_sems, recv_sems = refs[2 * n + 1], refs[2 * n + 2]
        token = refs[4 * n + 3]
        me = _mesh_pos()
        for a in range(n):
            for j, k in enumerate(GATHER_FLIPS):
                pltpu.make_async_remote_copy(
                    src_ref=ins[a], dst_ref=_block_rows(bufs[a], me, shards[a].shape[1]),
                    send_sem=send_sems.at[4 * a + j], recv_sem=recv_sems.at[4 * a + j],
                    device_id=_flip(me, k), device_id_type=MESH_ID).start()
        token[...] = jnp.zeros_like(token)

    outs = pl.pallas_call(
        body,
        name=name,
        in_specs=[HBM_SPEC] * (2 * n) + [ANY_SPEC],
        out_specs=[SEM_SPEC, SEM_SPEC] + [HBM_SPEC] * (2 * n) + [pl.BlockSpec(memory_space=pltpu.VMEM)],
        out_shape=[pltpu.SemaphoreType.DMA((4 * n,)), pltpu.SemaphoreType.DMA((4 * n,))]
        + [_hbm_like(s) for s in shards] + [_hbm_like(b) for b in lands] + [TOKEN],
        input_output_aliases={i: 2 + i for i in range(2 * n)},
        compiler_params=pltpu.CompilerParams(has_side_effects=SPLIT_EFFECT),
    )(*[_hbm(s) for s in shards], *[_hbm(b) for b in lands], after)
    return outs[0], outs[1], outs[2:2 + n], outs[2 + n:2 + 2 * n], outs[2 + 2 * n]


def _gather_wait(started, after, name):
    send_sems, recv_sems, shards, lands, _ = started
    n = len(shards)

    def body(*refs):
        ins, bufs = refs[:n], refs[n:2 * n]
        send_sems, recv_sems = refs[2 * n], refs[2 * n + 1]
        me = _mesh_pos()
        for a in range(n):
            for j, k in enumerate(GATHER_FLIPS):
                cp = pltpu.make_async_remote_copy(
                    src_ref=ins[a], dst_ref=_block_rows(bufs[a], _flip(me, k), shards[a].shape[1]),
                    send_sem=send_sems.at[4 * a + j], recv_sem=recv_sems.at[4 * a + j],
                    device_id=_flip(me, k), device_id_type=MESH_ID)
                cp.wait_send()
                cp.wait_recv()

    outs = pl.pallas_call(
        body,
        name=name,
        in_specs=[HBM_SPEC] * (2 * n) + [SEM_SPEC, SEM_SPEC, ANY_SPEC],
        out_specs=[HBM_SPEC] * (2 * n),
        out_shape=[_hbm_like(s) for s in shards] + [_hbm_like(b) for b in lands],
        input_output_aliases={i: i for i in range(2 * n)},
        compiler_params=pltpu.CompilerParams(has_side_effects=SPLIT_EFFECT),
    )(*shards, *lands, send_sems, recv_sems, after)
    return outs[:n], outs[n:]


def _gather_pass(shards, lands, name):
    n = len(shards)

    def body(*refs):
        ins, bufs = refs[:n], refs[n:2 * n]
        token = refs[3 * n]
        send_sems, recv_sems, local_sems = refs[3 * n + 1:]
        me = _mesh_pos()
        sibling = _flip(me, 1)

        def copy(a, j, block):
            rows = _block_rows(bufs[a], block, shards[a].shape[1])
            return pltpu.make_async_remote_copy(
                src_ref=rows, dst_ref=rows, send_sem=send_sems.at[3 * a + j], recv_sem=recv_sems.at[3 * a + j],
                device_id=sibling, device_id_type=MESH_ID)

        mine = [pltpu.make_async_copy(ins[a], _block_rows(bufs[a], me, shards[a].shape[1]), local_sems.at[a])
                for a in range(n)]
        sends = [copy(a, j, _flip(me, k)) for a in range(n) for j, k in enumerate(CHIP_FLIPS)]
        for cp in mine + sends:
            cp.start()
        for a in range(n):
            for j, k in enumerate(CHIP_FLIPS):
                copy(a, j, _flip(sibling, k)).wait_recv()
        for cp in sends:
            cp.wait_send()
        for cp in mine:
            cp.wait()
        token[...] = jnp.zeros_like(token)

    outs = pl.pallas_call(
        body,
        name=name,
        in_specs=[pl.BlockSpec(memory_space=pltpu.VMEM)] * n + [ANY_SPEC] * n,
        out_specs=[ANY_SPEC] * n + [pl.BlockSpec(memory_space=pltpu.VMEM)],
        out_shape=[jax.ShapeDtypeStruct(b.shape, b.dtype) for b in lands] + [TOKEN],
        scratch_shapes=[pltpu.SemaphoreType.DMA((3 * n,)), pltpu.SemaphoreType.DMA((3 * n,)), pltpu.SemaphoreType.DMA((n,))],
        input_output_aliases={n + i: i for i in range(n)},
        compiler_params=pltpu.CompilerParams(has_side_effects=True, vmem_limit_bytes=VMEM_LIMIT_BYTES),
    )(*shards, *lands)
    return outs[:n], outs[n]


def _gather_own(shards, lands, name):
    n = len(shards)

    def body(*refs):
        ins, bufs, local_sems = refs[:n], refs[n:2 * n], refs[3 * n]
        me = _mesh_pos()
        cps = [pltpu.make_async_copy(ins[a], _block_rows(bufs[a], me, shards[a].shape[1]), local_sems.at[a])
               for a in range(n)]
        for cp in cps:
            cp.start()
        for cp in cps:
            cp.wait()

    return pl.pallas_call(
        body,
        name=name,
        in_specs=[pl.BlockSpec(memory_space=pltpu.VMEM)] * n + [ANY_SPEC] * n,
        out_specs=[ANY_SPEC] * n,
        out_shape=[jax.ShapeDtypeStruct(b.shape, b.dtype) for b in lands],
        scratch_shapes=[pltpu.SemaphoreType.DMA((n,))],
        input_output_aliases={n + i: i for i in range(n)},
        compiler_params=pltpu.CompilerParams(has_side_effects=True, vmem_limit_bytes=VMEM_LIMIT_BYTES),
    )(*shards, *lands)


def _pass_copy(bufs, send_sems, recv_sems, a, j, block, sibling):
    rows = _block_rows(bufs[a], block, bufs[a].shape[1] // N_DEV)
    return pltpu.make_async_remote_copy(
        src_ref=rows, dst_ref=rows, send_sem=send_sems.at[3 * a + j], recv_sem=recv_sems.at[3 * a + j],
        device_id=sibling, device_id_type=MESH_ID)


def _pass_start(lands, after, name):
    n = len(lands)

    def body(*refs):
        bufs = refs[:n]
        send_sems, recv_sems = refs[n + 1], refs[n + 2]
        token = refs[2 * n + 3]
        me = _mesh_pos()
        for a in range(n):
            for j, k in enumerate(CHIP_FLIPS):
                _pass_copy(bufs, send_sems, recv_sems, a, j, _flip(me, k), _flip(me, 1)).start()
        token[...] = jnp.zeros_like(token)

    outs = pl.pallas_call(
        body,
        name=name,
        in_specs=[HBM_SPEC] * n + [ANY_SPEC],
        out_specs=[SEM_SPEC, SEM_SPEC] + [HBM_SPEC] * n + [pl.BlockSpec(memory_space=pltpu.VMEM)],
        out_shape=[pltpu.SemaphoreType.DMA((3 * n,)), pltpu.SemaphoreType.DMA((3 * n,))]
        + [_hbm_like(b) for b in lands] + [TOKEN],
        input_output_aliases={i: 2 + i for i in range(n)},
        compiler_params=pltpu.CompilerParams(has_side_effects=SPLIT_EFFECT),
    )(*[_hbm(b) for b in lands], after)
    return outs[0], outs[1], outs[2:2 + n], outs[2 + n]


def _pass_wait(started, after, name):
    send_sems, recv_sems, lands, _ = started
    n = len(lands)

    def body(*refs):
        bufs = refs[:n]
        send_sems, recv_sems = refs[n], refs[n + 1]
        me = _mesh_pos()
        sibling = _flip(me, 1)
        for a in range(n):
            for j, k in enumerate(CHIP_FLIPS):
                _pass_copy(bufs, send_sems, recv_sems, a, j, _flip(me, k), sibling).wait_send()
                _pass_copy(bufs, send_sems, recv_sems, a, j, _flip(sibling, k), sibling).wait_recv()

    return pl.pallas_call(
        body,
        name=name,
        in_specs=[HBM_SPEC] * n + [SEM_SPEC, SEM_SPEC, ANY_SPEC],
        out_specs=[HBM_SPEC] * n,
        out_shape=[_hbm_like(b) for b in lands],
        input_output_aliases={i: i for i in range(n)},
        compiler_params=pltpu.CompilerParams(has_side_effects=SPLIT_EFFECT),
    )(*lands, send_sems, recv_sems, after)


def _place_own(grads, lands, layer, name):
    n = len(grads)
    blocks = [(g.shape[0], g.shape[1] // N_DEV, g.shape[2]) for g in grads]

    def body(*refs):
        ins, bufs = refs[:n], refs[n:2 * n]
        stage, in_sems, out_sems = refs[3 * n:4 * n], refs[4 * n], refs[4 * n + 1]
        me = _mesh_pos()
        loads = [pltpu.make_async_copy(_block_rows(ins[a], me, blocks[a][1]), stage[a], in_sems.at[a]) for a in range(n)]
        stores = [pltpu.make_async_copy(stage[a], bufs[a].at[_index(me), layer], out_sems.at[a]) for a in range(n)]
        for cp in loads:
            cp.start()
        for a in range(n):
            loads[a].wait()
            stores[a].start()
        for cp in stores:
            cp.wait()

    return pl.pallas_call(
        body,
        name=name,
        in_specs=[ANY_SPEC] * (2 * n),
        out_specs=[ANY_SPEC] * n,
        out_shape=[jax.ShapeDtypeStruct(b.shape, b.dtype) for b in lands],
        scratch_shapes=[pltpu.VMEM(blk, g.dtype) for blk, g in zip(blocks, grads)]
        + [pltpu.SemaphoreType.DMA((n,)), pltpu.SemaphoreType.DMA((n,))],
        input_output_aliases={n + i: i for i in range(n)},
        compiler_params=pltpu.CompilerParams(has_side_effects=True, vmem_limit_bytes=VMEM_LIMIT_BYTES),
    )(*grads, *lands)


def _scatter_copy(ins, bufs, send_sems, recv_sems, a, k, r, layer, me, slab):
    peer = _flip(me, k)
    return pltpu.make_async_remote_copy(
        src_ref=_block_rows(ins[a], peer, r), dst_ref=bufs[a].at[_index(slab), layer],
        send_sem=send_sems.at[7 * a + k - 1], recv_sem=recv_sems.at[7 * a + k - 1],
        device_id=peer, device_id_type=MESH_ID)


def _scatter_start(grads, lands, layer, name):
    n = len(grads)

    def body(*refs):
        ins, bufs = refs[:n], refs[n:2 * n]
        send_sems, recv_sems = refs[2 * n], refs[2 * n + 1]
        token = refs[4 * n + 2]
        me = _mesh_pos()
        for a in range(n):
            for k in range(1, N_DEV):
                _scatter_copy(ins, bufs, send_sems, recv_sems, a, k, grads[a].shape[1] // N_DEV, layer, me, me).start()
        token[...] = jnp.zeros_like(token)

    outs = pl.pallas_call(
        body,
        name=name,
        in_specs=[HBM_SPEC] * (2 * n),
        out_specs=[SEM_SPEC, SEM_SPEC] + [HBM_SPEC] * (2 * n) + [pl.BlockSpec(memory_space=pltpu.VMEM)],
        out_shape=[pltpu.SemaphoreType.DMA((7 * n,)), pltpu.SemaphoreType.DMA((7 * n,))]
        + [_hbm_like(g) for g in grads] + [_hbm_like(b) for b in lands] + [TOKEN],
        input_output_aliases={i: 2 + i for i in range(2 * n)},
        compiler_params=pltpu.CompilerParams(has_side_effects=SPLIT_EFFECT),
    )(*[_hbm(g) for g in grads], *[_hbm(b) for b in lands])
    return outs[0], outs[1], outs[2:2 + n], outs[2 + n:2 + 2 * n], outs[2 + 2 * n]


def _scatter_wait(started, layer, after, name):
    send_sems, recv_sems, grads, lands, _ = started
    n = len(grads)

    def body(*refs):
        ins, bufs = refs[:n], refs[n:2 * n]
        send_sems, recv_sems = refs[2 * n], refs[2 * n + 1]
        me = _mesh_pos()
        for a in range(n):
            for k in range(1, N_DEV):
                cp = _scatter_copy(ins, bufs, send_sems, recv_sems, a, k, grads[a].shape[1] // N_DEV, layer, me, _flip(me, k))
                cp.wait_send()
                cp.wait_recv()

    outs = pl.pallas_call(
        body,
        name=name,
        in_specs=[HBM_SPEC] * (2 * n) + [SEM_SPEC, SEM_SPEC, ANY_SPEC],
        out_specs=[HBM_SPEC] * (2 * n),
        out_shape=[_hbm_like(g) for g in grads] + [_hbm_like(b) for b in lands],
        input_output_aliases={i: i for i in range(2 * n)},
        compiler_params=pltpu.CompilerParams(has_side_effects=SPLIT_EFFECT),
    )(*grads, *lands, send_sems, recv_sems, after)
    return outs[n:]


def _allreduce_small(vec):
    R, C = vec.shape

    def body(v_ref, o_ref, buf, send_sems, recv_sems):
        me = _mesh_pos()
        buf[_index(me)] = v_ref[...]
        sends = []
        for k in range(1, N_DEV):
            sends.append(pltpu.make_async_remote_copy(
                src_ref=buf.at[_index(me)], dst_ref=buf.at[_index(me)],
                send_sem=send_sems.at[k - 1], recv_sem=recv_sems.at[k - 1],
                device_id=_flip(me, k), device_id_type=MESH_ID))
        for cp in sends:
            cp.start()
        for cp in sends:
            cp.wait_recv()
        for cp in sends:
            cp.wait_send()
        acc = buf[0]
        for s in range(1, N_DEV):
            acc = acc + buf[s]
        o_ref[...] = acc

    return pl.pallas_call(
        body,
        name="allreduce_small",
        in_specs=[pl.BlockSpec(memory_space=pltpu.VMEM)],
        out_specs=pl.BlockSpec(memory_space=pltpu.VMEM),
        out_shape=jax.ShapeDtypeStruct((R, C), F32),
        scratch_shapes=[pltpu.VMEM((N_DEV, R, C), F32), pltpu.SemaphoreType.DMA((7,)), pltpu.SemaphoreType.DMA((7,))],
        compiler_params=pltpu.CompilerParams(has_side_effects=True, vmem_limit_bytes=VMEM_LIMIT_BYTES),
    )(vec)


def _pack_small(parts):
    flat = jnp.concatenate([p.reshape(-1) for p in parts])
    n = flat.shape[0]
    rows = -(-n // SMALL_COLS)
    rows = -(-rows // 8) * 8
    return jnp.pad(flat, (0, rows * SMALL_COLS - n)).reshape(rows, SMALL_COLS)


def _unpack_small(packed, like):
    flat = packed.reshape(-1)
    out, pos = [], 0
    for p in like:
        out.append(flat[pos:pos + p.size].reshape(p.shape))
        pos += p.size
    return out


def kernel(x, p, w_in, b_in, w_out, attn_sinks, rel_bias, w_pool, pool_scale, w_ple, w_gate_ple, ln_gain, ln_bias, loss_target, m_w_in, m_b_in, m_w_out, m_attn_sinks, m_rel_bias, m_w_pool, m_pool_scale, m_w_ple, m_w_gate_ple, m_ln_gain, m_ln_bias, v_w_in, v_b_in, v_w_out, v_attn_sinks, v_rel_bias, v_w_pool, v_pool_scale, v_w_ple, v_w_gate_ple, v_ln_gain, v_ln_bias):
    L = w_in.shape[0]
    S = x.shape[1]
    alpha = (2.0 * L) ** 0.25
    bucket_np, masks_np, window_np = _band_constants()
    bucket, masks, window = jnp.asarray(bucket_np), jnp.asarray(masks_np), jnp.asarray(window_np)

    @functools.lru_cache(maxsize=None)
    def shards_of(l):
        return (jnp.swapaxes(w_in[l], 0, 1).astype(BF16)[None], w_out[l].astype(BF16)[None],
                w_gate_ple[l].astype(BF16)[None], jnp.swapaxes(w_ple[l], 0, 1).astype(BF16)[None], w_pool[l].astype(BF16))

    def gathered(started, after, tag):
        shards, lands = _gather_wait(started, after, name=f"gather_wait_{tag}")
        return _gather_pass(shards, lands, name=f"gather_pass_{tag}")

    bias = _bias_build(rel_bias, bucket).reshape(N_KV_HEADS, BIAS_ROWS, BLOCK)

    xs = x[0]
    xb = xs.astype(BF16)
    first_groups = ((0, 4), (1, 2, 3))
    token, first_started = rel_bias, []
    for tag, idxs in zip("ab", first_groups):
        first_started.append(_gather_start([shards_of(0)[i] for i in idxs], token, name=f"gather_start_0{tag}"))
        token = first_started[-1][4]
    started = {1: _gather_start(shards_of(1), token, name="gather_start_1")} if L > 1 else {}
    saved = []
    for l in range(L):
        pb = p[l, 0].astype(BF16)
        sinks_l = attn_sinks[l]
        scale_l = pool_scale[l].reshape(1, POOL_WIDTH)
        bias_l = b_in[l].reshape(1, IN_COLS)
        if l == 0:
            (w_in_f, w_pool_g), _ = gathered(first_started[0], started[1][4] if L > 1 else xb, "0a")
            w_in_t = w_in_f[0]
            h = _matmul(xb, w_in_t, tb=True, tm=512, tn=2176, tk=2048, out_dtype=F32, bias=bias_l, name=f"in_proj_{l}")
            (w_out_f, w_gate_f, w_ple_f), _ = gathered(first_started[1], h, "0b")
            w_out_g, w_gate_g, w_ple_t = w_out_f[0], w_gate_f[0], w_ple_f[0]
        else:
            w_in_t, w_out_g, w_gate_g, w_ple_t, w_pool_g = weights
            h = _matmul(xb, w_in_t, tb=True, tm=512, tn=2176, tk=2048, out_dtype=F32, bias=bias_l, name=f"in_proj_{l}")
        weights = (w_in_t, w_out_g, w_gate_g, w_ple_t, w_pool_g)
        ab, probs, p_sinks = _attn_fwd(h, bias, masks, sinks_l, name=f"attn_fwd_{l}")
        ab = _pool_fwd(h, ab, w_pool_g, scale_l, name=f"pool_fwd_{l}")
        pin, passing = ab, None
        if 1 <= l and l + 1 < L:
            shards, lands = _gather_wait(started[l + 1], ab, name=f"gather_wait_{l + 1}")
            passing = _pass_start(_gather_own(shards, lands, name=f"gather_own_{l + 1}"), ab, name=f"pass_start_{l + 1}")
            pin = passing[3]
        if l + 2 < L:
            started[l + 2] = _gather_start(shards_of(l + 2), pin, name=f"gather_start_{l + 2}")
            pin = started[l + 2][4]
        y, yb, xhat, rstd, gp, pe = _mix_ln_fwd(ab, xs, pb, w_out_g, w_gate_g, w_ple_t, ln_gain[l].reshape(1, D_MODEL),
                                                ln_bias[l].reshape(1, D_MODEL), alpha, pin, name=f"mix_ln_fwd_{l}")
        saved.append((xb, pb, h, gp, pe, ab, xhat, rstd, probs, p_sinks, scale_l, weights))
        xs, xb = y, yb
        if l + 1 < L:
            if passing is None:
                full, _ = gathered(started[l + 1], yb, l + 1)
            else:
                full = _pass_wait(passing, yb, name=f"pass_wait_{l + 1}")
            weights = (full[0][0], full[1][0], full[2][0], full[3][0], full[4])

    dy, loss_tile = _loss_head(xs, loss_target[0])

    dbias = jnp.zeros((N_KV_HEADS, BIAS_ROWS, BLOCK), F32)
    sh0 = shards_of(0)
    lands_a = [lax.empty((N_DEV, L) + sh0[i].shape, BF16) for i in (1, 2, 3)]
    lands_b = [lax.empty((N_DEV, L) + sh0[i].shape, BF16) for i in (0, 4)]
    g_b_in, g_sinks, g_scale, g_gain, g_beta = [], [], [], [], []
    pend_a = pend_b = None

    def scatter(grads, lands, pending, l, tag):
        if pending:
            lands = _scatter_wait(pending[0], pending[1], grads[0], name=f"scatter_wait_{pending[1]}{tag}")
        lands = _place_own(grads, lands, l, name=f"place_own_{l}{tag}")
        return _scatter_start(grads, lands, l, name=f"scatter_start_{l}{tag}"), l

    for l in reversed(range(L)):
        xb, pb, h, gp, pe, ab, xhat, rstd, probs, p_sinks, scale_l, weights = saved[l]
        w_in_t, w_out_g, w_gate_g, w_ple_t, w_pool_g = weights
        dzb, dpe, dgp, dab, dx, dgain, dbeta = _ln_dmix_bwd(
            dy, xhat, rstd, ln_gain[l].reshape(1, D_MODEL), gp, pe, w_out_g, w_gate_g, alpha,
            pend_b[0][4] if pend_b else rel_bias, name=f"ln_dmix_bwd_{l}")
        g_w_out = _matmul(ab, dzb, ta=True, tm=512, tn=1024, tk=4096, out_dtype=BF16, name=f"dw_out_{l}")
        g_w_gate = _matmul(xb, dgp, ta=True, tm=512, tn=1024, tk=4096, out_dtype=BF16, name=f"dw_gate_{l}")
        g_w_ple_t = _matmul(dpe, pb, ta=True, tm=1024, tn=256, tk=1024, out_dtype=BF16, name=f"dw_ple_{l}")
        pend_a = scatter([g_w_out[None], g_w_gate[None], g_w_ple_t[None]], lands_a, pend_a, l, "a")
        dh, dwp, dsc, db_pool = _pool_bwd(h, dab, w_pool_g, scale_l, pend_a[0][4], name=f"pool_bwd_{l}")
        dh, dbias, dsink, db_attn = _attn_bwd(h, dab, dh, probs, p_sinks, dbias, name=f"attn_bwd_{l}")
        g_w_in_t = _matmul(dh, xb, ta=True, tm=256, tn=1024, tk=4096, out_dtype=BF16, name=f"dw_in_{l}")
        pend_b = scatter([g_w_in_t[None], dwp.astype(BF16)], lands_b, pend_b, l, "b")
        g_b_in.append(jnp.concatenate([db_attn[0], db_pool[0, U_OFF - DH_POOL_COLS:]]))
        dy = _matmul(dh, w_in_t, tm=512, tn=1024, tk=4352, out_dtype=F32, add=dx, after=pend_b[0][4], name=f"dx_in_{l}")
        g_sinks.append(dsink[:, :, 0].reshape(N_HEADS))
        g_scale.append(dsc.reshape(POOL_WIDTH))
        g_gain.append(dgain.reshape(D_MODEL))
        g_beta.append(dbeta.reshape(D_MODEL))
    grad_x = dy[None]
    for lst in (g_b_in, g_sinks, g_scale, g_gain, g_beta):
        lst.reverse()
    g_rel = _bias_bwd(dbias.reshape((N_HEADS,) + BAND), bucket, window)[:, :N_HEADS]

    def big(w, g, m, v, name):
        shape = w.shape
        two_d = (shape[0] * shape[1], shape[2]) if len(shape) == 3 else (shape[0] * shape[1] * shape[2], shape[3])
        d, nm, nv = _adamw(w.reshape(two_d), g.reshape(two_d), m.reshape(two_d), v.reshape(two_d), name=name)
        return d.reshape(shape), nm.reshape(shape), nv.reshape(shape)

    r_out, r_gate, r_ple = _scatter_wait(pend_a[0], pend_a[1], dy, name="scatter_wait_0a")
    small_like = [b_in, attn_sinks, rel_bias, pool_scale, ln_gain, ln_bias]
    small_g = _allreduce_small(_pack_small([
        jnp.stack(g_b_in).reshape(L, IN_COLS), jnp.stack(g_sinks), g_rel, jnp.stack(g_scale), jnp.stack(g_gain),
        jnp.stack(g_beta), loss_tile[0, :1]]))
    grad_w_out = _sum_slabs(r_out.reshape(N_DEV, L * 256, D_MODEL), name="sum_w_out").reshape(L, 256, D_MODEL)
    grad_w_gate = _sum_slabs(r_gate.reshape(N_DEV, L * 256, D_MODEL), name="sum_w_gate").reshape(L, 256, D_MODEL)
    gt_ple = _sum_slabs(r_ple.reshape(N_DEV, L * 256, PLE_DIM), name="sum_w_ple")
    grad_w_ple = jnp.swapaxes(gt_ple.reshape(L, 256, PLE_DIM), 1, 2)
    upd_out = big(w_out, grad_w_out, m_w_out, v_w_out, "adamw_w_out")
    upd_ple = big(w_ple, grad_w_ple, m_w_ple, v_w_ple, "adamw_w_ple")
    upd_gate = big(w_gate_ple, grad_w_gate, m_w_gate_ple, v_w_gate_ple, "adamw_w_gate")

    r_in, r_pool = _scatter_wait(pend_b[0], pend_b[1], upd_gate[0], name="scatter_wait_0b")
    gt_in = _sum_slabs(r_in.reshape(N_DEV, L * 544, D_MODEL), name="sum_w_in")
    grad_w_in = jnp.swapaxes(gt_in.reshape(L, 544, D_MODEL), 1, 2)
    grad_w_pool = _sum_slabs(r_pool.reshape(N_DEV, L * 4 * 32, 256), name="sum_w_pool").reshape(L, 4, 32, 256)
    upd_in = big(w_in, grad_w_in, m_w_in, v_w_in, "adamw_w_in")
    upd_pool = big(w_pool, grad_w_pool, m_w_pool, v_w_pool, "adamw_w_pool")

    zero1 = jnp.zeros((1,), F32)
    sw = _pack_small(small_like + [zero1])
    sm = _pack_small([m_b_in, m_attn_sinks, m_rel_bias, m_pool_scale, m_ln_gain, m_ln_bias, zero1])
    sv = _pack_small([v_b_in, v_attn_sinks, v_rel_bias, v_pool_scale, v_ln_gain, v_ln_bias, zero1])
    sd, snm, snv = _adamw(sw, small_g, sm, sv, name="adamw_small")
    like = small_like + [zero1]
    sg_parts = _unpack_small(small_g, like)
    sd_parts, snm_parts, snv_parts = _unpack_small(sd, like), _unpack_small(snm, like), _unpack_small(snv, like)
    loss = sg_parts[6][0]

    def assemble(big_parts, small_parts):
        w_in_, w_out_, w_pool_, w_ple_, w_gate_ = big_parts
        b_in_, sinks_, rel_, scale_, gain_, beta_ = small_parts[:6]
        return [w_in_, b_in_, w_out_, sinks_, rel_, w_pool_, scale_, w_ple_, w_gate_, gain_, beta_]

    grads = assemble([grad_w_in, grad_w_out, grad_w_pool, grad_w_ple, grad_w_gate], sg_parts)
    ups = [upd_in, upd_out, upd_pool, upd_ple, upd_gate]
    deltas = assemble([u[0] for u in ups], sd_parts)
    new_m = assemble([u[1] for u in ups], snm_parts)
    new_v = assemble([u[2] for u in ups], snv_parts)
    return (loss, grad_x, *grads, *deltas, *new_m, *new_v)
```

```python
import functools
import math

import numpy as np
import jax
import jax.numpy as jnp
from jax import lax
from jax.experimental import pallas as pl
from jax.experimental.pallas import tpu as pltpu

F32 = jnp.float32
BF16 = jnp.bfloat16

D_MODEL = 2048
PLE_DIM = 256
ATTN_WIDTH = 1024
POOL_WIDTH = 1024
HEAD_DIM = 64
N_HEADS = 16
N_KV_HEADS = 2
KV_GROUP = 8
WINDOW = 128
BLOCK = 128
POOL_WINDOWS = (2, 4, 8, 16)
POOL_GROUP_DIM = 256
POOL_HALO = 16
REL_BUCKETS = 32
REL_MAX_DIST = 128
LN_EPS = 1e-5
KV_COLS = N_KV_HEADS * HEAD_DIM
IN_COLS = 4352
Q_OFF, KV_OFF, GA_OFF, U_OFF, GB_OFF = 0, 1024, 1280, 2304, 3328
ATTN_SCALE = 1.0 / math.sqrt(HEAD_DIM)
NEG_BIG = -1e30
LANES = 128

ADAM_LR = 0.001
ADAM_B1 = 0.9
ADAM_B2 = 0.999
ADAM_EPS = 1e-08
ADAM_WD = 0.01
ADAM_STEP = 10

N_DEV = 8
MESH_ID = pl.DeviceIdType.MESH
VMEM_LIMIT_BYTES = 52 * 1024 * 1024
SMALL_COLS = 1024


def _params(sem=None):
    return pltpu.CompilerParams(dimension_semantics=sem, vmem_limit_bytes=VMEM_LIMIT_BYTES)


def _sigmoid(x):
    return 1.0 / (1.0 + jnp.exp(-x))


def _tile(n, pref, unit=16):
    if n <= pref:
        return n
    t = pref - pref % unit
    while n % t:
        t -= unit
    assert t > 0, (n, pref)
    return t


def _matmul(a, b, *, name, ta=False, tb=False, tm, tn, tk, out_dtype, bias=None, add=None, add_scale=1.0, after=None):
    M, K = (a.shape[1], a.shape[0]) if ta else a.shape
    N = b.shape[0] if tb else b.shape[1]
    assert (b.shape[1] if tb else b.shape[0]) == K
    tm, tn, tk = _tile(M, tm), _tile(N, tn), _tile(K, tk)
    nm, nn, nk = M // tm, N // tn, K // tk
    a_spec = pl.BlockSpec((tk, tm), lambda j, i, k: (k, i)) if ta else pl.BlockSpec((tm, tk), lambda j, i, k: (i, k))
    b_spec = pl.BlockSpec((tn, tk), lambda j, i, k: (j, k)) if tb else pl.BlockSpec((tk, tn), lambda j, i, k: (k, j))
    dims = (((0 if ta else 1,), (1 if tb else 0,)), ((), ()))
    operands, in_specs = [a, b], [a_spec, b_spec]
    if bias is not None:
        operands.append(bias)
        in_specs.append(pl.BlockSpec((1, tn), lambda j, i, k: (0, j)))
    if add is not None:
        operands.append(add)
        in_specs.append(pl.BlockSpec((tm, tn), lambda j, i, k: (i, j)))
    if after is not None:
        operands.append(after)
        in_specs.append(pl.BlockSpec(memory_space=pl.ANY))

    def body(*refs):
        a_ref, b_ref = refs[0], refs[1]
        pos = 2
        bias_ref = add_ref = None
        if bias is not None:
            bias_ref = refs[pos]
            pos += 1
        if add is not None:
            add_ref = refs[pos]
            pos += 1
        if after is not None:
            pos += 1
        o_ref = refs[pos]
        part = lax.dot_general(a_ref[...].astype(BF16), b_ref[...].astype(BF16), dims, preferred_element_type=F32)

        def finish(acc):
            if bias_ref is not None:
                acc = acc + bias_ref[...]
            if add_ref is not None:
                acc = acc + add_scale * add_ref[...].astype(F32)
            o_ref[...] = acc.astype(out_dtype)

        if nk == 1:
            finish(part)
        else:
            acc_ref = refs[pos + 1]
            k = pl.program_id(2)

            @pl.when(k == 0)
            def _():
                acc_ref[...] = part

            @pl.when(k > 0)
            def _():
                acc_ref[...] += part

            @pl.when(k == nk - 1)
            def _():
                finish(acc_ref[...])

    return pl.pallas_call(
        body,
        name=name,
        grid=(nn, nm, nk),
        in_specs=in_specs,
        out_specs=pl.BlockSpec((tm, tn), lambda j, i, k: (i, j)),
        out_shape=jax.ShapeDtypeStruct((M, N), out_dtype),
        scratch_shapes=[pltpu.VMEM((tm, tn), F32)] if nk > 1 else [],
        compiler_params=_params(("parallel", "parallel", "arbitrary")),
    )(*operands)


BAND = (2 * BLOCK, BLOCK)
BIAS_ROWS = KV_GROUP * 2 * BLOCK


def _band_constants():
    qq = np.arange(BLOCK)[None, :]
    kk = np.arange(2 * BLOCK)[:, None]
    dist = qq + BLOCK - kk
    in_window = (dist >= 0) & (dist < WINDOW)
    max_exact = REL_BUCKETS // 2
    d = np.maximum(dist, 0)
    d_f = np.maximum(d, 1).astype(np.float32)
    large = max_exact + (
        np.log(d_f / np.float32(max_exact)) / np.float32(math.log(REL_MAX_DIST / max_exact)) * np.float32(REL_BUCKETS - max_exact)
    ).astype(np.int32)
    large = np.minimum(large, REL_BUCKETS - 1)
    bucket = np.where(d < max_exact, d, large).astype(np.int32)
    bucket = np.where(in_window, bucket, 0).astype(np.int32)
    first = in_window & (kk >= BLOCK)
    masks = np.stack([first, in_window]).astype(np.float32)
    return bucket, masks, in_window.astype(np.float32)


def _bias_build(rel_bias, bucket):
    def body(rb_ref, bkt_ref, o_ref):
        h = pl.program_id(0)
        bkt = bkt_ref[...]

        def step(b, acc):
            return jnp.where(bkt == b, rb_ref[b, h], acc)

        o_ref[0] = lax.fori_loop(0, REL_BUCKETS, step, jnp.zeros(BAND, F32))

    return pl.pallas_call(
        body,
        name="bias_build",
        grid=(N_HEADS,),
        in_specs=[pl.BlockSpec(memory_space=pltpu.SMEM), pl.BlockSpec(BAND, lambda h: (0, 0))],
        out_specs=pl.BlockSpec((1,) + BAND, lambda h: (h, 0, 0)),
        out_shape=jax.ShapeDtypeStruct((N_HEADS,) + BAND, F32),
        compiler_params=_params(("arbitrary",)),
    )(rel_bias, bucket)


def _bias_bwd(dbias, bucket, window):
    def body(db_ref, bkt_ref, win_ref, o_ref):
        h = pl.program_id(0)

        @pl.when(h == 0)
        def _():
            o_ref[...] = jnp.zeros_like(o_ref)

        bkt = bkt_ref[...]
        x = jnp.where(win_ref[...] > 0.5, db_ref[0], 0.0)
        row = lax.broadcasted_iota(jnp.int32, (REL_BUCKETS, LANES), 0)
        col = lax.broadcasted_iota(jnp.int32, (REL_BUCKETS, LANES), 1)

        def step(b, acc):
            s = jnp.sum(jnp.where(bkt == b, x, 0.0), axis=0, keepdims=True)
            return acc + jnp.where(row == b, s, 0.0)

        per_lane = lax.fori_loop(0, REL_BUCKETS, step, jnp.zeros((REL_BUCKETS, LANES), F32))
        o_ref[...] += jnp.where(col == h, jnp.sum(per_lane, axis=1, keepdims=True), 0.0)

    return pl.pallas_call(
        body,
        name="bias_bwd",
        grid=(N_HEADS,),
        in_specs=[
            pl.BlockSpec((1,) + BAND, lambda h: (h, 0, 0)),
            pl.BlockSpec(BAND, lambda h: (0, 0)),
            pl.BlockSpec(BAND, lambda h: (0, 0)),
        ],
        out_specs=pl.BlockSpec((REL_BUCKETS, LANES), lambda h: (0, 0)),
        out_shape=jax.ShapeDtypeStruct((REL_BUCKETS, LANES), F32),
        compiler_params=_params(("arbitrary",)),
    )(dbias, bucket, window)


def _lane_lo(shape):
    return lax.broadcasted_iota(jnp.int32, shape, 1) < HEAD_DIM


def _row_lo(shape):
    return lax.broadcasted_iota(jnp.int32, shape, 0) < HEAD_DIM


def _dup_heads(x):
    r = pltpu.roll(x, HEAD_DIM, axis=1)
    lo = _lane_lo(x.shape)
    return jnp.where(lo, x, r), jnp.where(lo, r, x)


def _kv_operands(kvp_ref, kvc_ref):
    kvp, kvc = kvp_ref[...], kvc_ref[...]
    k2 = jnp.concatenate([kvp[:, :KV_COLS], kvc[:, :KV_COLS]], axis=0)
    v2 = jnp.concatenate([kvp[:, KV_COLS:], kvc[:, KV_COLS:]], axis=0)
    return _dup_heads(k2), _dup_heads(v2)


def _head_probs(k_r, qs_t, bias, mask, sink):
    s = jnp.dot(k_r, qs_t, preferred_element_type=F32) * ATTN_SCALE + bias
    s = jnp.where(mask, s, NEG_BIG)
    m = jnp.maximum(jnp.max(s, axis=0, keepdims=True), sink)
    e = jnp.exp(s - m)
    e_sink = jnp.exp(sink - m)
    inv = 1.0 / (jnp.sum(e, axis=0, keepdims=True) + e_sink)
    return e * inv, e_sink * inv


def _gate_cols(ga_refs, pair):
    off = LANES * (pair % 2)
    return ga_refs[pair // 2][:, off:off + LANES]


def _attn_specs(order):
    return [
        pl.BlockSpec((BLOCK, ATTN_WIDTH), lambda t: (order(t), Q_OFF // ATTN_WIDTH)),
        pl.BlockSpec((BLOCK, 2 * KV_COLS), lambda t: (order(t), KV_OFF // (2 * KV_COLS))),
        pl.BlockSpec((BLOCK, 2 * KV_COLS), lambda t: (jnp.maximum(order(t) - 1, 0), KV_OFF // (2 * KV_COLS))),
    ] + [
        pl.BlockSpec((BLOCK, 256), functools.partial(lambda t, c: (order(t), GA_OFF // 256 + c), c=c)) for c in range(4)
    ] + [
        pl.BlockSpec((N_KV_HEADS, BIAS_ROWS, BLOCK), lambda t: (0, 0, 0)),
        pl.BlockSpec((None,) + BAND, lambda t: (jnp.minimum(order(t), 1), 0, 0)),
        pl.BlockSpec(memory_space=pltpu.SMEM),
    ]


def _attn_fwd(h, bias, masks, sinks, name):
    S = h.shape[0]
    nb = S // BLOCK

    def body(q_ref, kvc_ref, kvp_ref, ga0, ga1, ga2, ga3, bias_ref, mask_ref, sink_ref, o_ref):
        kd, vd = _kv_operands(kvp_ref, kvc_ref)
        mask = mask_ref[...] > 0.5
        lo = _row_lo((LANES, BLOCK))
        for g in range(N_KV_HEADS):
            k_r = kd[g].astype(BF16)
            v_t = vd[g].T.astype(BF16)
            for pr in range(KV_GROUP // 2):
                pair = (KV_GROUP // 2) * g + pr
                qp_t = q_ref[:, LANES * pair:LANES * (pair + 1)].T
                outs = []
                for hh in range(2):
                    j = 2 * pr + hh
                    qs_t = jnp.where(lo if hh == 0 else ~lo, qp_t, 0.0).astype(BF16)
                    p, _ = _head_probs(k_r, qs_t, bias_ref[g, 2 * BLOCK * j:2 * BLOCK * (j + 1), :], mask,
                                       sink_ref[KV_GROUP * g + j])
                    outs.append(jnp.dot(v_t, p.astype(BF16), preferred_element_type=F32))
                ga = _gate_cols((ga0, ga1, ga2, ga3), pair)
                o_ref[:, LANES * pair:LANES * (pair + 1)] = (
                    jnp.where(lo, outs[0], outs[1]).T * (ga * _sigmoid(ga))).astype(BF16)

    return pl.pallas_call(
        body,
        name=name,
        grid=(nb,),
        in_specs=_attn_specs(lambda t: t),
        out_specs=pl.BlockSpec((BLOCK, ATTN_WIDTH), lambda t: (t, 0)),
        out_shape=jax.ShapeDtypeStruct((S, ATTN_WIDTH + POOL_WIDTH), BF16),
        compiler_params=_params(("arbitrary",)),
    )(h, h, h, h, h, h, h, bias, masks, sinks)


DH_ATTN_COLS = U_OFF


def _attn_bwd(h, dab, dh, bias, masks, sinks, dbias_in, name):
    S = h.shape[0]
    nb = S // BLOCK

    def order(t):
        return nb - 1 - t

    def body(q_ref, kvc_ref, kvp_ref, ga0, ga1, ga2, ga3, bias_ref, mask_ref, sink_ref, da_ref, dbin_ref, dh_in_ref,
             dh_ref, dbias_ref, dsink_ref, db_ref, carry_scr):
        del dh_in_ref
        t = pl.program_id(0)

        @pl.when(t == 0)
        def _():
            dbias_ref[...] = dbin_ref[...]
            dsink_ref[...] = jnp.zeros_like(dsink_ref)
            db_ref[...] = jnp.zeros_like(db_ref)
            carry_scr[...] = jnp.zeros_like(carry_scr)

        kd, vd = _kv_operands(kvp_ref, kvc_ref)
        mask = mask_ref[...] > 0.5
        lo = _lane_lo((BLOCK, LANES))
        lo_t = _row_lo((LANES, BLOCK))
        dk_tot, dv_tot = [], []
        for g in range(N_KV_HEADS):
            k_t, k_r = kd[g].T.astype(BF16), kd[g].astype(BF16)
            v_t, v_r = vd[g].T.astype(BF16), vd[g].astype(BF16)
            dk = jnp.zeros((2 * BLOCK, LANES), F32)
            dv = jnp.zeros((2 * BLOCK, LANES), F32)
            for pr in range(KV_GROUP // 2):
                pair = (KV_GROUP // 2) * g + pr
                cols = slice(LANES * pair, LANES * (pair + 1))
                qp = q_ref[:, cols]
                qp_t = qp.T
                ga = _gate_cols((ga0, ga1, ga2, ga3), pair)
                sg = _sigmoid(ga)
                da = da_ref[:, cols]
                do_p = da * (ga * sg)
                do_t = do_p.T
                outs, dqs = [], []
                for hh in range(2):
                    j = 2 * pr + hh
                    rows = slice(2 * BLOCK * j, 2 * BLOCK * (j + 1))
                    half, half_t = (lo, lo_t) if hh == 0 else (~lo, ~lo_t)
                    qs_t = jnp.where(half_t, qp_t, 0.0).astype(BF16)
                    p, p_sink = _head_probs(k_r, qs_t, bias_ref[g, rows, :], mask, sink_ref[KV_GROUP * g + j])
                    pb = p.astype(BF16)
                    outs.append(jnp.dot(v_t, pb, preferred_element_type=F32))
                    dos_t = jnp.where(half_t, do_t, 0.0).astype(BF16)
                    dp = jnp.dot(v_r, dos_t, preferred_element_type=F32)
                    dsum = jnp.sum(p * dp, axis=0, keepdims=True)
                    ds = p * (dp - dsum)
                    dbias_ref[g, rows, :] += ds
                    tot = jnp.sum(-(p_sink * dsum), axis=1, keepdims=True)
                    dsink_ref[g, j:j + 1, :] += jnp.broadcast_to(tot, (1, LANES))
                    dsb = ds.astype(BF16)
                    dqs.append(jnp.dot(k_t, dsb, preferred_element_type=F32))
                    dk = dk + jnp.dot(dsb, jnp.where(half, qp, 0.0).astype(BF16), preferred_element_type=F32)
                    dv = dv + jnp.dot(pb, jnp.where(half, do_p, 0.0).astype(BF16), preferred_element_type=F32)
                attn = jnp.where(lo_t, outs[0], outs[1]).T
                dq = jnp.where(lo_t, dqs[0], dqs[1]).T * ATTN_SCALE
                dga = da * attn * (sg * (1.0 + ga * (1.0 - sg)))
                ga_cols = slice(GA_OFF + LANES * pair, GA_OFF + LANES * (pair + 1))
                dh_ref[:, cols] = dq.astype(BF16)
                dh_ref[:, ga_cols] = dga.astype(BF16)
                db_ref[:, cols] += jnp.sum(dq, axis=0, keepdims=True)
                db_ref[:, ga_cols] += jnp.sum(dga, axis=0, keepdims=True)
            dk = dk * ATTN_SCALE
            dk_tot.append(dk + pltpu.roll(dk, HEAD_DIM, axis=1))
            dv_tot.append(dv + pltpu.roll(dv, HEAD_DIM, axis=1))
        lo2 = _lane_lo((2 * BLOCK, LANES))
        dkv = jnp.concatenate([jnp.where(lo2, dk_tot[0], dk_tot[1]), jnp.where(lo2, dv_tot[0], dv_tot[1])], axis=1)
        dkv_done = dkv[BLOCK:, :] + carry_scr[...]
        dh_ref[:, KV_OFF:KV_OFF + 2 * KV_COLS] = dkv_done.astype(BF16)
        db_ref[:, KV_OFF:KV_OFF + 2 * KV_COLS] += jnp.sum(dkv_done, axis=0, keepdims=True)
        carry_scr[...] = dkv[:BLOCK, :]

    n_in = 12
    return pl.pallas_call(
        body,
        name=name,
        grid=(nb,),
        in_specs=_attn_specs(order) + [
            pl.BlockSpec((BLOCK, ATTN_WIDTH), lambda t: (order(t), 0)),
            pl.BlockSpec((N_KV_HEADS, BIAS_ROWS, BLOCK), lambda t: (0, 0, 0)),
            pl.BlockSpec(memory_space=pl.ANY),
        ],
        out_specs=[
            pl.BlockSpec((BLOCK, DH_ATTN_COLS), lambda t: (order(t), 0)),
            pl.BlockSpec((N_KV_HEADS, BIAS_ROWS, BLOCK), lambda t: (0, 0, 0)),
            pl.BlockSpec((N_KV_HEADS, KV_GROUP, LANES), lambda t: (0, 0, 0)),
            pl.BlockSpec((1, DH_ATTN_COLS), lambda t: (0, 0)),
        ],
        out_shape=[
            jax.ShapeDtypeStruct((S, IN_COLS), BF16),
            jax.ShapeDtypeStruct((N_KV_HEADS, BIAS_ROWS, BLOCK), F32),
            jax.ShapeDtypeStruct((N_KV_HEADS, KV_GROUP, LANES), F32),
            jax.ShapeDtypeStruct((1, DH_ATTN_COLS), F32),
        ],
        scratch_shapes=[pltpu.VMEM((BLOCK, 2 * KV_COLS), F32)],
        input_output_aliases={n_in: 0},
        compiler_params=_params(("arbitrary",)),
    )(h, h, h, h, h, h, h, bias, masks, sinks, dab, dbias_in, dh)


def _window_sum(x, w, back):
    n = x.shape[0]
    s, sh = x, 1
    while sh < w:
        s = s + pltpu.roll(s, sh if back else n - sh, axis=0)
        sh *= 2
    return s


def _pool_counts(first_row, n, w):
    t = first_row + lax.broadcasted_iota(jnp.int32, (n, 1), 0)
    return jnp.minimum(t + 1, w).astype(F32)


def _pool_diff(u_ref, uh_ref, i, T, g):
    u = u_ref[...]
    halo = jnp.where(i > 0, uh_ref[...], 0.0)
    ext = jnp.concatenate([halo, u], axis=0)
    w = POOL_WINDOWS[g]
    s = _window_sum(ext, w, back=True)[POOL_HALO:, :]
    return s / _pool_counts(i * T, T, w) - u


def _pool_in_specs(T):
    hb = T // POOL_HALO
    specs = []
    for g in range(4):
        specs.append(pl.BlockSpec((T, 256), functools.partial(lambda i, g: (i, U_OFF // 256 + g), g=g)))
        specs.append(pl.BlockSpec((POOL_HALO, 256), functools.partial(
            lambda i, g: (jnp.maximum(i * hb - 1, 0), U_OFF // 256 + g), g=g)))
    return specs


def _pool_weight_specs():
    return [pl.BlockSpec((4, 256, 256), lambda i: (0, 0, 0)), pl.BlockSpec((1, POOL_WIDTH), lambda i: (0, 0))]


def _pool_fwd(h, ab, w_pool, pool_scale, name):
    S = h.shape[0]
    T = _tile(S, 512)

    def body(*refs):
        u_refs = refs[0:8]
        gb_refs = refs[8:12]
        wp_ref, sc_ref, o_ref = refs[12], refs[13], refs[15]
        i = pl.program_id(0)
        for g in range(4):
            diff = _pool_diff(u_refs[2 * g], u_refs[2 * g + 1], i, T, g).astype(BF16)
            mixed = jnp.dot(diff, wp_ref[g], preferred_element_type=F32) * sc_ref[:, 256 * g:256 * (g + 1)]
            gb = gb_refs[g][...]
            o_ref[:, 256 * g:256 * (g + 1)] = (mixed * (gb * _sigmoid(gb))).astype(BF16)

    in_specs = _pool_in_specs(T) + [
        pl.BlockSpec((T, 256), functools.partial(lambda i, g: (i, GB_OFF // 256 + g), g=g)) for g in range(4)
    ] + _pool_weight_specs() + [pl.BlockSpec(memory_space=pl.ANY)]
    return pl.pallas_call(
        body,
        name=name,
        grid=(S // T,),
        in_specs=in_specs,
        out_specs=pl.BlockSpec((T, POOL_WIDTH), lambda i: (i, 1)),
        out_shape=jax.ShapeDtypeStruct(ab.shape, BF16),
        input_output_aliases={14: 0},
        compiler_params=_params(("arbitrary",)),
    )(*([h] * 12), w_pool, pool_scale, ab)


DH_POOL_COLS = IN_COLS // 2


def _pool_bwd(h, dab, w_pool, pool_scale, after, name):
    S = h.shape[0]
    T = _tile(S, 512)
    nt = S // T
    hb = T // POOL_HALO
    E = T + POOL_HALO
    lead = U_OFF - DH_POOL_COLS

    def body(*refs):
        u_refs = refs[0:8]
        gb_refs = refs[8:16]
        db_refs = refs[16:24]
        wp_ref, sc_ref = refs[24], refs[25]
        dh_ref, dwp_ref, dsc_ref, dbi_ref = refs[27:31]
        i = pl.program_id(0)

        @pl.when(i == 0)
        def _():
            dwp_ref[...] = jnp.zeros_like(dwp_ref)
            dsc_ref[...] = jnp.zeros_like(dsc_ref)
            dbi_ref[...] = jnp.zeros_like(dbi_ref)

        dh_ref[:, 0:lead] = jnp.zeros((T, lead), BF16)
        for g in range(4):
            w = POOL_WINDOWS[g]
            cols = slice(256 * g, 256 * (g + 1))
            scale = sc_ref[:, cols]
            wp = wp_ref[g]
            diff = _pool_diff(u_refs[2 * g], u_refs[2 * g + 1], i, T, g).astype(BF16)
            mixed = jnp.dot(diff, wp, preferred_element_type=F32)
            keep = i < nt - 1
            gb = jnp.concatenate([gb_refs[2 * g][...], jnp.where(keep, gb_refs[2 * g + 1][...], 0.0)], axis=0)
            db = jnp.concatenate([db_refs[2 * g][...], jnp.where(keep, db_refs[2 * g + 1][...], 0.0)], axis=0)
            sg = _sigmoid(gb)
            dms = db * (gb * sg)
            dmixed = (dms * scale).astype(BF16)
            ddiff = lax.dot_general(dmixed, wp, (((1,), (1,)), ((), ())), preferred_element_type=F32)
            r = ddiff / _pool_counts(i * T, E, w)
            du = _window_sum(r, w, back=False)[:T, :] - ddiff[:T, :]
            dgb = db[:T, :] * (mixed * scale) * (sg[:T, :] * (1.0 + gb[:T, :] * (1.0 - sg[:T, :])))
            u_cols = slice(lead + 256 * g, lead + 256 * (g + 1))
            gb_cols = slice(lead + POOL_WIDTH + 256 * g, lead + POOL_WIDTH + 256 * (g + 1))
            dh_ref[:, u_cols] = du.astype(BF16)
            dh_ref[:, gb_cols] = dgb.astype(BF16)
            dbi_ref[:, u_cols] += jnp.sum(du, axis=0, keepdims=True)
            dbi_ref[:, gb_cols] += jnp.sum(dgb, axis=0, keepdims=True)
            dsc_ref[:, cols] += jnp.sum(dms[:T, :] * mixed, axis=0, keepdims=True)
            dwp_ref[g] += lax.dot_general(diff, dmixed[:T, :], (((0,), (0,)), ((), ())), preferred_element_type=F32)

    def rows_after(i):
        return jnp.minimum((i + 1) * hb, S // POOL_HALO - 1)

    in_specs = _pool_in_specs(T)
    for off in (GB_OFF // 256, ATTN_WIDTH // 256):
        for g in range(4):
            in_specs.append(pl.BlockSpec((T, 256), functools.partial(lambda i, c: (i, c), c=off + g)))
            in_specs.append(pl.BlockSpec((POOL_HALO, 256), functools.partial(lambda i, c: (rows_after(i), c), c=off + g)))
    in_specs += _pool_weight_specs() + [pl.BlockSpec(memory_space=pl.ANY)]
    return pl.pallas_call(
        body,
        name=name,
        grid=(nt,),
        in_specs=in_specs,
        out_specs=[
            pl.BlockSpec((T, DH_POOL_COLS), lambda i: (i, 1)),
            pl.BlockSpec((4, 256, 256), lambda i: (0, 0, 0)),
            pl.BlockSpec((1, POOL_WIDTH), lambda i: (0, 0)),
            pl.BlockSpec((1, DH_POOL_COLS), lambda i: (0, 0)),
        ],
        out_shape=[
            jax.ShapeDtypeStruct((S, IN_COLS), BF16),
            jax.ShapeDtypeStruct((4, 256, 256), F32),
            jax.ShapeDtypeStruct((1, POOL_WIDTH), F32),
            jax.ShapeDtypeStruct((1, DH_POOL_COLS), F32),
        ],
        compiler_params=_params(("arbitrary",)),
    )(*([h] * 16), *([dab] * 8), w_pool, pool_scale, after)


def _load_resident(pairs, sems):
    @pl.when(pl.program_id(0) == 0)
    def _():
        cps = [pltpu.make_async_copy(src, dst, sems.at[n]) for n, (src, dst) in enumerate(pairs)]
        for cp in cps:
            cp.start()
        for cp in cps:
            cp.wait()


def _mix_ln_fwd(ab, x, pb, w_out, w_gate, w_ple_t, gain, bias, alpha, after, name):
    S = x.shape[0]
    T = _tile(S, 256)

    def body(ab_ref, x_ref, p_ref, wo_hbm, wg_hbm, wp_hbm, g_ref, b_ref, after_ref,
             y_ref, yb_ref, xh_ref, rs_ref, gp_ref, pe_ref, wo, wg, wp, sems):
        del after_ref
        _load_resident(((wo_hbm, wo), (wg_hbm, wg), (wp_hbm, wp)), sems)
        x = x_ref[...]
        mix = jnp.dot(ab_ref[...], wo[...], preferred_element_type=F32)
        gp = jnp.dot(x.astype(BF16), wg[...], preferred_element_type=F32)
        pe = lax.dot_general(p_ref[...], wp[...], (((1,), (1,)), ((), ())), preferred_element_type=F32)
        z = alpha * x + mix + _sigmoid(gp) * pe
        mu = jnp.mean(z, axis=-1, keepdims=True)
        zc = z - mu
        var = jnp.mean(zc * zc, axis=-1, keepdims=True)
        rstd = lax.rsqrt(var + LN_EPS)
        xhat = zc * rstd
        y = xhat * g_ref[...] + b_ref[...]
        y_ref[...] = y
        yb_ref[...] = y.astype(BF16)
        xh_ref[...] = xhat
        rs_ref[...] = rstd
        gp_ref[...] = gp
        pe_ref[...] = pe

    row = pl.BlockSpec((T, D_MODEL), lambda i: (i, 0))
    vec = pl.BlockSpec((1, D_MODEL), lambda i: (0, 0))
    any_spec = pl.BlockSpec(memory_space=pl.ANY)
    f32_rows = jax.ShapeDtypeStruct((S, D_MODEL), F32)
    return pl.pallas_call(
        body,
        name=name,
        grid=(S // T,),
        in_specs=[row, row, pl.BlockSpec((T, PLE_DIM), lambda i: (i, 0)), any_spec, any_spec, any_spec, vec, vec, any_spec],
        out_specs=[row, row, row, pl.BlockSpec((T, 1), lambda i: (i, 0)), row, row],
        out_shape=[f32_rows, jax.ShapeDtypeStruct((S, D_MODEL), BF16), f32_rows, jax.ShapeDtypeStruct((S, 1), F32),
                   f32_rows, f32_rows],
        scratch_shapes=[pltpu.VMEM(w_out.shape, BF16), pltpu.VMEM(w_gate.shape, BF16), pltpu.VMEM(w_ple_t.shape, BF16),
                        pltpu.SemaphoreType.DMA((3,))],
        compiler_params=_params(("arbitrary",)),
    )(ab, x, pb, w_out, w_gate, w_ple_t, gain, bias, after)


def _ln_dmix_bwd(dy, xhat, rstd, gain, gp, pe, w_out, w_gate, alpha, after, name):
    S = dy.shape[0]
    T = _tile(S, 256)

    def body(dy_ref, xh_ref, rs_ref, g_ref, gp_ref, pe_ref, wo_hbm, wg_hbm, after_ref,
             dzb_ref, dpe_ref, dgp_ref, dab_ref, dx_ref, dgain_ref, dbias_ref, wo, wg, sems):
        del after_ref
        _load_resident(((wo_hbm, wo), (wg_hbm, wg)), sems)

        @pl.when(pl.program_id(0) == 0)
        def _():
            dgain_ref[...] = jnp.zeros_like(dgain_ref)
            dbias_ref[...] = jnp.zeros_like(dbias_ref)

        dy = dy_ref[...]
        xhat = xh_ref[...]
        dyg = dy * g_ref[...]
        c1 = jnp.mean(dyg, axis=-1, keepdims=True)
        c2 = jnp.mean(dyg * xhat, axis=-1, keepdims=True)
        dz = rs_ref[...] * (dyg - c1 - xhat * c2)
        dgain_ref[...] += jnp.sum(dy * xhat, axis=0, keepdims=True)
        dbias_ref[...] += jnp.sum(dy, axis=0, keepdims=True)
        sg = _sigmoid(gp_ref[...])
        dzb = dz.astype(BF16)
        dgp = (dz * pe_ref[...] * (sg * (1.0 - sg))).astype(BF16)
        nt = (((1,), (1,)), ((), ()))
        dzb_ref[...] = dzb
        dpe_ref[...] = (dz * sg).astype(BF16)
        dgp_ref[...] = dgp
        dab_ref[...] = lax.dot_general(dzb, wo[...], nt, preferred_element_type=F32)
        dx_ref[...] = lax.dot_general(dgp, wg[...], nt, preferred_element_type=F32) + alpha * dz

    row = pl.BlockSpec((T, D_MODEL), lambda i: (i, 0))
    vec = pl.BlockSpec((1, D_MODEL), lambda i: (0, 0))
    any_spec = pl.BlockSpec(memory_space=pl.ANY)
    bf16_rows = jax.ShapeDtypeStruct((S, D_MODEL), BF16)
    f32_rows = jax.ShapeDtypeStruct((S, D_MODEL), F32)
    return pl.pallas_call(
        body,
        name=name,
        grid=(S // T,),
        in_specs=[row, row, pl.BlockSpec((T, 1), lambda i: (i, 0)), vec, row, row, any_spec, any_spec, any_spec],
        out_specs=[row, row, row, row, row, vec, vec],
        out_shape=[bf16_rows, bf16_rows, bf16_rows, f32_rows, f32_rows,
                   jax.ShapeDtypeStruct((1, D_MODEL), F32), jax.ShapeDtypeStruct((1, D_MODEL), F32)],
        scratch_shapes=[pltpu.VMEM(w_out.shape, BF16), pltpu.VMEM(w_gate.shape, BF16), pltpu.SemaphoreType.DMA((2,))],
        compiler_params=_params(("arbitrary",)),
    )(dy, xhat, rstd, gain, gp, pe, w_out, w_gate, after)


def _loss_head(y, target):
    S = y.shape[0]
    T = _tile(S, 256)

    def body(y_ref, t_ref, dy_ref, l_ref):
        @pl.when(pl.program_id(0) == 0)
        def _():
            l_ref[...] = jnp.zeros_like(l_ref)

        err = y_ref[...] - t_ref[...]
        dy_ref[...] = err * (1.0 / D_MODEL)
        per_token = jnp.mean(err * err, axis=-1, keepdims=True)
        l_ref[...] += 0.5 * jnp.sum(per_token, axis=0, keepdims=True)

    row = pl.BlockSpec((T, D_MODEL), lambda i: (i, 0))
    return pl.pallas_call(
        body,
        name="loss_head",
        grid=(S // T,),
        in_specs=[row, row],
        out_specs=[row, pl.BlockSpec((8, LANES), lambda i: (0, 0))],
        out_shape=[jax.ShapeDtypeStruct((S, D_MODEL), F32), jax.ShapeDtypeStruct((8, LANES), F32)],
        compiler_params=_params(("arbitrary",)),
    )(y, target)


def _sum_slabs(r, name):
    _, R, C = r.shape
    T = _tile(R, 256)

    def body(r_ref, o_ref):
        acc = r_ref[0].astype(F32)
        for s in range(1, N_DEV):
            acc = acc + r_ref[s].astype(F32)
        o_ref[...] = acc

    return pl.pallas_call(
        body,
        name=name,
        grid=(R // T,),
        in_specs=[pl.BlockSpec((N_DEV, T, C), lambda i: (0, i, 0))],
        out_specs=pl.BlockSpec((T, C), lambda i: (i, 0)),
        out_shape=jax.ShapeDtypeStruct((R, C), F32),
        compiler_params=_params(("parallel",)),
    )(r)


def _adamw(w, g, m, v, name):
    R, C = w.shape
    T = _tile(R, 256)
    grid = (R // T,)
    blk = pl.BlockSpec((T, C), lambda i: (i, 0))

    def body(w_ref, g_ref, m_ref, v_ref, d_ref, nm_ref, nv_ref):
        g = g_ref[...]
        m = ADAM_B1 * m_ref[...] + (1.0 - ADAM_B1) * g
        v = ADAM_B2 * v_ref[...] + (1.0 - ADAM_B2) * jnp.square(g)
        m_hat = m / (1.0 - ADAM_B1 ** ADAM_STEP)
        v_hat = v / (1.0 - ADAM_B2 ** ADAM_STEP)
        d_ref[...] = -ADAM_LR * (m_hat / (jnp.sqrt(v_hat) + ADAM_EPS) + ADAM_WD * w_ref[...])
        nm_ref[...] = m
        nv_ref[...] = v

    shp = jax.ShapeDtypeStruct(w.shape, F32)
    return pl.pallas_call(
        body,
        name=name,
        grid=grid,
        in_specs=[blk] * 4,
        out_specs=[blk] * 3,
        out_shape=[shp] * 3,
        compiler_params=_params(("parallel",) * len(grid)),
    )(w, g, m, v)


def _mesh_pos():
    return lax.axis_index("x"), lax.axis_index("y"), lax.axis_index("c")


def _flip(pos, k):
    x, y, c = pos
    return (1 - x if k & 4 else x, 1 - y if k & 2 else y, 1 - c if k & 1 else c)


def _index(pos):
    return 4 * pos[0] + 2 * pos[1] + pos[2]


HBM_SPEC = pl.BlockSpec(memory_space=pltpu.HBM)
SEM_SPEC = pl.BlockSpec(memory_space=pltpu.SEMAPHORE)
ANY_SPEC = pl.BlockSpec(memory_space=pl.ANY)
SPLIT_EFFECT = pltpu.SideEffectType.DATAFLOW_SIDE_EFFECTING
GATHER_FLIPS = (1, 4, 2, 6)
CHIP_FLIPS = (4, 2, 6)
TOKEN = jax.ShapeDtypeStruct((8, LANES), F32)


def _hbm(a):
    return pltpu.with_memory_space_constraint(a, pltpu.HBM)


def _hbm_like(a):
    return pltpu.HBM(a.shape, a.dtype)


def _block_rows(ref, pos, r):
    return ref.at[:, pl.ds(_index(pos) * r, r), :]


def _gather_start(shards, after, name):
    n = len(shards)
    lands = [lax.empty((s.shape[0], N_DEV * s.shape[1], s.shape[2]), s.dtype) for s in shards]

    def body(*refs):
        ins, bufs = refs[:n], refs[n:2 * n]
        send_sems, recv_sems = refs[2 * n + 1], refs[2 * n + 2]
        token = refs[4 * n + 3]
        me = _mesh_pos()
        for a in range(n):
            for j, k in enumerate(GATHER_FLIPS):
                pltpu.make_async_remote_copy(
                    src_ref=ins[a], dst_ref=_block_rows(bufs[a], me, shards[a].shape[1]),
                    send_sem=send_sems.at[4 * a + j], recv_sem=recv_sems.at[4 * a + j],
                    device_id=_flip(me, k), device_id_type=MESH_ID).start()
        token[...] = jnp.zeros_like(token)

    outs = pl.pallas_call(
        body,
        name=name,
        in_specs=[HBM_SPEC] * (2 * n) + [ANY_SPEC],
        out_specs=[SEM_SPEC, SEM_SPEC] + [HBM_SPEC] * (2 * n) + [pl.BlockSpec(memory_space=pltpu.VMEM)],
        out_shape=[pltpu.SemaphoreType.DMA((4 * n,)), pltpu.SemaphoreType.DMA((4 * n,))]
        + [_hbm_like(s) for s in shards] + [_hbm_like(b) for b in lands] + [TOKEN],
        input_output_aliases={i: 2 + i for i in range(2 * n)},
        compiler_params=pltpu.CompilerParams(has_side_effects=SPLIT_EFFECT),
    )(*[_hbm(s) for s in shards], *[_hbm(b) for b in lands], after)
    return outs[0], outs[1], outs[2:2 + n], outs[2 + n:2 + 2 * n], outs[2 + 2 * n]


def _gather_wait(started, after, name):
    send_sems, recv_sems, shards, lands, _ = started
    n = len(shards)

    def body(*refs):
        ins, bufs = refs[:n], refs[n:2 * n]
        send_sems, recv_sems = refs[2 * n], refs[2 * n + 1]
        me = _mesh_pos()
        for a in range(n):
            for j, k in enumerate(GATHER_FLIPS):
                cp = pltpu.make_async_remote_copy(
                    src_ref=ins[a], dst_ref=_block_rows(bufs[a], _flip(me, k), shards[a].shape[1]),
                    send_sem=send_sems.at[4 * a + j], recv_sem=recv_sems.at[4 * a + j],
                    device_id=_flip(me, k), device_id_type=MESH_ID)
                cp.wait_send()
                cp.wait_recv()

    outs = pl.pallas_call(
        body,
        name=name,
        in_specs=[HBM_SPEC] * (2 * n) + [SEM_SPEC, SEM_SPEC, ANY_SPEC],
        out_specs=[HBM_SPEC] * (2 * n),
        out_shape=[_hbm_like(s) for s in shards] + [_hbm_like(b) for b in lands],
        input_output_aliases={i: i for i in range(2 * n)},
        compiler_params=pltpu.CompilerParams(has_side_effects=SPLIT_EFFECT),
    )(*shards, *lands, send_sems, recv_sems, after)
    return outs[:n], outs[n:]


def _gather_pass(shards, lands, name):
    n = len(shards)

    def body(*refs):
        ins, bufs = refs[:n], refs[n:2 * n]
        token = refs[3 * n]
        send_sems, recv_sems, local_sems = refs[3 * n + 1:]
        me = _mesh_pos()
        sibling = _flip(me, 1)

        def copy(a, j, block):
            rows = _block_rows(bufs[a], block, shards[a].shape[1])
            return pltpu.make_async_remote_copy(
                src_ref=rows, dst_ref=rows, send_sem=send_sems.at[3 * a + j], recv_sem=recv_sems.at[3 * a + j],
                device_id=sibling, device_id_type=MESH_ID)

        mine = [pltpu.make_async_copy(ins[a], _block_rows(bufs[a], me, shards[a].shape[1]), local_sems.at[a])
                for a in range(n)]
        sends = [copy(a, j, _flip(me, k)) for a in range(n) for j, k in enumerate(CHIP_FLIPS)]
        for cp in mine + sends:
            cp.start()
        for a in range(n):
            for j, k in enumerate(CHIP_FLIPS):
                copy(a, j, _flip(sibling, k)).wait_recv()
        for cp in sends:
            cp.wait_send()
        for cp in mine:
            cp.wait()
        token[...] = jnp.zeros_like(token)

    outs = pl.pallas_call(
        body,
        name=name,
        in_specs=[pl.BlockSpec(memory_space=pltpu.VMEM)] * n + [ANY_SPEC] * n,
        out_specs=[ANY_SPEC] * n + [pl.BlockSpec(memory_space=pltpu.VMEM)],
        out_shape=[jax.ShapeDtypeStruct(b.shape, b.dtype) for b in lands] + [TOKEN],
        scratch_shapes=[pltpu.SemaphoreType.DMA((3 * n,)), pltpu.SemaphoreType.DMA((3 * n,)), pltpu.SemaphoreType.DMA((n,))],
        input_output_aliases={n + i: i for i in range(n)},
        compiler_params=pltpu.CompilerParams(has_side_effects=True, vmem_limit_bytes=VMEM_LIMIT_BYTES),
    )(*shards, *lands)
    return outs[:n], outs[n]


def _gather_own(shards, lands, name):
    n = len(shards)

    def body(*refs):
        ins, bufs, local_sems = refs[:n], refs[n:2 * n], refs[3 * n]
        me = _mesh_pos()
        cps = [pltpu.make_async_copy(ins[a], _block_rows(bufs[a], me, shards[a].shape[1]), local_sems.at[a])
               for a in range(n)]
        for cp in cps:
            cp.start()
        for cp in cps:
            cp.wait()

    return pl.pallas_call(
        body,
        name=name,
        in_specs=[pl.BlockSpec(memory_space=pltpu.VMEM)] * n + [ANY_SPEC] * n,
        out_specs=[ANY_SPEC] * n,
        out_shape=[jax.ShapeDtypeStruct(b.shape, b.dtype) for b in lands],
        scratch_shapes=[pltpu.SemaphoreType.DMA((n,))],
        input_output_aliases={n + i: i for i in range(n)},
        compiler_params=pltpu.CompilerParams(has_side_effects=True, vmem_limit_bytes=VMEM_LIMIT_BYTES),
    )(*shards, *lands)


def _pass_copy(bufs, send_sems, recv_sems, a, j, block, sibling):
    rows = _block_rows(bufs[a], block, bufs[a].shape[1] // N_DEV)
    return pltpu.make_async_remote_copy(
        src_ref=rows, dst_ref=rows, send_sem=send_sems.at[3 * a + j], recv_sem=recv_sems.at[3 * a + j],
        device_id=sibling, device_id_type=MESH_ID)


def _pass_start(lands, after, name):
    n = len(lands)

    def body(*refs):
        bufs = refs[:n]
        send_sems, recv_sems = refs[n + 1], refs[n + 2]
        token = refs[2 * n + 3]
        me = _mesh_pos()
        for a in range(n):
            for j, k in enumerate(CHIP_FLIPS):
                _pass_copy(bufs, send_sems, recv_sems, a, j, _flip(me, k), _flip(me, 1)).start()
        token[...] = jnp.zeros_like(token)

    outs = pl.pallas_call(
        body,
        name=name,
        in_specs=[HBM_SPEC] * n + [ANY_SPEC],
        out_specs=[SEM_SPEC, SEM_SPEC] + [HBM_SPEC] * n + [pl.BlockSpec(memory_space=pltpu.VMEM)],
        out_shape=[pltpu.SemaphoreType.DMA((3 * n,)), pltpu.SemaphoreType.DMA((3 * n,))]
        + [_hbm_like(b) for b in lands] + [TOKEN],
        input_output_aliases={i: 2 + i for i in range(n)},
        compiler_params=pltpu.CompilerParams(has_side_effects=SPLIT_EFFECT),
    )(*[_hbm(b) for b in lands], after)
    return outs[0], outs[1], outs[2:2 + n], outs[2 + n]


def _pass_wait(started, after, name):
    send_sems, recv_sems, lands, _ = started
    n = len(lands)

    def body(*refs):
        bufs = refs[:n]
        send_sems, recv_sems = refs[n], refs[n + 1]
        me = _mesh_pos()
        sibling = _flip(me, 1)
        for a in range(n):
            for j, k in enumerate(CHIP_FLIPS):
                _pass_copy(bufs, send_sems, recv_sems, a, j, _flip(me, k), sibling).wait_send()
                _pass_copy(bufs, send_sems, recv_sems, a, j, _flip(sibling, k), sibling).wait_recv()

    return pl.pallas_call(
        body,
        name=name,
        in_specs=[HBM_SPEC] * n + [SEM_SPEC, SEM_SPEC, ANY_SPEC],
        out_specs=[HBM_SPEC] * n,
        out_shape=[_hbm_like(b) for b in lands],
        input_output_aliases={i: i for i in range(n)},
        compiler_params=pltpu.CompilerParams(has_side_effects=SPLIT_EFFECT),
    )(*lands, send_sems, recv_sems, after)


def _place_own(grads, lands, layer, name):
    n = len(grads)
    blocks = [(g.shape[0], g.shape[1] // N_DEV, g.shape[2]) for g in grads]

    def body(*refs):
        ins, bufs = refs[:n], refs[n:2 * n]
        stage, in_sems, out_sems = refs[3 * n:4 * n], refs[4 * n], refs[4 * n + 1]
        me = _mesh_pos()
        loads = [pltpu.make_async_copy(_block_rows(ins[a], me, blocks[a][1]), stage[a], in_sems.at[a]) for a in range(n)]
        stores = [pltpu.make_async_copy(stage[a], bufs[a].at[_index(me), layer], out_sems.at[a]) for a in range(n)]
        for cp in loads:
            cp.start()
        for a in range(n):
            loads[a].wait()
            stores[a].start()
        for cp in stores:
            cp.wait()

    return pl.pallas_call(
        body,
        name=name,
        in_specs=[ANY_SPEC] * (2 * n),
        out_specs=[ANY_SPEC] * n,
        out_shape=[jax.ShapeDtypeStruct(b.shape, b.dtype) for b in lands],
        scratch_shapes=[pltpu.VMEM(blk, g.dtype) for blk, g in zip(blocks, grads)]
        + [pltpu.SemaphoreType.DMA((n,)), pltpu.SemaphoreType.DMA((n,))],
        input_output_aliases={n + i: i for i in range(n)},
        compiler_params=pltpu.CompilerParams(has_side_effects=True, vmem_limit_bytes=VMEM_LIMIT_BYTES),
    )(*grads, *lands)


def _scatter_copy(ins, bufs, send_sems, recv_sems, a, k, r, layer, me, slab):
    peer = _flip(me, k)
    return pltpu.make_async_remote_copy(
        src_ref=_block_rows(ins[a], peer, r), dst_ref=bufs[a].at[_index(slab), layer],
        send_sem=send_sems.at[7 * a + k - 1], recv_sem=recv_sems.at[7 * a + k - 1],
        device_id=peer, device_id_type=MESH_ID)


def _scatter_start(grads, lands, layer, name):
    n = len(grads)

    def body(*refs):
        ins, bufs = refs[:n], refs[n:2 * n]
        send_sems, recv_sems = refs[2 * n], refs[2 * n + 1]
        token = refs[4 * n + 2]
        me = _mesh_pos()
        for a in range(n):
            for k in range(1, N_DEV):
                _scatter_copy(ins, bufs, send_sems, recv_sems, a, k, grads[a].shape[1] // N_DEV, layer, me, me).start()
        token[...] = jnp.zeros_like(token)

    outs = pl.pallas_call(
        body,
        name=name,
        in_specs=[HBM_SPEC] * (2 * n),
        out_specs=[SEM_SPEC, SEM_SPEC] + [HBM_SPEC] * (2 * n) + [pl.BlockSpec(memory_space=pltpu.VMEM)],
        out_shape=[pltpu.SemaphoreType.DMA((7 * n,)), pltpu.SemaphoreType.DMA((7 * n,))]
        + [_hbm_like(g) for g in grads] + [_hbm_like(b) for b in lands] + [TOKEN],
        input_output_aliases={i: 2 + i for i in range(2 * n)},
        compiler_params=pltpu.CompilerParams(has_side_effects=SPLIT_EFFECT),
    )(*[_hbm(g) for g in grads], *[_hbm(b) for b in lands])
    return outs[0], outs[1], outs[2:2 + n], outs[2 + n:2 + 2 * n], outs[2 + 2 * n]


def _scatter_wait(started, layer, after, name):
    send_sems, recv_sems, grads, lands, _ = started
    n = len(grads)

    def body(*refs):
        ins, bufs = refs[:n], refs[n:2 * n]
        send_sems, recv_sems = refs[2 * n], refs[2 * n + 1]
        me = _mesh_pos()
        for a in range(n):
            for k in range(1, N_DEV):
                cp = _scatter_copy(ins, bufs, send_sems, recv_sems, a, k, grads[a].shape[1] // N_DEV, layer, me, _flip(me, k))
                cp.wait_send()
                cp.wait_recv()

    outs = pl.pallas_call(
        body,
        name=name,
        in_specs=[HBM_SPEC] * (2 * n) + [SEM_SPEC, SEM_SPEC, ANY_SPEC],
        out_specs=[HBM_SPEC] * (2 * n),
        out_shape=[_hbm_like(g) for g in grads] + [_hbm_like(b) for b in lands],
        input_output_aliases={i: i for i in range(2 * n)},
        compiler_params=pltpu.CompilerParams(has_side_effects=SPLIT_EFFECT),
    )(*grads, *lands, send_sems, recv_sems, after)
    return outs[n:]


def _allreduce_small(vec):
    R, C = vec.shape

    def body(v_ref, o_ref, buf, send_sems, recv_sems):
        me = _mesh_pos()
        buf[_index(me)] = v_ref[...]
        sends = []
        for k in range(1, N_DEV):
            sends.append(pltpu.make_async_remote_copy(
                src_ref=buf.at[_index(me)], dst_ref=buf.at[_index(me)],
                send_sem=send_sems.at[k - 1], recv_sem=recv_sems.at[k - 1],
                device_id=_flip(me, k), device_id_type=MESH_ID))
        for cp in sends:
            cp.start()
        for cp in sends:
            cp.wait_recv()
        for cp in sends:
            cp.wait_send()
        acc = buf[0]
        for s in range(1, N_DEV):
            acc = acc + buf[s]
        o_ref[...] = acc

    return pl.pallas_call(
        body,
        name="allreduce_small",
        in_specs=[pl.BlockSpec(memory_space=pltpu.VMEM)],
        out_specs=pl.BlockSpec(memory_space=pltpu.VMEM),
        out_shape=jax.ShapeDtypeStruct((R, C), F32),
        scratch_shapes=[pltpu.VMEM((N_DEV, R, C), F32), pltpu.SemaphoreType.DMA((7,)), pltpu.SemaphoreType.DMA((7,))],
        compiler_params=pltpu.CompilerParams(has_side_effects=True, vmem_limit_bytes=VMEM_LIMIT_BYTES),
    )(vec)


def _pack_small(parts):
    flat = jnp.concatenate([p.reshape(-1) for p in parts])
    n = flat.shape[0]
    rows = -(-n // SMALL_COLS)
    rows = -(-rows // 8) * 8
    return jnp.pad(flat, (0, rows * SMALL_COLS - n)).reshape(rows, SMALL_COLS)


def _unpack_small(packed, like):
    flat = packed.reshape(-1)
    out, pos = [], 0
    for p in like:
        out.append(flat[pos:pos + p.size].reshape(p.shape))
        pos += p.size
    return out


def kernel(x, p, w_in, b_in, w_out, attn_sinks, rel_bias, w_pool, pool_scale, w_ple, w_gate_ple, ln_gain, ln_bias, loss_target, m_w_in, m_b_in, m_w_out, m_attn_sinks, m_rel_bias, m_w_pool, m_pool_scale, m_w_ple, m_w_gate_ple, m_ln_gain, m_ln_bias, v_w_in, v_b_in, v_w_out, v_attn_sinks, v_rel_bias, v_w_pool, v_pool_scale, v_w_ple, v_w_gate_ple, v_ln_gain, v_ln_bias):
    L = w_in.shape[0]
    S = x.shape[1]
    alpha = (2.0 * L) ** 0.25
    bucket_np, masks_np, window_np = _band_constants()
    bucket, masks, window = jnp.asarray(bucket_np), jnp.asarray(masks_np), jnp.asarray(window_np)

    @functools.lru_cache(maxsize=None)
    def shards_of(l):
        return (jnp.swapaxes(w_in[l], 0, 1).astype(BF16)[None], w_out[l].astype(BF16)[None],
                w_gate_ple[l].astype(BF16)[None], jnp.swapaxes(w_ple[l], 0, 1).astype(BF16)[None], w_pool[l].astype(BF16))

    def gathered(started, after, tag):
        shards, lands = _gather_wait(started, after, name=f"gather_wait_{tag}")
        return _gather_pass(shards, lands, name=f"gather_pass_{tag}")

    bias = _bias_build(rel_bias, bucket).reshape(N_KV_HEADS, BIAS_ROWS, BLOCK)

    xs = x[0]
    xb = xs.astype(BF16)
    first_groups = ((0, 4), (1, 2, 3))
    token, first_started = rel_bias, []
    for tag, idxs in zip("ab", first_groups):
        first_started.append(_gather_start([shards_of(0)[i] for i in idxs], token, name=f"gather_start_0{tag}"))
        token = first_started[-1][4]
    started = {1: _gather_start(shards_of(1), token, name="gather_start_1")} if L > 1 else {}
    saved = []
    for l in range(L):
        pb = p[l, 0].astype(BF16)
        sinks_l = attn_sinks[l]
        scale_l = pool_scale[l].reshape(1, POOL_WIDTH)
        bias_l = b_in[l].reshape(1, IN_COLS)
        if l == 0:
            (w_in_f, w_pool_g), _ = gathered(first_started[0], started[1][4] if L > 1 else xb, "0a")
            w_in_t = w_in_f[0]
            h = _matmul(xb, w_in_t, tb=True, tm=512, tn=2176, tk=2048, out_dtype=F32, bias=bias_l, name=f"in_proj_{l}")
            (w_out_f, w_gate_f, w_ple_f), _ = gathered(first_started[1], h, "0b")
            w_out_g, w_gate_g, w_ple_t = w_out_f[0], w_gate_f[0], w_ple_f[0]
        else:
            w_in_t, w_out_g, w_gate_g, w_ple_t, w_pool_g = weights
            h = _matmul(xb, w_in_t, tb=True, tm=512, tn=2176, tk=2048, out_dtype=F32, bias=bias_l, name=f"in_proj_{l}")
        weights = (w_in_t, w_out_g, w_gate_g, w_ple_t, w_pool_g)
        ab = _attn_fwd(h, bias, masks, sinks_l, name=f"attn_fwd_{l}")
        ab = _pool_fwd(h, ab, w_pool_g, scale_l, name=f"pool_fwd_{l}")
        pin, passing = ab, None
        if 1 <= l and l + 1 < L:
            shards, lands = _gather_wait(started[l + 1], ab, name=f"gather_wait_{l + 1}")
            passing = _pass_start(_gather_own(shards, lands, name=f"gather_own_{l + 1}"), ab, name=f"pass_start_{l + 1}")
            pin = passing[3]
        if l + 2 < L:
            started[l + 2] = _gather_start(shards_of(l + 2), pin, name=f"gather_start_{l + 2}")
            pin = started[l + 2][4]
        y, yb, xhat, rstd, gp, pe = _mix_ln_fwd(ab, xs, pb, w_out_g, w_gate_g, w_ple_t, ln_gain[l].reshape(1, D_MODEL),
                                                ln_bias[l].reshape(1, D_MODEL), alpha, pin, name=f"mix_ln_fwd_{l}")
        saved.append((xb, pb, h, gp, pe, ab, xhat, rstd, sinks_l, scale_l, weights))
        xs, xb = y, yb
        if l + 1 < L:
            if passing is None:
                full, _ = gathered(started[l + 1], yb, l + 1)
            else:
                full = _pass_wait(passing, yb, name=f"pass_wait_{l + 1}")
            weights = (full[0][0], full[1][0], full[2][0], full[3][0], full[4])

    dy, loss_tile = _loss_head(xs, loss_target[0])

    dbias = jnp.zeros((N_KV_HEADS, BIAS_ROWS, BLOCK), F32)
    sh0 = shards_of(0)
    lands_a = [lax.empty((N_DEV, L) + sh0[i].shape, BF16) for i in (1, 2, 3)]
    lands_b = [lax.empty((N_DEV, L) + sh0[i].shape, BF16) for i in (0, 4)]
    g_b_in, g_sinks, g_scale, g_gain, g_beta = [], [], [], [], []
    pend_a = pend_b = None

    def scatter(grads, lands, pending, l, tag):
        if pending:
            lands = _scatter_wait(pending[0], pending[1], grads[0], name=f"scatter_wait_{pending[1]}{tag}")
        lands = _place_own(grads, lands, l, name=f"place_own_{l}{tag}")
        return _scatter_start(grads, lands, l, name=f"scatter_start_{l}{tag}"), l

    for l in reversed(range(L)):
        xb, pb, h, gp, pe, ab, xhat, rstd, sinks_l, scale_l, weights = saved[l]
        w_in_t, w_out_g, w_gate_g, w_ple_t, w_pool_g = weights
        dzb, dpe, dgp, dab, dx, dgain, dbeta = _ln_dmix_bwd(
            dy, xhat, rstd, ln_gain[l].reshape(1, D_MODEL), gp, pe, w_out_g, w_gate_g, alpha,
            pend_b[0][4] if pend_b else rel_bias, name=f"ln_dmix_bwd_{l}")
        g_w_out = _matmul(ab, dzb, ta=True, tm=512, tn=1024, tk=4096, out_dtype=BF16, name=f"dw_out_{l}")
        g_w_gate = _matmul(xb, dgp, ta=True, tm=512, tn=1024, tk=4096, out_dtype=BF16, name=f"dw_gate_{l}")
        g_w_ple_t = _matmul(dpe, pb, ta=True, tm=1024, tn=256, tk=1024, out_dtype=BF16, name=f"dw_ple_{l}")
        pend_a = scatter([g_w_out[None], g_w_gate[None], g_w_ple_t[None]], lands_a, pend_a, l, "a")
        dh, dwp, dsc, db_pool = _pool_bwd(h, dab, w_pool_g, scale_l, pend_a[0][4], name=f"pool_bwd_{l}")
        dh, dbias, dsink, db_attn = _attn_bwd(h, dab, dh, bias, masks, sinks_l, dbias, name=f"attn_bwd_{l}")
        g_w_in_t = _matmul(dh, xb, ta=True, tm=256, tn=1024, tk=4096, out_dtype=BF16, name=f"dw_in_{l}")
        pend_b = scatter([g_w_in_t[None], dwp.astype(BF16)], lands_b, pend_b, l, "b")
        g_b_in.append(jnp.concatenate([db_attn[0], db_pool[0, U_OFF - DH_POOL_COLS:]]))
        dy = _matmul(dh, w_in_t, tm=512, tn=1024, tk=4352, out_dtype=F32, add=dx, after=pend_b[0][4], name=f"dx_in_{l}")
        g_sinks.append(dsink[:, :, 0].reshape(N_HEADS))
        g_scale.append(dsc.reshape(POOL_WIDTH))
        g_gain.append(dgain.reshape(D_MODEL))
        g_beta.append(dbeta.reshape(D_MODEL))
    grad_x = dy[None]
    for lst in (g_b_in, g_sinks, g_scale, g_gain, g_beta):
        lst.reverse()
    g_rel = _bias_bwd(dbias.reshape((N_HEADS,) + BAND), bucket, window)[:, :N_HEADS]

    def big(w, g, m, v, name):
        shape = w.shape
        two_d = (shape[0] * shape[1], shape[2]) if len(shape) == 3 else (shape[0] * shape[1] * shape[2], shape[3])
        d, nm, nv = _adamw(w.reshape(two_d), g.reshape(two_d), m.reshape(two_d), v.reshape(two_d), name=name)
        return d.reshape(shape), nm.reshape(shape), nv.reshape(shape)

    r_out, r_gate, r_ple = _scatter_wait(pend_a[0], pend_a[1], dy, name="scatter_wait_0a")
    small_like = [b_in, attn_sinks, rel_bias, pool_scale, ln_gain, ln_bias]
    small_g = _allreduce_small(_pack_small([
        jnp.stack(g_b_in).reshape(L, IN_COLS), jnp.stack(g_sinks), g_rel, jnp.stack(g_scale), jnp.stack(g_gain),
        jnp.stack(g_beta), loss_tile[0, :1]]))
    grad_w_out = _sum_slabs(r_out.reshape(N_DEV, L * 256, D_MODEL), name="sum_w_out").reshape(L, 256, D_MODEL)
    grad_w_gate = _sum_slabs(r_gate.reshape(N_DEV, L * 256, D_MODEL), name="sum_w_gate").reshape(L, 256, D_MODEL)
    gt_ple = _sum_slabs(r_ple.reshape(N_DEV, L * 256, PLE_DIM), name="sum_w_ple")
    grad_w_ple = jnp.swapaxes(gt_ple.reshape(L, 256, PLE_DIM), 1, 2)
    upd_out = big(w_out, grad_w_out, m_w_out, v_w_out, "adamw_w_out")
    upd_ple = big(w_ple, grad_w_ple, m_w_ple, v_w_ple, "adamw_w_ple")
    upd_gate = big(w_gate_ple, grad_w_gate, m_w_gate_ple, v_w_gate_ple, "adamw_w_gate")

    r_in, r_pool = _scatter_wait(pend_b[0], pend_b[1], upd_gate[0], name="scatter_wait_0b")
    gt_in = _sum_slabs(r_in.reshape(N_DEV, L * 544, D_MODEL), name="sum_w_in")
    grad_w_in = jnp.swapaxes(gt_in.reshape(L, 544, D_MODEL), 1, 2)
    grad_w_pool = _sum_slabs(r_pool.reshape(N_DEV, L * 4 * 32, 256), name="sum_w_pool").reshape(L, 4, 32, 256)
    upd_in = tuple(jnp.swapaxes(u, 1, 2) for u in big(
        jnp.swapaxes(w_in, 1, 2), gt_in.reshape(L, 544, D_MODEL), jnp.swapaxes(m_w_in, 1, 2), jnp.swapaxes(v_w_in, 1, 2),
        "adamw_w_in"))
    upd_pool = big(w_pool, grad_w_pool, m_w_pool, v_w_pool, "adamw_w_pool")

    zero1 = jnp.zeros((1,), F32)
    sw = _pack_small(small_like + [zero1])
    sm = _pack_small([m_b_in, m_attn_sinks, m_rel_bias, m_pool_scale, m_ln_gain, m_ln_bias, zero1])
    sv = _pack_small([v_b_in, v_attn_sinks, v_rel_bias, v_pool_scale, v_ln_gain, v_ln_bias, zero1])
    sd, snm, snv = _adamw(sw, small_g, sm, sv, name="adamw_small")
    like = small_like + [zero1]
    sg_parts = _unpack_small(small_g, like)
    sd_parts, snm_parts, snv_parts = _unpack_small(sd, like), _unpack_small(snm, like), _unpack_small(snv, like)
    loss = sg_parts[6][0]

    def assemble(big_parts, small_parts):
        w_in_, w_out_, w_pool_, w_ple_, w_gate_ = big_parts
        b_in_, sinks_, rel_, scale_, gain_, beta_ = small_parts[:6]
        return [w_in_, b_in_, w_out_, sinks_, rel_, w_pool_, scale_, w_ple_, w_gate_, gain_, beta_]

    grads = assemble([grad_w_in, grad_w_out, grad_w_pool, grad_w_ple, grad_w_gate], sg_parts)
    ups = [upd_in, upd_out, upd_pool, upd_ple, upd_gate]
    deltas = assemble([u[0] for u in ups], sd_parts)
    new_m = assemble([u[1] for u in ups], snm_parts)
    new_v = assemble([u[2] for u in ups], snv_parts)
    return (loss, grad_x, *grads, *deltas, *new_m, *new_v)
```

```python
import functools
import math

import numpy as np
import jax
import jax.numpy as jnp
from jax import lax
from jax.experimental import pallas as pl
from jax.experimental.pallas import tpu as pltpu

F32 = jnp.float32
BF16 = jnp.bfloat16

D_MODEL = 2048
PLE_DIM = 256
ATTN_WIDTH = 1024
POOL_WIDTH = 1024
HEAD_DIM = 64
N_HEADS = 16
N_KV_HEADS = 2
KV_GROUP = 8
WINDOW = 128
BLOCK = 128
POOL_WINDOWS = (2, 4, 8, 16)
POOL_GROUP_DIM = 256
POOL_HALO = 16
REL_BUCKETS = 32
REL_MAX_DIST = 128
LN_EPS = 1e-5
KV_COLS = N_KV_HEADS * HEAD_DIM
IN_COLS = 4352
Q_OFF, KV_OFF, GA_OFF, U_OFF, GB_OFF = 0, 1024, 1280, 2304, 3328
ATTN_SCALE = 1.0 / math.sqrt(HEAD_DIM)
NEG_BIG = -1e30
LANES = 128

ADAM_LR = 0.001
ADAM_B1 = 0.9
ADAM_B2 = 0.999
ADAM_EPS = 1e-08
ADAM_WD = 0.01
ADAM_STEP = 10

N_DEV = 8
MESH_ID = pl.DeviceIdType.MESH
VMEM_LIMIT_BYTES = 52 * 1024 * 1024
SMALL_COLS = 1024


def _params(sem=None):
    return pltpu.CompilerParams(dimension_semantics=sem, vmem_limit_bytes=VMEM_LIMIT_BYTES)


def _sigmoid(x):
    return 1.0 / (1.0 + jnp.exp(-x))


def _tile(n, pref, unit=16):
    if n <= pref:
        return n
    t = pref - pref % unit
    while n % t:
        t -= unit
    assert t > 0, (n, pref)
    return t


def _matmul(a, b, *, name, ta=False, tb=False, tm, tn, tk, out_dtype, bias=None, add=None, add_scale=1.0, after=None):
    M, K = (a.shape[1], a.shape[0]) if ta else a.shape
    N = b.shape[0] if tb else b.shape[1]
    assert (b.shape[1] if tb else b.shape[0]) == K
    tm, tn, tk = _tile(M, tm), _tile(N, tn), _tile(K, tk)
    nm, nn, nk = M // tm, N // tn, K // tk
    a_spec = pl.BlockSpec((tk, tm), lambda j, i, k: (k, i)) if ta else pl.BlockSpec((tm, tk), lambda j, i, k: (i, k))
    b_spec = pl.BlockSpec((tn, tk), lambda j, i, k: (j, k)) if tb else pl.BlockSpec((tk, tn), lambda j, i, k: (k, j))
    dims = (((0 if ta else 1,), (1 if tb else 0,)), ((), ()))
    operands, in_specs = [a, b], [a_spec, b_spec]
    if bias is not None:
        operands.append(bias)
        in_specs.append(pl.BlockSpec((1, tn), lambda j, i, k: (0, j)))
    if add is not None:
        operands.append(add)
        in_specs.append(pl.BlockSpec((tm, tn), lambda j, i, k: (i, j)))
    if after is not None:
        operands.append(after)
        in_specs.append(pl.BlockSpec(memory_space=pl.ANY))

    def body(*refs):
        a_ref, b_ref = refs[0], refs[1]
        pos = 2
        bias_ref = add_ref = None
        if bias is not None:
            bias_ref = refs[pos]
            pos += 1
        if add is not None:
            add_ref = refs[pos]
            pos += 1
        if after is not None:
            pos += 1
        o_ref = refs[pos]
        part = lax.dot_general(a_ref[...].astype(BF16), b_ref[...].astype(BF16), dims, preferred_element_type=F32)

        def finish(acc):
            if bias_ref is not None:
                acc = acc + bias_ref[...]
            if add_ref is not None:
                acc = acc + add_scale * add_ref[...].astype(F32)
            o_ref[...] = acc.astype(out_dtype)

        if nk == 1:
            finish(part)
        else:
            acc_ref = refs[pos + 1]
            k = pl.program_id(2)

            @pl.when(k == 0)
            def _():
                acc_ref[...] = part

            @pl.when(k > 0)
            def _():
                acc_ref[...] += part

            @pl.when(k == nk - 1)
            def _():
                finish(acc_ref[...])

    return pl.pallas_call(
        body,
        name=name,
        grid=(nn, nm, nk),
        in_specs=in_specs,
        out_specs=pl.BlockSpec((tm, tn), lambda j, i, k: (i, j)),
        out_shape=jax.ShapeDtypeStruct((M, N), out_dtype),
        scratch_shapes=[pltpu.VMEM((tm, tn), F32)] if nk > 1 else [],
        compiler_params=_params(("parallel", "parallel", "arbitrary")),
    )(*operands)


BAND = (2 * BLOCK, BLOCK)
BIAS_ROWS = KV_GROUP * 2 * BLOCK


def _band_constants():
    qq = np.arange(BLOCK)[None, :]
    kk = np.arange(2 * BLOCK)[:, None]
    dist = qq + BLOCK - kk
    in_window = (dist >= 0) & (dist < WINDOW)
    max_exact = REL_BUCKETS // 2
    d = np.maximum(dist, 0)
    d_f = np.maximum(d, 1).astype(np.float32)
    large = max_exact + (
        np.log(d_f / np.float32(max_exact)) / np.float32(math.log(REL_MAX_DIST / max_exact)) * np.float32(REL_BUCKETS - max_exact)
    ).astype(np.int32)
    large = np.minimum(large, REL_BUCKETS - 1)
    bucket = np.where(d < max_exact, d, large).astype(np.int32)
    bucket = np.where(in_window, bucket, 0).astype(np.int32)
    first = in_window & (kk >= BLOCK)
    masks = np.stack([first, in_window]).astype(np.float32)
    return bucket, masks, in_window.astype(np.float32)


def _bias_build(rel_bias, bucket):
    def body(rb_ref, bkt_ref, o_ref):
        h = pl.program_id(0)
        bkt = bkt_ref[...]

        def step(b, acc):
            return jnp.where(bkt == b, rb_ref[b, h], acc)

        o_ref[0] = lax.fori_loop(0, REL_BUCKETS, step, jnp.zeros(BAND, F32))

    return pl.pallas_call(
        body,
        name="bias_build",
        grid=(N_HEADS,),
        in_specs=[pl.BlockSpec(memory_space=pltpu.SMEM), pl.BlockSpec(BAND, lambda h: (0, 0))],
        out_specs=pl.BlockSpec((1,) + BAND, lambda h: (h, 0, 0)),
        out_shape=jax.ShapeDtypeStruct((N_HEADS,) + BAND, F32),
        compiler_params=_params(("arbitrary",)),
    )(rel_bias, bucket)


def _bias_bwd(dbias, bucket, window):
    def body(db_ref, bkt_ref, win_ref, o_ref):
        h = pl.program_id(0)

        @pl.when(h == 0)
        def _():
            o_ref[...] = jnp.zeros_like(o_ref)

        bkt = bkt_ref[...]
        x = jnp.where(win_ref[...] > 0.5, db_ref[0], 0.0)
        row = lax.broadcasted_iota(jnp.int32, (REL_BUCKETS, LANES), 0)
        col = lax.broadcasted_iota(jnp.int32, (REL_BUCKETS, LANES), 1)

        def step(b, acc):
            s = jnp.sum(jnp.where(bkt == b, x, 0.0), axis=0, keepdims=True)
            return acc + jnp.where(row == b, s, 0.0)

        per_lane = lax.fori_loop(0, REL_BUCKETS, step, jnp.zeros((REL_BUCKETS, LANES), F32))
        o_ref[...] += jnp.where(col == h, jnp.sum(per_lane, axis=1, keepdims=True), 0.0)

    return pl.pallas_call(
        body,
        name="bias_bwd",
        grid=(N_HEADS,),
        in_specs=[
            pl.BlockSpec((1,) + BAND, lambda h: (h, 0, 0)),
            pl.BlockSpec(BAND, lambda h: (0, 0)),
            pl.BlockSpec(BAND, lambda h: (0, 0)),
        ],
        out_specs=pl.BlockSpec((REL_BUCKETS, LANES), lambda h: (0, 0)),
        out_shape=jax.ShapeDtypeStruct((REL_BUCKETS, LANES), F32),
        compiler_params=_params(("arbitrary",)),
    )(dbias, bucket, window)


def _lane_lo(shape):
    return lax.broadcasted_iota(jnp.int32, shape, 1) < HEAD_DIM


def _row_lo(shape):
    return lax.broadcasted_iota(jnp.int32, shape, 0) < HEAD_DIM


def _dup_heads(x):
    r = pltpu.roll(x, HEAD_DIM, axis=1)
    lo = _lane_lo(x.shape)
    return jnp.where(lo, x, r), jnp.where(lo, r, x)


def _kv_operands(kvp_ref, kvc_ref):
    kvp, kvc = kvp_ref[...], kvc_ref[...]
    k2 = jnp.concatenate([kvp[:, :KV_COLS], kvc[:, :KV_COLS]], axis=0)
    v2 = jnp.concatenate([kvp[:, KV_COLS:], kvc[:, KV_COLS:]], axis=0)
    return _dup_heads(k2), _dup_heads(v2)


def _head_probs(k_r, qs_t, bias, mask, sink):
    s = jnp.dot(k_r, qs_t, preferred_element_type=F32) * ATTN_SCALE + bias
    s = jnp.where(mask, s, NEG_BIG)
    m = jnp.maximum(jnp.max(s, axis=0, keepdims=True), sink)
    e = jnp.exp(s - m)
    e_sink = jnp.exp(sink - m)
    inv = 1.0 / (jnp.sum(e, axis=0, keepdims=True) + e_sink)
    return e * inv, e_sink * inv


def _gate_cols(ga_refs, pair):
    off = LANES * (pair % 2)
    return ga_refs[pair // 2][:, off:off + LANES]


def _attn_specs(order):
    return [
        pl.BlockSpec((BLOCK, ATTN_WIDTH), lambda t: (order(t), Q_OFF // ATTN_WIDTH)),
        pl.BlockSpec((BLOCK, 2 * KV_COLS), lambda t: (order(t), KV_OFF // (2 * KV_COLS))),
        pl.BlockSpec((BLOCK, 2 * KV_COLS), lambda t: (jnp.maximum(order(t) - 1, 0), KV_OFF // (2 * KV_COLS))),
    ] + [
        pl.BlockSpec((BLOCK, 256), functools.partial(lambda t, c: (order(t), GA_OFF // 256 + c), c=c)) for c in range(4)
    ] + [
        pl.BlockSpec((N_KV_HEADS, BIAS_ROWS, BLOCK), lambda t: (0, 0, 0)),
        pl.BlockSpec((None,) + BAND, lambda t: (jnp.minimum(order(t), 1), 0, 0)),
        pl.BlockSpec(memory_space=pltpu.SMEM),
    ]


def _attn_fwd(h, bias, masks, sinks, name):
    S = h.shape[0]
    nb = S // BLOCK

    def body(q_ref, kvc_ref, kvp_ref, ga0, ga1, ga2, ga3, bias_ref, mask_ref, sink_ref, o_ref):
        kd, vd = _kv_operands(kvp_ref, kvc_ref)
        mask = mask_ref[...] > 0.5
        lo = _row_lo((LANES, BLOCK))
        for g in range(N_KV_HEADS):
            k_r = kd[g].astype(BF16)
            v_t = vd[g].T.astype(BF16)
            for pr in range(KV_GROUP // 2):
                pair = (KV_GROUP // 2) * g + pr
                qp_t = q_ref[:, LANES * pair:LANES * (pair + 1)].T
                outs = []
                for hh in range(2):
                    j = 2 * pr + hh
                    qs_t = jnp.where(lo if hh == 0 else ~lo, qp_t, 0.0).astype(BF16)
                    p, _ = _head_probs(k_r, qs_t, bias_ref[g, 2 * BLOCK * j:2 * BLOCK * (j + 1), :], mask,
                                       sink_ref[KV_GROUP * g + j])
                    outs.append(jnp.dot(v_t, p.astype(BF16), preferred_element_type=F32))
                ga = _gate_cols((ga0, ga1, ga2, ga3), pair)
                o_ref[:, LANES * pair:LANES * (pair + 1)] = (
                    jnp.where(lo, outs[0], outs[1]).T * (ga * _sigmoid(ga))).astype(BF16)

    return pl.pallas_call(
        body,
        name=name,
        grid=(nb,),
        in_specs=_attn_specs(lambda t: t),
        out_specs=pl.BlockSpec((BLOCK, ATTN_WIDTH), lambda t: (t, 0)),
        out_shape=jax.ShapeDtypeStruct((S, ATTN_WIDTH + POOL_WIDTH), BF16),
        compiler_params=_params(("arbitrary",)),
    )(h, h, h, h, h, h, h, bias, masks, sinks)


DH_ATTN_COLS = U_OFF


def _attn_bwd(h, dab, dh, bias, masks, sinks, dbias_in, name):
    S = h.shape[0]
    nb = S // BLOCK

    def order(t):
        return nb - 1 - t

    def body(q_ref, kvc_ref, kvp_ref, ga0, ga1, ga2, ga3, bias_ref, mask_ref, sink_ref, da_ref, dbin_ref, dh_in_ref,
             dh_ref, dbias_ref, dsink_ref, db_ref, carry_scr):
        del dh_in_ref
        t = pl.program_id(0)

        @pl.when(t == 0)
        def _():
            dbias_ref[...] = dbin_ref[...]
            dsink_ref[...] = jnp.zeros_like(dsink_ref)
            db_ref[...] = jnp.zeros_like(db_ref)
            carry_scr[...] = jnp.zeros_like(carry_scr)

        kd, vd = _kv_operands(kvp_ref, kvc_ref)
        mask = mask_ref[...] > 0.5
        lo = _lane_lo((BLOCK, LANES))
        lo_t = _row_lo((LANES, BLOCK))
        dk_tot, dv_tot = [], []
        for g in range(N_KV_HEADS):
            k_t, k_r = kd[g].T.astype(BF16), kd[g].astype(BF16)
            v_t, v_r = vd[g].T.astype(BF16), vd[g].astype(BF16)
            pbs, dsbs, qss, doss = [], [], [], []
            for pr in range(KV_GROUP // 2):
                pair = (KV_GROUP // 2) * g + pr
                cols = slice(LANES * pair, LANES * (pair + 1))
                qp = q_ref[:, cols]
                qp_t = qp.T
                ga = _gate_cols((ga0, ga1, ga2, ga3), pair)
                sg = _sigmoid(ga)
                da = da_ref[:, cols]
                do_p = da * (ga * sg)
                do_t = do_p.T
                outs, dqs = [], []
                for hh in range(2):
                    j = 2 * pr + hh
                    rows = slice(2 * BLOCK * j, 2 * BLOCK * (j + 1))
                    half_t = lo_t if hh == 0 else ~lo_t
                    qs_t = jnp.where(half_t, qp_t, 0.0).astype(BF16)
                    p, p_sink = _head_probs(k_r, qs_t, bias_ref[g, rows, :], mask, sink_ref[KV_GROUP * g + j])
                    pb = p.astype(BF16)
                    outs.append(jnp.dot(v_t, pb, preferred_element_type=F32))
                    dos_t = jnp.where(half_t, do_t, 0.0).astype(BF16)
                    dp = jnp.dot(v_r, dos_t, preferred_element_type=F32)
                    dsum = jnp.sum(p * dp, axis=0, keepdims=True)
                    ds = p * (dp - dsum)
                    dbias_ref[g, rows, :] += ds
                    tot = jnp.sum(-(p_sink * dsum), axis=1, keepdims=True)
                    dsink_ref[g, j:j + 1, :] += jnp.broadcast_to(tot, (1, LANES))
                    dsb = ds.astype(BF16)
                    dqs.append(jnp.dot(k_t, dsb, preferred_element_type=F32))
                    pbs.append(pb)
                    dsbs.append(dsb)
                qss += [jnp.where(lo, qp, 0.0).astype(BF16), jnp.where(lo, 0.0, qp).astype(BF16)]
                doss += [jnp.where(lo, do_p, 0.0).astype(BF16), jnp.where(lo, 0.0, do_p).astype(BF16)]
                attn = jnp.where(lo_t, outs[0], outs[1]).T
                dq = jnp.where(lo_t, dqs[0], dqs[1]).T * ATTN_SCALE
                dga = da * attn * (sg * (1.0 + ga * (1.0 - sg)))
                ga_cols = slice(GA_OFF + LANES * pair, GA_OFF + LANES * (pair + 1))
                dh_ref[:, cols] = dq.astype(BF16)
                dh_ref[:, ga_cols] = dga.astype(BF16)
                db_ref[:, cols] += jnp.sum(dq, axis=0, keepdims=True)
                db_ref[:, ga_cols] += jnp.sum(dga, axis=0, keepdims=True)
            dk = jnp.dot(jnp.concatenate(dsbs, axis=1), jnp.concatenate(qss, axis=0), preferred_element_type=F32)
            dv = jnp.dot(jnp.concatenate(pbs, axis=1), jnp.concatenate(doss, axis=0), preferred_element_type=F32)
            dk = dk * ATTN_SCALE
            dk_tot.append(dk + pltpu.roll(dk, HEAD_DIM, axis=1))
            dv_tot.append(dv + pltpu.roll(dv, HEAD_DIM, axis=1))
        lo2 = _lane_lo((2 * BLOCK, LANES))
        dkv = jnp.concatenate([jnp.where(lo2, dk_tot[0], dk_tot[1]), jnp.where(lo2, dv_tot[0], dv_tot[1])], axis=1)
        dkv_done = dkv[BLOCK:, :] + carry_scr[...]
        dh_ref[:, KV_OFF:KV_OFF + 2 * KV_COLS] = dkv_done.astype(BF16)
        db_ref[:, KV_OFF:KV_OFF + 2 * KV_COLS] += jnp.sum(dkv_done, axis=0, keepdims=True)
        carry_scr[...] = dkv[:BLOCK, :]

    n_in = 12
    return pl.pallas_call(
        body,
        name=name,
        grid=(nb,),
        in_specs=_attn_specs(order) + [
            pl.BlockSpec((BLOCK, ATTN_WIDTH), lambda t: (order(t), 0)),
            pl.BlockSpec((N_KV_HEADS, BIAS_ROWS, BLOCK), lambda t: (0, 0, 0)),
            pl.BlockSpec(memory_space=pl.ANY),
        ],
        out_specs=[
            pl.BlockSpec((BLOCK, DH_ATTN_COLS), lambda t: (order(t), 0)),
            pl.BlockSpec((N_KV_HEADS, BIAS_ROWS, BLOCK), lambda t: (0, 0, 0)),
            pl.BlockSpec((N_KV_HEADS, KV_GROUP, LANES), lambda t: (0, 0, 0)),
            pl.BlockSpec((1, DH_ATTN_COLS), lambda t: (0, 0)),
        ],
        out_shape=[
            jax.ShapeDtypeStruct((S, IN_COLS), BF16),
            jax.ShapeDtypeStruct((N_KV_HEADS, BIAS_ROWS, BLOCK), F32),
            jax.ShapeDtypeStruct((N_KV_HEADS, KV_GROUP, LANES), F32),
            jax.ShapeDtypeStruct((1, DH_ATTN_COLS), F32),
        ],
        scratch_shapes=[pltpu.VMEM((BLOCK, 2 * KV_COLS), F32)],
        input_output_aliases={n_in: 0},
        compiler_params=_params(("arbitrary",)),
    )(h, h, h, h, h, h, h, bias, masks, sinks, dab, dbias_in, dh)


def _window_sum(x, w, back):
    n = x.shape[0]
    s, sh = x, 1
    while sh < w:
        s = s + pltpu.roll(s, sh if back else n - sh, axis=0)
        sh *= 2
    return s


def _pool_counts(first_row, n, w):
    t = first_row + lax.broadcasted_iota(jnp.int32, (n, 1), 0)
    return jnp.minimum(t + 1, w).astype(F32)


def _pool_diff(u_ref, uh_ref, i, T, g):
    u = u_ref[...]
    halo = jnp.where(i > 0, uh_ref[...], 0.0)
    ext = jnp.concatenate([halo, u], axis=0)
    w = POOL_WINDOWS[g]
    s = _window_sum(ext, w, back=True)[POOL_HALO:, :]
    return s / _pool_counts(i * T, T, w) - u


def _pool_in_specs(T):
    hb = T // POOL_HALO
    specs = []
    for g in range(4):
        specs.append(pl.BlockSpec((T, 256), functools.partial(lambda i, g: (i, U_OFF // 256 + g), g=g)))
        specs.append(pl.BlockSpec((POOL_HALO, 256), functools.partial(
            lambda i, g: (jnp.maximum(i * hb - 1, 0), U_OFF // 256 + g), g=g)))
    return specs


def _pool_weight_specs():
    return [pl.BlockSpec((4, 256, 256), lambda i: (0, 0, 0)), pl.BlockSpec((1, POOL_WIDTH), lambda i: (0, 0))]


def _pool_fwd(h, ab, w_pool, pool_scale, name):
    S = h.shape[0]
    T = _tile(S, 512)

    def body(*refs):
        u_refs = refs[0:8]
        gb_refs = refs[8:12]
        wp_ref, sc_ref, o_ref = refs[12], refs[13], refs[15]
        i = pl.program_id(0)
        for g in range(4):
            diff = _pool_diff(u_refs[2 * g], u_refs[2 * g + 1], i, T, g).astype(BF16)
            mixed = jnp.dot(diff, wp_ref[g], preferred_element_type=F32) * sc_ref[:, 256 * g:256 * (g + 1)]
            gb = gb_refs[g][...]
            o_ref[:, 256 * g:256 * (g + 1)] = (mixed * (gb * _sigmoid(gb))).astype(BF16)

    in_specs = _pool_in_specs(T) + [
        pl.BlockSpec((T, 256), functools.partial(lambda i, g: (i, GB_OFF // 256 + g), g=g)) for g in range(4)
    ] + _pool_weight_specs() + [pl.BlockSpec(memory_space=pl.ANY)]
    return pl.pallas_call(
        body,
        name=name,
        grid=(S // T,),
        in_specs=in_specs,
        out_specs=pl.BlockSpec((T, POOL_WIDTH), lambda i: (i, 1)),
        out_shape=jax.ShapeDtypeStruct(ab.shape, BF16),
        input_output_aliases={14: 0},
        compiler_params=_params(("arbitrary",)),
    )(*([h] * 12), w_pool, pool_scale, ab)


DH_POOL_COLS = IN_COLS // 2


def _pool_bwd(h, dab, w_pool, pool_scale, after, name):
    S = h.shape[0]
    T = _tile(S, 512)
    nt = S // T
    hb = T // POOL_HALO
    E = T + POOL_HALO
    lead = U_OFF - DH_POOL_COLS

    def body(*refs):
        u_refs = refs[0:8]
        gb_refs = refs[8:16]
        db_refs = refs[16:24]
        wp_ref, sc_ref = refs[24], refs[25]
        dh_ref, dwp_ref, dsc_ref, dbi_ref = refs[27:31]
        i = pl.program_id(0)

        @pl.when(i == 0)
        def _():
            dwp_ref[...] = jnp.zeros_like(dwp_ref)
            dsc_ref[...] = jnp.zeros_like(dsc_ref)
            dbi_ref[...] = jnp.zeros_like(dbi_ref)

        dh_ref[:, 0:lead] = jnp.zeros((T, lead), BF16)
        for g in range(4):
            w = POOL_WINDOWS[g]
            cols = slice(256 * g, 256 * (g + 1))
            scale = sc_ref[:, cols]
            wp = wp_ref[g]
            diff = _pool_diff(u_refs[2 * g], u_refs[2 * g + 1], i, T, g).astype(BF16)
            mixed = jnp.dot(diff, wp, preferred_element_type=F32)
            keep = i < nt - 1
            gb = jnp.concatenate([gb_refs[2 * g][...], jnp.where(keep, gb_refs[2 * g + 1][...], 0.0)], axis=0)
            db = jnp.concatenate([db_refs[2 * g][...], jnp.where(keep, db_refs[2 * g + 1][...], 0.0)], axis=0)
            sg = _sigmoid(gb)
            dms = db * (gb * sg)
            dmixed = (dms * scale).astype(BF16)
            ddiff = lax.dot_general(dmixed, wp, (((1,), (1,)), ((), ())), preferred_element_type=F32)
            r = ddiff / _pool_counts(i * T, E, w)
            du = _window_sum(r, w, back=False)[:T, :] - ddiff[:T, :]
            dgb = db[:T, :] * (mixed * scale) * (sg[:T, :] * (1.0 + gb[:T, :] * (1.0 - sg[:T, :])))
            u_cols = slice(lead + 256 * g, lead + 256 * (g + 1))
            gb_cols = slice(lead + POOL_WIDTH + 256 * g, lead + POOL_WIDTH + 256 * (g + 1))
            dh_ref[:, u_cols] = du.astype(BF16)
            dh_ref[:, gb_cols] = dgb.astype(BF16)
            dbi_ref[:, u_cols] += jnp.sum(du, axis=0, keepdims=True)
            dbi_ref[:, gb_cols] += jnp.sum(dgb, axis=0, keepdims=True)
            dsc_ref[:, cols] += jnp.sum(dms[:T, :] * mixed, axis=0, keepdims=True)
            dwp_ref[g] += lax.dot_general(diff, dmixed[:T, :], (((0,), (0,)), ((), ())), preferred_element_type=F32)

    def rows_after(i):
        return jnp.minimum((i + 1) * hb, S // POOL_HALO - 1)

    in_specs = _pool_in_specs(T)
    for off in (GB_OFF // 256, ATTN_WIDTH // 256):
        for g in range(4):
            in_specs.append(pl.BlockSpec((T, 256), functools.partial(lambda i, c: (i, c), c=off + g)))
            in_specs.append(pl.BlockSpec((POOL_HALO, 256), functools.partial(lambda i, c: (rows_after(i), c), c=off + g)))
    in_specs += _pool_weight_specs() + [pl.BlockSpec(memory_space=pl.ANY)]
    return pl.pallas_call(
        body,
        name=name,
        grid=(nt,),
        in_specs=in_specs,
        out_specs=[
            pl.BlockSpec((T, DH_POOL_COLS), lambda i: (i, 1)),
            pl.BlockSpec((4, 256, 256), lambda i: (0, 0, 0)),
            pl.BlockSpec((1, POOL_WIDTH), lambda i: (0, 0)),
            pl.BlockSpec((1, DH_POOL_COLS), lambda i: (0, 0)),
        ],
        out_shape=[
            jax.ShapeDtypeStruct((S, IN_COLS), BF16),
            jax.ShapeDtypeStruct((4, 256, 256), F32),
            jax.ShapeDtypeStruct((1, POOL_WIDTH), F32),
            jax.ShapeDtypeStruct((1, DH_POOL_COLS), F32),
        ],
        compiler_params=_params(("arbitrary",)),
    )(*([h] * 16), *([dab] * 8), w_pool, pool_scale, after)


def _load_resident(pairs, sems):
    @pl.when(pl.program_id(0) == 0)
    def _():
        cps = [pltpu.make_async_copy(src, dst, sems.at[n]) for n, (src, dst) in enumerate(pairs)]
        for cp in cps:
            cp.start()
        for cp in cps:
            cp.wait()


def _mix_ln_fwd(ab, x, pb, w_out, w_gate, w_ple_t, gain, bias, alpha, after, name):
    S = x.shape[0]
    T = _tile(S, 256)

    def body(ab_ref, x_ref, p_ref, wo_hbm, wg_hbm, wp_hbm, g_ref, b_ref, after_ref,
             y_ref, yb_ref, xh_ref, rs_ref, gp_ref, pe_ref, wo, wg, wp, sems):
        del after_ref
        _load_resident(((wo_hbm, wo), (wg_hbm, wg), (wp_hbm, wp)), sems)
        x = x_ref[...]
        mix = jnp.dot(ab_ref[...], wo[...], preferred_element_type=F32)
        gp = jnp.dot(x.astype(BF16), wg[...], preferred_element_type=F32)
        pe = lax.dot_general(p_ref[...], wp[...], (((1,), (1,)), ((), ())), preferred_element_type=F32)
        z = alpha * x + mix + _sigmoid(gp) * pe
        mu = jnp.mean(z, axis=-1, keepdims=True)
        zc = z - mu
        var = jnp.mean(zc * zc, axis=-1, keepdims=True)
        rstd = lax.rsqrt(var + LN_EPS)
        xhat = zc * rstd
        y = xhat * g_ref[...] + b_ref[...]
        y_ref[...] = y
        yb_ref[...] = y.astype(BF16)
        xh_ref[...] = xhat
        rs_ref[...] = rstd
        gp_ref[...] = gp
        pe_ref[...] = pe

    row = pl.BlockSpec((T, D_MODEL), lambda i: (i, 0))
    vec = pl.BlockSpec((1, D_MODEL), lambda i: (0, 0))
    any_spec = pl.BlockSpec(memory_space=pl.ANY)
    f32_rows = jax.ShapeDtypeStruct((S, D_MODEL), F32)
    return pl.pallas_call(
        body,
        name=name,
        grid=(S // T,),
        in_specs=[row, row, pl.BlockSpec((T, PLE_DIM), lambda i: (i, 0)), any_spec, any_spec, any_spec, vec, vec, any_spec],
        out_specs=[row, row, row, pl.BlockSpec((T, 1), lambda i: (i, 0)), row, row],
        out_shape=[f32_rows, jax.ShapeDtypeStruct((S, D_MODEL), BF16), f32_rows, jax.ShapeDtypeStruct((S, 1), F32),
                   f32_rows, f32_rows],
        scratch_shapes=[pltpu.VMEM(w_out.shape, BF16), pltpu.VMEM(w_gate.shape, BF16), pltpu.VMEM(w_ple_t.shape, BF16),
                        pltpu.SemaphoreType.DMA((3,))],
        compiler_params=_params(("arbitrary",)),
    )(ab, x, pb, w_out, w_gate, w_ple_t, gain, bias, after)


def _ln_dmix_bwd(dy, xhat, rstd, gain, gp, pe, w_out, w_gate, alpha, after, name):
    S = dy.shape[0]
    T = _tile(S, 256)

    def body(dy_ref, xh_ref, rs_ref, g_ref, gp_ref, pe_ref, wo_hbm, wg_hbm, after_ref,
             dzb_ref, dpe_ref, dgp_ref, dab_ref, dx_ref, dgain_ref, dbias_ref, wo, wg, sems):
        del after_ref
        _load_resident(((wo_hbm, wo), (wg_hbm, wg)), sems)

        @pl.when(pl.program_id(0) == 0)
        def _():
            dgain_ref[...] = jnp.zeros_like(dgain_ref)
            dbias_ref[...] = jnp.zeros_like(dbias_ref)

        dy = dy_ref[...]
        xhat = xh_ref[...]
        dyg = dy * g_ref[...]
        c1 = jnp.mean(dyg, axis=-1, keepdims=True)
        c2 = jnp.mean(dyg * xhat, axis=-1, keepdims=True)
        dz = rs_ref[...] * (dyg - c1 - xhat * c2)
        dgain_ref[...] += jnp.sum(dy * xhat, axis=0, keepdims=True)
        dbias_ref[...] += jnp.sum(dy, axis=0, keepdims=True)
        sg = _sigmoid(gp_ref[...])
        dzb = dz.astype(BF16)
        dgp = (dz * pe_ref[...] * (sg * (1.0 - sg))).astype(BF16)
        nt = (((1,), (1,)), ((), ()))
        dzb_ref[...] = dzb
        dpe_ref[...] = (dz * sg).astype(BF16)
        dgp_ref[...] = dgp
        dab_ref[...] = lax.dot_general(dzb, wo[...], nt, preferred_element_type=F32)
        dx_ref[...] = lax.dot_general(dgp, wg[...], nt, preferred_element_type=F32) + alpha * dz

    row = pl.BlockSpec((T, D_MODEL), lambda i: (i, 0))
    vec = pl.BlockSpec((1, D_MODEL), lambda i: (0, 0))
    any_spec = pl.BlockSpec(memory_space=pl.ANY)
    bf16_rows = jax.ShapeDtypeStruct((S, D_MODEL), BF16)
    f32_rows = jax.ShapeDtypeStruct((S, D_MODEL), F32)
    return pl.pallas_call(
        body,
        name=name,
        grid=(S // T,),
        in_specs=[row, row, pl.BlockSpec((T, 1), lambda i: (i, 0)), vec, row, row, any_spec, any_spec, any_spec],
        out_specs=[row, row, row, row, row, vec, vec],
        out_shape=[bf16_rows, bf16_rows, bf16_rows, f32_rows, f32_rows,
                   jax.ShapeDtypeStruct((1, D_MODEL), F32), jax.ShapeDtypeStruct((1, D_MODEL), F32)],
        scratch_shapes=[pltpu.VMEM(w_out.shape, BF16), pltpu.VMEM(w_gate.shape, BF16), pltpu.SemaphoreType.DMA((2,))],
        compiler_params=_params(("arbitrary",)),
    )(dy, xhat, rstd, gain, gp, pe, w_out, w_gate, after)


def _loss_head(y, target):
    S = y.shape[0]
    T = _tile(S, 256)

    def body(y_ref, t_ref, dy_ref, l_ref):
        @pl.when(pl.program_id(0) == 0)
        def _():
            l_ref[...] = jnp.zeros_like(l_ref)

        err = y_ref[...] - t_ref[...]
        dy_ref[...] = err * (1.0 / D_MODEL)
        per_token = jnp.mean(err * err, axis=-1, keepdims=True)
        l_ref[...] += 0.5 * jnp.sum(per_token, axis=0, keepdims=True)

    row = pl.BlockSpec((T, D_MODEL), lambda i: (i, 0))
    return pl.pallas_call(
        body,
        name="loss_head",
        grid=(S // T,),
        in_specs=[row, row],
        out_specs=[row, pl.BlockSpec((8, LANES), lambda i: (0, 0))],
        out_shape=[jax.ShapeDtypeStruct((S, D_MODEL), F32), jax.ShapeDtypeStruct((8, LANES), F32)],
        compiler_params=_params(("arbitrary",)),
    )(y, target)


def _sum_slabs(r, name):
    _, R, C = r.shape
    T = _tile(R, 256)

    def body(r_ref, o_ref):
        acc = r_ref[0].astype(F32)
        for s in range(1, N_DEV):
            acc = acc + r_ref[s].astype(F32)
        o_ref[...] = acc

    return pl.pallas_call(
        body,
        name=name,
        grid=(R // T,),
        in_specs=[pl.BlockSpec((N_DEV, T, C), lambda i: (0, i, 0))],
        out_specs=pl.BlockSpec((T, C), lambda i: (i, 0)),
        out_shape=jax.ShapeDtypeStruct((R, C), F32),
        compiler_params=_params(("parallel",)),
    )(r)


def _adamw(w, g, m, v, name):
    R, C = w.shape
    T = _tile(R, 256)
    grid = (R // T,)
    blk = pl.BlockSpec((T, C), lambda i: (i, 0))

    def body(w_ref, g_ref, m_ref, v_ref, d_ref, nm_ref, nv_ref):
        g = g_ref[...]
        m = ADAM_B1 * m_ref[...] + (1.0 - ADAM_B1) * g
        v = ADAM_B2 * v_ref[...] + (1.0 - ADAM_B2) * jnp.square(g)
        m_hat = m / (1.0 - ADAM_B1 ** ADAM_STEP)
        v_hat = v / (1.0 - ADAM_B2 ** ADAM_STEP)
        d_ref[...] = -ADAM_LR * (m_hat / (jnp.sqrt(v_hat) + ADAM_EPS) + ADAM_WD * w_ref[...])
        nm_ref[...] = m
        nv_ref[...] = v

    shp = jax.ShapeDtypeStruct(w.shape, F32)
    return pl.pallas_call(
        body,
        name=name,
        grid=grid,
        in_specs=[blk] * 4,
        out_specs=[blk] * 3,
        out_shape=[shp] * 3,
        compiler_params=_params(("parallel",) * len(grid)),
    )(w, g, m, v)


def _mesh_pos():
    return lax.axis_index("x"), lax.axis_index("y"), lax.axis_index("c")


def _flip(pos, k):
    x, y, c = pos
    return (1 - x if k & 4 else x, 1 - y if k & 2 else y, 1 - c if k & 1 else c)


def _index(pos):
    return 4 * pos[0] + 2 * pos[1] + pos[2]


HBM_SPEC = pl.BlockSpec(memory_space=pltpu.HBM)
SEM_SPEC = pl.BlockSpec(memory_space=pltpu.SEMAPHORE)
ANY_SPEC = pl.BlockSpec(memory_space=pl.ANY)
SPLIT_EFFECT = pltpu.SideEffectType.DATAFLOW_SIDE_EFFECTING
GATHER_FLIPS = (1, 4, 2, 6)
CHIP_FLIPS = (4, 2, 6)
TOKEN = jax.ShapeDtypeStruct((8, LANES), F32)


def _hbm(a):
    return pltpu.with_memory_space_constraint(a, pltpu.HBM)


def _hbm_like(a):
    return pltpu.HBM(a.shape, a.dtype)


def _block_rows(ref, pos, r):
    return ref.at[:, pl.ds(_index(pos) * r, r), :]


def _gather_start(shards, after, name):
    n = len(shards)
    lands = [lax.empty((s.shape[0], N_DEV * s.shape[1], s.shape[2]), s.dtype) for s in shards]

    def body(*refs):
        ins, bufs = refs[:n], refs[n:2 * n]
        send_sems, recv_sems = refs[2 * n + 1], refs[2 * n + 2]
        token = refs[4 * n + 3]
        me = _mesh_pos()
        for a in range(n):
            for j, k in enumerate(GATHER_FLIPS):
                pltpu.make_async_remote_copy(
                    src_ref=ins[a], dst_ref=_block_rows(bufs[a], me, shards[a].shape[1]),
                    send_sem=send_sems.at[4 * a + j], recv_sem=recv_sems.at[4 * a + j],
                    device_id=_flip(me, k), device_id_type=MESH_ID).start()
        token[...] = jnp.zeros_like(token)

    outs = pl.pallas_call(
        body,
        name=name,
        in_specs=[HBM_SPEC] * (2 * n) + [ANY_SPEC],
        out_specs=[SEM_SPEC, SEM_SPEC] + [HBM_SPEC] * (2 * n) + [pl.BlockSpec(memory_space=pltpu.VMEM)],
        out_shape=[pltpu.SemaphoreType.DMA((4 * n,)), pltpu.SemaphoreType.DMA((4 * n,))]
        + [_hbm_like(s) for s in shards] + [_hbm_like(b) for b in lands] + [TOKEN],
        input_output_aliases={i: 2 + i for i in range(2 * n)},
        compiler_params=pltpu.CompilerParams(has_side_effects=SPLIT_EFFECT),
    )(*[_hbm(s) for s in shards], *[_hbm(b) for b in lands], after)
    return outs[0], outs[1], outs[2:2 + n], outs[2 + n:2 + 2 * n], outs[2 + 2 * n]


def _gather_wait(started, after, name):
    send_sems, recv_sems, shards, lands, _ = started
    n = len(shards)

    def body(*refs):
        ins, bufs = refs[:n], refs[n:2 * n]
        send_sems, recv_sems = refs[2 * n], refs[2 * n + 1]
        me = _mesh_pos()
        for a in range(n):
            for j, k in enumerate(GATHER_FLIPS):
                cp = pltpu.make_async_remote_copy(
                    src_ref=ins[a], dst_ref=_block_rows(bufs[a], _flip(me, k), shards[a].shape[1]),
                    send_sem=send_sems.at[4 * a + j], recv_sem=recv_sems.at[4 * a + j],
                    device_id=_flip(me, k), device_id_type=MESH_ID)
                cp.wait_send()
                cp.wait_recv()

    outs = pl.pallas_call(
        body,
        name=name,
        in_specs=[HBM_SPEC] * (2 * n) + [SEM_SPEC, SEM_SPEC, ANY_SPEC],
        out_specs=[HBM_SPEC] * (2 * n),
        out_shape=[_hbm_like(s) for s in shards] + [_hbm_like(b) for b in lands],
        input_output_aliases={i: i for i in range(2 * n)},
        compiler_params=pltpu.CompilerParams(has_side_effects=SPLIT_EFFECT),
    )(*shards, *lands, send_sems, recv_sems, after)
    return outs[:n], outs[n:]


def _gather_pass(shards, lands, name):
    n = len(shards)

    def body(*refs):
        ins, bufs = refs[:n], refs[n:2 * n]
        token = refs[3 * n]
        send_sems, recv_sems, local_sems = refs[3 * n + 1:]
        me = _mesh_pos()
        sibling = _flip(me, 1)

        def copy(a, j, block):
            rows = _block_rows(bufs[a], block, shards[a].shape[1])
            return pltpu.make_async_remote_copy(
                src_ref=rows, dst_ref=rows, send_sem=send_sems.at[3 * a + j], recv_sem=recv_sems.at[3 * a + j],
                device_id=sibling, device_id_type=MESH_ID)

        mine = [pltpu.make_async_copy(ins[a], _block_rows(bufs[a], me, shards[a].shape[1]), local_sems.at[a])
                for a in range(n)]
        sends = [copy(a, j, _flip(me, k)) for a in range(n) for j, k in enumerate(CHIP_FLIPS)]
        for cp in mine + sends:
            cp.start()
        for a in range(n):
            for j, k in enumerate(CHIP_FLIPS):
                copy(a, j, _flip(sibling, k)).wait_recv()
        for cp in sends:
            cp.wait_send()
        for cp in mine:
            cp.wait()
        token[...] = jnp.zeros_like(token)

    outs = pl.pallas_call(
        body,
        name=name,
        in_specs=[pl.BlockSpec(memory_space=pltpu.VMEM)] * n + [ANY_SPEC] * n,
        out_specs=[ANY_SPEC] * n + [pl.BlockSpec(memory_space=pltpu.VMEM)],
        out_shape=[jax.ShapeDtypeStruct(b.shape, b.dtype) for b in lands] + [TOKEN],
        scratch_shapes=[pltpu.SemaphoreType.DMA((3 * n,)), pltpu.SemaphoreType.DMA((3 * n,)), pltpu.SemaphoreType.DMA((n,))],
        input_output_aliases={n + i: i for i in range(n)},
        compiler_params=pltpu.CompilerParams(has_side_effects=True, vmem_limit_bytes=VMEM_LIMIT_BYTES),
    )(*shards, *lands)
    return outs[:n], outs[n]


def _gather_own(shards, lands, name):
    n = len(shards)

    def body(*refs):
        ins, bufs, local_sems = refs[:n], refs[n:2 * n], refs[3 * n]
        me = _mesh_pos()
        cps = [pltpu.make_async_copy(ins[a], _block_rows(bufs[a], me, shards[a].shape[1]), local_sems.at[a])
               for a in range(n)]
        for cp in cps:
            cp.start()
        for cp in cps:
            cp.wait()

    return pl.pallas_call(
        body,
        name=name,
        in_specs=[pl.BlockSpec(memory_space=pltpu.VMEM)] * n + [ANY_SPEC] * n,
        out_specs=[ANY_SPEC] * n,
        out_shape=[jax.ShapeDtypeStruct(b.shape, b.dtype) for b in lands],
        scratch_shapes=[pltpu.SemaphoreType.DMA((n,))],
        input_output_aliases={n + i: i for i in range(n)},
        compiler_params=pltpu.CompilerParams(has_side_effects=True, vmem_limit_bytes=VMEM_LIMIT_BYTES),
    )(*shards, *lands)


def _pass_copy(bufs, send_sems, recv_sems, a, j, block, sibling):
    rows = _block_rows(bufs[a], block, bufs[a].shape[1] // N_DEV)
    return pltpu.make_async_remote_copy(
        src_ref=rows, dst_ref=rows, send_sem=send_sems.at[3 * a + j], recv_sem=recv_sems.at[3 * a + j],
        device_id=sibling, device_id_type=MESH_ID)


def _pass_start(lands, after, name):
    n = len(lands)

    def body(*refs):
        bufs = refs[:n]
        send_sems, recv_sems = refs[n + 1], refs[n + 2]
        token = refs[2 * n + 3]
        me = _mesh_pos()
        for a in range(n):
            for j, k in enumerate(CHIP_FLIPS):
                _pass_copy(bufs, send_sems, recv_sems, a, j, _flip(me, k), _flip(me, 1)).start()
        token[...] = jnp.zeros_like(token)

    outs = pl.pallas_call(
        body,
        name=name,
        in_specs=[HBM_SPEC] * n + [ANY_SPEC],
        out_specs=[SEM_SPEC, SEM_SPEC] + [HBM_SPEC] * n + [pl.BlockSpec(memory_space=pltpu.VMEM)],
        out_shape=[pltpu.SemaphoreType.DMA((3 * n,)), pltpu.SemaphoreType.DMA((3 * n,))]
        + [_hbm_like(b) for b in lands] + [TOKEN],
        input_output_aliases={i: 2 + i for i in range(n)},
        compiler_params=pltpu.CompilerParams(has_side_effects=SPLIT_EFFECT),
    )(*[_hbm(b) for b in lands], after)
    return outs[0], outs[1], outs[2:2 + n], outs[2 + n]


def _pass_wait(started, after, name):
    send_sems, recv_sems, lands, _ = started
    n = len(lands)

    def body(*refs):
        bufs = refs[:n]
        send_sems, recv_sems = refs[n], refs[n + 1]
        me = _mesh_pos()
        sibling = _flip(me, 1)
        for a in range(n):
            for j, k in enumerate(CHIP_FLIPS):
                _pass_copy(bufs, send_sems, recv_sems, a, j, _flip(me, k), sibling).wait_send()
                _pass_copy(bufs, send_sems, recv_sems, a, j, _flip(sibling, k), sibling).wait_recv()

    return pl.pallas_call(
        body,
        name=name,
        in_specs=[HBM_SPEC] * n + [SEM_SPEC, SEM_SPEC, ANY_SPEC],
        out_specs=[HBM_SPEC] * n,
        out_shape=[_hbm_like(b) for b in lands],
        input_output_aliases={i: i for i in range(n)},
        compiler_params=pltpu.CompilerParams(has_side_effects=SPLIT_EFFECT),
    )(*lands, send_sems, recv_sems, after)


def _place_own(grads, lands, layer, name):
    n = len(grads)
    blocks = [(g.shape[0], g.shape[1] // N_DEV, g.shape[2]) for g in grads]

    def body(*refs):
        ins, bufs = refs[:n], refs[n:2 * n]
        stage, in_sems, out_sems = refs[3 * n:4 * n], refs[4 * n], refs[4 * n + 1]
        me = _mesh_pos()
        loads = [pltpu.make_async_copy(_block_rows(ins[a], me, blocks[a][1]), stage[a], in_sems.at[a]) for a in range(n)]
        stores = [pltpu.make_async_copy(stage[a], bufs[a].at[_index(me), layer], out_sems.at[a]) for a in range(n)]
        for cp in loads:
            cp.start()
        for a in range(n):
            loads[a].wait()
            stores[a].start()
        for cp in stores:
            cp.wait()

    return pl.pallas_call(
        body,
        name=name,
        in_specs=[ANY_SPEC] * (2 * n),
        out_specs=[ANY_SPEC] * n,
        out_shape=[jax.ShapeDtypeStruct(b.shape, b.dtype) for b in lands],
        scratch_shapes=[pltpu.VMEM(blk, g.dtype) for blk, g in zip(blocks, grads)]
        + [pltpu.SemaphoreType.DMA((n,)), pltpu.SemaphoreType.DMA((n,))],
        input_output_aliases={n + i: i for i in range(n)},
        compiler_params=pltpu.CompilerParams(has_side_effects=True, vmem_limit_bytes=VMEM_LIMIT_BYTES),
    )(*grads, *lands)


def _scatter_copy(ins, bufs, send_sems, recv_sems, a, k, r, layer, me, slab):
    peer = _flip(me, k)
    return pltpu.make_async_remote_copy(
        src_ref=_block_rows(ins[a], peer, r), dst_ref=bufs[a].at[_index(slab), layer],
        send_sem=send_sems.at[7 * a + k - 1], recv_sem=recv_sems.at[7 * a + k - 1],
        device_id=peer, device_id_type=MESH_ID)


def _scatter_start(grads, lands, layer, name):
    n = len(grads)

    def body(*refs):
        ins, bufs = refs[:n], refs[n:2 * n]
        send_sems, recv_sems = refs[2 * n], refs[2 * n + 1]
        token = refs[4 * n + 2]
        me = _mesh_pos()
        for a in range(n):
            for k in range(1, N_DEV):
                _scatter_copy(ins, bufs, send_sems, recv_sems, a, k, grads[a].shape[1] // N_DEV, layer, me, me).start()
        token[...] = jnp.zeros_like(token)

    outs = pl.pallas_call(
        body,
        name=name,
        in_specs=[HBM_SPEC] * (2 * n),
        out_specs=[SEM_SPEC, SEM_SPEC] + [HBM_SPEC] * (2 * n) + [pl.BlockSpec(memory_space=pltpu.VMEM)],
        out_shape=[pltpu.SemaphoreType.DMA((7 * n,)), pltpu.SemaphoreType.DMA((7 * n,))]
        + [_hbm_like(g) for g in grads] + [_hbm_like(b) for b in lands] + [TOKEN],
        input_output_aliases={i: 2 + i for i in range(2 * n)},
        compiler_params=pltpu.CompilerParams(has_side_effects=SPLIT_EFFECT),
    )(*[_hbm(g) for g in grads], *[_hbm(b) for b in lands])
    return outs[0], outs[1], outs[2:2 + n], outs[2 + n:2 + 2 * n], outs[2 + 2 * n]


def _scatter_wait(started, layer, after, name):
    send_sems, recv_sems, grads, lands, _ = started
    n = len(grads)

    def body(*refs):
        ins, bufs = refs[:n], refs[n:2 * n]
        send_sems, recv_sems = refs[2 * n], refs[2 * n + 1]
        me = _mesh_pos()
        for a in range(n):
            for k in range(1, N_DEV):
                cp = _scatter_copy(ins, bufs, send_sems, recv_sems, a, k, grads[a].shape[1] // N_DEV, layer, me, _flip(me, k))
                cp.wait_send()
                cp.wait_recv()

    outs = pl.pallas_call(
        body,
        name=name,
        in_specs=[HBM_SPEC] * (2 * n) + [SEM_SPEC, SEM_SPEC, ANY_SPEC],
        out_specs=[HBM_SPEC] * (2 * n),
        out_shape=[_hbm_like(g) for g in grads] + [_hbm_like(b) for b in lands],
        input_output_aliases={i: i for i in range(2 * n)},
        compiler_params=pltpu.CompilerParams(has_side_effects=SPLIT_EFFECT),
    )(*grads, *lands, send_sems, recv_sems, after)
    return outs[n:]


def _allreduce_small(vec):
    R, C = vec.shape

    def body(v_ref, o_ref, buf, send_sems, recv_sems):
        me = _mesh_pos()
        buf[_index(me)] = v_ref[...]
        sends = []
        for k in range(1, N_DEV):
            sends.append(pltpu.make_async_remote_copy(
                src_ref=buf.at[_index(me)], dst_ref=buf.at[_index(me)],
                send_sem=send_sems.at[k - 1], recv_sem=recv_sems.at[k - 1],
                device_id=_flip(me, k), device_id_type=MESH_ID))
        for cp in sends:
            cp.start()
        for cp in sends:
            cp.wait_recv()
        for cp in sends:
            cp.wait_send()
        acc = buf[0]
        for s in range(1, N_DEV):
            acc = acc + buf[s]
        o_ref[...] = acc

    return pl.pallas_call(
        body,
        name="allreduce_small",
        in_specs=[pl.BlockSpec(memory_space=pltpu.VMEM)],
        out_specs=pl.BlockSpec(memory_space=pltpu.VMEM),
        out_shape=jax.ShapeDtypeStruct((R, C), F32),
        scratch_shapes=[pltpu.VMEM((N_DEV, R, C), F32), pltpu.SemaphoreType.DMA((7,)), pltpu.SemaphoreType.DMA((7,))],
        compiler_params=pltpu.CompilerParams(has_side_effects=True, vmem_limit_bytes=VMEM_LIMIT_BYTES),
    )(vec)


def _pack_small(parts):
    flat = jnp.concatenate([p.reshape(-1) for p in parts])
    n = flat.shape[0]
    rows = -(-n // SMALL_COLS)
    rows = -(-rows // 8) * 8
    return jnp.pad(flat, (0, rows * SMALL_COLS - n)).reshape(rows, SMALL_COLS)


def _unpack_small(packed, like):
    flat = packed.reshape(-1)
    out, pos = [], 0
    for p in like:
        out.append(flat[pos:pos + p.size].reshape(p.shape))
        pos += p.size
    return out


def kernel(x, p, w_in, b_in, w_out, attn_sinks, rel_bias, w_pool, pool_scale, w_ple, w_gate_ple, ln_gain, ln_bias, loss_target, m_w_in, m_b_in, m_w_out, m_attn_sinks, m_rel_bias, m_w_pool, m_pool_scale, m_w_ple, m_w_gate_ple, m_ln_gain, m_ln_bias, v_w_in, v_b_in, v_w_out, v_attn_sinks, v_rel_bias, v_w_pool, v_pool_scale, v_w_ple, v_w_gate_ple, v_ln_gain, v_ln_bias):
    L = w_in.shape[0]
    S = x.shape[1]
    alpha = (2.0 * L) ** 0.25
    bucket_np, masks_np, window_np = _band_constants()
    bucket, masks, window = jnp.asarray(bucket_np), jnp.asarray(masks_np), jnp.asarray(window_np)

    @functools.lru_cache(maxsize=None)
    def shards_of(l):
        return (jnp.swapaxes(w_in[l], 0, 1).astype(BF16)[None], w_out[l].astype(BF16)[None],
                w_gate_ple[l].astype(BF16)[None], jnp.swapaxes(w_ple[l], 0, 1).astype(BF16)[None], w_pool[l].astype(BF16))

    def gathered(started, after, tag):
        shards, lands = _gather_wait(started, after, name=f"gather_wait_{tag}")
        return _gather_pass(shards, lands, name=f"gather_pass_{tag}")

    bias = _bias_build(rel_bias, bucket).reshape(N_KV_HEADS, BIAS_ROWS, BLOCK)

    xs = x[0]
    xb = xs.astype(BF16)
    first_groups = ((0, 4), (1, 2, 3))
    token, first_started = rel_bias, []
    for tag, idxs in zip("ab", first_groups):
        first_started.append(_gather_start([shards_of(0)[i] for i in idxs], token, name=f"gather_start_0{tag}"))
        token = first_started[-1][4]
    started = {1: _gather_start(shards_of(1), token, name="gather_start_1")} if L > 1 else {}
    saved = []
    for l in range(L):
        pb = p[l, 0].astype(BF16)
        sinks_l = attn_sinks[l]
        scale_l = pool_scale[l].reshape(1, POOL_WIDTH)
        bias_l = b_in[l].reshape(1, IN_COLS)
        if l == 0:
            (w_in_f, w_pool_g), _ = gathered(first_started[0], started[1][4] if L > 1 else xb, "0a")
            w_in_t = w_in_f[0]
            h = _matmul(xb, w_in_t, tb=True, tm=512, tn=2176, tk=2048, out_dtype=F32, bias=bias_l, name=f"in_proj_{l}")
            (w_out_f, w_gate_f, w_ple_f), _ = gathered(first_started[1], h, "0b")
            w_out_g, w_gate_g, w_ple_t = w_out_f[0], w_gate_f[0], w_ple_f[0]
        else:
            w_in_t, w_out_g, w_gate_g, w_ple_t, w_pool_g = weights
            h = _matmul(xb, w_in_t, tb=True, tm=512, tn=2176, tk=2048, out_dtype=F32, bias=bias_l, name=f"in_proj_{l}")
        weights = (w_in_t, w_out_g, w_gate_g, w_ple_t, w_pool_g)
        ab = _attn_fwd(h, bias, masks, sinks_l, name=f"attn_fwd_{l}")
        ab = _pool_fwd(h, ab, w_pool_g, scale_l, name=f"pool_fwd_{l}")
        pin, passing = ab, None
        if 1 <= l and l + 1 < L:
            shards, lands = _gather_wait(started[l + 1], ab, name=f"gather_wait_{l + 1}")
            passing = _pass_start(_gather_own(shards, lands, name=f"gather_own_{l + 1}"), ab, name=f"pass_start_{l + 1}")
            pin = passing[3]
        if l + 2 < L:
            started[l + 2] = _gather_start(shards_of(l + 2), pin, name=f"gather_start_{l + 2}")
            pin = started[l + 2][4]
        y, yb, xhat, rstd, gp, pe = _mix_ln_fwd(ab, xs, pb, w_out_g, w_gate_g, w_ple_t, ln_gain[l].reshape(1, D_MODEL),
                                                ln_bias[l].reshape(1, D_MODEL), alpha, pin, name=f"mix_ln_fwd_{l}")
        saved.append((xb, pb, h, gp, pe, ab, xhat, rstd, sinks_l, scale_l, weights))
        xs, xb = y, yb
        if l + 1 < L:
            if passing is None:
                full, _ = gathered(started[l + 1], yb, l + 1)
            else:
                full = _pass_wait(passing, yb, name=f"pass_wait_{l + 1}")
            weights = (full[0][0], full[1][0], full[2][0], full[3][0], full[4])

    dy, loss_tile = _loss_head(xs, loss_target[0])

    dbias = jnp.zeros((N_KV_HEADS, BIAS_ROWS, BLOCK), F32)
    sh0 = shards_of(0)
    lands_a = [lax.empty((N_DEV, L) + sh0[i].shape, BF16) for i in (1, 2, 3)]
    lands_b = [lax.empty((N_DEV, L) + sh0[i].shape, BF16) for i in (0, 4)]
    g_b_in, g_sinks, g_scale, g_gain, g_beta = [], [], [], [], []
    pend_a = pend_b = None

    def scatter(grads, lands, pending, l, tag):
        if pending:
            lands = _scatter_wait(pending[0], pending[1], grads[0], name=f"scatter_wait_{pending[1]}{tag}")
        lands = _place_own(grads, lands, l, name=f"place_own_{l}{tag}")
        return _scatter_start(grads, lands, l, name=f"scatter_start_{l}{tag}"), l

    for l in reversed(range(L)):
        xb, pb, h, gp, pe, ab, xhat, rstd, sinks_l, scale_l, weights = saved[l]
        w_in_t, w_out_g, w_gate_g, w_ple_t, w_pool_g = weights
        dzb, dpe, dgp, dab, dx, dgain, dbeta = _ln_dmix_bwd(
            dy, xhat, rstd, ln_gain[l].reshape(1, D_MODEL), gp, pe, w_out_g, w_gate_g, alpha,
            pend_b[0][4] if pend_b else rel_bias, name=f"ln_dmix_bwd_{l}")
        g_w_out = _matmul(ab, dzb, ta=True, tm=512, tn=1024, tk=4096, out_dtype=BF16, name=f"dw_out_{l}")
        g_w_gate = _matmul(xb, dgp, ta=True, tm=512, tn=1024, tk=4096, out_dtype=BF16, name=f"dw_gate_{l}")
        g_w_ple_t = _matmul(dpe, pb, ta=True, tm=1024, tn=256, tk=1024, out_dtype=BF16, name=f"dw_ple_{l}")
        pend_a = scatter([g_w_out[None], g_w_gate[None], g_w_ple_t[None]], lands_a, pend_a, l, "a")
        dh, dwp, dsc, db_pool = _pool_bwd(h, dab, w_pool_g, scale_l, pend_a[0][4], name=f"pool_bwd_{l}")
        dh, dbias, dsink, db_attn = _attn_bwd(h, dab, dh, bias, masks, sinks_l, dbias, name=f"attn_bwd_{l}")
        g_w_in_t = _matmul(dh, xb, ta=True, tm=256, tn=1024, tk=4096, out_dtype=BF16, name=f"dw_in_{l}")
        pend_b = scatter([g_w_in_t[None], dwp.astype(BF16)], lands_b, pend_b, l, "b")
        g_b_in.append(jnp.concatenate([db_attn[0], db_pool[0, U_OFF - DH_POOL_COLS:]]))
        dy = _matmul(dh, w_in_t, tm=512, tn=1024, tk=4352, out_dtype=F32, add=dx, after=pend_b[0][4], name=f"dx_in_{l}")
        g_sinks.append(dsink[:, :, 0].reshape(N_HEADS))
        g_scale.append(dsc.reshape(POOL_WIDTH))
        g_gain.append(dgain.reshape(D_MODEL))
        g_beta.append(dbeta.reshape(D_MODEL))
    grad_x = dy[None]
    for lst in (g_b_in, g_sinks, g_scale, g_gain, g_beta):
        lst.reverse()
    g_rel = _bias_bwd(dbias.reshape((N_HEADS,) + BAND), bucket, window)[:, :N_HEADS]

    def big(w, g, m, v, name):
        shape = w.shape
        two_d = (shape[0] * shape[1], shape[2]) if len(shape) == 3 else (shape[0] * shape[1] * shape[2], shape[3])
        d, nm, nv = _adamw(w.reshape(two_d), g.reshape(two_d), m.reshape(two_d), v.reshape(two_d), name=name)
        return d.reshape(shape), nm.reshape(shape), nv.reshape(shape)

    r_out, r_gate, r_ple = _scatter_wait(pend_a[0], pend_a[1], dy, name="scatter_wait_0a")
    small_like = [b_in, attn_sinks, rel_bias, pool_scale, ln_gain, ln_bias]
    small_g = _allreduce_small(_pack_small([
        jnp.stack(g_b_in).reshape(L, IN_COLS), jnp.stack(g_sinks), g_rel, jnp.stack(g_scale), jnp.stack(g_gain),
        jnp.stack(g_beta), loss_tile[0, :1]]))
    grad_w_out = _sum_slabs(r_out.reshape(N_DEV, L * 256, D_MODEL), name="sum_w_out").reshape(L, 256, D_MODEL)
    grad_w_gate = _sum_slabs(r_gate.reshape(N_DEV, L * 256, D_MODEL), name="sum_w_gate").reshape(L, 256, D_MODEL)
    gt_ple = _sum_slabs(r_ple.reshape(N_DEV, L * 256, PLE_DIM), name="sum_w_ple")
    grad_w_ple = jnp.swapaxes(gt_ple.reshape(L, 256, PLE_DIM), 1, 2)
    upd_out = big(w_out, grad_w_out, m_w_out, v_w_out, "adamw_w_out")
    upd_ple = big(w_ple, grad_w_ple, m_w_ple, v_w_ple, "adamw_w_ple")
    upd_gate = big(w_gate_ple, grad_w_gate, m_w_gate_ple, v_w_gate_ple, "adamw_w_gate")

    r_in, r_pool = _scatter_wait(pend_b[0], pend_b[1], upd_gate[0], name="scatter_wait_0b")
    gt_in = _sum_slabs(r_in.reshape(N_DEV, L * 544, D_MODEL), name="sum_w_in")
    grad_w_in = jnp.swapaxes(gt_in.reshape(L, 544, D_MODEL), 1, 2)
    grad_w_pool = _sum_slabs(r_pool.reshape(N_DEV, L * 4 * 32, 256), name="sum_w_pool").reshape(L, 4, 32, 256)
    upd_in = tuple(jnp.swapaxes(u, 1, 2) for u in big(
        jnp.swapaxes(w_in, 1, 2), gt_in.reshape(L, 544, D_MODEL), jnp.swapaxes(m_w_in, 1, 2), jnp.swapaxes(v_w_in, 1, 2),
        "adamw_w_in"))
    upd_pool = big(w_pool, grad_w_pool, m_w_pool, v_w_pool, "adamw_w_pool")

    zero1 = jnp.zeros((1,), F32)
    sw = _pack_small(small_like + [zero1])
    sm = _pack_small([m_b_in, m_attn_sinks, m_rel_bias, m_pool_scale, m_ln_gain, m_ln_bias, zero1])
    sv = _pack_small([v_b_in, v_attn_sinks, v_rel_bias, v_pool_scale, v_ln_gain, v_ln_bias, zero1])
    sd, snm, snv = _adamw(sw, small_g, sm, sv, name="adamw_small")
    like = small_like + [zero1]
    sg_parts = _unpack_small(small_g, like)
    sd_parts, snm_parts, snv_parts = _unpack_small(sd, like), _unpack_small(snm, like), _unpack_small(snv, like)
    loss = sg_parts[6][0]

    def assemble(big_parts, small_parts):
        w_in_, w_out_, w_pool_, w_ple_, w_gate_ = big_parts
        b_in_, sinks_, rel_, scale_, gain_, beta_ = small_parts[:6]
        return [w_in_, b_in_, w_out_, sinks_, rel_, w_pool_, scale_, w_ple_, w_gate_, gain_, beta_]

    grads = assemble([grad_w_in, grad_w_out, grad_w_pool, grad_w_ple, grad_w_gate], sg_parts)
    ups = [upd_in, upd_out, upd_pool, upd_ple, upd_gate]
    deltas = assemble([u[0] for u in ups], sd_parts)
    new_m = assemble([u[1] for u in ups], snm_parts)
    new_v = assemble([u[2] for u in ups], snv_parts)
    return (loss, grad_x, *grads, *deltas, *new_m, *new_v)
```

```python
import functools
import math

import numpy as np
import jax
import jax.numpy as jnp
from jax import lax
from jax.experimental import pallas as pl
from jax.experimental.pallas import tpu as pltpu

F32 = jnp.float32
BF16 = jnp.bfloat16

D_MODEL = 2048
PLE_DIM = 256
ATTN_WIDTH = 1024
POOL_WIDTH = 1024
HEAD_DIM = 64
N_HEADS = 16
N_KV_HEADS = 2
KV_GROUP = 8
WINDOW = 128
BLOCK = 128
POOL_WINDOWS = (2, 4, 8, 16)
POOL_GROUP_DIM = 256
POOL_HALO = 16
REL_BUCKETS = 32
REL_MAX_DIST = 128
LN_EPS = 1e-5
KV_COLS = N_KV_HEADS * HEAD_DIM
IN_COLS = 4352
Q_OFF, KV_OFF, GA_OFF, U_OFF, GB_OFF = 0, 1024, 1280, 2304, 3328
ATTN_SCALE = 1.0 / math.sqrt(HEAD_DIM)
NEG_BIG = -1e30
LANES = 128

ADAM_LR = 0.001
ADAM_B1 = 0.9
ADAM_B2 = 0.999
ADAM_EPS = 1e-08
ADAM_WD = 0.01
ADAM_STEP = 10

N_DEV = 8
MESH_ID = pl.DeviceIdType.MESH
VMEM_LIMIT_BYTES = 52 * 1024 * 1024
SMALL_COLS = 1024


def _params(sem=None):
    return pltpu.CompilerParams(dimension_semantics=sem, vmem_limit_bytes=VMEM_LIMIT_BYTES)


def _sigmoid(x):
    return 1.0 / (1.0 + jnp.exp(-x))


def _tile(n, pref, unit=16):
    if n <= pref:
        return n
    t = pref - pref % unit
    while n % t:
        t -= unit
    assert t > 0, (n, pref)
    return t


def _matmul(a, b, *, name, ta=False, tb=False, tm, tn, tk, out_dtype, bias=None, add=None, add_scale=1.0, after=None):
    M, K = (a.shape[1], a.shape[0]) if ta else a.shape
    N = b.shape[0] if tb else b.shape[1]
    assert (b.shape[1] if tb else b.shape[0]) == K
    tm, tn, tk = _tile(M, tm), _tile(N, tn), _tile(K, tk)
    nm, nn, nk = M // tm, N // tn, K // tk
    a_spec = pl.BlockSpec((tk, tm), lambda j, i, k: (k, i)) if ta else pl.BlockSpec((tm, tk), lambda j, i, k: (i, k))
    b_spec = pl.BlockSpec((tn, tk), lambda j, i, k: (j, k)) if tb else pl.BlockSpec((tk, tn), lambda j, i, k: (k, j))
    dims = (((0 if ta else 1,), (1 if tb else 0,)), ((), ()))
    operands, in_specs = [a, b], [a_spec, b_spec]
    if bias is not None:
        operands.append(bias)
        in_specs.append(pl.BlockSpec((1, tn), lambda j, i, k: (0, j)))
    if add is not None:
        operands.append(add)
        in_specs.append(pl.BlockSpec((tm, tn), lambda j, i, k: (i, j)))
    if after is not None:
        operands.append(after)
        in_specs.append(pl.BlockSpec(memory_space=pl.ANY))

    def body(*refs):
        a_ref, b_ref = refs[0], refs[1]
        pos = 2
        bias_ref = add_ref = None
        if bias is not None:
            bias_ref = refs[pos]
            pos += 1
        if add is not None:
            add_ref = refs[pos]
            pos += 1
        if after is not None:
            pos += 1
        o_ref = refs[pos]
        part = lax.dot_general(a_ref[...].astype(BF16), b_ref[...].astype(BF16), dims, preferred_element_type=F32)

        def finish(acc):
            if bias_ref is not None:
                acc = acc + bias_ref[...]
            if add_ref is not None:
                acc = acc + add_scale * add_ref[...].astype(F32)
            o_ref[...] = acc.astype(out_dtype)

        if nk == 1:
            finish(part)
        else:
            acc_ref = refs[pos + 1]
            k = pl.program_id(2)

            @pl.when(k == 0)
            def _():
                acc_ref[...] = part

            @pl.when(k > 0)
            def _():
                acc_ref[...] += part

            @pl.when(k == nk - 1)
            def _():
                finish(acc_ref[...])

    return pl.pallas_call(
        body,
        name=name,
        grid=(nn, nm, nk),
        in_specs=in_specs,
        out_specs=pl.BlockSpec((tm, tn), lambda j, i, k: (i, j)),
        out_shape=jax.ShapeDtypeStruct((M, N), out_dtype),
        scratch_shapes=[pltpu.VMEM((tm, tn), F32)] if nk > 1 else [],
        compiler_params=_params(("parallel", "parallel", "arbitrary")),
    )(*operands)


BAND = (2 * BLOCK, BLOCK)
BIAS_ROWS = KV_GROUP * 2 * BLOCK


def _band_constants():
    qq = np.arange(BLOCK)[None, :]
    kk = np.arange(2 * BLOCK)[:, None]
    dist = qq + BLOCK - kk
    in_window = (dist >= 0) & (dist < WINDOW)
    max_exact = REL_BUCKETS // 2
    d = np.maximum(dist, 0)
    d_f = np.maximum(d, 1).astype(np.float32)
    large = max_exact + (
        np.log(d_f / np.float32(max_exact)) / np.float32(math.log(REL_MAX_DIST / max_exact)) * np.float32(REL_BUCKETS - max_exact)
    ).astype(np.int32)
    large = np.minimum(large, REL_BUCKETS - 1)
    bucket = np.where(d < max_exact, d, large).astype(np.int32)
    bucket = np.where(in_window, bucket, 0).astype(np.int32)
    first = in_window & (kk >= BLOCK)
    masks = np.stack([first, in_window]).astype(np.float32)
    return bucket, masks, in_window.astype(np.float32)


def _bias_build(rel_bias, bucket):
    def body(rb_ref, bkt_ref, o_ref):
        h = pl.program_id(0)
        bkt = bkt_ref[...]

        def step(b, acc):
            return jnp.where(bkt == b, rb_ref[b, h], acc)

        o_ref[0] = lax.fori_loop(0, REL_BUCKETS, step, jnp.zeros(BAND, F32))

    return pl.pallas_call(
        body,
        name="bias_build",
        grid=(N_HEADS,),
        in_specs=[pl.BlockSpec(memory_space=pltpu.SMEM), pl.BlockSpec(BAND, lambda h: (0, 0))],
        out_specs=pl.BlockSpec((1,) + BAND, lambda h: (h, 0, 0)),
        out_shape=jax.ShapeDtypeStruct((N_HEADS,) + BAND, F32),
        compiler_params=_params(("arbitrary",)),
    )(rel_bias, bucket)


def _bias_bwd(dbias, bucket, window):
    def body(db_ref, bkt_ref, win_ref, o_ref):
        h = pl.program_id(0)

        @pl.when(h == 0)
        def _():
            o_ref[...] = jnp.zeros_like(o_ref)

        bkt = bkt_ref[...]
        x = jnp.where(win_ref[...] > 0.5, db_ref[0], 0.0)
        row = lax.broadcasted_iota(jnp.int32, (REL_BUCKETS, LANES), 0)
        col = lax.broadcasted_iota(jnp.int32, (REL_BUCKETS, LANES), 1)

        def step(b, acc):
            s = jnp.sum(jnp.where(bkt == b, x, 0.0), axis=0, keepdims=True)
            return acc + jnp.where(row == b, s, 0.0)

        per_lane = lax.fori_loop(0, REL_BUCKETS, step, jnp.zeros((REL_BUCKETS, LANES), F32))
        o_ref[...] += jnp.where(col == h, jnp.sum(per_lane, axis=1, keepdims=True), 0.0)

    return pl.pallas_call(
        body,
        name="bias_bwd",
        grid=(N_HEADS,),
        in_specs=[
            pl.BlockSpec((1,) + BAND, lambda h: (h, 0, 0)),
            pl.BlockSpec(BAND, lambda h: (0, 0)),
            pl.BlockSpec(BAND, lambda h: (0, 0)),
        ],
        out_specs=pl.BlockSpec((REL_BUCKETS, LANES), lambda h: (0, 0)),
        out_shape=jax.ShapeDtypeStruct((REL_BUCKETS, LANES), F32),
        compiler_params=_params(("arbitrary",)),
    )(dbias, bucket, window)


def _lane_lo(shape):
    return lax.broadcasted_iota(jnp.int32, shape, 1) < HEAD_DIM


def _row_lo(shape):
    return lax.broadcasted_iota(jnp.int32, shape, 0) < HEAD_DIM


def _dup_heads(x):
    r = pltpu.roll(x, HEAD_DIM, axis=1)
    lo = _lane_lo(x.shape)
    return jnp.where(lo, x, r), jnp.where(lo, r, x)


def _kv_operands(kvp_ref, kvc_ref):
    kvp, kvc = kvp_ref[...], kvc_ref[...]
    k2 = jnp.concatenate([kvp[:, :KV_COLS], kvc[:, :KV_COLS]], axis=0)
    v2 = jnp.concatenate([kvp[:, KV_COLS:], kvc[:, KV_COLS:]], axis=0)
    return _dup_heads(k2), _dup_heads(v2)


def _head_probs(k_r, qs_t, bias, mask, sink):
    s = jnp.dot(k_r, qs_t, preferred_element_type=F32) * ATTN_SCALE + bias
    s = jnp.where(mask, s, NEG_BIG)
    m = jnp.maximum(jnp.max(s, axis=0, keepdims=True), sink)
    e = jnp.exp(s - m)
    e_sink = jnp.exp(sink - m)
    inv = 1.0 / (jnp.sum(e, axis=0, keepdims=True) + e_sink)
    return e * inv, e_sink * inv


def _gate_cols(ga_refs, pair):
    off = LANES * (pair % 2)
    return ga_refs[pair // 2][:, off:off + LANES]


def _attn_specs(order):
    return [
        pl.BlockSpec((BLOCK, ATTN_WIDTH), lambda t: (order(t), Q_OFF // ATTN_WIDTH)),
        pl.BlockSpec((BLOCK, 2 * KV_COLS), lambda t: (order(t), KV_OFF // (2 * KV_COLS))),
        pl.BlockSpec((BLOCK, 2 * KV_COLS), lambda t: (jnp.maximum(order(t) - 1, 0), KV_OFF // (2 * KV_COLS))),
    ] + [
        pl.BlockSpec((BLOCK, 256), functools.partial(lambda t, c: (order(t), GA_OFF // 256 + c), c=c)) for c in range(4)
    ] + [
        pl.BlockSpec((N_KV_HEADS, BIAS_ROWS, BLOCK), lambda t: (0, 0, 0)),
        pl.BlockSpec((None,) + BAND, lambda t: (jnp.minimum(order(t), 1), 0, 0)),
        pl.BlockSpec(memory_space=pltpu.SMEM),
    ]


def _attn_fwd(h, bias, masks, sinks, name):
    S = h.shape[0]
    nb = S // BLOCK

    def body(q_ref, kvc_ref, kvp_ref, ga0, ga1, ga2, ga3, bias_ref, mask_ref, sink_ref, o_ref):
        kd, vd = _kv_operands(kvp_ref, kvc_ref)
        mask = mask_ref[...] > 0.5
        lo = _row_lo((LANES, BLOCK))
        for g in range(N_KV_HEADS):
            k_r = kd[g].astype(BF16)
            v_t = vd[g].T.astype(BF16)
            for pr in range(KV_GROUP // 2):
                pair = (KV_GROUP // 2) * g + pr
                qp_t = q_ref[:, LANES * pair:LANES * (pair + 1)].T
                outs = []
                for hh in range(2):
                    j = 2 * pr + hh
                    qs_t = jnp.where(lo if hh == 0 else ~lo, qp_t, 0.0).astype(BF16)
                    p, _ = _head_probs(k_r, qs_t, bias_ref[g, 2 * BLOCK * j:2 * BLOCK * (j + 1), :], mask,
                                       sink_ref[KV_GROUP * g + j])
                    outs.append(jnp.dot(v_t, p.astype(BF16), preferred_element_type=F32))
                ga = _gate_cols((ga0, ga1, ga2, ga3), pair)
                o_ref[:, LANES * pair:LANES * (pair + 1)] = (
                    jnp.where(lo, outs[0], outs[1]).T * (ga * _sigmoid(ga))).astype(BF16)

    return pl.pallas_call(
        body,
        name=name,
        grid=(nb,),
        in_specs=_attn_specs(lambda t: t),
        out_specs=pl.BlockSpec((BLOCK, ATTN_WIDTH), lambda t: (t, 0)),
        out_shape=jax.ShapeDtypeStruct((S, ATTN_WIDTH + POOL_WIDTH), BF16),
        compiler_params=_params(("arbitrary",)),
    )(h, h, h, h, h, h, h, bias, masks, sinks)


DH_ATTN_COLS = U_OFF


def _attn_bwd(h, dab, dh, bias, masks, sinks, dbias_in, name):
    S = h.shape[0]
    nb = S // BLOCK

    def order(t):
        return nb - 1 - t

    def body(q_ref, kvc_ref, kvp_ref, ga0, ga1, ga2, ga3, bias_ref, mask_ref, sink_ref, da_ref, dbin_ref, dh_in_ref,
             dh_ref, dbias_ref, dsink_ref, db_ref, carry_scr):
        del dh_in_ref
        t = pl.program_id(0)

        @pl.when(t == 0)
        def _():
            dbias_ref[...] = dbin_ref[...]
            dsink_ref[...] = jnp.zeros_like(dsink_ref)
            db_ref[...] = jnp.zeros_like(db_ref)
            carry_scr[...] = jnp.zeros_like(carry_scr)

        kd, vd = _kv_operands(kvp_ref, kvc_ref)
        mask = mask_ref[...] > 0.5
        lo = _lane_lo((BLOCK, LANES))
        lo_t = _row_lo((LANES, BLOCK))
        dk_tot, dv_tot = [], []
        for g in range(N_KV_HEADS):
            k_t, k_r = kd[g].T.astype(BF16), kd[g].astype(BF16)
            v_t, v_r = vd[g].T.astype(BF16), vd[g].astype(BF16)
            pbs, dsbs, qss, doss = [], [], [], []
            for pr in range(KV_GROUP // 2):
                pair = (KV_GROUP // 2) * g + pr
                cols = slice(LANES * pair, LANES * (pair + 1))
                qp = q_ref[:, cols]
                qp_t = qp.T
                ga = _gate_cols((ga0, ga1, ga2, ga3), pair)
                sg = _sigmoid(ga)
                da = da_ref[:, cols]
                do_p = da * (ga * sg)
                do_t = do_p.T
                outs, dqs = [], []
                for hh in range(2):
                    j = 2 * pr + hh
                    rows = slice(2 * BLOCK * j, 2 * BLOCK * (j + 1))
                    half_t = lo_t if hh == 0 else ~lo_t
                    qs_t = jnp.where(half_t, qp_t, 0.0).astype(BF16)
                    p, p_sink = _head_probs(k_r, qs_t, bias_ref[g, rows, :], mask, sink_ref[KV_GROUP * g + j])
                    pb = p.astype(BF16)
                    outs.append(jnp.dot(v_t, pb, preferred_element_type=F32))
                    dos_t = jnp.where(half_t, do_t, 0.0).astype(BF16)
                    dp = jnp.dot(v_r, dos_t, preferred_element_type=F32)
                    dsum = jnp.sum(p * dp, axis=0, keepdims=True)
                    ds = p * (dp - dsum)
                    dbias_ref[g, rows, :] += ds
                    tot = jnp.sum(-(p_sink * dsum), axis=1, keepdims=True)
                    dsink_ref[g, j:j + 1, :] += jnp.broadcast_to(tot, (1, LANES))
                    dsb = ds.astype(BF16)
                    dqs.append(jnp.dot(k_t, dsb, preferred_element_type=F32))
                    pbs.append(pb)
                    dsbs.append(dsb)
                qss += [jnp.where(lo, qp, 0.0).astype(BF16), jnp.where(lo, 0.0, qp).astype(BF16)]
                doss += [jnp.where(lo, do_p, 0.0).astype(BF16), jnp.where(lo, 0.0, do_p).astype(BF16)]
                attn = jnp.where(lo_t, outs[0], outs[1]).T
                dq = jnp.where(lo_t, dqs[0], dqs[1]).T * ATTN_SCALE
                dga = da * attn * (sg * (1.0 + ga * (1.0 - sg)))
                ga_cols = slice(GA_OFF + LANES * pair, GA_OFF + LANES * (pair + 1))
                dh_ref[:, cols] = dq.astype(BF16)
                dh_ref[:, ga_cols] = dga.astype(BF16)
                db_ref[:, cols] += jnp.sum(dq, axis=0, keepdims=True)
                db_ref[:, ga_cols] += jnp.sum(dga, axis=0, keepdims=True)
            dk = jnp.dot(jnp.concatenate(dsbs, axis=1), jnp.concatenate(qss, axis=0), preferred_element_type=F32)
            dv = jnp.dot(jnp.concatenate(pbs, axis=1), jnp.concatenate(doss, axis=0), preferred_element_type=F32)
            dk = dk * ATTN_SCALE
            dk_tot.append(dk + pltpu.roll(dk, HEAD_DIM, axis=1))
            dv_tot.append(dv + pltpu.roll(dv, HEAD_DIM, axis=1))
        lo2 = _lane_lo((2 * BLOCK, LANES))
        dkv = jnp.concatenate([jnp.where(lo2, dk_tot[0], dk_tot[1]), jnp.where(lo2, dv_tot[0], dv_tot[1])], axis=1)
        dkv_done = dkv[BLOCK:, :] + carry_scr[...]
        dh_ref[:, KV_OFF:KV_OFF + 2 * KV_COLS] = dkv_done.astype(BF16)
        db_ref[:, KV_OFF:KV_OFF + 2 * KV_COLS] += jnp.sum(dkv_done, axis=0, keepdims=True)
        carry_scr[...] = dkv[:BLOCK, :]

    n_in = 12
    return pl.pallas_call(
        body,
        name=name,
        grid=(nb,),
        in_specs=_attn_specs(order) + [
            pl.BlockSpec((BLOCK, ATTN_WIDTH), lambda t: (order(t), 0)),
            pl.BlockSpec((N_KV_HEADS, BIAS_ROWS, BLOCK), lambda t: (0, 0, 0)),
            pl.BlockSpec(memory_space=pl.ANY),
        ],
        out_specs=[
            pl.BlockSpec((BLOCK, DH_ATTN_COLS), lambda t: (order(t), 0)),
            pl.BlockSpec((N_KV_HEADS, BIAS_ROWS, BLOCK), lambda t: (0, 0, 0)),
            pl.BlockSpec((N_KV_HEADS, KV_GROUP, LANES), lambda t: (0, 0, 0)),
            pl.BlockSpec((1, DH_ATTN_COLS), lambda t: (0, 0)),
        ],
        out_shape=[
            jax.ShapeDtypeStruct((S, IN_COLS), BF16),
            jax.ShapeDtypeStruct((N_KV_HEADS, BIAS_ROWS, BLOCK), F32),
            jax.ShapeDtypeStruct((N_KV_HEADS, KV_GROUP, LANES), F32),
            jax.ShapeDtypeStruct((1, DH_ATTN_COLS), F32),
        ],
        scratch_shapes=[pltpu.VMEM((BLOCK, 2 * KV_COLS), F32)],
        input_output_aliases={n_in: 0},
        compiler_params=_params(("arbitrary",)),
    )(h, h, h, h, h, h, h, bias, masks, sinks, dab, dbias_in, dh)


def _window_sum(x, w, back):
    n = x.shape[0]
    s, sh = x, 1
    while sh < w:
        s = s + pltpu.roll(s, sh if back else n - sh, axis=0)
        sh *= 2
    return s


def _pool_counts(first_row, n, w):
    t = first_row + lax.broadcasted_iota(jnp.int32, (n, 1), 0)
    return jnp.minimum(t + 1, w).astype(F32)


def _pool_diff(u_ref, uh_ref, i, T, g):
    u = u_ref[...]
    halo = jnp.where(i > 0, uh_ref[...], 0.0)
    ext = jnp.concatenate([halo, u], axis=0)
    w = POOL_WINDOWS[g]
    s = _window_sum(ext, w, back=True)[POOL_HALO:, :]
    return s / _pool_counts(i * T, T, w) - u


def _pool_in_specs(T):
    hb = T // POOL_HALO
    specs = []
    for g in range(4):
        specs.append(pl.BlockSpec((T, 256), functools.partial(lambda i, g: (i, U_OFF // 256 + g), g=g)))
        specs.append(pl.BlockSpec((POOL_HALO, 256), functools.partial(
            lambda i, g: (jnp.maximum(i * hb - 1, 0), U_OFF // 256 + g), g=g)))
    return specs


def _pool_weight_specs():
    return [pl.BlockSpec((4, 256, 256), lambda i: (0, 0, 0)), pl.BlockSpec((1, POOL_WIDTH), lambda i: (0, 0))]


def _pool_fwd(h, ab, w_pool, pool_scale, name):
    S = h.shape[0]
    T = _tile(S, 512)

    def body(*refs):
        u_refs = refs[0:8]
        gb_refs = refs[8:12]
        wp_ref, sc_ref, o_ref = refs[12], refs[13], refs[15]
        i = pl.program_id(0)
        for g in range(4):
            diff = _pool_diff(u_refs[2 * g], u_refs[2 * g + 1], i, T, g).astype(BF16)
            mixed = jnp.dot(diff, wp_ref[g], preferred_element_type=F32) * sc_ref[:, 256 * g:256 * (g + 1)]
            gb = gb_refs[g][...]
            o_ref[:, 256 * g:256 * (g + 1)] = (mixed * (gb * _sigmoid(gb))).astype(BF16)

    in_specs = _pool_in_specs(T) + [
        pl.BlockSpec((T, 256), functools.partial(lambda i, g: (i, GB_OFF // 256 + g), g=g)) for g in range(4)
    ] + _pool_weight_specs() + [pl.BlockSpec(memory_space=pl.ANY)]
    return pl.pallas_call(
        body,
        name=name,
        grid=(S // T,),
        in_specs=in_specs,
        out_specs=pl.BlockSpec((T, POOL_WIDTH), lambda i: (i, 1)),
        out_shape=jax.ShapeDtypeStruct(ab.shape, BF16),
        input_output_aliases={14: 0},
        compiler_params=_params(("arbitrary",)),
    )(*([h] * 12), w_pool, pool_scale, ab)


DH_POOL_COLS = IN_COLS // 2


def _pool_bwd(h, dab, w_pool, pool_scale, after, name):
    S = h.shape[0]
    T = _tile(S, 512)
    nt = S // T
    hb = T // POOL_HALO
    E = T + POOL_HALO
    lead = U_OFF - DH_POOL_COLS

    def body(*refs):
        u_refs = refs[0:8]
        gb_refs = refs[8:16]
        db_refs = refs[16:24]
        wp_ref, sc_ref = refs[24], refs[25]
        dh_ref, dwp_ref, dsc_ref, dbi_ref = refs[27:31]
        i = pl.program_id(0)

        @pl.when(i == 0)
        def _():
            dwp_ref[...] = jnp.zeros_like(dwp_ref)
            dsc_ref[...] = jnp.zeros_like(dsc_ref)
            dbi_ref[...] = jnp.zeros_like(dbi_ref)

        dh_ref[:, 0:lead] = jnp.zeros((T, lead), BF16)
        for g in range(4):
            w = POOL_WINDOWS[g]
            cols = slice(256 * g, 256 * (g + 1))
            scale = sc_ref[:, cols]
            wp = wp_ref[g]
            diff = _pool_diff(u_refs[2 * g], u_refs[2 * g + 1], i, T, g).astype(BF16)
            mixed = jnp.dot(diff, wp, preferred_element_type=F32)
            keep = i < nt - 1
            gb = jnp.concatenate([gb_refs[2 * g][...], jnp.where(keep, gb_refs[2 * g + 1][...], 0.0)], axis=0)
            db = jnp.concatenate([db_refs[2 * g][...], jnp.where(keep, db_refs[2 * g + 1][...], 0.0)], axis=0)
            sg = _sigmoid(gb)
            dms = db * (gb * sg)
            dmixed = (dms * scale).astype(BF16)
            ddiff = lax.dot_general(dmixed, wp, (((1,), (1,)), ((), ())), preferred_element_type=F32)
            r = ddiff / _pool_counts(i * T, E, w)
            du = _window_sum(r, w, back=False)[:T, :] - ddiff[:T, :]
            dgb = db[:T, :] * (mixed * scale) * (sg[:T, :] * (1.0 + gb[:T, :] * (1.0 - sg[:T, :])))
            u_cols = slice(lead + 256 * g, lead + 256 * (g + 1))
            gb_cols = slice(lead + POOL_WIDTH + 256 * g, lead + POOL_WIDTH + 256 * (g + 1))
            dh_ref[:, u_cols] = du.astype(BF16)
            dh_ref[:, gb_cols] = dgb.astype(BF16)
            dbi_ref[:, u_cols] += jnp.sum(du, axis=0, keepdims=True)
            dbi_ref[:, gb_cols] += jnp.sum(dgb, axis=0, keepdims=True)
            dsc_ref[:, cols] += jnp.sum(dms[:T, :] * mixed, axis=0, keepdims=True)
            dwp_ref[g] += lax.dot_general(diff, dmixed[:T, :], (((0,), (0,)), ((), ())), preferred_element_type=F32)

    def rows_after(i):
        return jnp.minimum((i + 1) * hb, S // POOL_HALO - 1)

    in_specs = _pool_in_specs(T)
    for off in (GB_OFF // 256, ATTN_WIDTH // 256):
        for g in range(4):
            in_specs.append(pl.BlockSpec((T, 256), functools.partial(lambda i, c: (i, c), c=off + g)))
            in_specs.append(pl.BlockSpec((POOL_HALO, 256), functools.partial(lambda i, c: (rows_after(i), c), c=off + g)))
    in_specs += _pool_weight_specs() + [pl.BlockSpec(memory_space=pl.ANY)]
    return pl.pallas_call(
        body,
        name=name,
        grid=(nt,),
        in_specs=in_specs,
        out_specs=[
            pl.BlockSpec((T, DH_POOL_COLS), lambda i: (i, 1)),
            pl.BlockSpec((4, 256, 256), lambda i: (0, 0, 0)),
            pl.BlockSpec((1, POOL_WIDTH), lambda i: (0, 0)),
            pl.BlockSpec((1, DH_POOL_COLS), lambda i: (0, 0)),
        ],
        out_shape=[
            jax.ShapeDtypeStruct((S, IN_COLS), BF16),
            jax.ShapeDtypeStruct((4, 256, 256), F32),
            jax.ShapeDtypeStruct((1, POOL_WIDTH), F32),
            jax.ShapeDtypeStruct((1, DH_POOL_COLS), F32),
        ],
        compiler_params=_params(("arbitrary",)),
    )(*([h] * 16), *([dab] * 8), w_pool, pool_scale, after)


def _load_resident(pairs, sems):
    @pl.when(pl.program_id(0) == 0)
    def _():
        cps = [pltpu.make_async_copy(src, dst, sems.at[n]) for n, (src, dst) in enumerate(pairs)]
        for cp in cps:
            cp.start()
        for cp in cps:
            cp.wait()


def _mix_ln_fwd(ab, x, pb, w_out, w_gate, w_ple_t, gain, bias, alpha, after, name):
    S = x.shape[0]
    T = _tile(S, 256)

    def body(ab_ref, x_ref, p_ref, wo_hbm, wg_hbm, wp_hbm, g_ref, b_ref, after_ref,
             y_ref, yb_ref, xh_ref, rs_ref, gp_ref, pe_ref, wo, wg, wp, sems):
        del after_ref
        _load_resident(((wo_hbm, wo), (wg_hbm, wg), (wp_hbm, wp)), sems)
        x = x_ref[...]
        mix = jnp.dot(ab_ref[...], wo[...], preferred_element_type=F32)
        gp = jnp.dot(x.astype(BF16), wg[...], preferred_element_type=F32)
        pe = lax.dot_general(p_ref[...], wp[...], (((1,), (1,)), ((), ())), preferred_element_type=F32)
        z = alpha * x + mix + _sigmoid(gp) * pe
        mu = jnp.mean(z, axis=-1, keepdims=True)
        zc = z - mu
        var = jnp.mean(zc * zc, axis=-1, keepdims=True)
        rstd = lax.rsqrt(var + LN_EPS)
        xhat = zc * rstd
        y = xhat * g_ref[...] + b_ref[...]
        y_ref[...] = y
        yb_ref[...] = y.astype(BF16)
        xh_ref[...] = xhat
        rs_ref[...] = rstd
        gp_ref[...] = gp
        pe_ref[...] = pe

    row = pl.BlockSpec((T, D_MODEL), lambda i: (i, 0))
    vec = pl.BlockSpec((1, D_MODEL), lambda i: (0, 0))
    any_spec = pl.BlockSpec(memory_space=pl.ANY)
    f32_rows = jax.ShapeDtypeStruct((S, D_MODEL), F32)
    return pl.pallas_call(
        body,
        name=name,
        grid=(S // T,),
        in_specs=[row, row, pl.BlockSpec((T, PLE_DIM), lambda i: (i, 0)), any_spec, any_spec, any_spec, vec, vec, any_spec],
        out_specs=[row, row, row, pl.BlockSpec((T, 1), lambda i: (i, 0)), row, row],
        out_shape=[f32_rows, jax.ShapeDtypeStruct((S, D_MODEL), BF16), f32_rows, jax.ShapeDtypeStruct((S, 1), F32),
                   f32_rows, f32_rows],
        scratch_shapes=[pltpu.VMEM(w_out.shape, BF16), pltpu.VMEM(w_gate.shape, BF16), pltpu.VMEM(w_ple_t.shape, BF16),
                        pltpu.SemaphoreType.DMA((3,))],
        compiler_params=_params(("arbitrary",)),
    )(ab, x, pb, w_out, w_gate, w_ple_t, gain, bias, after)


def _ln_dmix_bwd(dy, xhat, rstd, gain, gp, pe, w_out, w_gate, alpha, after, name):
    S = dy.shape[0]
    T = _tile(S, 256)

    def body(dy_ref, xh_ref, rs_ref, g_ref, gp_ref, pe_ref, wo_hbm, wg_hbm, after_ref,
             dzb_ref, dpe_ref, dgp_ref, dab_ref, dx_ref, dgain_ref, dbias_ref, wo, wg, sems):
        del after_ref
        _load_resident(((wo_hbm, wo), (wg_hbm, wg)), sems)

        @pl.when(pl.program_id(0) == 0)
        def _():
            dgain_ref[...] = jnp.zeros_like(dgain_ref)
            dbias_ref[...] = jnp.zeros_like(dbias_ref)

        dy = dy_ref[...]
        xhat = xh_ref[...]
        dyg = dy * g_ref[...]
        c1 = jnp.mean(dyg, axis=-1, keepdims=True)
        c2 = jnp.mean(dyg * xhat, axis=-1, keepdims=True)
        dz = rs_ref[...] * (dyg - c1 - xhat * c2)
        dgain_ref[...] += jnp.sum(dy * xhat, axis=0, keepdims=True)
        dbias_ref[...] += jnp.sum(dy, axis=0, keepdims=True)
        sg = _sigmoid(gp_ref[...])
        dzb = dz.astype(BF16)
        dgp = (dz * pe_ref[...] * (sg * (1.0 - sg))).astype(BF16)
        nt = (((1,), (1,)), ((), ()))
        dzb_ref[...] = dzb
        dpe_ref[...] = (dz * sg).astype(BF16)
        dgp_ref[...] = dgp
        dab_ref[...] = lax.dot_general(dzb, wo[...], nt, preferred_element_type=F32)
        dx_ref[...] = lax.dot_general(dgp, wg[...], nt, preferred_element_type=F32) + alpha * dz

    row = pl.BlockSpec((T, D_MODEL), lambda i: (i, 0))
    vec = pl.BlockSpec((1, D_MODEL), lambda i: (0, 0))
    any_spec = pl.BlockSpec(memory_space=pl.ANY)
    bf16_rows = jax.ShapeDtypeStruct((S, D_MODEL), BF16)
    f32_rows = jax.ShapeDtypeStruct((S, D_MODEL), F32)
    return pl.pallas_call(
        body,
        name=name,
        grid=(S // T,),
        in_specs=[row, row, pl.BlockSpec((T, 1), lambda i: (i, 0)), vec, row, row, any_spec, any_spec, any_spec],
        out_specs=[row, row, row, row, row, vec, vec],
        out_shape=[bf16_rows, bf16_rows, bf16_rows, f32_rows, f32_rows,
                   jax.ShapeDtypeStruct((1, D_MODEL), F32), jax.ShapeDtypeStruct((1, D_MODEL), F32)],
        scratch_shapes=[pltpu.VMEM(w_out.shape, BF16), pltpu.VMEM(w_gate.shape, BF16), pltpu.SemaphoreType.DMA((2,))],
        compiler_params=_params(("arbitrary",)),
    )(dy, xhat, rstd, gain, gp, pe, w_out, w_gate, after)


def _loss_head(y, target):
    S = y.shape[0]
    T = _tile(S, 256)

    def body(y_ref, t_ref, dy_ref, l_ref):
        @pl.when(pl.program_id(0) == 0)
        def _():
            l_ref[...] = jnp.zeros_like(l_ref)

        err = y_ref[...] - t_ref[...]
        dy_ref[...] = err * (1.0 / D_MODEL)
        per_token = jnp.mean(err * err, axis=-1, keepdims=True)
        l_ref[...] += 0.5 * jnp.sum(per_token, axis=0, keepdims=True)

    row = pl.BlockSpec((T, D_MODEL), lambda i: (i, 0))
    return pl.pallas_call(
        body,
        name="loss_head",
        grid=(S // T,),
        in_specs=[row, row],
        out_specs=[row, pl.BlockSpec((8, LANES), lambda i: (0, 0))],
        out_shape=[jax.ShapeDtypeStruct((S, D_MODEL), F32), jax.ShapeDtypeStruct((8, LANES), F32)],
        compiler_params=_params(("arbitrary",)),
    )(y, target)


def _sum_slabs(r, name):
    _, R, C = r.shape
    T = _tile(R, 256)

    def body(r_ref, o_ref):
        acc = r_ref[0].astype(F32)
        for s in range(1, N_DEV):
            acc = acc + r_ref[s].astype(F32)
        o_ref[...] = acc

    return pl.pallas_call(
        body,
        name=name,
        grid=(R // T,),
        in_specs=[pl.BlockSpec((N_DEV, T, C), lambda i: (0, i, 0))],
        out_specs=pl.BlockSpec((T, C), lambda i: (i, 0)),
        out_shape=jax.ShapeDtypeStruct((R, C), F32),
        compiler_params=_params(("parallel",)),
    )(r)


def _adamw_math(w, g, m, v):
    m = ADAM_B1 * m + (1.0 - ADAM_B1) * g
    v = ADAM_B2 * v + (1.0 - ADAM_B2) * jnp.square(g)
    m_hat = m / (1.0 - ADAM_B1 ** ADAM_STEP)
    v_hat = v / (1.0 - ADAM_B2 ** ADAM_STEP)
    return -ADAM_LR * (m_hat / (jnp.sqrt(v_hat) + ADAM_EPS) + ADAM_WD * w), m, v


def _sum_adamw(r, w, m, v, name):
    R, C = w.shape
    T = _tile(R, 256)

    def body(r_ref, w_ref, m_ref, v_ref, g_ref, d_ref, nm_ref, nv_ref):
        g = r_ref[0].astype(F32)
        for s in range(1, N_DEV):
            g = g + r_ref[s].astype(F32)
        g_ref[...] = g
        d_ref[...], nm_ref[...], nv_ref[...] = _adamw_math(w_ref[...], g, m_ref[...], v_ref[...])

    blk = pl.BlockSpec((T, C), lambda i: (i, 0))
    shp = jax.ShapeDtypeStruct((R, C), F32)
    return pl.pallas_call(
        body,
        name=name,
        grid=(R // T,),
        in_specs=[pl.BlockSpec((N_DEV, T, C), lambda i: (0, i, 0))] + [blk] * 3,
        out_specs=[blk] * 4,
        out_shape=[shp] * 4,
        compiler_params=_params(("parallel",)),
    )(r, w, m, v)


def _adamw(w, g, m, v, name):
    R, C = w.shape
    T = _tile(R, 256)
    grid = (R // T,)
    blk = pl.BlockSpec((T, C), lambda i: (i, 0))

    def body(w_ref, g_ref, m_ref, v_ref, d_ref, nm_ref, nv_ref):
        d_ref[...], nm_ref[...], nv_ref[...] = _adamw_math(w_ref[...], g_ref[...], m_ref[...], v_ref[...])

    shp = jax.ShapeDtypeStruct(w.shape, F32)
    return pl.pallas_call(
        body,
        name=name,
        grid=grid,
        in_specs=[blk] * 4,
        out_specs=[blk] * 3,
        out_shape=[shp] * 3,
        compiler_params=_params(("parallel",) * len(grid)),
    )(w, g, m, v)


def _mesh_pos():
    return lax.axis_index("x"), lax.axis_index("y"), lax.axis_index("c")


def _flip(pos, k):
    x, y, c = pos
    return (1 - x if k & 4 else x, 1 - y if k & 2 else y, 1 - c if k & 1 else c)


def _index(pos):
    return 4 * pos[0] + 2 * pos[1] + pos[2]


HBM_SPEC = pl.BlockSpec(memory_space=pltpu.HBM)
SEM_SPEC = pl.BlockSpec(memory_space=pltpu.SEMAPHORE)
ANY_SPEC = pl.BlockSpec(memory_space=pl.ANY)
SPLIT_EFFECT = pltpu.SideEffectType.DATAFLOW_SIDE_EFFECTING
GATHER_FLIPS = (1, 4, 2, 6)
CHIP_FLIPS = (4, 2, 6)
TOKEN = jax.ShapeDtypeStruct((8, LANES), F32)


def _hbm(a):
    return pltpu.with_memory_space_constraint(a, pltpu.HBM)


def _hbm_like(a):
    return pltpu.HBM(a.shape, a.dtype)


def _block_rows(ref, pos, r):
    return ref.at[:, pl.ds(_index(pos) * r, r), :]


def _gather_start(shards, after, name):
    n = len(shards)
    lands = [lax.empty((s.shape[0], N_DEV * s.shape[1], s.shape[2]), s.dtype) for s in shards]

    def body(*refs):
        ins, bufs = refs[:n], refs[n:2 * n]
        send_sems, recv_sems = refs[2 * n + 1], refs[2 * n + 2]
        token = refs[4 * n + 3]
        me = _mesh_pos()
        for a in range(n):
            for j, k in enumerate(GATHER_FLIPS):
                pltpu.make_async_remote_copy(
                    src_ref=ins[a], dst_ref=_block_rows(bufs[a], me, shards[a].shape[1]),
                    send_sem=send_sems.at[4 * a + j], recv_sem=recv_sems.at[4 * a + j],
                    device_id=_flip(me, k), device_id_type=MESH_ID).start()
        token[...] = jnp.zeros_like(token)

    outs = pl.pallas_call(
        body,
        name=name,
        in_specs=[HBM_SPEC] * (2 * n) + [ANY_SPEC],
        out_specs=[SEM_SPEC, SEM_SPEC] + [HBM_SPEC] * (2 * n) + [pl.BlockSpec(memory_space=pltpu.VMEM)],
        out_shape=[pltpu.SemaphoreType.DMA((4 * n,)), pltpu.SemaphoreType.DMA((4 * n,))]
        + [_hbm_like(s) for s in shards] + [_hbm_like(b) for b in lands] + [TOKEN],
        input_output_aliases={i: 2 + i for i in range(2 * n)},
        compiler_params=pltpu.CompilerParams(has_side_effects=SPLIT_EFFECT),
    )(*[_hbm(s) for s in shards], *[_hbm(b) for b in lands], after)
    return outs[0], outs[1], outs[2:2 + n], outs[2 + n:2 + 2 * n], outs[2 + 2 * n]


def _gather_wait(started, after, name):
    send_sems, recv_sems, shards, lands, _ = started
    n = len(shards)

    def body(*refs):
        ins, bufs = refs[:n], refs[n:2 * n]
        send_sems, recv_sems = refs[2 * n], refs[2 * n + 1]
        me = _mesh_pos()
        for a in range(n):
            for j, k in enumerate(GATHER_FLIPS):
                cp = pltpu.make_async_remote_copy(
                    src_ref=ins[a], dst_ref=_block_rows(bufs[a], _flip(me, k), shards[a].shape[1]),
                    send_sem=send_sems.at[4 * a + j], recv_sem=recv_sems.at[4 * a + j],
                    device_id=_flip(me, k), device_id_type=MESH_ID)
                cp.wait_send()
                cp.wait_recv()

    outs = pl.pallas_call(
        body,
        name=name,
        in_specs=[HBM_SPEC] * (2 * n) + [SEM_SPEC, SEM_SPEC, ANY_SPEC],
        out_specs=[HBM_SPEC] * (2 * n),
        out_shape=[_hbm_like(s) for s in shards] + [_hbm_like(b) for b in lands],
        input_output_aliases={i: i for i in range(2 * n)},
        compiler_params=pltpu.CompilerParams(has_side_effects=SPLIT_EFFECT),
    )(*shards, *lands, send_sems, recv_sems, after)
    return outs[:n], outs[n:]


def _gather_pass(shards, lands, name):
    n = len(shards)

    def body(*refs):
        ins, bufs = refs[:n], refs[n:2 * n]
        token = refs[3 * n]
        send_sems, recv_sems, local_sems = refs[3 * n + 1:]
        me = _mesh_pos()
        sibling = _flip(me, 1)

        def copy(a, j, block):
            rows = _block_rows(bufs[a], block, shards[a].shape[1])
            return pltpu.make_async_remote_copy(
                src_ref=rows, dst_ref=rows, send_sem=send_sems.at[3 * a + j], recv_sem=recv_sems.at[3 * a + j],
                device_id=sibling, device_id_type=MESH_ID)

        mine = [pltpu.make_async_copy(ins[a], _block_rows(bufs[a], me, shards[a].shape[1]), local_sems.at[a])
                for a in range(n)]
        sends = [copy(a, j, _flip(me, k)) for a in range(n) for j, k in enumerate(CHIP_FLIPS)]
        for cp in mine + sends:
            cp.start()
        for a in range(n):
            for j, k in enumerate(CHIP_FLIPS):
                copy(a, j, _flip(sibling, k)).wait_recv()
        for cp in sends:
            cp.wait_send()
        for cp in mine:
            cp.wait()
        token[...] = jnp.zeros_like(token)

    outs = pl.pallas_call(
        body,
        name=name,
        in_specs=[pl.BlockSpec(memory_space=pltpu.VMEM)] * n + [ANY_SPEC] * n,
        out_specs=[ANY_SPEC] * n + [pl.BlockSpec(memory_space=pltpu.VMEM)],
        out_shape=[jax.ShapeDtypeStruct(b.shape, b.dtype) for b in lands] + [TOKEN],
        scratch_shapes=[pltpu.SemaphoreType.DMA((3 * n,)), pltpu.SemaphoreType.DMA((3 * n,)), pltpu.SemaphoreType.DMA((n,))],
        input_output_aliases={n + i: i for i in range(n)},
        compiler_params=pltpu.CompilerParams(has_side_effects=True, vmem_limit_bytes=VMEM_LIMIT_BYTES),
    )(*shards, *lands)
    return outs[:n], outs[n]


def _gather_own(shards, lands, name):
    n = len(shards)

    def body(*refs):
        ins, bufs, local_sems = refs[:n], refs[n:2 * n], refs[3 * n]
        me = _mesh_pos()
        cps = [pltpu.make_async_copy(ins[a], _block_rows(bufs[a], me, shards[a].shape[1]), local_sems.at[a])
               for a in range(n)]
        for cp in cps:
            cp.start()
        for cp in cps:
            cp.wait()

    return pl.pallas_call(
        body,
        name=name,
        in_specs=[pl.BlockSpec(memory_space=pltpu.VMEM)] * n + [ANY_SPEC] * n,
        out_specs=[ANY_SPEC] * n,
        out_shape=[jax.ShapeDtypeStruct(b.shape, b.dtype) for b in lands],
        scratch_shapes=[pltpu.SemaphoreType.DMA((n,))],
        input_output_aliases={n + i: i for i in range(n)},
        compiler_params=pltpu.CompilerParams(has_side_effects=True, vmem_limit_bytes=VMEM_LIMIT_BYTES),
    )(*shards, *lands)


def _pass_copy(bufs, send_sems, recv_sems, a, j, block, sibling):
    rows = _block_rows(bufs[a], block, bufs[a].shape[1] // N_DEV)
    return pltpu.make_async_remote_copy(
        src_ref=rows, dst_ref=rows, send_sem=send_sems.at[3 * a + j], recv_sem=recv_sems.at[3 * a + j],
        device_id=sibling, device_id_type=MESH_ID)


def _pass_start(lands, after, name):
    n = len(lands)

    def body(*refs):
        bufs = refs[:n]
        send_sems, recv_sems = refs[n + 1], refs[n + 2]
        token = refs[2 * n + 3]
        me = _mesh_pos()
        for a in range(n):
            for j, k in enumerate(CHIP_FLIPS):
                _pass_copy(bufs, send_sems, recv_sems, a, j, _flip(me, k), _flip(me, 1)).start()
        token[...] = jnp.zeros_like(token)

    outs = pl.pallas_call(
        body,
        name=name,
        in_specs=[HBM_SPEC] * n + [ANY_SPEC],
        out_specs=[SEM_SPEC, SEM_SPEC] + [HBM_SPEC] * n + [pl.BlockSpec(memory_space=pltpu.VMEM)],
        out_shape=[pltpu.SemaphoreType.DMA((3 * n,)), pltpu.SemaphoreType.DMA((3 * n,))]
        + [_hbm_like(b) for b in lands] + [TOKEN],
        input_output_aliases={i: 2 + i for i in range(n)},
        compiler_params=pltpu.CompilerParams(has_side_effects=SPLIT_EFFECT),
    )(*[_hbm(b) for b in lands], after)
    return outs[0], outs[1], outs[2:2 + n], outs[2 + n]


def _pass_wait(started, after, name):
    send_sems, recv_sems, lands, _ = started
    n = len(lands)

    def body(*refs):
        bufs = refs[:n]
        send_sems, recv_sems = refs[n], refs[n + 1]
        me = _mesh_pos()
        sibling = _flip(me, 1)
        for a in range(n):
            for j, k in enumerate(CHIP_FLIPS):
                _pass_copy(bufs, send_sems, recv_sems, a, j, _flip(me, k), sibling).wait_send()
                _pass_copy(bufs, send_sems, recv_sems, a, j, _flip(sibling, k), sibling).wait_recv()

    return pl.pallas_call(
        body,
        name=name,
        in_specs=[HBM_SPEC] * n + [SEM_SPEC, SEM_SPEC, ANY_SPEC],
        out_specs=[HBM_SPEC] * n,
        out_shape=[_hbm_like(b) for b in lands],
        input_output_aliases={i: i for i in range(n)},
        compiler_params=pltpu.CompilerParams(has_side_effects=SPLIT_EFFECT),
    )(*lands, send_sems, recv_sems, after)


def _place_own(grads, lands, layer, name):
    n = len(grads)
    blocks = [(g.shape[0], g.shape[1] // N_DEV, g.shape[2]) for g in grads]

    def body(*refs):
        ins, bufs = refs[:n], refs[n:2 * n]
        stage, in_sems, out_sems = refs[3 * n:4 * n], refs[4 * n], refs[4 * n + 1]
        me = _mesh_pos()
        loads = [pltpu.make_async_copy(_block_rows(ins[a], me, blocks[a][1]), stage[a], in_sems.at[a]) for a in range(n)]
        stores = [pltpu.make_async_copy(stage[a], bufs[a].at[_index(me), layer], out_sems.at[a]) for a in range(n)]
        for cp in loads:
            cp.start()
        for a in range(n):
            loads[a].wait()
            stores[a].start()
        for cp in stores:
            cp.wait()

    return pl.pallas_call(
        body,
        name=name,
        in_specs=[ANY_SPEC] * (2 * n),
        out_specs=[ANY_SPEC] * n,
        out_shape=[jax.ShapeDtypeStruct(b.shape, b.dtype) for b in lands],
        scratch_shapes=[pltpu.VMEM(blk, g.dtype) for blk, g in zip(blocks, grads)]
        + [pltpu.SemaphoreType.DMA((n,)), pltpu.SemaphoreType.DMA((n,))],
        input_output_aliases={n + i: i for i in range(n)},
        compiler_params=pltpu.CompilerParams(has_side_effects=True, vmem_limit_bytes=VMEM_LIMIT_BYTES),
    )(*grads, *lands)


def _scatter_copy(ins, bufs, send_sems, recv_sems, a, k, r, layer, me, slab):
    peer = _flip(me, k)
    return pltpu.make_async_remote_copy(
        src_ref=_block_rows(ins[a], peer, r), dst_ref=bufs[a].at[_index(slab), layer],
        send_sem=send_sems.at[7 * a + k - 1], recv_sem=recv_sems.at[7 * a + k - 1],
        device_id=peer, device_id_type=MESH_ID)


def _scatter_start(grads, lands, layer, name):
    n = len(grads)

    def body(*refs):
        ins, bufs = refs[:n], refs[n:2 * n]
        send_sems, recv_sems = refs[2 * n], refs[2 * n + 1]
        token = refs[4 * n + 2]
        me = _mesh_pos()
        for a in range(n):
            for k in range(1, N_DEV):
                _scatter_copy(ins, bufs, send_sems, recv_sems, a, k, grads[a].shape[1] // N_DEV, layer, me, me).start()
        token[...] = jnp.zeros_like(token)

    outs = pl.pallas_call(
        body,
        name=name,
        in_specs=[HBM_SPEC] * (2 * n),
        out_specs=[SEM_SPEC, SEM_SPEC] + [HBM_SPEC] * (2 * n) + [pl.BlockSpec(memory_space=pltpu.VMEM)],
        out_shape=[pltpu.SemaphoreType.DMA((7 * n,)), pltpu.SemaphoreType.DMA((7 * n,))]
        + [_hbm_like(g) for g in grads] + [_hbm_like(b) for b in lands] + [TOKEN],
        input_output_aliases={i: 2 + i for i in range(2 * n)},
        compiler_params=pltpu.CompilerParams(has_side_effects=SPLIT_EFFECT),
    )(*[_hbm(g) for g in grads], *[_hbm(b) for b in lands])
    return outs[0], outs[1], outs[2:2 + n], outs[2 + n:2 + 2 * n], outs[2 + 2 * n]


def _scatter_wait(started, layer, after, name):
    send_sems, recv_sems, grads, lands, _ = started
    n = len(grads)

    def body(*refs):
        ins, bufs = refs[:n], refs[n:2 * n]
        send_sems, recv_sems = refs[2 * n], refs[2 * n + 1]
        me = _mesh_pos()
        for a in range(n):
            for k in range(1, N_DEV):
                cp = _scatter_copy(ins, bufs, send_sems, recv_sems, a, k, grads[a].shape[1] // N_DEV, layer, me, _flip(me, k))
                cp.wait_send()
                cp.wait_recv()

    outs = pl.pallas_call(
        body,
        name=name,
        in_specs=[HBM_SPEC] * (2 * n) + [SEM_SPEC, SEM_SPEC, ANY_SPEC],
        out_specs=[HBM_SPEC] * (2 * n),
        out_shape=[_hbm_like(g) for g in grads] + [_hbm_like(b) for b in lands],
        input_output_aliases={i: i for i in range(2 * n)},
        compiler_params=pltpu.CompilerParams(has_side_effects=SPLIT_EFFECT),
    )(*grads, *lands, send_sems, recv_sems, after)
    return outs[n:]


def _allreduce_small(vec):
    R, C = vec.shape

    def body(v_ref, o_ref, buf, send_sems, recv_sems):
        me = _mesh_pos()
        buf[_index(me)] = v_ref[...]
        sends = []
        for k in range(1, N_DEV):
            sends.append(pltpu.make_async_remote_copy(
                src_ref=buf.at[_index(me)], dst_ref=buf.at[_index(me)],
                send_sem=send_sems.at[k - 1], recv_sem=recv_sems.at[k - 1],
                device_id=_flip(me, k), device_id_type=MESH_ID))
        for cp in sends:
            cp.start()
        for cp in sends:
            cp.wait_recv()
        for cp in sends:
            cp.wait_send()
        acc = buf[0]
        for s in range(1, N_DEV):
            acc = acc + buf[s]
        o_ref[...] = acc

    return pl.pallas_call(
        body,
        name="allreduce_small",
        in_specs=[pl.BlockSpec(memory_space=pltpu.VMEM)],
        out_specs=pl.BlockSpec(memory_space=pltpu.VMEM),
        out_shape=jax.ShapeDtypeStruct((R, C), F32),
        scratch_shapes=[pltpu.VMEM((N_DEV, R, C), F32), pltpu.SemaphoreType.DMA((7,)), pltpu.SemaphoreType.DMA((7,))],
        compiler_params=pltpu.CompilerParams(has_side_effects=True, vmem_limit_bytes=VMEM_LIMIT_BYTES),
    )(vec)


def _pack_small(parts):
    flat = jnp.concatenate([p.reshape(-1) for p in parts])
    n = flat.shape[0]
    rows = -(-n // SMALL_COLS)
    rows = -(-rows // 8) * 8
    return jnp.pad(flat, (0, rows * SMALL_COLS - n)).reshape(rows, SMALL_COLS)


def _unpack_small(packed, like):
    flat = packed.reshape(-1)
    out, pos = [], 0
    for p in like:
        out.append(flat[pos:pos + p.size].reshape(p.shape))
        pos += p.size
    return out


def kernel(x, p, w_in, b_in, w_out, attn_sinks, rel_bias, w_pool, pool_scale, w_ple, w_gate_ple, ln_gain, ln_bias, loss_target, m_w_in, m_b_in, m_w_out, m_attn_sinks, m_rel_bias, m_w_pool, m_pool_scale, m_w_ple, m_w_gate_ple, m_ln_gain, m_ln_bias, v_w_in, v_b_in, v_w_out, v_attn_sinks, v_rel_bias, v_w_pool, v_pool_scale, v_w_ple, v_w_gate_ple, v_ln_gain, v_ln_bias):
    L = w_in.shape[0]
    S = x.shape[1]
    alpha = (2.0 * L) ** 0.25
    bucket_np, masks_np, window_np = _band_constants()
    bucket, masks, window = jnp.asarray(bucket_np), jnp.asarray(masks_np), jnp.asarray(window_np)

    @functools.lru_cache(maxsize=None)
    def shards_of(l):
        return (jnp.swapaxes(w_in[l], 0, 1).astype(BF16)[None], w_out[l].astype(BF16)[None],
                w_gate_ple[l].astype(BF16)[None], jnp.swapaxes(w_ple[l], 0, 1).astype(BF16)[None], w_pool[l].astype(BF16))

    def gathered(started, after, tag):
        shards, lands = _gather_wait(started, after, name=f"gather_wait_{tag}")
        return _gather_pass(shards, lands, name=f"gather_pass_{tag}")

    bias = _bias_build(rel_bias, bucket).reshape(N_KV_HEADS, BIAS_ROWS, BLOCK)

    xs = x[0]
    xb = xs.astype(BF16)
    first_groups = ((0, 4), (1, 2, 3))
    token, first_started = rel_bias, []
    for tag, idxs in zip("ab", first_groups):
        first_started.append(_gather_start([shards_of(0)[i] for i in idxs], token, name=f"gather_start_0{tag}"))
        token = first_started[-1][4]
    started = {1: _gather_start(shards_of(1), token, name="gather_start_1")} if L > 1 else {}
    saved = []
    for l in range(L):
        pb = p[l, 0].astype(BF16)
        sinks_l = attn_sinks[l]
        scale_l = pool_scale[l].reshape(1, POOL_WIDTH)
        bias_l = b_in[l].reshape(1, IN_COLS)
        if l == 0:
            (w_in_f, w_pool_g), _ = gathered(first_started[0], started[1][4] if L > 1 else xb, "0a")
            w_in_t = w_in_f[0]
            h = _matmul(xb, w_in_t, tb=True, tm=512, tn=2176, tk=2048, out_dtype=F32, bias=bias_l, name=f"in_proj_{l}")
            (w_out_f, w_gate_f, w_ple_f), _ = gathered(first_started[1], h, "0b")
            w_out_g, w_gate_g, w_ple_t = w_out_f[0], w_gate_f[0], w_ple_f[0]
        else:
            w_in_t, w_out_g, w_gate_g, w_ple_t, w_pool_g = weights
            h = _matmul(xb, w_in_t, tb=True, tm=512, tn=2176, tk=2048, out_dtype=F32, bias=bias_l, name=f"in_proj_{l}")
        weights = (w_in_t, w_out_g, w_gate_g, w_ple_t, w_pool_g)
        ab = _attn_fwd(h, bias, masks, sinks_l, name=f"attn_fwd_{l}")
        ab = _pool_fwd(h, ab, w_pool_g, scale_l, name=f"pool_fwd_{l}")
        pin, passing = ab, None
        if 1 <= l and l + 1 < L:
            shards, lands = _gather_wait(started[l + 1], ab, name=f"gather_wait_{l + 1}")
            passing = _pass_start(_gather_own(shards, lands, name=f"gather_own_{l + 1}"), ab, name=f"pass_start_{l + 1}")
            pin = passing[3]
        if l + 2 < L:
            started[l + 2] = _gather_start(shards_of(l + 2), pin, name=f"gather_start_{l + 2}")
            pin = started[l + 2][4]
        y, yb, xhat, rstd, gp, pe = _mix_ln_fwd(ab, xs, pb, w_out_g, w_gate_g, w_ple_t, ln_gain[l].reshape(1, D_MODEL),
                                                ln_bias[l].reshape(1, D_MODEL), alpha, pin, name=f"mix_ln_fwd_{l}")
        saved.append((xb, pb, h, gp, pe, ab, xhat, rstd, sinks_l, scale_l, weights))
        xs, xb = y, yb
        if l + 1 < L:
            if passing is None:
                full, _ = gathered(started[l + 1], yb, l + 1)
            else:
                full = _pass_wait(passing, yb, name=f"pass_wait_{l + 1}")
            weights = (full[0][0], full[1][0], full[2][0], full[3][0], full[4])

    dy, loss_tile = _loss_head(xs, loss_target[0])

    dbias = jnp.zeros((N_KV_HEADS, BIAS_ROWS, BLOCK), F32)
    sh0 = shards_of(0)
    lands_a = [lax.empty((N_DEV, L) + sh0[i].shape, BF16) for i in (1, 2, 3)]
    lands_b = [lax.empty((N_DEV, L) + sh0[i].shape, BF16) for i in (0, 4)]
    g_b_in, g_sinks, g_scale, g_gain, g_beta = [], [], [], [], []
    pend_a = pend_b = None

    def scatter(grads, lands, pending, l, tag):
        if pending:
            lands = _scatter_wait(pending[0], pending[1], grads[0], name=f"scatter_wait_{pending[1]}{tag}")
        lands = _place_own(grads, lands, l, name=f"place_own_{l}{tag}")
        return _scatter_start(grads, lands, l, name=f"scatter_start_{l}{tag}"), l

    for l in reversed(range(L)):
        xb, pb, h, gp, pe, ab, xhat, rstd, sinks_l, scale_l, weights = saved[l]
        w_in_t, w_out_g, w_gate_g, w_ple_t, w_pool_g = weights
        dzb, dpe, dgp, dab, dx, dgain, dbeta = _ln_dmix_bwd(
            dy, xhat, rstd, ln_gain[l].reshape(1, D_MODEL), gp, pe, w_out_g, w_gate_g, alpha,
            pend_b[0][4] if pend_b else rel_bias, name=f"ln_dmix_bwd_{l}")
        g_w_out = _matmul(ab, dzb, ta=True, tm=512, tn=1024, tk=4096, out_dtype=BF16, name=f"dw_out_{l}")
        g_w_gate = _matmul(xb, dgp, ta=True, tm=512, tn=1024, tk=4096, out_dtype=BF16, name=f"dw_gate_{l}")
        g_w_ple_t = _matmul(dpe, pb, ta=True, tm=1024, tn=256, tk=1024, out_dtype=BF16, name=f"dw_ple_{l}")
        pend_a = scatter([g_w_out[None], g_w_gate[None], g_w_ple_t[None]], lands_a, pend_a, l, "a")
        dh, dwp, dsc, db_pool = _pool_bwd(h, dab, w_pool_g, scale_l, pend_a[0][4], name=f"pool_bwd_{l}")
        dh, dbias, dsink, db_attn = _attn_bwd(h, dab, dh, bias, masks, sinks_l, dbias, name=f"attn_bwd_{l}")
        g_w_in_t = _matmul(dh, xb, ta=True, tm=256, tn=1024, tk=4096, out_dtype=BF16, name=f"dw_in_{l}")
        pend_b = scatter([g_w_in_t[None], dwp.astype(BF16)], lands_b, pend_b, l, "b")
        g_b_in.append(jnp.concatenate([db_attn[0], db_pool[0, U_OFF - DH_POOL_COLS:]]))
        dy = _matmul(dh, w_in_t, tm=512, tn=1024, tk=4352, out_dtype=F32, add=dx, after=pend_b[0][4], name=f"dx_in_{l}")
        g_sinks.append(dsink[:, :, 0].reshape(N_HEADS))
        g_scale.append(dsc.reshape(POOL_WIDTH))
        g_gain.append(dgain.reshape(D_MODEL))
        g_beta.append(dbeta.reshape(D_MODEL))
    grad_x = dy[None]
    for lst in (g_b_in, g_sinks, g_scale, g_gain, g_beta):
        lst.reverse()
    g_rel = _bias_bwd(dbias.reshape((N_HEADS,) + BAND), bucket, window)[:, :N_HEADS]

    def flat(a):
        return a.reshape((-1, a.shape[-1]))

    def big(w, g, m, v, name):
        d, nm, nv = _adamw(flat(w), flat(g), flat(m), flat(v), name=name)
        return d.reshape(w.shape), nm.reshape(w.shape), nv.reshape(w.shape)

    def reduced(r, w, m, v, name):
        g, d, nm, nv = _sum_adamw(r.reshape(N_DEV, -1, r.shape[-1]), flat(w), flat(m), flat(v), name=name)
        return g.reshape(w.shape), (d.reshape(w.shape), nm.reshape(w.shape), nv.reshape(w.shape))

    r_out, r_gate, r_ple = _scatter_wait(pend_a[0], pend_a[1], dy, name="scatter_wait_0a")
    small_like = [b_in, attn_sinks, rel_bias, pool_scale, ln_gain, ln_bias]
    small_g = _allreduce_small(_pack_small([
        jnp.stack(g_b_in).reshape(L, IN_COLS), jnp.stack(g_sinks), g_rel, jnp.stack(g_scale), jnp.stack(g_gain),
        jnp.stack(g_beta), loss_tile[0, :1]]))
    grad_w_out, upd_out = reduced(r_out, w_out, m_w_out, v_w_out, "adamw_w_out")
    grad_w_gate, upd_gate = reduced(r_gate, w_gate_ple, m_w_gate_ple, v_w_gate_ple, "adamw_w_gate")
    gt_ple = _sum_slabs(r_ple.reshape(N_DEV, L * 256, PLE_DIM), name="sum_w_ple")
    grad_w_ple = jnp.swapaxes(gt_ple.reshape(L, 256, PLE_DIM), 1, 2)
    upd_ple = big(w_ple, grad_w_ple, m_w_ple, v_w_ple, "adamw_w_ple")

    r_in, r_pool = _scatter_wait(pend_b[0], pend_b[1], upd_gate[0], name="scatter_wait_0b")
    gt_in, upd_in_t = reduced(r_in, jnp.swapaxes(w_in, 1, 2), jnp.swapaxes(m_w_in, 1, 2), jnp.swapaxes(v_w_in, 1, 2),
                              "adamw_w_in")
    grad_w_in = jnp.swapaxes(gt_in, 1, 2)
    upd_in = tuple(jnp.swapaxes(u, 1, 2) for u in upd_in_t)
    grad_w_pool, upd_pool = reduced(r_pool, w_pool, m_w_pool, v_w_pool, "adamw_w_pool")

    zero1 = jnp.zeros((1,), F32)
    sw = _pack_small(small_like + [zero1])
    sm = _pack_small([m_b_in, m_attn_sinks, m_rel_bias, m_pool_scale, m_ln_gain, m_ln_bias, zero1])
    sv = _pack_small([v_b_in, v_attn_sinks, v_rel_bias, v_pool_scale, v_ln_gain, v_ln_bias, zero1])
    sd, snm, snv = _adamw(sw, small_g, sm, sv, name="adamw_small")
    like = small_like + [zero1]
    sg_parts = _unpack_small(small_g, like)
    sd_parts, snm_parts, snv_parts = _unpack_small(sd, like), _unpack_small(snm, like), _unpack_small(snv, like)
    loss = sg_parts[6][0]

    def assemble(big_parts, small_parts):
        w_in_, w_out_, w_pool_, w_ple_, w_gate_ = big_parts
        b_in_, sinks_, rel_, scale_, gain_, beta_ = small_parts[:6]
        return [w_in_, b_in_, w_out_, sinks_, rel_, w_pool_, scale_, w_ple_, w_gate_, gain_, beta_]

    grads = assemble([grad_w_in, grad_w_out, grad_w_pool, grad_w_ple, grad_w_gate], sg_parts)
    ups = [upd_in, upd_out, upd_pool, upd_ple, upd_gate]
    deltas = assemble([u[0] for u in ups], sd_parts)
    new_m = assemble([u[1] for u in ups], snm_parts)
    new_v = assemble([u[2] for u in ups], snv_parts)
    return (loss, grad_x, *grads, *deltas, *new_m, *new_v)
```

```python
import functools
import math

import numpy as np
import jax
import jax.numpy as jnp
from jax import lax
from jax.experimental import pallas as pl
from jax.experimental.pallas import tpu as pltpu

F32 = jnp.float32
BF16 = jnp.bfloat16

D_MODEL = 2048
PLE_DIM = 256
ATTN_WIDTH = 1024
POOL_WIDTH = 1024
HEAD_DIM = 64
N_HEADS = 16
N_KV_HEADS = 2
KV_GROUP = 8
WINDOW = 128
BLOCK = 128
POOL_WINDOWS = (2, 4, 8, 16)
POOL_GROUP_DIM = 256
POOL_HALO = 16
REL_BUCKETS = 32
REL_MAX_DIST = 128
LN_EPS = 1e-5
KV_COLS = N_KV_HEADS * HEAD_DIM
IN_COLS = 4352
Q_OFF, KV_OFF, GA_OFF, U_OFF, GB_OFF = 0, 1024, 1280, 2304, 3328
ATTN_SCALE = 1.0 / math.sqrt(HEAD_DIM)
NEG_BIG = -1e30
LANES = 128

ADAM_LR = 0.001
ADAM_B1 = 0.9
ADAM_B2 = 0.999
ADAM_EPS = 1e-08
ADAM_WD = 0.01
ADAM_STEP = 10

N_DEV = 8
MESH_ID = pl.DeviceIdType.MESH
VMEM_LIMIT_BYTES = 52 * 1024 * 1024
SMALL_COLS = 1024


def _params(sem=None):
    return pltpu.CompilerParams(dimension_semantics=sem, vmem_limit_bytes=VMEM_LIMIT_BYTES)


def _sigmoid(x):
    return 1.0 / (1.0 + jnp.exp(-x))


def _tile(n, pref, unit=16):
    if n <= pref:
        return n
    t = pref - pref % unit
    while n % t:
        t -= unit
    assert t > 0, (n, pref)
    return t


def _matmul(a, b, *, name, ta=False, tb=False, tm, tn, tk, out_dtype, bias=None, add=None, add_scale=1.0, after=None):
    M, K = (a.shape[1], a.shape[0]) if ta else a.shape
    N = b.shape[0] if tb else b.shape[1]
    assert (b.shape[1] if tb else b.shape[0]) == K
    tm, tn, tk = _tile(M, tm), _tile(N, tn), _tile(K, tk)
    nm, nn, nk = M // tm, N // tn, K // tk
    a_spec = pl.BlockSpec((tk, tm), lambda j, i, k: (k, i)) if ta else pl.BlockSpec((tm, tk), lambda j, i, k: (i, k))
    b_spec = pl.BlockSpec((tn, tk), lambda j, i, k: (j, k)) if tb else pl.BlockSpec((tk, tn), lambda j, i, k: (k, j))
    dims = (((0 if ta else 1,), (1 if tb else 0,)), ((), ()))
    operands, in_specs = [a, b], [a_spec, b_spec]
    if bias is not None:
        operands.append(bias)
        in_specs.append(pl.BlockSpec((1, tn), lambda j, i, k: (0, j)))
    if add is not None:
        operands.append(add)
        in_specs.append(pl.BlockSpec((tm, tn), lambda j, i, k: (i, j)))
    if after is not None:
        operands.append(after)
        in_specs.append(pl.BlockSpec(memory_space=pl.ANY))

    def body(*refs):
        a_ref, b_ref = refs[0], refs[1]
        pos = 2
        bias_ref = add_ref = None
        if bias is not None:
            bias_ref = refs[pos]
            pos += 1
        if add is not None:
            add_ref = refs[pos]
            pos += 1
        if after is not None:
            pos += 1
        o_ref = refs[pos]
        part = lax.dot_general(a_ref[...].astype(BF16), b_ref[...].astype(BF16), dims, preferred_element_type=F32)

        def finish(acc):
            if bias_ref is not None:
                acc = acc + bias_ref[...]
            if add_ref is not None:
                acc = acc + add_scale * add_ref[...].astype(F32)
            o_ref[...] = acc.astype(out_dtype)

        if nk == 1:
            finish(part)
        else:
            acc_ref = refs[pos + 1]
            k = pl.program_id(2)

            @pl.when(k == 0)
            def _():
                acc_ref[...] = part

            @pl.when(k > 0)
            def _():
                acc_ref[...] += part

            @pl.when(k == nk - 1)
            def _():
                finish(acc_ref[...])

    return pl.pallas_call(
        body,
        name=name,
        grid=(nn, nm, nk),
        in_specs=in_specs,
        out_specs=pl.BlockSpec((tm, tn), lambda j, i, k: (i, j)),
        out_shape=jax.ShapeDtypeStruct((M, N), out_dtype),
        scratch_shapes=[pltpu.VMEM((tm, tn), F32)] if nk > 1 else [],
        compiler_params=_params(("parallel", "parallel", "arbitrary")),
    )(*operands)


BAND = (2 * BLOCK, BLOCK)
BIAS_ROWS = KV_GROUP * 2 * BLOCK


def _band_constants():
    qq = np.arange(BLOCK)[None, :]
    kk = np.arange(2 * BLOCK)[:, None]
    dist = qq + BLOCK - kk
    in_window = (dist >= 0) & (dist < WINDOW)
    max_exact = REL_BUCKETS // 2
    d = np.maximum(dist, 0)
    d_f = np.maximum(d, 1).astype(np.float32)
    large = max_exact + (
        np.log(d_f / np.float32(max_exact)) / np.float32(math.log(REL_MAX_DIST / max_exact)) * np.float32(REL_BUCKETS - max_exact)
    ).astype(np.int32)
    large = np.minimum(large, REL_BUCKETS - 1)
    bucket = np.where(d < max_exact, d, large).astype(np.int32)
    bucket = np.where(in_window, bucket, 0).astype(np.int32)
    first = in_window & (kk >= BLOCK)
    masks = np.stack([first, in_window]).astype(np.float32)
    return bucket, masks, in_window.astype(np.float32)


def _bias_build(rel_bias, bucket):
    def body(rb_ref, bkt_ref, o_ref):
        h = pl.program_id(0)
        bkt = bkt_ref[...]

        def step(b, acc):
            return jnp.where(bkt == b, rb_ref[b, h], acc)

        o_ref[0] = lax.fori_loop(0, REL_BUCKETS, step, jnp.zeros(BAND, F32))

    return pl.pallas_call(
        body,
        name="bias_build",
        grid=(N_HEADS,),
        in_specs=[pl.BlockSpec(memory_space=pltpu.SMEM), pl.BlockSpec(BAND, lambda h: (0, 0))],
        out_specs=pl.BlockSpec((1,) + BAND, lambda h: (h, 0, 0)),
        out_shape=jax.ShapeDtypeStruct((N_HEADS,) + BAND, F32),
        compiler_params=_params(("arbitrary",)),
    )(rel_bias, bucket)


def _bias_bwd(dbias, bucket, window):
    def body(db_ref, bkt_ref, win_ref, o_ref):
        h = pl.program_id(0)

        @pl.when(h == 0)
        def _():
            o_ref[...] = jnp.zeros_like(o_ref)

        bkt = bkt_ref[...]
        x = jnp.where(win_ref[...] > 0.5, db_ref[0], 0.0)
        row = lax.broadcasted_iota(jnp.int32, (REL_BUCKETS, LANES), 0)
        col = lax.broadcasted_iota(jnp.int32, (REL_BUCKETS, LANES), 1)

        def step(b, acc):
            s = jnp.sum(jnp.where(bkt == b, x, 0.0), axis=0, keepdims=True)
            return acc + jnp.where(row == b, s, 0.0)

        per_lane = lax.fori_loop(0, REL_BUCKETS, step, jnp.zeros((REL_BUCKETS, LANES), F32))
        o_ref[...] += jnp.where(col == h, jnp.sum(per_lane, axis=1, keepdims=True), 0.0)

    return pl.pallas_call(
        body,
        name="bias_bwd",
        grid=(N_HEADS,),
        in_specs=[
            pl.BlockSpec((1,) + BAND, lambda h: (h, 0, 0)),
            pl.BlockSpec(BAND, lambda h: (0, 0)),
            pl.BlockSpec(BAND, lambda h: (0, 0)),
        ],
        out_specs=pl.BlockSpec((REL_BUCKETS, LANES), lambda h: (0, 0)),
        out_shape=jax.ShapeDtypeStruct((REL_BUCKETS, LANES), F32),
        compiler_params=_params(("arbitrary",)),
    )(dbias, bucket, window)


def _lane_lo(shape):
    return lax.broadcasted_iota(jnp.int32, shape, 1) < HEAD_DIM


def _row_lo(shape):
    return lax.broadcasted_iota(jnp.int32, shape, 0) < HEAD_DIM


def _dup_heads(x):
    r = pltpu.roll(x, HEAD_DIM, axis=1)
    lo = _lane_lo(x.shape)
    return jnp.where(lo, x, r), jnp.where(lo, r, x)


def _kv_operands(kvp_ref, kvc_ref):
    kvp, kvc = kvp_ref[...], kvc_ref[...]
    k2 = jnp.concatenate([kvp[:, :KV_COLS], kvc[:, :KV_COLS]], axis=0)
    v2 = jnp.concatenate([kvp[:, KV_COLS:], kvc[:, KV_COLS:]], axis=0)
    return _dup_heads(k2), _dup_heads(v2)


def _head_probs(k_r, qs_t, bias, mask, sink):
    s = jnp.dot(k_r, qs_t, preferred_element_type=F32) * ATTN_SCALE + bias
    s = jnp.where(mask, s, NEG_BIG)
    m = jnp.maximum(jnp.max(s, axis=0, keepdims=True), sink)
    e = jnp.exp(s - m)
    e_sink = jnp.exp(sink - m)
    inv = 1.0 / (jnp.sum(e, axis=0, keepdims=True) + e_sink)
    return e * inv, e_sink * inv


def _gate_cols(ga_refs, pair):
    off = LANES * (pair % 2)
    return ga_refs[pair // 2][:, off:off + LANES]


def _attn_specs(order):
    return [
        pl.BlockSpec((BLOCK, ATTN_WIDTH), lambda t: (order(t), Q_OFF // ATTN_WIDTH)),
        pl.BlockSpec((BLOCK, 2 * KV_COLS), lambda t: (order(t), KV_OFF // (2 * KV_COLS))),
        pl.BlockSpec((BLOCK, 2 * KV_COLS), lambda t: (jnp.maximum(order(t) - 1, 0), KV_OFF // (2 * KV_COLS))),
    ] + [
        pl.BlockSpec((BLOCK, 256), functools.partial(lambda t, c: (order(t), GA_OFF // 256 + c), c=c)) for c in range(4)
    ] + [
        pl.BlockSpec((N_KV_HEADS, BIAS_ROWS, BLOCK), lambda t: (0, 0, 0)),
        pl.BlockSpec((None,) + BAND, lambda t: (jnp.minimum(order(t), 1), 0, 0)),
        pl.BlockSpec(memory_space=pltpu.SMEM),
    ]


def _attn_fwd(h, bias, masks, sinks, name):
    S = h.shape[0]
    nb = S // BLOCK

    def body(q_ref, kvc_ref, kvp_ref, ga0, ga1, ga2, ga3, bias_ref, mask_ref, sink_ref, o_ref):
        kd, vd = _kv_operands(kvp_ref, kvc_ref)
        mask = mask_ref[...] > 0.5
        lo = _row_lo((LANES, BLOCK))
        for g in range(N_KV_HEADS):
            k_r = kd[g].astype(BF16)
            v_t = vd[g].T.astype(BF16)
            for pr in range(KV_GROUP // 2):
                pair = (KV_GROUP // 2) * g + pr
                qp_t = q_ref[:, LANES * pair:LANES * (pair + 1)].T
                outs = []
                for hh in range(2):
                    j = 2 * pr + hh
                    qs_t = jnp.where(lo if hh == 0 else ~lo, qp_t, 0.0).astype(BF16)
                    p, _ = _head_probs(k_r, qs_t, bias_ref[g, 2 * BLOCK * j:2 * BLOCK * (j + 1), :], mask,
                                       sink_ref[KV_GROUP * g + j])
                    outs.append(jnp.dot(v_t, p.astype(BF16), preferred_element_type=F32))
                ga = _gate_cols((ga0, ga1, ga2, ga3), pair)
                o_ref[:, LANES * pair:LANES * (pair + 1)] = (
                    jnp.where(lo, outs[0], outs[1]).T * (ga * _sigmoid(ga))).astype(BF16)

    return pl.pallas_call(
        body,
        name=name,
        grid=(nb,),
        in_specs=_attn_specs(lambda t: t),
        out_specs=pl.BlockSpec((BLOCK, ATTN_WIDTH), lambda t: (t, 0)),
        out_shape=jax.ShapeDtypeStruct((S, ATTN_WIDTH + POOL_WIDTH), BF16),
        compiler_params=_params(("arbitrary",)),
    )(h, h, h, h, h, h, h, bias, masks, sinks)


DH_ATTN_COLS = U_OFF


def _attn_bwd(h, dab, dh, bias, masks, sinks, dbias_in, name):
    S = h.shape[0]
    nb = S // BLOCK

    def order(t):
        return nb - 1 - t

    def body(q_ref, kvc_ref, kvp_ref, ga0, ga1, ga2, ga3, bias_ref, mask_ref, sink_ref, da_ref, dbin_ref, dh_in_ref,
             dh_ref, dbias_ref, dsink_ref, db_ref, carry_scr):
        del dh_in_ref
        t = pl.program_id(0)

        @pl.when(t == 0)
        def _():
            dbias_ref[...] = dbin_ref[...]
            dsink_ref[...] = jnp.zeros_like(dsink_ref)
            db_ref[...] = jnp.zeros_like(db_ref)
            carry_scr[...] = jnp.zeros_like(carry_scr)

        kd, vd = _kv_operands(kvp_ref, kvc_ref)
        mask = mask_ref[...] > 0.5
        lo = _lane_lo((BLOCK, LANES))
        lo_t = _row_lo((LANES, BLOCK))
        dk_tot, dv_tot = [], []
        for g in range(N_KV_HEADS):
            k_t, k_r = kd[g].T.astype(BF16), kd[g].astype(BF16)
            v_t, v_r = vd[g].T.astype(BF16), vd[g].astype(BF16)
            pbs, dsbs, qss, doss = [], [], [], []
            for pr in range(KV_GROUP // 2):
                pair = (KV_GROUP // 2) * g + pr
                cols = slice(LANES * pair, LANES * (pair + 1))
                qp = q_ref[:, cols]
                qp_t = qp.T
                ga = _gate_cols((ga0, ga1, ga2, ga3), pair)
                sg = _sigmoid(ga)
                da = da_ref[:, cols]
                do_p = da * (ga * sg)
                do_t = do_p.T
                outs, dqs = [], []
                for hh in range(2):
                    j = 2 * pr + hh
                    rows = slice(2 * BLOCK * j, 2 * BLOCK * (j + 1))
                    half_t = lo_t if hh == 0 else ~lo_t
                    qs_t = jnp.where(half_t, qp_t, 0.0).astype(BF16)
                    p, p_sink = _head_probs(k_r, qs_t, bias_ref[g, rows, :], mask, sink_ref[KV_GROUP * g + j])
                    pb = p.astype(BF16)
                    outs.append(jnp.dot(v_t, pb, preferred_element_type=F32))
                    dos_t = jnp.where(half_t, do_t, 0.0).astype(BF16)
                    dp = jnp.dot(v_r, dos_t, preferred_element_type=F32)
                    dsum = jnp.sum(p * dp, axis=0, keepdims=True)
                    ds = p * (dp - dsum)
                    dbias_ref[g, rows, :] += ds
                    tot = jnp.sum(-(p_sink * dsum), axis=1, keepdims=True)
                    dsink_ref[g, j:j + 1, :] += jnp.broadcast_to(tot, (1, LANES))
                    dsb = ds.astype(BF16)
                    dqs.append(jnp.dot(k_t, dsb, preferred_element_type=F32))
                    pbs.append(pb)
                    dsbs.append(dsb)
                qss += [jnp.where(lo, qp, 0.0).astype(BF16), jnp.where(lo, 0.0, qp).astype(BF16)]
                doss += [jnp.where(lo, do_p, 0.0).astype(BF16), jnp.where(lo, 0.0, do_p).astype(BF16)]
                attn = jnp.where(lo_t, outs[0], outs[1]).T
                dq = jnp.where(lo_t, dqs[0], dqs[1]).T * ATTN_SCALE
                dga = da * attn * (sg * (1.0 + ga * (1.0 - sg)))
                ga_cols = slice(GA_OFF + LANES * pair, GA_OFF + LANES * (pair + 1))
                dh_ref[:, cols] = dq.astype(BF16)
                dh_ref[:, ga_cols] = dga.astype(BF16)
                db_ref[:, cols] += jnp.sum(dq, axis=0, keepdims=True)
                db_ref[:, ga_cols] += jnp.sum(dga, axis=0, keepdims=True)
            dk = jnp.dot(jnp.concatenate(dsbs, axis=1), jnp.concatenate(qss, axis=0), preferred_element_type=F32)
            dv = jnp.dot(jnp.concatenate(pbs, axis=1), jnp.concatenate(doss, axis=0), preferred_element_type=F32)
            dk = dk * ATTN_SCALE
            dk_tot.append(dk + pltpu.roll(dk, HEAD_DIM, axis=1))
            dv_tot.append(dv + pltpu.roll(dv, HEAD_DIM, axis=1))
        lo2 = _lane_lo((2 * BLOCK, LANES))
        dkv = jnp.concatenate([jnp.where(lo2, dk_tot[0], dk_tot[1]), jnp.where(lo2, dv_tot[0], dv_tot[1])], axis=1)
        dkv_done = dkv[BLOCK:, :] + carry_scr[...]
        dh_ref[:, KV_OFF:KV_OFF + 2 * KV_COLS] = dkv_done.astype(BF16)
        db_ref[:, KV_OFF:KV_OFF + 2 * KV_COLS] += jnp.sum(dkv_done, axis=0, keepdims=True)
        carry_scr[...] = dkv[:BLOCK, :]

    n_in = 12
    return pl.pallas_call(
        body,
        name=name,
        grid=(nb,),
        in_specs=_attn_specs(order) + [
            pl.BlockSpec((BLOCK, ATTN_WIDTH), lambda t: (order(t), 0)),
            pl.BlockSpec((N_KV_HEADS, BIAS_ROWS, BLOCK), lambda t: (0, 0, 0)),
            pl.BlockSpec(memory_space=pl.ANY),
        ],
        out_specs=[
            pl.BlockSpec((BLOCK, DH_ATTN_COLS), lambda t: (order(t), 0)),
            pl.BlockSpec((N_KV_HEADS, BIAS_ROWS, BLOCK), lambda t: (0, 0, 0)),
            pl.BlockSpec((N_KV_HEADS, KV_GROUP, LANES), lambda t: (0, 0, 0)),
            pl.BlockSpec((1, DH_ATTN_COLS), lambda t: (0, 0)),
        ],
        out_shape=[
            jax.ShapeDtypeStruct((S, IN_COLS), BF16),
            jax.ShapeDtypeStruct((N_KV_HEADS, BIAS_ROWS, BLOCK), F32),
            jax.ShapeDtypeStruct((N_KV_HEADS, KV_GROUP, LANES), F32),
            jax.ShapeDtypeStruct((1, DH_ATTN_COLS), F32),
        ],
        scratch_shapes=[pltpu.VMEM((BLOCK, 2 * KV_COLS), F32)],
        input_output_aliases={n_in: 0},
        compiler_params=_params(("arbitrary",)),
    )(h, h, h, h, h, h, h, bias, masks, sinks, dab, dbias_in, dh)


def _window_sum(x, w, back):
    n = x.shape[0]
    s, sh = x, 1
    while sh < w:
        s = s + pltpu.roll(s, sh if back else n - sh, axis=0)
        sh *= 2
    return s


def _pool_counts(first_row, n, w):
    t = first_row + lax.broadcasted_iota(jnp.int32, (n, 1), 0)
    return jnp.minimum(t + 1, w).astype(F32)


def _pool_diff(u_ref, uh_ref, i, T, g):
    u = u_ref[...]
    halo = jnp.where(i > 0, uh_ref[...], 0.0)
    ext = jnp.concatenate([halo, u], axis=0)
    w = POOL_WINDOWS[g]
    s = _window_sum(ext, w, back=True)[POOL_HALO:, :]
    return s / _pool_counts(i * T, T, w) - u


def _pool_in_specs(T):
    hb = T // POOL_HALO
    specs = []
    for g in range(4):
        specs.append(pl.BlockSpec((T, 256), functools.partial(lambda i, g: (i, U_OFF // 256 + g), g=g)))
        specs.append(pl.BlockSpec((POOL_HALO, 256), functools.partial(
            lambda i, g: (jnp.maximum(i * hb - 1, 0), U_OFF // 256 + g), g=g)))
    return specs


def _pool_weight_specs():
    return [pl.BlockSpec((4, 256, 256), lambda i: (0, 0, 0)), pl.BlockSpec((1, POOL_WIDTH), lambda i: (0, 0))]


def _pool_fwd(h, ab, w_pool, pool_scale, name):
    S = h.shape[0]
    T = _tile(S, 512)

    def body(*refs):
        u_refs = refs[0:8]
        gb_refs = refs[8:12]
        wp_ref, sc_ref, o_ref = refs[12], refs[13], refs[15]
        i = pl.program_id(0)
        for g in range(4):
            diff = _pool_diff(u_refs[2 * g], u_refs[2 * g + 1], i, T, g).astype(BF16)
            mixed = jnp.dot(diff, wp_ref[g], preferred_element_type=F32) * sc_ref[:, 256 * g:256 * (g + 1)]
            gb = gb_refs[g][...]
            o_ref[:, 256 * g:256 * (g + 1)] = (mixed * (gb * _sigmoid(gb))).astype(BF16)

    in_specs = _pool_in_specs(T) + [
        pl.BlockSpec((T, 256), functools.partial(lambda i, g: (i, GB_OFF // 256 + g), g=g)) for g in range(4)
    ] + _pool_weight_specs() + [pl.BlockSpec(memory_space=pl.ANY)]
    return pl.pallas_call(
        body,
        name=name,
        grid=(S // T,),
        in_specs=in_specs,
        out_specs=pl.BlockSpec((T, POOL_WIDTH), lambda i: (i, 1)),
        out_shape=jax.ShapeDtypeStruct(ab.shape, BF16),
        input_output_aliases={14: 0},
        compiler_params=_params(("arbitrary",)),
    )(*([h] * 12), w_pool, pool_scale, ab)


DH_POOL_COLS = IN_COLS // 2


def _pool_bwd(h, dab, w_pool, pool_scale, after, name):
    S = h.shape[0]
    T = _tile(S, 512)
    nt = S // T
    hb = T // POOL_HALO
    E = T + POOL_HALO
    lead = U_OFF - DH_POOL_COLS

    def body(*refs):
        u_refs = refs[0:8]
        gb_refs = refs[8:16]
        db_refs = refs[16:24]
        wp_ref, sc_ref = refs[24], refs[25]
        dh_ref, dwp_ref, dsc_ref, dbi_ref = refs[27:31]
        i = pl.program_id(0)

        @pl.when(i == 0)
        def _():
            dwp_ref[...] = jnp.zeros_like(dwp_ref)
            dsc_ref[...] = jnp.zeros_like(dsc_ref)
            dbi_ref[...] = jnp.zeros_like(dbi_ref)

        dh_ref[:, 0:lead] = jnp.zeros((T, lead), BF16)
        for g in range(4):
            w = POOL_WINDOWS[g]
            cols = slice(256 * g, 256 * (g + 1))
            scale = sc_ref[:, cols]
            wp = wp_ref[g]
            diff = _pool_diff(u_refs[2 * g], u_refs[2 * g + 1], i, T, g).astype(BF16)
            mixed = jnp.dot(diff, wp, preferred_element_type=F32)
            keep = i < nt - 1
            gb = jnp.concatenate([gb_refs[2 * g][...], jnp.where(keep, gb_refs[2 * g + 1][...], 0.0)], axis=0)
            db = jnp.concatenate([db_refs[2 * g][...], jnp.where(keep, db_refs[2 * g + 1][...], 0.0)], axis=0)
            sg = _sigmoid(gb)
            dms = db * (gb * sg)
            dmixed = (dms * scale).astype(BF16)
            ddiff = lax.dot_general(dmixed, wp, (((1,), (1,)), ((), ())), preferred_element_type=F32)
            r = ddiff / _pool_counts(i * T, E, w)
            du = _window_sum(r, w, back=False)[:T, :] - ddiff[:T, :]
            dgb = db[:T, :] * (mixed * scale) * (sg[:T, :] * (1.0 + gb[:T, :] * (1.0 - sg[:T, :])))
            u_cols = slice(lead + 256 * g, lead + 256 * (g + 1))
            gb_cols = slice(lead + POOL_WIDTH + 256 * g, lead + POOL_WIDTH + 256 * (g + 1))
            dh_ref[:, u_cols] = du.astype(BF16)
            dh_ref[:, gb_cols] = dgb.astype(BF16)
            dbi_ref[:, u_cols] += jnp.sum(du, axis=0, keepdims=True)
            dbi_ref[:, gb_cols] += jnp.sum(dgb, axis=0, keepdims=True)
            dsc_ref[:, cols] += jnp.sum(dms[:T, :] * mixed, axis=0, keepdims=True)
            dwp_ref[g] += lax.dot_general(diff, dmixed[:T, :], (((0,), (0,)), ((), ())), preferred_element_type=F32)

    def rows_after(i):
        return jnp.minimum((i + 1) * hb, S // POOL_HALO - 1)

    in_specs = _pool_in_specs(T)
    for off in (GB_OFF // 256, ATTN_WIDTH // 256):
        for g in range(4):
            in_specs.append(pl.BlockSpec((T, 256), functools.partial(lambda i, c: (i, c), c=off + g)))
            in_specs.append(pl.BlockSpec((POOL_HALO, 256), functools.partial(lambda i, c: (rows_after(i), c), c=off + g)))
    in_specs += _pool_weight_specs() + [pl.BlockSpec(memory_space=pl.ANY)]
    return pl.pallas_call(
        body,
        name=name,
        grid=(nt,),
        in_specs=in_specs,
        out_specs=[
            pl.BlockSpec((T, DH_POOL_COLS), lambda i: (i, 1)),
            pl.BlockSpec((4, 256, 256), lambda i: (0, 0, 0)),
            pl.BlockSpec((1, POOL_WIDTH), lambda i: (0, 0)),
            pl.BlockSpec((1, DH_POOL_COLS), lambda i: (0, 0)),
        ],
        out_shape=[
            jax.ShapeDtypeStruct((S, IN_COLS), BF16),
            jax.ShapeDtypeStruct((4, 256, 256), F32),
            jax.ShapeDtypeStruct((1, POOL_WIDTH), F32),
            jax.ShapeDtypeStruct((1, DH_POOL_COLS), F32),
        ],
        compiler_params=_params(("arbitrary",)),
    )(*([h] * 16), *([dab] * 8), w_pool, pool_scale, after)


def _load_resident(pairs, sems):
    @pl.when(pl.program_id(0) == 0)
    def _():
        cps = [pltpu.make_async_copy(src, dst, sems.at[n]) for n, (src, dst) in enumerate(pairs)]
        for cp in cps:
            cp.start()
        for cp in cps:
            cp.wait()


def _mix_ln_fwd(ab, x, pb, w_out, w_gate, w_ple_t, gain, bias, alpha, after, name):
    S = x.shape[0]
    T = _tile(S, 256)

    def body(ab_ref, x_ref, p_ref, wo_hbm, wg_hbm, wp_hbm, g_ref, b_ref, after_ref,
             y_ref, yb_ref, xh_ref, rs_ref, gp_ref, pe_ref, wo, wg, wp, sems):
        del after_ref
        _load_resident(((wo_hbm, wo), (wg_hbm, wg), (wp_hbm, wp)), sems)
        x = x_ref[...]
        mix = jnp.dot(ab_ref[...], wo[...], preferred_element_type=F32)
        gp = jnp.dot(x.astype(BF16), wg[...], preferred_element_type=F32)
        pe = lax.dot_general(p_ref[...], wp[...], (((1,), (1,)), ((), ())), preferred_element_type=F32)
        z = alpha * x + mix + _sigmoid(gp) * pe
        mu = jnp.mean(z, axis=-1, keepdims=True)
        zc = z - mu
        var = jnp.mean(zc * zc, axis=-1, keepdims=True)
        rstd = lax.rsqrt(var + LN_EPS)
        xhat = zc * rstd
        y = xhat * g_ref[...] + b_ref[...]
        y_ref[...] = y
        yb_ref[...] = y.astype(BF16)
        xh_ref[...] = xhat
        rs_ref[...] = rstd
        gp_ref[...] = gp
        pe_ref[...] = pe

    row = pl.BlockSpec((T, D_MODEL), lambda i: (i, 0))
    vec = pl.BlockSpec((1, D_MODEL), lambda i: (0, 0))
    any_spec = pl.BlockSpec(memory_space=pl.ANY)
    f32_rows = jax.ShapeDtypeStruct((S, D_MODEL), F32)
    return pl.pallas_call(
        body,
        name=name,
        grid=(S // T,),
        in_specs=[row, row, pl.BlockSpec((T, PLE_DIM), lambda i: (i, 0)), any_spec, any_spec, any_spec, vec, vec, any_spec],
        out_specs=[row, row, row, pl.BlockSpec((T, 1), lambda i: (i, 0)), row, row],
        out_shape=[f32_rows, jax.ShapeDtypeStruct((S, D_MODEL), BF16), f32_rows, jax.ShapeDtypeStruct((S, 1), F32),
                   f32_rows, f32_rows],
        scratch_shapes=[pltpu.VMEM(w_out.shape, BF16), pltpu.VMEM(w_gate.shape, BF16), pltpu.VMEM(w_ple_t.shape, BF16),
                        pltpu.SemaphoreType.DMA((3,))],
        compiler_params=_params(("arbitrary",)),
    )(ab, x, pb, w_out, w_gate, w_ple_t, gain, bias, after)


def _ln_dmix_bwd(dy, xhat, rstd, gain, gp, pe, w_out, w_gate, alpha, after, name):
    S = dy.shape[0]
    T = _tile(S, 256)

    def body(dy_ref, xh_ref, rs_ref, g_ref, gp_ref, pe_ref, wo_hbm, wg_hbm, after_ref,
             dzb_ref, dpe_ref, dgp_ref, dab_ref, dx_ref, dgain_ref, dbias_ref, wo, wg, sems):
        del after_ref
        _load_resident(((wo_hbm, wo), (wg_hbm, wg)), sems)

        @pl.when(pl.program_id(0) == 0)
        def _():
            dgain_ref[...] = jnp.zeros_like(dgain_ref)
            dbias_ref[...] = jnp.zeros_like(dbias_ref)

        dy = dy_ref[...]
        xhat = xh_ref[...]
        dyg = dy * g_ref[...]
        c1 = jnp.mean(dyg, axis=-1, keepdims=True)
        c2 = jnp.mean(dyg * xhat, axis=-1, keepdims=True)
        dz = rs_ref[...] * (dyg - c1 - xhat * c2)
        dgain_ref[...] += jnp.sum(dy * xhat, axis=0, keepdims=True)
        dbias_ref[...] += jnp.sum(dy, axis=0, keepdims=True)
        sg = _sigmoid(gp_ref[...])
        dzb = dz.astype(BF16)
        dgp = (dz * pe_ref[...] * (sg * (1.0 - sg))).astype(BF16)
        nt = (((1,), (1,)), ((), ()))
        dzb_ref[...] = dzb
        dpe_ref[...] = (dz * sg).astype(BF16)
        dgp_ref[...] = dgp
        dab_ref[...] = lax.dot_general(dzb, wo[...], nt, preferred_element_type=F32)
        dx_ref[...] = lax.dot_general(dgp, wg[...], nt, preferred_element_type=F32) + alpha * dz

    row = pl.BlockSpec((T, D_MODEL), lambda i: (i, 0))
    vec = pl.BlockSpec((1, D_MODEL), lambda i: (0, 0))
    any_spec = pl.BlockSpec(memory_space=pl.ANY)
    bf16_rows = jax.ShapeDtypeStruct((S, D_MODEL), BF16)
    f32_rows = jax.ShapeDtypeStruct((S, D_MODEL), F32)
    return pl.pallas_call(
        body,
        name=name,
        grid=(S // T,),
        in_specs=[row, row, pl.BlockSpec((T, 1), lambda i: (i, 0)), vec, row, row, any_spec, any_spec, any_spec],
        out_specs=[row, row, row, row, row, vec, vec],
        out_shape=[bf16_rows, bf16_rows, bf16_rows, f32_rows, f32_rows,
                   jax.ShapeDtypeStruct((1, D_MODEL), F32), jax.ShapeDtypeStruct((1, D_MODEL), F32)],
        scratch_shapes=[pltpu.VMEM(w_out.shape, BF16), pltpu.VMEM(w_gate.shape, BF16), pltpu.SemaphoreType.DMA((2,))],
        compiler_params=_params(("arbitrary",)),
    )(dy, xhat, rstd, gain, gp, pe, w_out, w_gate, after)


def _loss_head(y, target):
    S = y.shape[0]
    T = _tile(S, 256)

    def body(y_ref, t_ref, dy_ref, l_ref):
        @pl.when(pl.program_id(0) == 0)
        def _():
            l_ref[...] = jnp.zeros_like(l_ref)

        err = y_ref[...] - t_ref[...]
        dy_ref[...] = err * (1.0 / D_MODEL)
        per_token = jnp.mean(err * err, axis=-1, keepdims=True)
        l_ref[...] += 0.5 * jnp.sum(per_token, axis=0, keepdims=True)

    row = pl.BlockSpec((T, D_MODEL), lambda i: (i, 0))
    return pl.pallas_call(
        body,
        name="loss_head",
        grid=(S // T,),
        in_specs=[row, row],
        out_specs=[row, pl.BlockSpec((8, LANES), lambda i: (0, 0))],
        out_shape=[jax.ShapeDtypeStruct((S, D_MODEL), F32), jax.ShapeDtypeStruct((8, LANES), F32)],
        compiler_params=_params(("arbitrary",)),
    )(y, target)


def _sum_slabs(r, name):
    _, R, C = r.shape
    T = _tile(R, 256)

    def body(r_ref, o_ref):
        acc = r_ref[0].astype(F32)
        for s in range(1, N_DEV):
            acc = acc + r_ref[s].astype(F32)
        o_ref[...] = acc

    return pl.pallas_call(
        body,
        name=name,
        grid=(R // T,),
        in_specs=[pl.BlockSpec((N_DEV, T, C), lambda i: (0, i, 0))],
        out_specs=pl.BlockSpec((T, C), lambda i: (i, 0)),
        out_shape=jax.ShapeDtypeStruct((R, C), F32),
        compiler_params=_params(("parallel",)),
    )(r)


def _adamw_math(w, g, m, v):
    m = ADAM_B1 * m + (1.0 - ADAM_B1) * g
    v = ADAM_B2 * v + (1.0 - ADAM_B2) * jnp.square(g)
    m_hat = m / (1.0 - ADAM_B1 ** ADAM_STEP)
    v_hat = v / (1.0 - ADAM_B2 ** ADAM_STEP)
    return -ADAM_LR * (m_hat / (jnp.sqrt(v_hat) + ADAM_EPS) + ADAM_WD * w), m, v


def _sum_adamw(r, w, m, v, name, tile=None, rows=None, prev=None):
    R, C = w.shape
    T = _tile(R, 256) if tile is None else tile
    first, stop = (0, R) if rows is None else rows
    assert first % T == 0 and stop % T == 0
    t0 = first // T

    def body(r_ref, w_ref, m_ref, v_ref, *rest):
        g_ref, d_ref, nm_ref, nv_ref = rest[-4:]
        g = r_ref[0].astype(F32)
        for s in range(1, N_DEV):
            g = g + r_ref[s].astype(F32)
        g_ref[...] = g
        d_ref[...], nm_ref[...], nv_ref[...] = _adamw_math(w_ref[...], g, m_ref[...], v_ref[...])

    tc = _tile(C, 1024, unit=LANES)
    blk = pl.BlockSpec((T, tc), lambda i, j: (i + t0, j))
    shp = jax.ShapeDtypeStruct((R, C), F32)
    n_prev = 0 if prev is None else 4
    return pl.pallas_call(
        body,
        name=name,
        grid=((stop - first) // T, C // tc),
        in_specs=[pl.BlockSpec((N_DEV, T, tc), lambda i, j: (0, i + t0, j))] + [blk] * 3
        + [pl.BlockSpec(memory_space=pl.ANY)] * n_prev,
        out_specs=[blk] * 4,
        out_shape=[shp] * 4,
        input_output_aliases={4 + i: i for i in range(n_prev)},
        compiler_params=_params(("parallel", "parallel")),
    )(r, w, m, v, *(prev or ()))


def _adamw(w, g, m, v, name):
    R, C = w.shape
    T = _tile(R, 256)
    grid = (R // T,)
    blk = pl.BlockSpec((T, C), lambda i: (i, 0))

    def body(w_ref, g_ref, m_ref, v_ref, d_ref, nm_ref, nv_ref):
        d_ref[...], nm_ref[...], nv_ref[...] = _adamw_math(w_ref[...], g_ref[...], m_ref[...], v_ref[...])

    shp = jax.ShapeDtypeStruct(w.shape, F32)
    return pl.pallas_call(
        body,
        name=name,
        grid=grid,
        in_specs=[blk] * 4,
        out_specs=[blk] * 3,
        out_shape=[shp] * 3,
        compiler_params=_params(("parallel",) * len(grid)),
    )(w, g, m, v)


def _mesh_pos():
    return lax.axis_index("x"), lax.axis_index("y"), lax.axis_index("c")


def _flip(pos, k):
    x, y, c = pos
    return (1 - x if k & 4 else x, 1 - y if k & 2 else y, 1 - c if k & 1 else c)


def _index(pos):
    return 4 * pos[0] + 2 * pos[1] + pos[2]


HBM_SPEC = pl.BlockSpec(memory_space=pltpu.HBM)
SEM_SPEC = pl.BlockSpec(memory_space=pltpu.SEMAPHORE)
ANY_SPEC = pl.BlockSpec(memory_space=pl.ANY)
SPLIT_EFFECT = pltpu.SideEffectType.DATAFLOW_SIDE_EFFECTING
GATHER_FLIPS = (1, 4, 2, 6)
CHIP_FLIPS = (4, 2, 6)
TOKEN = jax.ShapeDtypeStruct((8, LANES), F32)


def _hbm(a):
    return pltpu.with_memory_space_constraint(a, pltpu.HBM)


def _hbm_like(a):
    return pltpu.HBM(a.shape, a.dtype)


def _block_rows(ref, pos, r):
    return ref.at[:, pl.ds(_index(pos) * r, r), :]


def _gather_start(shards, after, name):
    n = len(shards)
    lands = [lax.empty((s.shape[0], N_DEV * s.shape[1], s.shape[2]), s.dtype) for s in shards]

    def body(*refs):
        ins, bufs = refs[:n], refs[n:2 * n]
        send_sems, recv_sems = refs[2 * n + 1], refs[2 * n + 2]
        token = refs[4 * n + 3]
        me = _mesh_pos()
        for a in range(n):
            for j, k in enumerate(GATHER_FLIPS):
                pltpu.make_async_remote_copy(
                    src_ref=ins[a], dst_ref=_block_rows(bufs[a], me, shards[a].shape[1]),
                    send_sem=send_sems.at[4 * a + j], recv_sem=recv_sems.at[4 * a + j],
                    device_id=_flip(me, k), device_id_type=MESH_ID).start()
        token[...] = jnp.zeros_like(token)

    outs = pl.pallas_call(
        body,
        name=name,
        in_specs=[HBM_SPEC] * (2 * n) + [ANY_SPEC],
        out_specs=[SEM_SPEC, SEM_SPEC] + [HBM_SPEC] * (2 * n) + [pl.BlockSpec(memory_space=pltpu.VMEM)],
        out_shape=[pltpu.SemaphoreType.DMA((4 * n,)), pltpu.SemaphoreType.DMA((4 * n,))]
        + [_hbm_like(s) for s in shards] + [_hbm_like(b) for b in lands] + [TOKEN],
        input_output_aliases={i: 2 + i for i in range(2 * n)},
        compiler_params=pltpu.CompilerParams(has_side_effects=SPLIT_EFFECT),
    )(*[_hbm(s) for s in shards], *[_hbm(b) for b in lands], after)
    return outs[0], outs[1], outs[2:2 + n], outs[2 + n:2 + 2 * n], outs[2 + 2 * n]


def _gather_wait(started, after, name):
    send_sems, recv_sems, shards, lands, _ = started
    n = len(shards)

    def body(*refs):
        ins, bufs = refs[:n], refs[n:2 * n]
        send_sems, recv_sems = refs[2 * n], refs[2 * n + 1]
        me = _mesh_pos()
        for a in range(n):
            for j, k in enumerate(GATHER_FLIPS):
                cp = pltpu.make_async_remote_copy(
                    src_ref=ins[a], dst_ref=_block_rows(bufs[a], _flip(me, k), shards[a].shape[1]),
                    send_sem=send_sems.at[4 * a + j], recv_sem=recv_sems.at[4 * a + j],
                    device_id=_flip(me, k), device_id_type=MESH_ID)
                cp.wait_send()
                cp.wait_recv()

    outs = pl.pallas_call(
        body,
        name=name,
        in_specs=[HBM_SPEC] * (2 * n) + [SEM_SPEC, SEM_SPEC, ANY_SPEC],
        out_specs=[HBM_SPEC] * (2 * n),
        out_shape=[_hbm_like(s) for s in shards] + [_hbm_like(b) for b in lands],
        input_output_aliases={i: i for i in range(2 * n)},
        compiler_params=pltpu.CompilerParams(has_side_effects=SPLIT_EFFECT),
    )(*shards, *lands, send_sems, recv_sems, after)
    return outs[:n], outs[n:]


def _gather_pass(shards, lands, name):
    n = len(shards)

    def body(*refs):
        ins, bufs = refs[:n], refs[n:2 * n]
        token = refs[3 * n]
        send_sems, recv_sems, local_sems = refs[3 * n + 1:]
        me = _mesh_pos()
        sibling = _flip(me, 1)

        def copy(a, j, block):
            rows = _block_rows(bufs[a], block, shards[a].shape[1])
            return pltpu.make_async_remote_copy(
                src_ref=rows, dst_ref=rows, send_sem=send_sems.at[3 * a + j], recv_sem=recv_sems.at[3 * a + j],
                device_id=sibling, device_id_type=MESH_ID)

        mine = [pltpu.make_async_copy(ins[a], _block_rows(bufs[a], me, shards[a].shape[1]), local_sems.at[a])
                for a in range(n)]
        sends = [copy(a, j, _flip(me, k)) for a in range(n) for j, k in enumerate(CHIP_FLIPS)]
        for cp in mine + sends:
            cp.start()
        for a in range(n):
            for j, k in enumerate(CHIP_FLIPS):
                copy(a, j, _flip(sibling, k)).wait_recv()
        for cp in sends:
            cp.wait_send()
        for cp in mine:
            cp.wait()
        token[...] = jnp.zeros_like(token)

    outs = pl.pallas_call(
        body,
        name=name,
        in_specs=[pl.BlockSpec(memory_space=pltpu.VMEM)] * n + [ANY_SPEC] * n,
        out_specs=[ANY_SPEC] * n + [pl.BlockSpec(memory_space=pltpu.VMEM)],
        out_shape=[jax.ShapeDtypeStruct(b.shape, b.dtype) for b in lands] + [TOKEN],
        scratch_shapes=[pltpu.SemaphoreType.DMA((3 * n,)), pltpu.SemaphoreType.DMA((3 * n,)), pltpu.SemaphoreType.DMA((n,))],
        input_output_aliases={n + i: i for i in range(n)},
        compiler_params=pltpu.CompilerParams(has_side_effects=True, vmem_limit_bytes=VMEM_LIMIT_BYTES),
    )(*shards, *lands)
    return outs[:n], outs[n]


def _gather_own(shards, lands, name):
    n = len(shards)

    def body(*refs):
        ins, bufs, local_sems = refs[:n], refs[n:2 * n], refs[3 * n]
        me = _mesh_pos()
        cps = [pltpu.make_async_copy(ins[a], _block_rows(bufs[a], me, shards[a].shape[1]), local_sems.at[a])
               for a in range(n)]
        for cp in cps:
            cp.start()
        for cp in cps:
            cp.wait()

    return pl.pallas_call(
        body,
        name=name,
        in_specs=[pl.BlockSpec(memory_space=pltpu.VMEM)] * n + [ANY_SPEC] * n,
        out_specs=[ANY_SPEC] * n,
        out_shape=[jax.ShapeDtypeStruct(b.shape, b.dtype) for b in lands],
        scratch_shapes=[pltpu.SemaphoreType.DMA((n,))],
        input_output_aliases={n + i: i for i in range(n)},
        compiler_params=pltpu.CompilerParams(has_side_effects=True, vmem_limit_bytes=VMEM_LIMIT_BYTES),
    )(*shards, *lands)


def _pass_copy(bufs, send_sems, recv_sems, a, j, block, sibling):
    rows = _block_rows(bufs[a], block, bufs[a].shape[1] // N_DEV)
    return pltpu.make_async_remote_copy(
        src_ref=rows, dst_ref=rows, send_sem=send_sems.at[3 * a + j], recv_sem=recv_sems.at[3 * a + j],
        device_id=sibling, device_id_type=MESH_ID)


def _pass_start(lands, after, name):
    n = len(lands)

    def body(*refs):
        bufs = refs[:n]
        send_sems, recv_sems = refs[n + 1], refs[n + 2]
        token = refs[2 * n + 3]
        me = _mesh_pos()
        for a in range(n):
            for j, k in enumerate(CHIP_FLIPS):
                _pass_copy(bufs, send_sems, recv_sems, a, j, _flip(me, k), _flip(me, 1)).start()
        token[...] = jnp.zeros_like(token)

    outs = pl.pallas_call(
        body,
        name=name,
        in_specs=[HBM_SPEC] * n + [ANY_SPEC],
        out_specs=[SEM_SPEC, SEM_SPEC] + [HBM_SPEC] * n + [pl.BlockSpec(memory_space=pltpu.VMEM)],
        out_shape=[pltpu.SemaphoreType.DMA((3 * n,)), pltpu.SemaphoreType.DMA((3 * n,))]
        + [_hbm_like(b) for b in lands] + [TOKEN],
        input_output_aliases={i: 2 + i for i in range(n)},
        compiler_params=pltpu.CompilerParams(has_side_effects=SPLIT_EFFECT),
    )(*[_hbm(b) for b in lands], after)
    return outs[0], outs[1], outs[2:2 + n], outs[2 + n]


def _pass_wait(started, after, name):
    send_sems, recv_sems, lands, _ = started
    n = len(lands)

    def body(*refs):
        bufs = refs[:n]
        send_sems, recv_sems = refs[n], refs[n + 1]
        me = _mesh_pos()
        sibling = _flip(me, 1)
        for a in range(n):
            for j, k in enumerate(CHIP_FLIPS):
                _pass_copy(bufs, send_sems, recv_sems, a, j, _flip(me, k), sibling).wait_send()
                _pass_copy(bufs, send_sems, recv_sems, a, j, _flip(sibling, k), sibling).wait_recv()

    return pl.pallas_call(
        body,
        name=name,
        in_specs=[HBM_SPEC] * n + [SEM_SPEC, SEM_SPEC, ANY_SPEC],
        out_specs=[HBM_SPEC] * n,
        out_shape=[_hbm_like(b) for b in lands],
        input_output_aliases={i: i for i in range(n)},
        compiler_params=pltpu.CompilerParams(has_side_effects=SPLIT_EFFECT),
    )(*lands, send_sems, recv_sems, after)


def _place_own(grads, lands, layer, name):
    n = len(grads)
    blocks = [(g.shape[0], g.shape[1] // N_DEV, g.shape[2]) for g in grads]

    def body(*refs):
        ins, bufs = refs[:n], refs[n:2 * n]
        stage, in_sems, out_sems = refs[3 * n:4 * n], refs[4 * n], refs[4 * n + 1]
        me = _mesh_pos()
        loads = [pltpu.make_async_copy(_block_rows(ins[a], me, blocks[a][1]), stage[a], in_sems.at[a]) for a in range(n)]
        stores = [pltpu.make_async_copy(stage[a], bufs[a].at[_index(me), layer], out_sems.at[a]) for a in range(n)]
        for cp in loads:
            cp.start()
        for a in range(n):
            loads[a].wait()
            stores[a].start()
        for cp in stores:
            cp.wait()

    return pl.pallas_call(
        body,
        name=name,
        in_specs=[ANY_SPEC] * (2 * n),
        out_specs=[ANY_SPEC] * n,
        out_shape=[jax.ShapeDtypeStruct(b.shape, b.dtype) for b in lands],
        scratch_shapes=[pltpu.VMEM(blk, g.dtype) for blk, g in zip(blocks, grads)]
        + [pltpu.SemaphoreType.DMA((n,)), pltpu.SemaphoreType.DMA((n,))],
        input_output_aliases={n + i: i for i in range(n)},
        compiler_params=pltpu.CompilerParams(has_side_effects=True, vmem_limit_bytes=VMEM_LIMIT_BYTES),
    )(*grads, *lands)


def _scatter_copy(ins, bufs, send_sems, recv_sems, a, k, r, layer, me, slab):
    peer = _flip(me, k)
    return pltpu.make_async_remote_copy(
        src_ref=_block_rows(ins[a], peer, r), dst_ref=bufs[a].at[_index(slab), layer],
        send_sem=send_sems.at[7 * a + k - 1], recv_sem=recv_sems.at[7 * a + k - 1],
        device_id=peer, device_id_type=MESH_ID)


def _scatter_start(grads, lands, layer, name):
    n = len(grads)

    def body(*refs):
        ins, bufs = refs[:n], refs[n:2 * n]
        send_sems, recv_sems = refs[2 * n], refs[2 * n + 1]
        token = refs[4 * n + 2]
        me = _mesh_pos()
        for a in range(n):
            for k in range(1, N_DEV):
                _scatter_copy(ins, bufs, send_sems, recv_sems, a, k, grads[a].shape[1] // N_DEV, layer, me, me).start()
        token[...] = jnp.zeros_like(token)

    outs = pl.pallas_call(
        body,
        name=name,
        in_specs=[HBM_SPEC] * (2 * n),
        out_specs=[SEM_SPEC, SEM_SPEC] + [HBM_SPEC] * (2 * n) + [pl.BlockSpec(memory_space=pltpu.VMEM)],
        out_shape=[pltpu.SemaphoreType.DMA((7 * n,)), pltpu.SemaphoreType.DMA((7 * n,))]
        + [_hbm_like(g) for g in grads] + [_hbm_like(b) for b in lands] + [TOKEN],
        input_output_aliases={i: 2 + i for i in range(2 * n)},
        compiler_params=pltpu.CompilerParams(has_side_effects=SPLIT_EFFECT),
    )(*[_hbm(g) for g in grads], *[_hbm(b) for b in lands])
    return outs[0], outs[1], outs[2:2 + n], outs[2 + n:2 + 2 * n], outs[2 + 2 * n]


def _scatter_wait(started, layer, after, name):
    send_sems, recv_sems, grads, lands, _ = started
    n = len(grads)

    def body(*refs):
        ins, bufs = refs[:n], refs[n:2 * n]
        send_sems, recv_sems = refs[2 * n], refs[2 * n + 1]
        me = _mesh_pos()
        for a in range(n):
            for k in range(1, N_DEV):
                cp = _scatter_copy(ins, bufs, send_sems, recv_sems, a, k, grads[a].shape[1] // N_DEV, layer, me, _flip(me, k))
                cp.wait_send()
                cp.wait_recv()

    outs = pl.pallas_call(
        body,
        name=name,
        in_specs=[HBM_SPEC] * (2 * n) + [SEM_SPEC, SEM_SPEC, ANY_SPEC],
        out_specs=[HBM_SPEC] * (2 * n),
        out_shape=[_hbm_like(g) for g in grads] + [_hbm_like(b) for b in lands],
        input_output_aliases={i: i for i in range(2 * n)},
        compiler_params=pltpu.CompilerParams(has_side_effects=SPLIT_EFFECT),
    )(*grads, *lands, send_sems, recv_sems, after)
    return outs[n:]


def _allreduce_small(vec):
    R, C = vec.shape

    def body(v_ref, o_ref, buf, send_sems, recv_sems):
        me = _mesh_pos()
        buf[_index(me)] = v_ref[...]
        sends = []
        for k in range(1, N_DEV):
            sends.append(pltpu.make_async_remote_copy(
                src_ref=buf.at[_index(me)], dst_ref=buf.at[_index(me)],
                send_sem=send_sems.at[k - 1], recv_sem=recv_sems.at[k - 1],
                device_id=_flip(me, k), device_id_type=MESH_ID))
        for cp in sends:
            cp.start()
        for cp in sends:
            cp.wait_recv()
        for cp in sends:
            cp.wait_send()
        acc = buf[0]
        for s in range(1, N_DEV):
            acc = acc + buf[s]
        o_ref[...] = acc

    return pl.pallas_call(
        body,
        name="allreduce_small",
        in_specs=[pl.BlockSpec(memory_space=pltpu.VMEM)],
        out_specs=pl.BlockSpec(memory_space=pltpu.VMEM),
        out_shape=jax.ShapeDtypeStruct((R, C), F32),
        scratch_shapes=[pltpu.VMEM((N_DEV, R, C), F32), pltpu.SemaphoreType.DMA((7,)), pltpu.SemaphoreType.DMA((7,))],
        compiler_params=pltpu.CompilerParams(has_side_effects=True, vmem_limit_bytes=VMEM_LIMIT_BYTES),
    )(vec)


def _pack_small(parts):
    flat = jnp.concatenate([p.reshape(-1) for p in parts])
    n = flat.shape[0]
    rows = -(-n // SMALL_COLS)
    rows = -(-rows // 8) * 8
    return jnp.pad(flat, (0, rows * SMALL_COLS - n)).reshape(rows, SMALL_COLS)


def _unpack_small(packed, like):
    flat = packed.reshape(-1)
    out, pos = [], 0
    for p in like:
        out.append(flat[pos:pos + p.size].reshape(p.shape))
        pos += p.size
    return out


def kernel(x, p, w_in, b_in, w_out, attn_sinks, rel_bias, w_pool, pool_scale, w_ple, w_gate_ple, ln_gain, ln_bias, loss_target, m_w_in, m_b_in, m_w_out, m_attn_sinks, m_rel_bias, m_w_pool, m_pool_scale, m_w_ple, m_w_gate_ple, m_ln_gain, m_ln_bias, v_w_in, v_b_in, v_w_out, v_attn_sinks, v_rel_bias, v_w_pool, v_pool_scale, v_w_ple, v_w_gate_ple, v_ln_gain, v_ln_bias):
    L = w_in.shape[0]
    S = x.shape[1]
    alpha = (2.0 * L) ** 0.25
    bucket_np, masks_np, window_np = _band_constants()
    bucket, masks, window = jnp.asarray(bucket_np), jnp.asarray(masks_np), jnp.asarray(window_np)

    @functools.lru_cache(maxsize=None)
    def shards_of(l):
        return (jnp.swapaxes(w_in[l], 0, 1).astype(BF16)[None], w_out[l].astype(BF16)[None],
                w_gate_ple[l].astype(BF16)[None], jnp.swapaxes(w_ple[l], 0, 1).astype(BF16)[None], w_pool[l].astype(BF16))

    def gathered(started, after, tag):
        shards, lands = _gather_wait(started, after, name=f"gather_wait_{tag}")
        return _gather_pass(shards, lands, name=f"gather_pass_{tag}")

    bias = _bias_build(rel_bias, bucket).reshape(N_KV_HEADS, BIAS_ROWS, BLOCK)

    xs = x[0]
    xb = xs.astype(BF16)
    first_groups = ((0, 4), (1, 2, 3))
    token, first_started = rel_bias, []
    for tag, idxs in zip("ab", first_groups):
        first_started.append(_gather_start([shards_of(0)[i] for i in idxs], token, name=f"gather_start_0{tag}"))
        token = first_started[-1][4]
    started = {1: _gather_start(shards_of(1), token, name="gather_start_1")} if L > 1 else {}
    saved = []
    for l in range(L):
        pb = p[l, 0].astype(BF16)
        sinks_l = attn_sinks[l]
        scale_l = pool_scale[l].reshape(1, POOL_WIDTH)
        bias_l = b_in[l].reshape(1, IN_COLS)
        if l == 0:
            (w_in_f, w_pool_g), _ = gathered(first_started[0], started[1][4] if L > 1 else xb, "0a")
            w_in_t = w_in_f[0]
            h = _matmul(xb, w_in_t, tb=True, tm=512, tn=2176, tk=2048, out_dtype=F32, bias=bias_l, name=f"in_proj_{l}")
            (w_out_f, w_gate_f, w_ple_f), _ = gathered(first_started[1], h, "0b")
            w_out_g, w_gate_g, w_ple_t = w_out_f[0], w_gate_f[0], w_ple_f[0]
        else:
            w_in_t, w_out_g, w_gate_g, w_ple_t, w_pool_g = weights
            h = _matmul(xb, w_in_t, tb=True, tm=512, tn=2176, tk=2048, out_dtype=F32, bias=bias_l, name=f"in_proj_{l}")
        weights = (w_in_t, w_out_g, w_gate_g, w_ple_t, w_pool_g)
        ab = _attn_fwd(h, bias, masks, sinks_l, name=f"attn_fwd_{l}")
        ab = _pool_fwd(h, ab, w_pool_g, scale_l, name=f"pool_fwd_{l}")
        pin, passing = ab, None
        if 1 <= l and l + 1 < L:
            shards, lands = _gather_wait(started[l + 1], ab, name=f"gather_wait_{l + 1}")
            passing = _pass_start(_gather_own(shards, lands, name=f"gather_own_{l + 1}"), ab, name=f"pass_start_{l + 1}")
            pin = passing[3]
        if l + 2 < L:
            started[l + 2] = _gather_start(shards_of(l + 2), pin, name=f"gather_start_{l + 2}")
            pin = started[l + 2][4]
        y, yb, xhat, rstd, gp, pe = _mix_ln_fwd(ab, xs, pb, w_out_g, w_gate_g, w_ple_t, ln_gain[l].reshape(1, D_MODEL),
                                                ln_bias[l].reshape(1, D_MODEL), alpha, pin, name=f"mix_ln_fwd_{l}")
        saved.append((xb, pb, h, gp, pe, ab, xhat, rstd, sinks_l, scale_l, weights))
        xs, xb = y, yb
        if l + 1 < L:
            if passing is None:
                full, _ = gathered(started[l + 1], yb, l + 1)
            else:
                full = _pass_wait(passing, yb, name=f"pass_wait_{l + 1}")
            weights = (full[0][0], full[1][0], full[2][0], full[3][0], full[4])

    dy, loss_tile = _loss_head(xs, loss_target[0])

    dbias = jnp.zeros((N_KV_HEADS, BIAS_ROWS, BLOCK), F32)
    sh0 = shards_of(0)
    lands_a = [lax.empty((N_DEV, L) + sh0[i].shape, BF16) for i in (1, 2, 3)]
    lands_b = [lax.empty((N_DEV, L) + sh0[i].shape, BF16) for i in (0, 4)]
    g_b_in, g_sinks, g_scale, g_gain, g_beta = [], [], [], [], []
    pend_a = pend_b = None

    def scatter(grads, lands, pending, l, tag):
        if pending:
            lands = _scatter_wait(pending[0], pending[1], grads[0], name=f"scatter_wait_{pending[1]}{tag}")
        lands = _place_own(grads, lands, l, name=f"place_own_{l}{tag}")
        return _scatter_start(grads, lands, l, name=f"scatter_start_{l}{tag}"), l

    for l in reversed(range(L)):
        xb, pb, h, gp, pe, ab, xhat, rstd, sinks_l, scale_l, weights = saved[l]
        w_in_t, w_out_g, w_gate_g, w_ple_t, w_pool_g = weights
        dzb, dpe, dgp, dab, dx, dgain, dbeta = _ln_dmix_bwd(
            dy, xhat, rstd, ln_gain[l].reshape(1, D_MODEL), gp, pe, w_out_g, w_gate_g, alpha,
            pend_b[0][4] if pend_b else rel_bias, name=f"ln_dmix_bwd_{l}")
        g_w_out = _matmul(ab, dzb, ta=True, tm=512, tn=1024, tk=4096, out_dtype=BF16, name=f"dw_out_{l}")
        g_w_gate = _matmul(xb, dgp, ta=True, tm=512, tn=1024, tk=4096, out_dtype=BF16, name=f"dw_gate_{l}")
        g_w_ple_t = _matmul(dpe, pb, ta=True, tm=1024, tn=256, tk=1024, out_dtype=BF16, name=f"dw_ple_{l}")
        pend_a = scatter([g_w_out[None], g_w_gate[None], g_w_ple_t[None]], lands_a, pend_a, l, "a")
        dh, dwp, dsc, db_pool = _pool_bwd(h, dab, w_pool_g, scale_l, pend_a[0][4], name=f"pool_bwd_{l}")
        dh, dbias, dsink, db_attn = _attn_bwd(h, dab, dh, bias, masks, sinks_l, dbias, name=f"attn_bwd_{l}")
        g_w_in_t = _matmul(dh, xb, ta=True, tm=256, tn=1024, tk=4096, out_dtype=BF16, name=f"dw_in_{l}")
        pend_b = scatter([g_w_in_t[None], dwp.astype(BF16)], lands_b, pend_b, l, "b")
        g_b_in.append(jnp.concatenate([db_attn[0], db_pool[0, U_OFF - DH_POOL_COLS:]]))
        dy = _matmul(dh, w_in_t, tm=512, tn=1024, tk=4352, out_dtype=F32, add=dx, after=pend_b[0][4], name=f"dx_in_{l}")
        g_sinks.append(dsink[:, :, 0].reshape(N_HEADS))
        g_scale.append(dsc.reshape(POOL_WIDTH))
        g_gain.append(dgain.reshape(D_MODEL))
        g_beta.append(dbeta.reshape(D_MODEL))
    grad_x = dy[None]
    for lst in (g_b_in, g_sinks, g_scale, g_gain, g_beta):
        lst.reverse()
    g_rel = _bias_bwd(dbias.reshape((N_HEADS,) + BAND), bucket, window)[:, :N_HEADS]

    def flat(a):
        return a.reshape((-1, a.shape[-1]))

    def big(w, g, m, v, name):
        d, nm, nv = _adamw(flat(w), flat(g), flat(m), flat(v), name=name)
        return d.reshape(w.shape), nm.reshape(w.shape), nv.reshape(w.shape)

    def reduced(r, w, m, v, name):
        g, d, nm, nv = _sum_adamw(r.reshape(N_DEV, -1, r.shape[-1]), flat(w), flat(m), flat(v), name=name)
        return g.reshape(w.shape), (d.reshape(w.shape), nm.reshape(w.shape), nv.reshape(w.shape))

    r_out, r_gate, r_ple = _scatter_wait(pend_a[0], pend_a[1], dy, name="scatter_wait_0a")
    small_like = [b_in, attn_sinks, rel_bias, pool_scale, ln_gain, ln_bias]
    small_g = _allreduce_small(_pack_small([
        jnp.stack(g_b_in).reshape(L, IN_COLS), jnp.stack(g_sinks), g_rel, jnp.stack(g_scale), jnp.stack(g_gain),
        jnp.stack(g_beta), loss_tile[0, :1]]))
    grad_w_out, upd_out = reduced(r_out, w_out, m_w_out, v_w_out, "adamw_w_out")
    grad_w_gate, upd_gate = reduced(r_gate, w_gate_ple, m_w_gate_ple, v_w_gate_ple, "adamw_w_gate")
    gt_ple = _sum_slabs(r_ple.reshape(N_DEV, L * 256, PLE_DIM), name="sum_w_ple")
    grad_w_ple = jnp.swapaxes(gt_ple.reshape(L, 256, PLE_DIM), 1, 2)
    upd_ple = big(w_ple, grad_w_ple, m_w_ple, v_w_ple, "adamw_w_ple")

    w_in_rows = 544
    in_ops = [flat(jnp.swapaxes(a, 1, 2)) for a in (w_in, m_w_in, v_w_in)]
    landed = pend_b[0][3][0].reshape(N_DEV, L * w_in_rows, D_MODEL)
    upper = _sum_adamw(landed, *in_ops, name="adamw_w_in_upper", tile=w_in_rows // 2,
                       rows=(w_in_rows, L * w_in_rows)) if L > 1 else None
    r_in, r_pool = _scatter_wait(pend_b[0], pend_b[1], upper[0] if L > 1 else upd_gate[0], name="scatter_wait_0b")
    in_res = _sum_adamw(r_in.reshape(N_DEV, L * w_in_rows, D_MODEL), *in_ops, name="adamw_w_in", tile=w_in_rows // 2,
                        rows=(0, w_in_rows), prev=upper)
    gt_in, *upd_in_t = [a.reshape(L, w_in_rows, D_MODEL) for a in in_res]
    grad_w_in = jnp.swapaxes(gt_in, 1, 2)
    upd_in = tuple(jnp.swapaxes(u, 1, 2) for u in upd_in_t)
    grad_w_pool, upd_pool = reduced(r_pool, w_pool, m_w_pool, v_w_pool, "adamw_w_pool")

    zero1 = jnp.zeros((1,), F32)
    sw = _pack_small(small_like + [zero1])
    sm = _pack_small([m_b_in, m_attn_sinks, m_rel_bias, m_pool_scale, m_ln_gain, m_ln_bias, zero1])
    sv = _pack_small([v_b_in, v_attn_sinks, v_rel_bias, v_pool_scale, v_ln_gain, v_ln_bias, zero1])
    sd, snm, snv = _adamw(sw, small_g, sm, sv, name="adamw_small")
    like = small_like + [zero1]
    sg_parts = _unpack_small(small_g, like)
    sd_parts, snm_parts, snv_parts = _unpack_small(sd, like), _unpack_small(snm, like), _unpack_small(snv, like)
    loss = sg_parts[6][0]

    def assemble(big_parts, small_parts):
        w_in_, w_out_, w_pool_, w_ple_, w_gate_ = big_parts
        b_in_, sinks_, rel_, scale_, gain_, beta_ = small_parts[:6]
        return [w_in_, b_in_, w_out_, sinks_, rel_, w_pool_, scale_, w_ple_, w_gate_, gain_, beta_]

    grads = assemble([grad_w_in, grad_w_out, grad_w_pool, grad_w_ple, grad_w_gate], sg_parts)
    ups = [upd_in, upd_out, upd_pool, upd_ple, upd_gate]
    deltas = assemble([u[0] for u in ups], sd_parts)
    new_m = assemble([u[1] for u in ups], snm_parts)
    new_v = assemble([u[2] for u in ups], snv_parts)
    return (loss, grad_x, *grads, *deltas, *new_m, *new_v)
```
